```python
import jax, jax.numpy as jnp
from jax import lax
import numpy as np

D_MODEL = 1024
BATCH = 2
SEQ = 8192
DEPTH = 2
DEC_BATCH = 1
DEC_SEQ = 16384
PAST_LEN = 128

GRID_W = 64
HEAD_DIM = 64
BLOCK_Q = 128
ROPE_THETA = 10000.0
EPS = 1e-6
A_HEADS = 8
A_KV_HEADS = 2
B_PATTERNS = ((128, 1), (512, 4), (2048, 16))
B_GROUPS = 3
B_HEADS = 8
C_HEADS = 4
C_KEY_DIM = 64
C_VAL_DIM = 128
C_CHUNK = 128
RET_BASE_EXP = 5
D_FF = 2816
N_EXPERTS = 8
TOP_K = 2
D_FF_EXPERT = 3584
N_BRANCHES = 3

A_Q = A_HEADS * HEAD_DIM
A_KV = A_KV_HEADS * HEAD_DIM
B_Q = B_GROUPS * B_HEADS * HEAD_DIM
B_KV = B_HEADS * HEAD_DIM
C_QK = C_HEADS * C_KEY_DIM
C_V = C_HEADS * C_VAL_DIM
IN_SPLITS = (A_Q, A_KV, A_KV, B_Q, B_KV, B_KV, C_QK, C_QK, C_V, C_V, N_BRANCHES * D_MODEL)
IN_WIDTH = sum(IN_SPLITS)
N_DENSE = (DEPTH + 1) // 2
N_MOE = DEPTH // 2

kernel_name = "hybrid_gated_gqa_dilated_retention_encoder"


def _rmsnorm(x, g):
    xf = x.astype(jnp.float32)
    y = xf * lax.rsqrt(jnp.mean(xf * xf, -1, keepdims=True) + EPS)
    return (y * g.astype(jnp.float32)).astype(x.dtype)


def _head_rms(x, g):
    xf = x.astype(jnp.float32)
    return xf * lax.rsqrt(jnp.mean(xf * xf, -1, keepdims=True) + EPS) * g.astype(jnp.float32)


def _rope_angles(pos, dim):
    inv = ROPE_THETA ** (-jnp.arange(0, dim, 2, dtype=jnp.float32) / dim)
    return pos.astype(jnp.float32)[:, None] * inv[None, :]


def _rotate(x, ang):
    half = x.shape[-1] // 2
    c = jnp.cos(ang)[:, None, :]
    s = jnp.sin(ang)[:, None, :]
    x1, x2 = x[..., :half], x[..., half:]
    return jnp.concatenate([x1 * c - x2 * s, x1 * s + x2 * c], -1)


def _rotate_axial(x, ang_row, ang_col):
    h = x.shape[-1] // 2
    return jnp.concatenate([_rotate(x[..., :h], ang_row), _rotate(x[..., h:], ang_col)], -1)


def _mixer_a(q, k, v, g_q, g_k, row_pos, col_pos):
    b, s = q.shape[:2]
    dt = v.dtype
    ar = _rope_angles(row_pos, HEAD_DIM // 2)
    ac = _rope_angles(col_pos, HEAD_DIM // 2)
    q = (_rotate_axial(_head_rms(q, g_q), ar, ac) * HEAD_DIM ** -0.5).astype(dt)
    k = _rotate_axial(_head_rms(k, g_k), ar, ac).astype(dt)
    n_rep = A_HEADS // A_KV_HEADS
    nb = s // BLOCK_Q
    qblocks = jnp.moveaxis(q.reshape(b, nb, BLOCK_Q, A_KV_HEADS, n_rep, HEAD_DIM), 1, 0)

    def blk(qb):
        sc = jnp.einsum('bqkgd,bskd->bkgqs', qb, k, preferred_element_type=jnp.float32)
        p = jax.nn.softmax(sc, axis=-1).astype(dt)
        return jnp.einsum('bkgqs,bskd->bqkgd', p, v)

    o = lax.map(blk, qblocks)
    return jnp.moveaxis(o, 0, 1).reshape(b, s, A_Q)


def _mixer_b(q, k, v, g_q, g_k, pos):
    b, s = q.shape[:2]
    dt = v.dtype
    ang = _rope_angles(pos, HEAD_DIM)
    q = (_rotate(_head_rms(q, g_q), ang) * HEAD_DIM ** -0.5).astype(dt)
    k = _rotate(_head_rms(k, g_k), ang).astype(dt)
    nb = s // BLOCK_Q
    qblocks = jnp.moveaxis(q.reshape(b, nb, BLOCK_Q, B_GROUPS, B_HEADS, HEAD_DIM), 1, 0)

    def blk(args):
        qb, i = args
        t = i * BLOCK_Q + jnp.arange(BLOCK_Q)
        outs, lses = [], []
        for g, (win, dil) in enumerate(B_PATTERNS):
            n_side = win // (2 * dil)
            offs = jnp.arange(-n_side, n_side + 1) * dil
            idx = t[:, None] + offs[None, :]
            valid = (idx >= 0) & (idx < s)
            idx = jnp.clip(idx, 0, s - 1)
            kg = k[:, idx]
            vg = v[:, idx]
            sc = jnp.einsum('bqhd,bqkhd->bqhk', qb[:, :, g], kg, preferred_element_type=jnp.float32)
            sc = jnp.where(valid[None, :, None, :], sc, -jnp.inf)
            m = jnp.max(sc, -1, keepdims=True)
            e = jnp.exp(sc - m)
            den = jnp.sum(e, -1, keepdims=True)
            o = jnp.einsum('bqhk,bqkhd->bqhd', (e / den).astype(dt), vg)
            outs.append(o.astype(jnp.float32))
            lses.append((m + jnp.log(den))[..., 0])
        w = jax.nn.softmax(jnp.stack(lses, 0), axis=0)
        return jnp.sum(w[..., None] * jnp.stack(outs, 0), 0).astype(dt)

    o = lax.map(blk, (qblocks, jnp.arange(nb)))
    return jnp.moveaxis(o, 0, 1).reshape(b, s, B_KV)


def _retention_dir(q, k, v, log_g, strict):
    b, s, h, dk = q.shape
    dv = v.shape[-1]
    n = s // C_CHUNK
    j = jnp.arange(C_CHUNK, dtype=jnp.float32)
    diff = j[:, None] - j[None, :]
    mask = (diff > 0) if strict else (diff >= 0)
    dec = jnp.where(mask[None], jnp.exp(jnp.maximum(diff, 0.0)[None] * log_g[:, None, None]), 0.0)
    xi = jnp.exp((j[:, None] + 1.0) * log_g[None, :])
    zeta = jnp.exp((C_CHUNK - 1.0 - j)[:, None] * log_g[None, :])
    chunk_dec = jnp.exp(C_CHUNK * log_g)

    def to_chunks(a):
        return jnp.moveaxis(a.reshape(b, n, C_CHUNK, h, a.shape[-1]), 1, 0)

    def step(r, xs):
        qc, kc, vc = xs
        att = jnp.einsum('bjhd,blhd->bhjl', qc, kc) * dec[None]
        inner = jnp.einsum('bhjl,blhe->bjhe', att, vc)
        cross = jnp.einsum('bjhd,bhde->bjhe', qc, r) * xi[None, :, :, None]
        r = r * chunk_dec[None, :, None, None] + jnp.einsum('blhd,blhe->bhde', kc * zeta[None, :, :, None], vc)
        return r, inner + cross

    r0 = jnp.zeros((b, h, dk, dv), jnp.float32)
    _, o = lax.scan(step, r0, (to_chunks(q), to_chunks(k), to_chunks(v)))
    return jnp.moveaxis(o, 0, 1).reshape(b, s, h, dv)


def _mixer_c(q, k, v, gate, dec_f, dec_b, norm_g, pos):
    b, s = q.shape[:2]
    dt = v.dtype
    ang = _rope_angles(pos, C_KEY_DIM)
    qf = _rotate(q.astype(jnp.float32), ang)
    kf = _rotate(k.astype(jnp.float32), ang) * C_KEY_DIM ** -0.5
    vf = v.astype(jnp.float32)
    log_f = -jnp.exp(dec_f.astype(jnp.float32))
    log_b = -jnp.exp(dec_b.astype(jnp.float32))
    o_f = _retention_dir(qf, kf, vf, log_f, False)
    o_b = jnp.flip(_retention_dir(jnp.flip(qf, 1), jnp.flip(kf, 1), jnp.flip(vf, 1), log_b, True), 1)
    o = o_f + o_b
    mu = jnp.mean(o, -1, keepdims=True)
    var = jnp.mean(jnp.square(o - mu), -1, keepdims=True)
    o = ((o - mu) * lax.rsqrt(var + EPS)).reshape(b, s, C_V) * norm_g.astype(jnp.float32)
    return (jax.nn.silu(gate.astype(jnp.float32)) * o).astype(dt)


def _swiglu(x, w1, w3, w2):
    return (jax.nn.silu(x @ w1) * (x @ w3)) @ w2


def _moe(x, router, w1, w3, w2):
    b, s, d = x.shape
    xt = x.reshape(b * s, d)
    logits = jnp.dot(xt, router, preferred_element_type=jnp.float32)
    top_v, top_i = lax.top_k(logits, TOP_K)
    top_w = jax.nn.softmax(top_v, axis=-1)
    gate = jnp.sum(jax.nn.one_hot(top_i, N_EXPERTS, dtype=jnp.float32) * top_w[..., None], 1)
    out = jnp.zeros((b * s, d), jnp.float32)
    for e in range(N_EXPERTS):
        out = out + gate[:, e:e + 1] * _swiglu(xt, w1[e], w3[e], w2[e]).astype(jnp.float32)
    return out.astype(x.dtype).reshape(b, s, d)


def _trunk(x, norm1_g, w_in, a_qn, a_kn, b_qn, b_kn, ret_dec_f, ret_dec_b, ret_norm_g,
           w_oa, w_ob, w_oc, w_out, norm2_g, ffn_w1, ffn_w3, ffn_w2,
           moe_router, moe_w1, moe_w3, moe_w2):
    b, s, _ = x.shape
    rows = s // GRID_W
    pos = jnp.arange(s)
    row_pos = jnp.repeat(jnp.arange(rows), GRID_W)
    col_pos = jnp.tile(jnp.arange(GRID_W), rows)
    cuts = [int(c) for c in np.cumsum(IN_SPLITS)[:-1]]
    for l in range(DEPTH):
        xn = _rmsnorm(x, norm1_g[l])
        proj = xn @ w_in[l]
        aq, ak, av, bq, bk, bv, cq, ck, cv, cg, gl = jnp.split(proj, cuts, axis=-1)
        o_a = _mixer_a(aq.reshape(b, s, A_HEADS, HEAD_DIM), ak.reshape(b, s, A_KV_HEADS, HEAD_DIM),
                       av.reshape(b, s, A_KV_HEADS, HEAD_DIM), a_qn[l], a_kn[l], row_pos, col_pos)
        o_b = _mixer_b(bq.reshape(b, s, B_GROUPS * B_HEADS, HEAD_DIM), bk.reshape(b, s, B_HEADS, HEAD_DIM),
                       bv.reshape(b, s, B_HEADS, HEAD_DIM), b_qn[l], b_kn[l], pos)
        o_c = _mixer_c(cq.reshape(b, s, C_HEADS, C_KEY_DIM), ck.reshape(b, s, C_HEADS, C_KEY_DIM),
                       cv.reshape(b, s, C_HEADS, C_VAL_DIM), cg, ret_dec_f[l], ret_dec_b[l], ret_norm_g[l], pos)
        gates = jax.nn.sigmoid(gl.astype(jnp.float32)).astype(x.dtype)
        g_a, g_b, g_c = jnp.split(gates, N_BRANCHES, axis=-1)
        merged = g_a * (o_a @ w_oa[l]) + g_b * (o_b @ w_ob[l]) + g_c * (o_c @ w_oc[l])
        x = x + merged @ w_out[l]
        hn = _rmsnorm(x, norm2_g[l])
        if l % 2 == 0:
            i = l // 2
            x = x + _swiglu(hn, ffn_w1[i], ffn_w3[i], ffn_w2[i])
        else:
            i = l // 2
            x = x + _moe(hn, moe_router[i], moe_w1[i], moe_w3[i], moe_w2[i])
    return x


def setup_inputs(seed: int = 0) -> dict:
    key = jax.random.key(seed)
    ks = jax.random.split(key, 24)
    f32 = jnp.float32

    def nrm(k, shape, scale):
        return jax.random.normal(k, shape, f32) * scale

    def gain(k, shape):
        return 1.0 + 0.02 * jax.random.normal(k, shape, f32)

    ret_base = jnp.log(-jnp.log(1.0 - 2.0 ** (-RET_BASE_EXP - jnp.arange(C_HEADS, dtype=f32))))
    return {
        "x_prompt": nrm(ks[0], (BATCH, SEQ, D_MODEL), 1.0),
        "x_sample": nrm(ks[1], (DEC_BATCH, DEC_SEQ, D_MODEL), 1.0),
        "norm1_g": gain(ks[2], (DEPTH, D_MODEL)),
        "w_in": nrm(ks[3], (DEPTH, D_MODEL, IN_WIDTH), D_MODEL ** -0.5),
        "a_qn": gain(ks[4], (DEPTH, HEAD_DIM)),
        "a_kn": gain(ks[5], (DEPTH, HEAD_DIM)),
        "b_qn": gain(ks[6], (DEPTH, HEAD_DIM)),
        "b_kn": gain(ks[7], (DEPTH, HEAD_DIM)),
        "ret_dec_f": ret_base[None, :] + nrm(ks[8], (DEPTH, C_HEADS), 0.05),
        "ret_dec_b": ret_base[None, :] + nrm(ks[9], (DEPTH, C_HEADS), 0.05),
        "ret_norm_g": gain(ks[10], (DEPTH, C_V)),
        "w_oa": nrm(ks[11], (DEPTH, A_Q, D_MODEL), A_Q ** -0.5),
        "w_ob": nrm(ks[12], (DEPTH, B_KV, D_MODEL), B_KV ** -0.5),
        "w_oc": nrm(ks[13], (DEPTH, C_V, D_MODEL), C_V ** -0.5),
        "w_out": nrm(ks[14], (DEPTH, D_MODEL, D_MODEL), D_MODEL ** -0.5),
        "norm2_g": gain(ks[15], (DEPTH, D_MODEL)),
        "ffn_w1": nrm(ks[16], (N_DENSE, D_MODEL, D_FF), D_MODEL ** -0.5),
        "ffn_w3": nrm(ks[17], (N_DENSE, D_MODEL, D_FF), D_MODEL ** -0.5),
        "ffn_w2": nrm(ks[18], (N_DENSE, D_FF, D_MODEL), D_FF ** -0.5),
        "moe_router": nrm(ks[19], (N_MOE, D_MODEL, N_EXPERTS), D_MODEL ** -0.5),
        "moe_w1": nrm(ks[20], (N_MOE, N_EXPERTS, D_MODEL, D_FF_EXPERT), D_MODEL ** -0.5),
        "moe_w3": nrm(ks[21], (N_MOE, N_EXPERTS, D_MODEL, D_FF_EXPERT), D_MODEL ** -0.5),
        "moe_w2": nrm(ks[22], (N_MOE, N_EXPERTS, D_FF_EXPERT, D_MODEL), D_FF_EXPERT ** -0.5),
    }


def reference(x_prompt, x_sample, norm1_g, w_in, a_qn, a_kn, b_qn, b_kn, ret_dec_f, ret_dec_b,
              ret_norm_g, w_oa, w_ob, w_oc, w_out, norm2_g, ffn_w1, ffn_w3, ffn_w2,
              moe_router, moe_w1, moe_w3, moe_w2):
    y_prompt = _trunk(x_prompt, norm1_g, w_in, a_qn, a_kn, b_qn, b_kn, ret_dec_f, ret_dec_b, ret_norm_g,
                      w_oa, w_ob, w_oc, w_out, norm2_g, ffn_w1, ffn_w3, ffn_w2,
                      moe_router, moe_w1, moe_w3, moe_w2)
    y_sample = _trunk(x_sample, norm1_g, w_in, a_qn, a_kn, b_qn, b_kn, ret_dec_f, ret_dec_b, ret_norm_g,
                      w_oa, w_ob, w_oc, w_out, norm2_g, ffn_w1, ffn_w3, ffn_w2,
                      moe_router, moe_w1, moe_w3, moe_w2)
    return (y_prompt, y_sample)
```

```python
import functools

import numpy as np
import jax
import jax.numpy as jnp
from jax import lax
from jax.experimental import pallas as pl
from jax.experimental.pallas import tpu as pltpu

F32 = jnp.float32
BF16 = jnp.bfloat16

D_MODEL = 1024
GRID_W = 64
HEAD_DIM = 64
ROPE_THETA = 10000.0
EPS = 1e-6
A_HEADS = 8
A_KV_HEADS = 2
B_PATTERNS = ((128, 1), (512, 4), (2048, 16))
B_GROUPS = 3
B_HEADS = 8
C_HEADS = 4
C_KEY_DIM = 64
C_VAL_DIM = 128
N_EXPERTS = 8
N_BRANCHES = 3

A_Q = A_HEADS * HEAD_DIM
A_KV = A_KV_HEADS * HEAD_DIM
B_Q = B_GROUPS * B_HEADS * HEAD_DIM
B_KV = B_HEADS * HEAD_DIM
C_QK = C_HEADS * C_KEY_DIM
C_V = C_HEADS * C_VAL_DIM

LANES = 128
VMEM_LIMIT = 56 * 1024 * 1024

_ORIG_SPLITS = (("aq", A_Q), ("ak", A_KV), ("av", A_KV), ("bq", B_Q), ("bk", B_KV), ("bv", B_KV),
                ("cq", C_QK), ("ck", C_QK), ("cv", C_V), ("cg", C_V), ("gl", N_BRANCHES * D_MODEL))
_NEW_ORDER = ("bq", "bk", "aq", "bv", "cv", "cg", "gl", "cq", "ck", "ak", "av")
PROJ_W = 8192


def _layout():
    orig, o = {}, 0
    for name, w in _ORIG_SPLITS:
        orig[name] = (o, w)
        o += w
    new, o = {}, 0
    for name in _NEW_ORDER:
        new[name] = o
        o += orig[name][1]
    return orig, new, o


_ORIG, OFF, _USED_W = _layout()


def _cparams(sem):
    return pltpu.CompilerParams(dimension_semantics=sem, vmem_limit_bytes=VMEM_LIMIT)


def _inproj_kernel(x_ref, g_ref, w_ref, o_ref, xn_ref):
    @pl.when(pl.program_id(1) == 0)
    def _():
        x = x_ref[...]
        ms = jnp.mean(x * x, axis=-1, keepdims=True)
        xn_ref[...] = (x * lax.rsqrt(ms + EPS) * g_ref[...]).astype(BF16)

    o_ref[...] = jnp.dot(xn_ref[...], w_ref[...], preferred_element_type=F32).astype(BF16)


def _inproj(x, g, w):
    n = x.shape[0]
    tm = min(1024, n)
    tn = 1024
    return pl.pallas_call(
        _inproj_kernel,
        grid=(n // tm, PROJ_W // tn),
        in_specs=[pl.BlockSpec((tm, D_MODEL), lambda i, j: (i, 0)),
                  pl.BlockSpec((1, D_MODEL), lambda i, j: (0, 0)),
                  pl.BlockSpec((D_MODEL, tn), lambda i, j: (0, j))],
        out_specs=pl.BlockSpec((tm, tn), lambda i, j: (i, j)),
        out_shape=jax.ShapeDtypeStruct((n, PROJ_W), BF16),
        scratch_shapes=[pltpu.VMEM((tm, D_MODEL), BF16)],
        compiler_params=_cparams(("parallel", "arbitrary")),
        name="inproj",
    )(x, g, w)


def _norm_rot(x, gain, cos, sin, mmat, half, do_norm):
    if do_norm:
        ms = jnp.dot((x * x).astype(BF16), mmat, preferred_element_type=F32)
        x = x * lax.rsqrt(ms + EPS)
    x = x * gain
    lane = lax.broadcasted_iota(jnp.int32, x.shape, 1)
    first = (lane % (2 * half)) < half
    swapped = jnp.where(first, pltpu.roll(x, LANES - half, 1), pltpu.roll(x, half, 1))
    return x * cos + swapped * sin


def _prep_kernel(bqk_ref, aq_ref, bv_ref, cq_ref, ck_ref, ak_ref,
                 cosb_ref, sinb_ref, cosa_ref, sina_ref, gain_ref, mmat_ref,
                 obq_ref, obk_ref, obv_ref, oaq_ref, oak_ref, ocq_ref, ock_ref):
    cosb, sinb = cosb_ref[...], sinb_ref[...]
    cosa, sina = cosa_ref[...], sina_ref[...]
    mmat = mmat_ref[...]
    gains = gain_ref[...]

    def run(src_ref, dst_ref, src_off, width, gain_row, cos, sin, half, do_norm):
        for c in range(width // LANES):
            x = src_ref[:, src_off + c * LANES:src_off + (c + 1) * LANES].astype(F32)
            y = _norm_rot(x, gains[gain_row:gain_row + 1, :], cos, sin, mmat, half, do_norm)
            dst_ref[:, c * LANES:(c + 1) * LANES] = y.astype(dst_ref.dtype)

    run(bqk_ref, obq_ref, 0, B_Q, 0, cosb, sinb, HEAD_DIM // 2, True)
    run(bqk_ref, obk_ref, B_Q, B_KV, 1, cosb, sinb, HEAD_DIM // 2, True)
    run(aq_ref, oaq_ref, 0, A_Q, 2, cosa, sina, HEAD_DIM // 4, True)
    run(ak_ref, oak_ref, 0, A_KV, 3, cosa, sina, HEAD_DIM // 4, True)
    run(cq_ref, ocq_ref, 0, C_QK, 4, cosb, sinb, C_KEY_DIM // 2, False)
    run(ck_ref, ock_ref, 0, C_QK, 5, cosb, sinb, C_KEY_DIM // 2, False)
    obv_ref[...] = bv_ref[...]


def _prep(proj, tables, gains, mmat):
    n = proj.shape[0]
    tm = min(512, n)
    cosb, sinb, cosa, sina = tables

    def col(width, name):
        idx = OFF[name] // width
        return pl.BlockSpec((tm, width), lambda i: (i, idx))

    def tab():
        return pl.BlockSpec((tm, LANES), lambda i: (i, 0))

    def out(width):
        return pl.BlockSpec((tm, width), lambda i: (i, 0))

    widths = (B_Q, B_KV, B_KV, A_Q, A_KV, C_QK, C_QK)
    return pl.pallas_call(
        _prep_kernel,
        grid=(n // tm,),
        in_specs=[col(B_Q + B_KV, "bq"), col(A_Q, "aq"), col(B_KV, "bv"), col(C_QK, "cq"),
                  col(C_QK, "ck"), col(A_KV, "ak"), tab(), tab(), tab(), tab(),
                  pl.BlockSpec((8, LANES), lambda i: (0, 0)),
                  pl.BlockSpec((LANES, LANES), lambda i: (0, 0))],
        out_specs=[out(w) for w in widths],
        out_shape=[jax.ShapeDtypeStruct((n, w), BF16) for w in widths],
        compiler_params=_cparams(("parallel",)),
        name="prep",
    )(proj, proj, proj, proj, proj, proj, cosb, sinb, cosa, sina, gains, mmat)


def _attn_a_kernel(q_ref, kt_ref, v_ref, o_ref, qs_ref, m_ref, l_ref, acc_ref, *, bq, nk):
    ik = pl.program_id(2)
    rep = A_HEADS // A_KV_HEADS

    @pl.when(ik == 0)
    def _():
        for g in range(A_KV_HEADS):
            for r in range(rep):
                h = g * rep + r
                qs_ref[g, r * bq:(r + 1) * bq, :] = q_ref[:, h * HEAD_DIM:(h + 1) * HEAD_DIM]
        m_ref[...] = jnp.full(m_ref.shape, -jnp.inf, F32)
        l_ref[...] = jnp.zeros(l_ref.shape, F32)
        acc_ref[...] = jnp.zeros(acc_ref.shape, F32)

    v = v_ref[...]
    for g in range(A_KV_HEADS):
        s = jnp.dot(qs_ref[g], kt_ref[g * HEAD_DIM:(g + 1) * HEAD_DIM, :], preferred_element_type=F32)
        m_prev = m_ref[g]
        m_new = jnp.maximum(m_prev, jnp.max(s, axis=1, keepdims=True))
        alpha = jnp.exp(m_prev - m_new)
        p = jnp.exp(s - m_new)
        l_ref[g] = alpha * l_ref[g] + jnp.sum(p, axis=1, keepdims=True)
        acc_ref[g] = alpha * acc_ref[g] + jnp.dot(p.astype(BF16), v, preferred_element_type=F32)
        m_ref[g] = m_new

    @pl.when(ik == nk - 1)
    def _():
        for g in range(A_KV_HEADS):
            o = acc_ref[g][:, g * HEAD_DIM:(g + 1) * HEAD_DIM] / l_ref[g]
            for r in range(rep):
                h = g * rep + r
                o_ref[:, h * HEAD_DIM:(h + 1) * HEAD_DIM] = o[r * bq:(r + 1) * bq, :].astype(o_ref.dtype)


def _mixer_a(aq, akt, proj, row_off, b, s):
    n = aq.shape[0]
    bq = min(256, s)
    bk = min(512, s)
    nq, nk = s // bq, s // bk
    rep = A_HEADS // A_KV_HEADS
    v_col = OFF["av"] // A_KV
    q0, k0 = row_off // bq, row_off // bk
    kern = functools.partial(_attn_a_kernel, bq=bq, nk=nk)
    return pl.pallas_call(
        kern,
        grid=(b, nq, nk),
        in_specs=[pl.BlockSpec((bq, A_Q), lambda ib, iq, ik: (q0 + ib * nq + iq, 0)),
                  pl.BlockSpec((A_KV, bk), lambda ib, iq, ik: (0, k0 + ib * nk + ik)),
                  pl.BlockSpec((bk, A_KV), lambda ib, iq, ik: (k0 + ib * nk + ik, v_col))],
        out_specs=pl.BlockSpec((bq, A_Q), lambda ib, iq, ik: (ib * nq + iq, 0)),
        out_shape=jax.ShapeDtypeStruct((b * s, A_Q), BF16),
        scratch_shapes=[pltpu.VMEM((A_KV_HEADS, rep * bq, HEAD_DIM), BF16),
                        pltpu.VMEM((A_KV_HEADS, rep * bq, 1), F32),
                        pltpu.VMEM((A_KV_HEADS, rep * bq, 1), F32),
                        pltpu.VMEM((A_KV_HEADS, rep * bq, A_KV), F32)],
        compiler_params=_cparams(("parallel", "parallel", "arbitrary")),
        name="mixer_a",
    )(aq, akt, proj)


B_BLOCK_Q = 128
B_HALO = 64


def _attn_b_kernel(q_ref, k0_ref, k1_ref, k2_ref, k3_ref, v0_ref, v1_ref, v2_ref, v3_ref,
                   o_ref, lse_ref, *, u_len):
    i = pl.program_id(2)
    kcat = jnp.concatenate([k0_ref[...], k1_ref[...], k2_ref[...], k3_ref[...]], axis=0)
    vcat = jnp.concatenate([v0_ref[...], v1_ref[...], v2_ref[...], v3_ref[...]], axis=0)
    nkeys = B_BLOCK_Q + 2 * B_HALO
    a = lax.broadcasted_iota(jnp.int32, (B_BLOCK_Q, nkeys), 0)
    c = lax.broadcasted_iota(jnp.int32, (B_BLOCK_Q, nkeys), 1)
    key_pos = i * B_BLOCK_Q - B_HALO + c
    valid = (c >= a) & (c <= a + 2 * B_HALO) & (key_pos >= 0) & (key_pos < u_len)
    for h in range(B_HEADS):
        sl = slice(h * HEAD_DIM, (h + 1) * HEAD_DIM)
        s = lax.dot_general(q_ref[:, sl], kcat[:, sl], (((1,), (1,)), ((), ())),
                            preferred_element_type=F32)
        s = jnp.where(valid, s, -1e30)
        m = jnp.max(s, axis=1, keepdims=True)
        e = jnp.exp(s - m)
        den = jnp.sum(e, axis=1, keepdims=True)
        o = jnp.dot(e.astype(BF16), vcat[:, sl], preferred_element_type=F32) / den
        o_ref[:, sl] = o.astype(o_ref.dtype)
        lse_ref[:, sl] = jnp.broadcast_to(m + jnp.log(den), (B_BLOCK_Q, HEAD_DIM))


def _mixer_b_group(bq, bk, bv, g, dil, row_off, b, s):
    n = bq.shape[0]
    u_len = s // dil
    nq = u_len // B_BLOCK_Q
    nkb = u_len // B_HALO
    q_rows0 = row_off // dil // B_BLOCK_Q
    k_rows0 = row_off // dil // B_HALO
    qv = bq.reshape(n // dil, dil * B_Q)
    kv = bk.reshape(n // dil, dil * B_KV)
    vv = bv.reshape(n // dil, dil * B_KV)

    def kspec(t):
        def imap(ib, r, i):
            blk = jnp.clip(2 * i - 1 + t, 0, nkb - 1)
            return (k_rows0 + ib * nkb + blk, r)
        return pl.BlockSpec((B_HALO, B_KV), imap)

    def ospec():
        return pl.BlockSpec((B_BLOCK_Q, B_KV), lambda ib, r, i: (ib * nq + i, r))

    kern = functools.partial(_attn_b_kernel, u_len=u_len)
    o, lse = pl.pallas_call(
        kern,
        grid=(b, dil, nq),
        in_specs=[pl.BlockSpec((B_BLOCK_Q, B_KV), lambda ib, r, i: (q_rows0 + ib * nq + i, r * B_GROUPS + g))]
                 + [kspec(t) for t in range(4)] + [kspec(t) for t in range(4)],
        out_specs=[ospec(), ospec()],
        out_shape=[jax.ShapeDtypeStruct((b * s // dil, dil * B_KV), BF16),
                   jax.ShapeDtypeStruct((b * s // dil, dil * B_KV), F32)],
        compiler_params=_cparams(("parallel", "parallel", "parallel")),
        name=f"mixer_b{g}",
    )(qv, kv, kv, kv, kv, vv, vv, vv, vv)
    return o.reshape(b * s, B_KV), lse.reshape(b * s, B_KV)


def _ret_chunk(q, k, v, dec_ref, xi_ref, zeta_ref, cdec_ref, r_ref):
    outs = []
    for h in range(C_HEADS):
        qh = q[:, h * C_KEY_DIM:(h + 1) * C_KEY_DIM]
        kh = k[:, h * C_KEY_DIM:(h + 1) * C_KEY_DIM]
        vh = v[:, h * C_VAL_DIM:(h + 1) * C_VAL_DIM]
        att = lax.dot_general(qh, kh, (((1,), (1,)), ((), ())), preferred_element_type=F32) * dec_ref[h]
        inner = jnp.dot(att.astype(BF16), vh, preferred_element_type=F32)
        r = r_ref[h]
        cross = jnp.dot(qh, r.astype(BF16), preferred_element_type=F32) * xi_ref[h]
        kz = (kh.astype(F32) * zeta_ref[h]).astype(BF16)
        r_ref[h] = r * cdec_ref[h] + lax.dot_general(kz, vh, (((0,), (0,)), ((), ())),
                                                     preferred_element_type=F32)
        outs.append(inner + cross)
    return jnp.concatenate(outs, axis=1)


def _ret_fwd_kernel(q_ref, k_ref, v_ref, dec_ref, xi_ref, zeta_ref, cdec_ref, o_ref, r_ref):
    @pl.when(pl.program_id(1) == 0)
    def _():
        r_ref[...] = jnp.zeros(r_ref.shape, F32)

    o_ref[...] = _ret_chunk(q_ref[...], k_ref[...], v_ref[...], dec_ref, xi_ref, zeta_ref, cdec_ref, r_ref)


def _ret_bwd_kernel(q_ref, k_ref, v_ref, dec_ref, xi_ref, zeta_ref, cdec_ref, of_ref, gate_ref, ng_ref,
                    o_ref, r_ref):
    @pl.when(pl.program_id(1) == 0)
    def _():
        r_ref[...] = jnp.zeros(r_ref.shape, F32)

    o = of_ref[...] + _ret_chunk(q_ref[...], k_ref[...], v_ref[...], dec_ref, xi_ref, zeta_ref, cdec_ref, r_ref)
    ng = ng_ref[...]
    gate = gate_ref[...].astype(F32)
    for h in range(C_HEADS):
        sl = slice(h * C_VAL_DIM, (h + 1) * C_VAL_DIM)
        oh = o[:, sl]
        mu = jnp.mean(oh, axis=1, keepdims=True)
        var = jnp.mean(jnp.square(oh - mu), axis=1, keepdims=True)
        y = (oh - mu) * lax.rsqrt(var + EPS) * ng[:, sl]
        gh = gate[:, sl]
        o_ref[:, sl] = (gh * jax.nn.sigmoid(gh) * y).astype(o_ref.dtype)


def _ret_tables(dec_param, chunk, strict):
    log_g = -jnp.exp(dec_param.astype(F32))
    j = jnp.arange(chunk, dtype=F32)
    lg = log_g[:, None, None]
    if strict:
        diff = j[None, :] - j[:, None]
        dec = jnp.where((diff > 0)[None], jnp.exp(jnp.maximum(diff, 0.0)[None] * lg), 0.0)
        xi = jnp.exp((chunk - j)[None, :, None] * lg)
        zeta = jnp.exp(j[None, :, None] * lg)
    else:
        diff = j[:, None] - j[None, :]
        dec = jnp.where((diff >= 0)[None], jnp.exp(jnp.maximum(diff, 0.0)[None] * lg), 0.0)
        xi = jnp.exp((j + 1.0)[None, :, None] * lg)
        zeta = jnp.exp((chunk - 1.0 - j)[None, :, None] * lg)
    xi = jnp.broadcast_to(xi, (C_HEADS, chunk, C_VAL_DIM))
    zeta = jnp.broadcast_to(zeta, (C_HEADS, chunk, C_KEY_DIM))
    cdec = jnp.broadcast_to(jnp.exp(chunk * log_g)[:, None, None], (C_HEADS, C_KEY_DIM, C_VAL_DIM))
    return dec, xi, zeta, cdec


C_CHUNK = 128


def _mixer_c(cq, ck, proj, tabs_f, tabs_b, norm_g, row_off, b, s):
    nc = s // C_CHUNK
    r0 = row_off // C_CHUNK
    v_col = OFF["cv"] // C_V
    g_col = OFF["cg"] // C_V

    def fwd_rows(ib, c):
        return r0 + ib * nc + c

    def bwd_rows(ib, c):
        return r0 + ib * nc + (nc - 1 - c)

    def specs(rows):
        def full(shape):
            return pl.BlockSpec(shape, lambda ib, c: (0,) * len(shape))
        return [pl.BlockSpec((C_CHUNK, C_QK), lambda ib, c: (rows(ib, c), 0)),
                pl.BlockSpec((C_CHUNK, C_QK), lambda ib, c: (rows(ib, c), 0)),
                pl.BlockSpec((C_CHUNK, C_V), lambda ib, c: (rows(ib, c), v_col)),
                full((C_HEADS, C_CHUNK, C_CHUNK)), full((C_HEADS, C_CHUNK, C_VAL_DIM)),
                full((C_HEADS, C_CHUNK, C_KEY_DIM)), full((C_HEADS, C_KEY_DIM, C_VAL_DIM))]

    scratch = [pltpu.VMEM((C_HEADS, C_KEY_DIM, C_VAL_DIM), F32)]
    o_f = pl.pallas_call(
        _ret_fwd_kernel,
        grid=(b, nc),
        in_specs=specs(fwd_rows),
        out_specs=pl.BlockSpec((C_CHUNK, C_V), lambda ib, c: (ib * nc + c, 0)),
        out_shape=jax.ShapeDtypeStruct((b * s, C_V), F32),
        scratch_shapes=scratch,
        compiler_params=_cparams(("parallel", "arbitrary")),
        name="ret_fwd",
    )(cq, ck, proj, *tabs_f)
    return pl.pallas_call(
        _ret_bwd_kernel,
        grid=(b, nc),
        in_specs=specs(bwd_rows) + [
            pl.BlockSpec((C_CHUNK, C_V), lambda ib, c: (ib * nc + (nc - 1 - c), 0)),
            pl.BlockSpec((C_CHUNK, C_V), lambda ib, c: (bwd_rows(ib, c), g_col)),
            pl.BlockSpec((1, C_V), lambda ib, c: (0, 0))],
        out_specs=pl.BlockSpec((C_CHUNK, C_V), lambda ib, c: (ib * nc + (nc - 1 - c), 0)),
        out_shape=jax.ShapeDtypeStruct((b * s, C_V), BF16),
        scratch_shapes=scratch,
        compiler_params=_cparams(("parallel", "arbitrary")),
        name="ret_bwd",
    )(cq, ck, proj, *tabs_b, o_f, proj, norm_g)


def _merge_kernel(oa_ref, ob0_ref, ob1_ref, ob2_ref, l0_ref, l1_ref, l2_ref, oc_ref,
                  ga_ref, gb_ref, gc_ref, x_ref, woa_ref, wob_ref, woc_ref, wout_ref, n2_ref,
                  xo_ref, hn_ref):
    l0, l1, l2 = l0_ref[...], l1_ref[...], l2_ref[...]
    mx = jnp.maximum(jnp.maximum(l0, l1), l2)
    e0, e1, e2 = jnp.exp(l0 - mx), jnp.exp(l1 - mx), jnp.exp(l2 - mx)
    ob = (e0 * ob0_ref[...].astype(F32) + e1 * ob1_ref[...].astype(F32)
          + e2 * ob2_ref[...].astype(F32)) / (e0 + e1 + e2)

    def branch(o, w_ref, gate_ref):
        y = jnp.dot(o, w_ref[...], preferred_element_type=F32)
        return jax.nn.sigmoid(gate_ref[...].astype(F32)) * y

    merged = (branch(oa_ref[...], woa_ref, ga_ref) + branch(ob.astype(BF16), wob_ref, gb_ref)
              + branch(oc_ref[...], woc_ref, gc_ref))
    x = x_ref[...] + jnp.dot(merged.astype(BF16), wout_ref[...], preferred_element_type=F32)
    xo_ref[...] = x
    ms = jnp.mean(x * x, axis=-1, keepdims=True)
    hn_ref[...] = (x * lax.rsqrt(ms + EPS) * n2_ref[...]).astype(BF16)


def _merge(o_a, o_b, lse_b, o_c, proj, x, w_oa, w_ob, w_oc, w_out, n2):
    n = x.shape[0]
    tm = min(256, n)
    gl0 = OFF["gl"] // D_MODEL

    def rows(width):
        return pl.BlockSpec((tm, width), lambda i: (i, 0))

    def gate(k):
        return pl.BlockSpec((tm, D_MODEL), lambda i: (i, gl0 + k))

    def full(r, c):
        return pl.BlockSpec((r, c), lambda i: (0, 0))

    return pl.pallas_call(
        _merge_kernel,
        grid=(n // tm,),
        in_specs=[rows(A_Q)] + [rows(B_KV)] * 6 + [rows(C_V), gate(0), gate(1), gate(2), rows(D_MODEL),
                  full(A_Q, D_MODEL), full(B_KV, D_MODEL), full(C_V, D_MODEL), full(D_MODEL, D_MODEL),
                  full(1, D_MODEL)],
        out_specs=[rows(D_MODEL), rows(D_MODEL)],
        out_shape=[jax.ShapeDtypeStruct((n, D_MODEL), F32), jax.ShapeDtypeStruct((n, D_MODEL), BF16)],
        compiler_params=_cparams(("parallel",)),
        name="merge_out",
    )(o_a, *o_b, *lse_b, o_c, proj, proj, proj, x, w_oa, w_ob, w_oc, w_out, n2)


def _ffn_kernel(hn_ref, x_ref, w1_ref, w3_ref, w2_ref, o_ref, acc_ref, *, nf):
    j = pl.program_id(1)
    hn = hn_ref[...]
    a = jnp.dot(hn, w1_ref[...], preferred_element_type=F32)
    g = jnp.dot(hn, w3_ref[...], preferred_element_type=F32)
    h = (a * jax.nn.sigmoid(a) * g).astype(BF16)
    y = jnp.dot(h, w2_ref[...], preferred_element_type=F32)

    @pl.when(j == 0)
    def _():
        acc_ref[...] = x_ref[...] + y

    @pl.when(j > 0)
    def _():
        acc_ref[...] += y

    @pl.when(j == nf - 1)
    def _():
        o_ref[...] = acc_ref[...]


def _ffn(hn, x, w1, w3, w2):
    n = x.shape[0]
    d_ff = w1.shape[1]
    tm = min(512, n)
    tf = d_ff // 2
    nf = d_ff // tf
    return pl.pallas_call(
        functools.partial(_ffn_kernel, nf=nf),
        grid=(n // tm, nf),
        in_specs=[pl.BlockSpec((tm, D_MODEL), lambda i, j: (i, 0)),
                  pl.BlockSpec((tm, D_MODEL), lambda i, j: (i, 0)),
                  pl.BlockSpec((D_MODEL, tf), lambda i, j: (0, j)),
                  pl.BlockSpec((D_MODEL, tf), lambda i, j: (0, j)),
                  pl.BlockSpec((tf, D_MODEL), lambda i, j: (j, 0))],
        out_specs=pl.BlockSpec((tm, D_MODEL), lambda i, j: (i, 0)),
        out_shape=jax.ShapeDtypeStruct((n, D_MODEL), F32),
        scratch_shapes=[pltpu.VMEM((tm, D_MODEL), F32)],
        compiler_params=_cparams(("parallel", "arbitrary")),
        name="ffn",
    )(hn, x, w1, w3, w2)


def _router_kernel(hn_ref, wr_ref, gate_ref):
    logits = jnp.dot(hn_ref[...], wr_ref[...], preferred_element_type=F32)
    col = lax.broadcasted_iota(jnp.int32, logits.shape, 1)
    m1 = jnp.max(logits, axis=1, keepdims=True)
    i1 = jnp.min(jnp.where(logits == m1, col, N_EXPERTS), axis=1, keepdims=True)
    rest = jnp.where(col == i1, -jnp.inf, logits)
    m2 = jnp.max(rest, axis=1, keepdims=True)
    i2 = jnp.min(jnp.where(rest == m2, col, N_EXPERTS), axis=1, keepdims=True)
    e2 = jnp.exp(m2 - m1)
    w1 = 1.0 / (1.0 + e2)
    gate_ref[...] = jnp.where(col == i1, w1, 0.0) + jnp.where(col == i2, e2 * w1, 0.0)


def _router(hn, wr):
    n = hn.shape[0]
    tm = min(1024, n)
    return pl.pallas_call(
        _router_kernel,
        grid=(n // tm,),
        in_specs=[pl.BlockSpec((tm, D_MODEL), lambda i: (i, 0)),
                  pl.BlockSpec((D_MODEL, N_EXPERTS), lambda i: (0, 0))],
        out_specs=pl.BlockSpec((tm, N_EXPERTS), lambda i: (i, 0)),
        out_shape=jax.ShapeDtypeStruct((n, N_EXPERTS), F32),
        compiler_params=_cparams(("parallel",)),
        name="router",
    )(hn, wr)


def _moe_kernel(hn_ref, x_ref, gate_ref, w1_ref, w3_ref, w2_ref, o_ref, acc_ref, *, nf):
    e = pl.program_id(1)
    j = pl.program_id(2)
    hn = hn_ref[...]
    a = jnp.dot(hn, w1_ref[0], preferred_element_type=F32)
    g = jnp.dot(hn, w3_ref[0], preferred_element_type=F32)
    h = (a * jax.nn.sigmoid(a) * g).astype(BF16)
    gate = gate_ref[...]
    col = lax.broadcasted_iota(jnp.int32, gate.shape, 1)
    ge = jnp.sum(jnp.where(col == e, gate, 0.0), axis=1, keepdims=True)
    y = ge * jnp.dot(h, w2_ref[0], preferred_element_type=F32)
    first = (e == 0) & (j == 0)

    @pl.when(first)
    def _():
        acc_ref[...] = x_ref[...] + y

    @pl.when(jnp.logical_not(first))
    def _():
        acc_ref[...] += y

    @pl.when((e == N_EXPERTS - 1) & (j == nf - 1))
    def _():
        o_ref[...] = acc_ref[...]


def _moe(hn, x, gate, w1, w3, w2):
    n = x.shape[0]
    d_ff = w1.shape[2]
    tm = min(512, n)
    tf = d_ff // 4
    nf = d_ff // tf
    return pl.pallas_call(
        functools.partial(_moe_kernel, nf=nf),
        grid=(n // tm, N_EXPERTS, nf),
        in_specs=[pl.BlockSpec((tm, D_MODEL), lambda i, e, j: (i, 0)),
                  pl.BlockSpec((tm, D_MODEL), lambda i, e, j: (i, 0)),
                  pl.BlockSpec((tm, N_EXPERTS), lambda i, e, j: (i, 0)),
                  pl.BlockSpec((1, D_MODEL, tf), lambda i, e, j: (e, 0, j)),
                  pl.BlockSpec((1, D_MODEL, tf), lambda i, e, j: (e, 0, j)),
                  pl.BlockSpec((1, tf, D_MODEL), lambda i, e, j: (e, j, 0))],
        out_specs=pl.BlockSpec((tm, D_MODEL), lambda i, e, j: (i, 0)),
        out_shape=jax.ShapeDtypeStruct((n, D_MODEL), F32),
        scratch_shapes=[pltpu.VMEM((tm, D_MODEL), F32)],
        compiler_params=_cparams(("parallel", "arbitrary", "arbitrary")),
        name="moe",
    )(hn, x, gate, w1, w3, w2)


def _permute_w_in(w):
    cols = [w[:, _ORIG[name][0]:_ORIG[name][0] + _ORIG[name][1]] for name in _NEW_ORDER]
    cols.append(jnp.zeros((w.shape[0], PROJ_W - _USED_W), w.dtype))
    return jnp.concatenate(cols, axis=1).astype(BF16)


def _angles(pos, dim):
    inv = ROPE_THETA ** (-jnp.arange(0, dim, 2, dtype=F32) / dim)
    return pos.astype(F32)[:, None] * inv[None, :]


def _rope_tables(seqs):
    pos = jnp.concatenate([jnp.tile(jnp.arange(s), b) for b, s in seqs])
    ang = _angles(pos, HEAD_DIM)
    cosb = jnp.tile(jnp.cos(ang), (1, 4))
    sinb = jnp.tile(jnp.concatenate([-jnp.sin(ang), jnp.sin(ang)], axis=1), (1, 2))
    ar = _angles(pos // GRID_W, HEAD_DIM // 2)
    ac = _angles(pos % GRID_W, HEAD_DIM // 2)
    cosa = jnp.tile(jnp.concatenate([jnp.cos(ar), jnp.cos(ar), jnp.cos(ac), jnp.cos(ac)], axis=1), (1, 2))
    sina = jnp.tile(jnp.concatenate([-jnp.sin(ar), jnp.sin(ar), -jnp.sin(ac), jnp.sin(ac)], axis=1), (1, 2))
    return cosb, sinb, cosa, sina


def _head_mean_matrix():
    blk = np.kron(np.eye(LANES // HEAD_DIM), np.ones((HEAD_DIM, HEAD_DIM))) / HEAD_DIM
    return jnp.asarray(blk, BF16)


def _gain_rows(a_qn, a_kn, b_qn, b_kn):
    scale = HEAD_DIM ** -0.5
    rows = [jnp.tile(b_qn, 2) * scale, jnp.tile(b_kn, 2), jnp.tile(a_qn, 2) * scale, jnp.tile(a_kn, 2),
            jnp.ones((LANES,), F32), jnp.full((LANES,), C_KEY_DIM ** -0.5, F32),
            jnp.ones((LANES,), F32), jnp.ones((LANES,), F32)]
    return jnp.stack(rows).astype(F32)


def _trunk(x, seqs, norm1_g, w_in, a_qn, a_kn, b_qn, b_kn, ret_dec_f, ret_dec_b, ret_norm_g,
           w_oa, w_ob, w_oc, w_out, norm2_g, ffn_w1, ffn_w3, ffn_w2,
           moe_router, moe_w1, moe_w3, moe_w2):
    depth = w_in.shape[0]
    tables = _rope_tables(seqs)
    mmat = _head_mean_matrix()
    for l in range(depth):
        proj = _inproj(x, norm1_g[l][None, :], _permute_w_in(w_in[l]))
        bq, bk, bv, aq, ak, cq, ck = _prep(proj, tables, _gain_rows(a_qn[l], a_kn[l], b_qn[l], b_kn[l]), mmat)
        akt = ak.T
        tabs_f = _ret_tables(ret_dec_f[l], C_CHUNK, False)
        tabs_b = _ret_tables(ret_dec_b[l], C_CHUNK, True)
        ng = ret_norm_g[l][None, :].astype(F32)
        o_a, o_c = [], []
        o_b = [[] for _ in range(B_GROUPS)]
        lse_b = [[] for _ in range(B_GROUPS)]
        row_off = 0
        for b, s in seqs:
            o_a.append(_mixer_a(aq, akt, proj, row_off, b, s))
            for g, (_, dil) in enumerate(B_PATTERNS):
                og, lg = _mixer_b_group(bq, bk, bv, g, dil, row_off, b, s)
                o_b[g].append(og)
                lse_b[g].append(lg)
            o_c.append(_mixer_c(cq, ck, proj, tabs_f, tabs_b, ng, row_off, b, s))
            row_off += b * s
        cat = lambda parts: jnp.concatenate(parts, axis=0)
        x, hn = _merge(cat(o_a), [cat(p) for p in o_b], [cat(p) for p in lse_b], cat(o_c), proj, x,
                       w_oa[l].astype(BF16), w_ob[l].astype(BF16), w_oc[l].astype(BF16),
                       w_out[l].astype(BF16), norm2_g[l][None, :])
        i = l // 2
        if l % 2 == 0:
            x = _ffn(hn, x, ffn_w1[i].astype(BF16), ffn_w3[i].astype(BF16), ffn_w2[i].astype(BF16))
        else:
            gate = _router(hn, moe_router[i].astype(BF16))
            x = _moe(hn, x, gate, moe_w1[i].astype(BF16), moe_w3[i].astype(BF16), moe_w2[i].astype(BF16))
    return x


def kernel(x_prompt, x_sample, norm1_g, w_in, a_qn, a_kn, b_qn, b_kn, ret_dec_f, ret_dec_b, ret_norm_g,
           w_oa, w_ob, w_oc, w_out, norm2_g, ffn_w1, ffn_w3, ffn_w2, moe_router, moe_w1, moe_w3, moe_w2):
    seqs = (x_prompt.shape[:2], x_sample.shape[:2])
    x = jnp.concatenate([x_prompt.reshape(-1, D_MODEL), x_sample.reshape(-1, D_MODEL)], axis=0)
    y = _trunk(x, seqs, norm1_g, w_in, a_qn, a_kn, b_qn, b_kn, ret_dec_f, ret_dec_b, ret_norm_g,
               w_oa, w_ob, w_oc, w_out, norm2_g, ffn_w1, ffn_w3, ffn_w2,
               moe_router, moe_w1, moe_w3, moe_w2)
    n_p = x_prompt.shape[0] * x_prompt.shape[1]
    return (y[:n_p].reshape(x_prompt.shape), y[n_p:].reshape(x_sample.shape))
```

```python
import functools

import numpy as np
import jax
import jax.numpy as jnp
from jax import lax
from jax.experimental import pallas as pl
from jax.experimental.pallas import tpu as pltpu

F32 = jnp.float32
BF16 = jnp.bfloat16

D_MODEL = 1024
GRID_W = 64
HEAD_DIM = 64
ROPE_THETA = 10000.0
EPS = 1e-6
A_HEADS = 8
A_KV_HEADS = 2
B_PATTERNS = ((128, 1), (512, 4), (2048, 16))
B_GROUPS = 3
B_HEADS = 8
C_HEADS = 4
C_KEY_DIM = 64
C_VAL_DIM = 128
N_EXPERTS = 8
N_BRANCHES = 3

A_Q = A_HEADS * HEAD_DIM
A_KV = A_KV_HEADS * HEAD_DIM
B_Q = B_GROUPS * B_HEADS * HEAD_DIM
B_KV = B_HEADS * HEAD_DIM
C_QK = C_HEADS * C_KEY_DIM
C_V = C_HEADS * C_VAL_DIM

LANES = 128
VMEM_LIMIT = 56 * 1024 * 1024

_ORIG_SPLITS = (("aq", A_Q), ("ak", A_KV), ("av", A_KV), ("bq", B_Q), ("bk", B_KV), ("bv", B_KV),
                ("cq", C_QK), ("ck", C_QK), ("cv", C_V), ("cg", C_V), ("gl", N_BRANCHES * D_MODEL))
_NEW_ORDER = ("bq", "bk", "aq", "bv", "cv", "cg", "gl", "cq", "ck", "ak", "av")
PROJ_W = 8192


def _layout():
    orig, o = {}, 0
    for name, w in _ORIG_SPLITS:
        orig[name] = (o, w)
        o += w
    new, o = {}, 0
    for name in _NEW_ORDER:
        new[name] = o
        o += orig[name][1]
    return orig, new, o


_ORIG, OFF, _USED_W = _layout()


def _cparams(sem):
    return pltpu.CompilerParams(dimension_semantics=sem, vmem_limit_bytes=VMEM_LIMIT)


def _inproj_kernel(x_ref, g_ref, w_ref, o_ref, xn_ref):
    @pl.when(pl.program_id(1) == 0)
    def _():
        x = x_ref[...]
        ms = jnp.mean(x * x, axis=-1, keepdims=True)
        xn_ref[...] = (x * lax.rsqrt(ms + EPS) * g_ref[...]).astype(BF16)

    o_ref[...] = jnp.dot(xn_ref[...], w_ref[...], preferred_element_type=F32).astype(BF16)


def _inproj(x, g, w):
    n = x.shape[0]
    tm = min(1024, n)
    tn = 1024
    return pl.pallas_call(
        _inproj_kernel,
        grid=(n // tm, PROJ_W // tn),
        in_specs=[pl.BlockSpec((tm, D_MODEL), lambda i, j: (i, 0)),
                  pl.BlockSpec((1, D_MODEL), lambda i, j: (0, 0)),
                  pl.BlockSpec((D_MODEL, tn), lambda i, j: (0, j))],
        out_specs=pl.BlockSpec((tm, tn), lambda i, j: (i, j)),
        out_shape=jax.ShapeDtypeStruct((n, PROJ_W), BF16),
        scratch_shapes=[pltpu.VMEM((tm, D_MODEL), BF16)],
        compiler_params=_cparams(("parallel", "arbitrary")),
        name="inproj",
    )(x, g, w)


def _norm_rot(x, gain, cos, sin, mmat, half, do_norm):
    if do_norm:
        ms = jnp.dot((x * x).astype(BF16), mmat, preferred_element_type=F32)
        x = x * lax.rsqrt(ms + EPS)
    x = x * gain
    lane = lax.broadcasted_iota(jnp.int32, x.shape, 1)
    first = (lane % (2 * half)) < half
    swapped = jnp.where(first, pltpu.roll(x, LANES - half, 1), pltpu.roll(x, half, 1))
    return x * cos + swapped * sin


def _prep_kernel(bqk_ref, aq_ref, bv_ref, cq_ref, ck_ref, ak_ref,
                 cosb_ref, sinb_ref, cosa_ref, sina_ref, gain_ref, mmat_ref,
                 obq_ref, obk_ref, obv_ref, oaq_ref, oak_ref, ocq_ref, ock_ref):
    cosb, sinb = cosb_ref[...], sinb_ref[...]
    cosa, sina = cosa_ref[...], sina_ref[...]
    mmat = mmat_ref[...]
    gains = gain_ref[...]

    def run(src_ref, dst_ref, src_off, width, gain_row, cos, sin, half, do_norm):
        for c in range(width // LANES):
            x = src_ref[:, src_off + c * LANES:src_off + (c + 1) * LANES].astype(F32)
            y = _norm_rot(x, gains[gain_row:gain_row + 1, :], cos, sin, mmat, half, do_norm)
            dst_ref[:, c * LANES:(c + 1) * LANES] = y.astype(dst_ref.dtype)

    run(bqk_ref, obq_ref, 0, B_Q, 0, cosb, sinb, HEAD_DIM // 2, True)
    run(bqk_ref, obk_ref, B_Q, B_KV, 1, cosb, sinb, HEAD_DIM // 2, True)
    run(aq_ref, oaq_ref, 0, A_Q, 2, cosa, sina, HEAD_DIM // 4, True)
    run(ak_ref, oak_ref, 0, A_KV, 3, cosa, sina, HEAD_DIM // 4, True)
    run(cq_ref, ocq_ref, 0, C_QK, 4, cosb, sinb, C_KEY_DIM // 2, False)
    run(ck_ref, ock_ref, 0, C_QK, 5, cosb, sinb, C_KEY_DIM // 2, False)
    obv_ref[...] = bv_ref[...]


def _prep(proj, tables, gains, mmat):
    n = proj.shape[0]
    tm = min(512, n)
    cosb, sinb, cosa, sina = tables

    def col(width, name):
        idx = OFF[name] // width
        return pl.BlockSpec((tm, width), lambda i: (i, idx))

    def tab():
        return pl.BlockSpec((tm, LANES), lambda i: (i, 0))

    def out(width):
        return pl.BlockSpec((tm, width), lambda i: (i, 0))

    widths = (B_Q, B_KV, B_KV, A_Q, A_KV, C_QK, C_QK)
    return pl.pallas_call(
        _prep_kernel,
        grid=(n // tm,),
        in_specs=[col(B_Q + B_KV, "bq"), col(A_Q, "aq"), col(B_KV, "bv"), col(C_QK, "cq"),
                  col(C_QK, "ck"), col(A_KV, "ak"), tab(), tab(), tab(), tab(),
                  pl.BlockSpec((8, LANES), lambda i: (0, 0)),
                  pl.BlockSpec((LANES, LANES), lambda i: (0, 0))],
        out_specs=[out(w) for w in widths],
        out_shape=[jax.ShapeDtypeStruct((n, w), BF16) for w in widths],
        compiler_params=_cparams(("parallel",)),
        name="prep",
    )(proj, proj, proj, proj, proj, proj, cosb, sinb, cosa, sina, gains, mmat)


def _attn_a_kernel(qt_ref, k_ref, vt_ref, o_ref, qp_ref, m_ref, l_ref, acc_ref, *, bq, nk):
    ik = pl.program_id(2)
    rep = A_HEADS // A_KV_HEADS

    @pl.when(ik == 0)
    def _():
        qp_ref[...] = jnp.zeros(qp_ref.shape, BF16)
        for g in range(A_KV_HEADS):
            for r in range(rep):
                h = g * rep + r
                qp_ref[g, g * HEAD_DIM:(g + 1) * HEAD_DIM, r * bq:(r + 1) * bq] = (
                    qt_ref[h * HEAD_DIM:(h + 1) * HEAD_DIM, :])
        m_ref[...] = jnp.full(m_ref.shape, -jnp.inf, F32)
        l_ref[...] = jnp.zeros(l_ref.shape, F32)
        acc_ref[...] = jnp.zeros(acc_ref.shape, F32)

    k = k_ref[...]
    for g in range(A_KV_HEADS):
        s = jnp.dot(k, qp_ref[g], preferred_element_type=F32)
        m_prev = m_ref[g]
        m_new = jnp.maximum(m_prev, jnp.max(s, axis=0, keepdims=True))
        alpha = jnp.exp2(m_prev - m_new)
        p = jnp.exp2(s - m_new)
        l_ref[g] = alpha * l_ref[g] + jnp.sum(p, axis=0, keepdims=True)
        pv = jnp.dot(vt_ref[g * HEAD_DIM:(g + 1) * HEAD_DIM, :], p.astype(BF16),
                     preferred_element_type=F32)
        acc_ref[g] = alpha * acc_ref[g] + pv
        m_ref[g] = m_new

    @pl.when(ik == nk - 1)
    def _():
        for g in range(A_KV_HEADS):
            o = acc_ref[g] / l_ref[g]
            for r in range(rep):
                h = g * rep + r
                o_ref[h * HEAD_DIM:(h + 1) * HEAD_DIM, :] = o[:, r * bq:(r + 1) * bq].astype(o_ref.dtype)


def _mixer_a(aqt, ak, avt, row_off, b, s):
    bq = min(256, s)
    bk = min(512, s)
    nq, nk = s // bq, s // bk
    rep = A_HEADS // A_KV_HEADS
    q0, k0 = row_off // bq, row_off // bk
    kern = functools.partial(_attn_a_kernel, bq=bq, nk=nk)
    return pl.pallas_call(
        kern,
        grid=(b, nq, nk),
        in_specs=[pl.BlockSpec((A_Q, bq), lambda ib, iq, ik: (0, q0 + ib * nq + iq)),
                  pl.BlockSpec((bk, A_KV), lambda ib, iq, ik: (k0 + ib * nk + ik, 0)),
                  pl.BlockSpec((A_KV, bk), lambda ib, iq, ik: (0, k0 + ib * nk + ik))],
        out_specs=pl.BlockSpec((A_Q, bq), lambda ib, iq, ik: (0, ib * nq + iq)),
        out_shape=jax.ShapeDtypeStruct((A_Q, b * s), BF16),
        scratch_shapes=[pltpu.VMEM((A_KV_HEADS, A_KV, rep * bq), BF16),
                        pltpu.VMEM((A_KV_HEADS, 1, rep * bq), F32),
                        pltpu.VMEM((A_KV_HEADS, 1, rep * bq), F32),
                        pltpu.VMEM((A_KV_HEADS, HEAD_DIM, rep * bq), F32)],
        compiler_params=_cparams(("parallel", "parallel", "arbitrary")),
        name="mixer_a",
    )(aqt, ak, avt)


B_BLOCK_Q = 128
B_HALO = 64


def _attn_b_kernel(q_ref, k0_ref, k1_ref, k2_ref, k3_ref, v0_ref, v1_ref, v2_ref, v3_ref,
                   o_ref, lse_ref, *, u_len):
    i = pl.program_id(2)
    kcat = jnp.concatenate([k0_ref[...], k1_ref[...], k2_ref[...], k3_ref[...]], axis=0)
    vcat = jnp.concatenate([v0_ref[...], v1_ref[...], v2_ref[...], v3_ref[...]], axis=0)
    nkeys = B_BLOCK_Q + 2 * B_HALO
    a = lax.broadcasted_iota(jnp.int32, (B_BLOCK_Q, nkeys), 0)
    c = lax.broadcasted_iota(jnp.int32, (B_BLOCK_Q, nkeys), 1)
    key_pos = i * B_BLOCK_Q - B_HALO + c
    valid = (c >= a) & (c <= a + 2 * B_HALO) & (key_pos >= 0) & (key_pos < u_len)
    for h in range(B_HEADS):
        sl = slice(h * HEAD_DIM, (h + 1) * HEAD_DIM)
        s = lax.dot_general(q_ref[:, sl], kcat[:, sl], (((1,), (1,)), ((), ())),
                            preferred_element_type=F32)
        s = jnp.where(valid, s, -1e30)
        m = jnp.max(s, axis=1, keepdims=True)
        e = jnp.exp(s - m)
        den = jnp.sum(e, axis=1, keepdims=True)
        o = jnp.dot(e.astype(BF16), vcat[:, sl], preferred_element_type=F32) / den
        o_ref[:, sl] = o.astype(o_ref.dtype)
        lse_ref[:, sl] = jnp.broadcast_to(m + jnp.log(den), (B_BLOCK_Q, HEAD_DIM))


def _mixer_b_group(bq, bk, bv, g, dil, row_off, b, s):
    n = bq.shape[0]
    u_len = s // dil
    nq = u_len // B_BLOCK_Q
    nkb = u_len // B_HALO
    q_rows0 = row_off // dil // B_BLOCK_Q
    k_rows0 = row_off // dil // B_HALO
    qv = bq.reshape(n // dil, dil * B_Q)
    kv = bk.reshape(n // dil, dil * B_KV)
    vv = bv.reshape(n // dil, dil * B_KV)

    def kspec(t):
        def imap(ib, r, i):
            blk = jnp.clip(2 * i - 1 + t, 0, nkb - 1)
            return (k_rows0 + ib * nkb + blk, r)
        return pl.BlockSpec((B_HALO, B_KV), imap)

    def ospec():
        return pl.BlockSpec((B_BLOCK_Q, B_KV), lambda ib, r, i: (ib * nq + i, r))

    kern = functools.partial(_attn_b_kernel, u_len=u_len)
    o, lse = pl.pallas_call(
        kern,
        grid=(b, dil, nq),
        in_specs=[pl.BlockSpec((B_BLOCK_Q, B_KV), lambda ib, r, i: (q_rows0 + ib * nq + i, r * B_GROUPS + g))]
                 + [kspec(t) for t in range(4)] + [kspec(t) for t in range(4)],
        out_specs=[ospec(), ospec()],
        out_shape=[jax.ShapeDtypeStruct((b * s // dil, dil * B_KV), BF16),
                   jax.ShapeDtypeStruct((b * s // dil, dil * B_KV), F32)],
        compiler_params=_cparams(("parallel", "parallel", "parallel")),
        name=f"mixer_b{g}",
    )(qv, kv, kv, kv, kv, vv, vv, vv, vv)
    return o.reshape(b * s, B_KV), lse.reshape(b * s, B_KV)


def _ret_chunk(q, k, v, dec_ref, xi_ref, zeta_ref, cdec_ref, r_ref):
    outs = []
    for h in range(C_HEADS):
        qh = q[:, h * C_KEY_DIM:(h + 1) * C_KEY_DIM]
        kh = k[:, h * C_KEY_DIM:(h + 1) * C_KEY_DIM]
        vh = v[:, h * C_VAL_DIM:(h + 1) * C_VAL_DIM]
        att = lax.dot_general(qh, kh, (((1,), (1,)), ((), ())), preferred_element_type=F32) * dec_ref[h]
        inner = jnp.dot(att.astype(BF16), vh, preferred_element_type=F32)
        r = r_ref[h]
        cross = jnp.dot(qh, r.astype(BF16), preferred_element_type=F32) * xi_ref[h]
        kz = (kh.astype(F32) * zeta_ref[h]).astype(BF16)
        r_ref[h] = r * cdec_ref[h] + lax.dot_general(kz, vh, (((0,), (0,)), ((), ())),
                                                     preferred_element_type=F32)
        outs.append(inner + cross)
    return jnp.concatenate(outs, axis=1)


def _ret_fwd_kernel(q_ref, k_ref, v_ref, dec_ref, xi_ref, zeta_ref, cdec_ref, o_ref, r_ref):
    @pl.when(pl.program_id(1) == 0)
    def _():
        r_ref[...] = jnp.zeros(r_ref.shape, F32)

    o_ref[...] = _ret_chunk(q_ref[...], k_ref[...], v_ref[...], dec_ref, xi_ref, zeta_ref, cdec_ref, r_ref)


def _ret_bwd_kernel(q_ref, k_ref, v_ref, dec_ref, xi_ref, zeta_ref, cdec_ref, of_ref, gate_ref, ng_ref,
                    o_ref, r_ref):
    @pl.when(pl.program_id(1) == 0)
    def _():
        r_ref[...] = jnp.zeros(r_ref.shape, F32)

    o = of_ref[...] + _ret_chunk(q_ref[...], k_ref[...], v_ref[...], dec_ref, xi_ref, zeta_ref, cdec_ref, r_ref)
    ng = ng_ref[...]
    gate = gate_ref[...].astype(F32)
    for h in range(C_HEADS):
        sl = slice(h * C_VAL_DIM, (h + 1) * C_VAL_DIM)
        oh = o[:, sl]
        mu = jnp.mean(oh, axis=1, keepdims=True)
        var = jnp.mean(jnp.square(oh - mu), axis=1, keepdims=True)
        y = (oh - mu) * lax.rsqrt(var + EPS) * ng[:, sl]
        gh = gate[:, sl]
        o_ref[:, sl] = (gh * jax.nn.sigmoid(gh) * y).astype(o_ref.dtype)


def _ret_tables(dec_param, chunk, strict):
    log_g = -jnp.exp(dec_param.astype(F32))
    j = jnp.arange(chunk, dtype=F32)
    lg = log_g[:, None, None]
    if strict:
        diff = j[None, :] - j[:, None]
        dec = jnp.where((diff > 0)[None], jnp.exp(jnp.maximum(diff, 0.0)[None] * lg), 0.0)
        xi = jnp.exp((chunk - j)[None, :, None] * lg)
        zeta = jnp.exp(j[None, :, None] * lg)
    else:
        diff = j[:, None] - j[None, :]
        dec = jnp.where((diff >= 0)[None], jnp.exp(jnp.maximum(diff, 0.0)[None] * lg), 0.0)
        xi = jnp.exp((j + 1.0)[None, :, None] * lg)
        zeta = jnp.exp((chunk - 1.0 - j)[None, :, None] * lg)
    xi = jnp.broadcast_to(xi, (C_HEADS, chunk, C_VAL_DIM))
    zeta = jnp.broadcast_to(zeta, (C_HEADS, chunk, C_KEY_DIM))
    cdec = jnp.broadcast_to(jnp.exp(chunk * log_g)[:, None, None], (C_HEADS, C_KEY_DIM, C_VAL_DIM))
    return dec, xi, zeta, cdec


C_CHUNK = 128


def _mixer_c(cq, ck, proj, tabs_f, tabs_b, norm_g, row_off, b, s):
    nc = s // C_CHUNK
    r0 = row_off // C_CHUNK
    v_col = OFF["cv"] // C_V
    g_col = OFF["cg"] // C_V

    def fwd_rows(ib, c):
        return r0 + ib * nc + c

    def bwd_rows(ib, c):
        return r0 + ib * nc + (nc - 1 - c)

    def specs(rows):
        def full(shape):
            return pl.BlockSpec(shape, lambda ib, c: (0,) * len(shape))
        return [pl.BlockSpec((C_CHUNK, C_QK), lambda ib, c: (rows(ib, c), 0)),
                pl.BlockSpec((C_CHUNK, C_QK), lambda ib, c: (rows(ib, c), 0)),
                pl.BlockSpec((C_CHUNK, C_V), lambda ib, c: (rows(ib, c), v_col)),
                full((C_HEADS, C_CHUNK, C_CHUNK)), full((C_HEADS, C_CHUNK, C_VAL_DIM)),
                full((C_HEADS, C_CHUNK, C_KEY_DIM)), full((C_HEADS, C_KEY_DIM, C_VAL_DIM))]

    scratch = [pltpu.VMEM((C_HEADS, C_KEY_DIM, C_VAL_DIM), F32)]
    o_f = pl.pallas_call(
        _ret_fwd_kernel,
        grid=(b, nc),
        in_specs=specs(fwd_rows),
        out_specs=pl.BlockSpec((C_CHUNK, C_V), lambda ib, c: (ib * nc + c, 0)),
        out_shape=jax.ShapeDtypeStruct((b * s, C_V), F32),
        scratch_shapes=scratch,
        compiler_params=_cparams(("parallel", "arbitrary")),
        name="ret_fwd",
    )(cq, ck, proj, *tabs_f)
    return pl.pallas_call(
        _ret_bwd_kernel,
        grid=(b, nc),
        in_specs=specs(bwd_rows) + [
            pl.BlockSpec((C_CHUNK, C_V), lambda ib, c: (ib * nc + (nc - 1 - c), 0)),
            pl.BlockSpec((C_CHUNK, C_V), lambda ib, c: (bwd_rows(ib, c), g_col)),
            pl.BlockSpec((1, C_V), lambda ib, c: (0, 0))],
        out_specs=pl.BlockSpec((C_CHUNK, C_V), lambda ib, c: (ib * nc + (nc - 1 - c), 0)),
        out_shape=jax.ShapeDtypeStruct((b * s, C_V), BF16),
        scratch_shapes=scratch,
        compiler_params=_cparams(("parallel", "arbitrary")),
        name="ret_bwd",
    )(cq, ck, proj, *tabs_b, o_f, proj, norm_g)


def _merge_kernel(oa_ref, ob0_ref, ob1_ref, ob2_ref, l0_ref, l1_ref, l2_ref, oc_ref,
                  ga_ref, gb_ref, gc_ref, x_ref, woa_ref, wob_ref, woc_ref, wout_ref, n2_ref,
                  xo_ref, hn_ref):
    l0, l1, l2 = l0_ref[...], l1_ref[...], l2_ref[...]
    mx = jnp.maximum(jnp.maximum(l0, l1), l2)
    e0, e1, e2 = jnp.exp(l0 - mx), jnp.exp(l1 - mx), jnp.exp(l2 - mx)
    ob = (e0 * ob0_ref[...].astype(F32) + e1 * ob1_ref[...].astype(F32)
          + e2 * ob2_ref[...].astype(F32)) / (e0 + e1 + e2)

    def branch(o, w_ref, gate_ref):
        y = jnp.dot(o, w_ref[...], preferred_element_type=F32)
        return jax.nn.sigmoid(gate_ref[...].astype(F32)) * y

    oa = jnp.transpose(oa_ref[...].astype(F32)).astype(BF16)
    merged = (branch(oa, woa_ref, ga_ref) + branch(ob.astype(BF16), wob_ref, gb_ref)
              + branch(oc_ref[...], woc_ref, gc_ref))
    x = x_ref[...] + jnp.dot(merged.astype(BF16), wout_ref[...], preferred_element_type=F32)
    xo_ref[...] = x
    ms = jnp.mean(x * x, axis=-1, keepdims=True)
    hn_ref[...] = (x * lax.rsqrt(ms + EPS) * n2_ref[...]).astype(hn_ref.dtype)


def _merge(o_a, o_b, lse_b, o_c, proj, x, w_oa, w_ob, w_oc, w_out, n2, hn_dtype):
    n = x.shape[0]
    tm = min(256, n)
    gl0 = OFF["gl"] // D_MODEL

    def rows(width):
        return pl.BlockSpec((tm, width), lambda i: (i, 0))

    def gate(k):
        return pl.BlockSpec((tm, D_MODEL), lambda i: (i, gl0 + k))

    def full(r, c):
        return pl.BlockSpec((r, c), lambda i: (0, 0))

    return pl.pallas_call(
        _merge_kernel,
        grid=(n // tm,),
        in_specs=[pl.BlockSpec((A_Q, tm), lambda i: (0, i))] + [rows(B_KV)] * 6 + [rows(C_V), gate(0), gate(1), gate(2), rows(D_MODEL),
                  full(A_Q, D_MODEL), full(B_KV, D_MODEL), full(C_V, D_MODEL), full(D_MODEL, D_MODEL),
                  full(1, D_MODEL)],
        out_specs=[rows(D_MODEL), rows(D_MODEL)],
        out_shape=[jax.ShapeDtypeStruct((n, D_MODEL), F32), jax.ShapeDtypeStruct((n, D_MODEL), hn_dtype)],
        compiler_params=_cparams(("parallel",)),
        name="merge_out",
    )(o_a, *o_b, *lse_b, o_c, proj, proj, proj, x, w_oa, w_ob, w_oc, w_out, n2)


def _ffn_kernel(hn_ref, x_ref, w1_ref, w3_ref, w2_ref, o_ref, acc_ref, *, nf):
    j = pl.program_id(1)
    hn = hn_ref[...]
    a = jnp.dot(hn, w1_ref[...], preferred_element_type=F32)
    g = jnp.dot(hn, w3_ref[...], preferred_element_type=F32)
    h = (a * jax.nn.sigmoid(a) * g).astype(BF16)
    y = jnp.dot(h, w2_ref[...], preferred_element_type=F32)

    @pl.when(j == 0)
    def _():
        acc_ref[...] = x_ref[...] + y

    @pl.when(j > 0)
    def _():
        acc_ref[...] += y

    @pl.when(j == nf - 1)
    def _():
        o_ref[...] = acc_ref[...]


def _ffn(hn, x, w1, w3, w2):
    n = x.shape[0]
    d_ff = w1.shape[1]
    tm = min(512, n)
    tf = d_ff // 2
    nf = d_ff // tf
    return pl.pallas_call(
        functools.partial(_ffn_kernel, nf=nf),
        grid=(n // tm, nf),
        in_specs=[pl.BlockSpec((tm, D_MODEL), lambda i, j: (i, 0)),
                  pl.BlockSpec((tm, D_MODEL), lambda i, j: (i, 0)),
                  pl.BlockSpec((D_MODEL, tf), lambda i, j: (0, j)),
                  pl.BlockSpec((D_MODEL, tf), lambda i, j: (0, j)),
                  pl.BlockSpec((tf, D_MODEL), lambda i, j: (j, 0))],
        out_specs=pl.BlockSpec((tm, D_MODEL), lambda i, j: (i, 0)),
        out_shape=jax.ShapeDtypeStruct((n, D_MODEL), F32),
        scratch_shapes=[pltpu.VMEM((tm, D_MODEL), F32)],
        compiler_params=_cparams(("parallel", "arbitrary")),
        name="ffn",
    )(hn, x, w1, w3, w2)


TOP_K = 2
MOE_TM = 512


def _router_kernel(hn_ref, wr_ref, idx_ref, wts_ref):
    logits = jnp.dot(hn_ref[...], wr_ref[...], preferred_element_type=F32,
                     precision=lax.Precision.HIGHEST)
    col = lax.broadcasted_iota(jnp.int32, logits.shape, 1)
    m1 = jnp.max(logits, axis=1, keepdims=True)
    i1 = jnp.min(jnp.where(logits == m1, col, N_EXPERTS), axis=1, keepdims=True)
    rest = jnp.where(col == i1, -jnp.inf, logits)
    m2 = jnp.max(rest, axis=1, keepdims=True)
    i2 = jnp.min(jnp.where(rest == m2, col, N_EXPERTS), axis=1, keepdims=True)
    e2 = jnp.exp(m2 - m1)
    w1 = 1.0 / (1.0 + e2)
    idx_ref[...] = jnp.concatenate([i1, i2], axis=1)
    wts_ref[...] = jnp.concatenate([w1, e2 * w1], axis=1)


def _router(hn, wr):
    n = hn.shape[0]
    tm = min(1024, n)
    return pl.pallas_call(
        _router_kernel,
        grid=(n // tm,),
        in_specs=[pl.BlockSpec((tm, D_MODEL), lambda i: (i, 0)),
                  pl.BlockSpec((D_MODEL, N_EXPERTS), lambda i: (0, 0))],
        out_specs=[pl.BlockSpec((tm, TOP_K), lambda i: (i, 0)), pl.BlockSpec((tm, TOP_K), lambda i: (i, 0))],
        out_shape=[jax.ShapeDtypeStruct((n, TOP_K), jnp.int32), jax.ShapeDtypeStruct((n, TOP_K), F32)],
        compiler_params=_cparams(("parallel",)),
        name="router",
    )(hn, wr)


def _route(idx, tm):
    n = idx.shape[0]
    e_flat = idx.reshape(-1)
    onehot = (e_flat[:, None] == jnp.arange(N_EXPERTS, dtype=jnp.int32)[None, :]).astype(jnp.int32)
    csum = jnp.cumsum(onehot, axis=0)
    rank = jnp.sum((csum - onehot) * onehot, axis=1)
    gsz = ((csum[-1] + tm - 1) // tm) * tm
    gend = jnp.cumsum(gsz)
    pos = (gend - gsz)[e_flat] + rank
    n_rows = n * TOP_K + N_EXPERTS * tm
    n_tiles = n_rows // tm
    row_token = jnp.zeros((n_rows,), jnp.int32).at[pos].set(jnp.arange(n * TOP_K, dtype=jnp.int32) // TOP_K)
    tile_start = jnp.arange(n_tiles, dtype=jnp.int32) * tm
    tile_expert = jnp.minimum(jnp.sum((tile_start[:, None] >= gend[None, :]).astype(jnp.int32), axis=1),
                              N_EXPERTS - 1)
    n_used = (gend[-1] // tm).astype(jnp.int32).reshape(1)
    return pos.reshape(n, TOP_K), row_token.reshape(n_tiles, 1, tm), tile_expert, n_used


def _row_copy(src_ref, src_row, dst_ref, dst_row, sem):
    return pltpu.make_async_copy(src_ref.at[pl.ds(src_row, 1)], dst_ref.at[pl.ds(dst_row, 1)], sem)


def _gather_rows_kernel(idx_ref, src_ref, dst_ref, sem, *, tm, nt):
    i = pl.program_id(0)

    def issue(r, carry):
        _row_copy(src_ref, idx_ref[0, 0, r], dst_ref, i * tm + r, sem).start()
        return carry

    def drain(r, carry):
        _row_copy(src_ref, 0, dst_ref, 0, sem).wait()
        return carry

    lax.fori_loop(0, tm, issue, 0)

    @pl.when(i > 0)
    def _():
        lax.fori_loop(0, tm, drain, 0)

    @pl.when(i == nt - 1)
    def _():
        lax.fori_loop(0, tm, drain, 0)


def _gather_rows(src, row_token):
    nt, _, tm = row_token.shape
    return pl.pallas_call(
        functools.partial(_gather_rows_kernel, tm=tm, nt=nt),
        grid=(nt,),
        in_specs=[pl.BlockSpec((1, 1, tm), lambda i: (i, 0, 0), memory_space=pltpu.SMEM),
                  pl.BlockSpec(memory_space=pl.ANY)],
        out_specs=pl.BlockSpec(memory_space=pl.ANY),
        out_shape=jax.ShapeDtypeStruct((nt * tm, src.shape[1]), src.dtype),
        scratch_shapes=[pltpu.SemaphoreType.DMA(())],
        compiler_params=_cparams(("arbitrary",)),
        name="moe_gather",
    )(row_token, src)


def _expert_kernel(te_ref, nu_ref, xs_ref, w1_ref, w3_ref, w2_ref, y_ref, xb_ref, acc_ref, *, nf):
    i = pl.program_id(0)
    j = pl.program_id(1)
    active = i < nu_ref[0]

    @pl.when(active)
    def _():
        @pl.when(j == 0)
        def _():
            xb_ref[...] = xs_ref[...].astype(BF16)

        xb = xb_ref[...]
        a = jnp.dot(xb, w1_ref[0], preferred_element_type=F32)
        g = jnp.dot(xb, w3_ref[0], preferred_element_type=F32)
        h = (a * jax.nn.sigmoid(a) * g).astype(BF16)
        y = jnp.dot(h, w2_ref[0], preferred_element_type=F32)

        @pl.when(j == 0)
        def _():
            acc_ref[...] = y

        @pl.when(j > 0)
        def _():
            acc_ref[...] += y

        @pl.when(j == nf - 1)
        def _():
            y_ref[...] = acc_ref[...]

    @pl.when(jnp.logical_not(active) & (j == nf - 1))
    def _():
        y_ref[...] = jnp.zeros(y_ref.shape, y_ref.dtype)


def _experts(xs, tile_expert, n_used, w1, w3, w2):
    n_rows = xs.shape[0]
    d_ff = w1.shape[2]
    tm = MOE_TM
    tf = d_ff // 4
    nf = d_ff // tf

    def jcol(i, j, nu):
        return jnp.where(i < nu[0], j, nf - 1)

    grid_spec = pltpu.PrefetchScalarGridSpec(
        num_scalar_prefetch=2,
        grid=(n_rows // tm, nf),
        in_specs=[pl.BlockSpec((tm, D_MODEL), lambda i, j, te, nu: (i, 0)),
                  pl.BlockSpec((1, D_MODEL, tf), lambda i, j, te, nu: (te[i], 0, jcol(i, j, nu))),
                  pl.BlockSpec((1, D_MODEL, tf), lambda i, j, te, nu: (te[i], 0, jcol(i, j, nu))),
                  pl.BlockSpec((1, tf, D_MODEL), lambda i, j, te, nu: (te[i], jcol(i, j, nu), 0))],
        out_specs=pl.BlockSpec((tm, D_MODEL), lambda i, j, te, nu: (i, 0)),
        scratch_shapes=[pltpu.VMEM((tm, D_MODEL), BF16), pltpu.VMEM((tm, D_MODEL), F32)],
    )
    return pl.pallas_call(
        functools.partial(_expert_kernel, nf=nf),
        grid_spec=grid_spec,
        out_shape=jax.ShapeDtypeStruct((n_rows, D_MODEL), F32),
        compiler_params=_cparams(("arbitrary", "arbitrary")),
        name="moe_experts",
    )(tile_expert, n_used, xs, w1, w3, w2)


def _combine_kernel(pos_ref, wts_ref, x_ref, y_ref, o_ref, g0_ref, g1_ref, sem, *, tm):
    def issue(r, carry):
        _row_copy(y_ref, pos_ref[0, 0, 2 * r], g0_ref, r, sem).start()
        _row_copy(y_ref, pos_ref[0, 0, 2 * r + 1], g1_ref, r, sem).start()
        return carry

    def drain(r, carry):
        _row_copy(y_ref, 0, g0_ref, 0, sem).wait()
        return carry

    lax.fori_loop(0, tm, issue, 0)
    lax.fori_loop(0, 2 * tm, drain, 0)
    w = wts_ref[...]
    o_ref[...] = x_ref[...] + w[:, 0:1] * g0_ref[...] + w[:, 1:2] * g1_ref[...]


def _combine(pos, wts, x, y):
    n = x.shape[0]
    tm = min(256, n)
    pos3 = pos.reshape(n // tm, 1, TOP_K * tm)
    return pl.pallas_call(
        functools.partial(_combine_kernel, tm=tm),
        grid=(n // tm,),
        in_specs=[pl.BlockSpec((1, 1, TOP_K * tm), lambda i: (i, 0, 0), memory_space=pltpu.SMEM),
                  pl.BlockSpec((tm, TOP_K), lambda i: (i, 0)),
                  pl.BlockSpec((tm, D_MODEL), lambda i: (i, 0)),
                  pl.BlockSpec(memory_space=pl.ANY)],
        out_specs=pl.BlockSpec((tm, D_MODEL), lambda i: (i, 0)),
        out_shape=jax.ShapeDtypeStruct((n, D_MODEL), F32),
        scratch_shapes=[pltpu.VMEM((tm, D_MODEL), F32), pltpu.VMEM((tm, D_MODEL), F32),
                        pltpu.SemaphoreType.DMA(())],
        compiler_params=_cparams(("arbitrary",)),
        name="moe_combine",
    )(pos3, wts, x, y)


def _moe(hn, x, wr, w1, w3, w2):
    idx, wts = _router(hn, wr)
    pos, row_token, tile_expert, n_used = _route(idx, MOE_TM)
    xs = _gather_rows(hn, row_token)
    y = _experts(xs, tile_expert, n_used, w1, w3, w2)
    return _combine(pos, wts, x, y)


def _permute_w_in(w):
    cols = [w[:, _ORIG[name][0]:_ORIG[name][0] + _ORIG[name][1]] for name in _NEW_ORDER]
    cols.append(jnp.zeros((w.shape[0], PROJ_W - _USED_W), w.dtype))
    return jnp.concatenate(cols, axis=1).astype(BF16)


def _angles(pos, dim):
    inv = ROPE_THETA ** (-jnp.arange(0, dim, 2, dtype=F32) / dim)
    return pos.astype(F32)[:, None] * inv[None, :]


def _rope_tables(seqs):
    pos = jnp.concatenate([jnp.tile(jnp.arange(s), b) for b, s in seqs])
    ang = _angles(pos, HEAD_DIM)
    cosb = jnp.tile(jnp.cos(ang), (1, 4))
    sinb = jnp.tile(jnp.concatenate([-jnp.sin(ang), jnp.sin(ang)], axis=1), (1, 2))
    ar = _angles(pos // GRID_W, HEAD_DIM // 2)
    ac = _angles(pos % GRID_W, HEAD_DIM // 2)
    cosa = jnp.tile(jnp.concatenate([jnp.cos(ar), jnp.cos(ar), jnp.cos(ac), jnp.cos(ac)], axis=1), (1, 2))
    sina = jnp.tile(jnp.concatenate([-jnp.sin(ar), jnp.sin(ar), -jnp.sin(ac), jnp.sin(ac)], axis=1), (1, 2))
    return cosb, sinb, cosa, sina


def _head_mean_matrix():
    blk = np.kron(np.eye(LANES // HEAD_DIM), np.ones((HEAD_DIM, HEAD_DIM))) / HEAD_DIM
    return jnp.asarray(blk, BF16)


def _gain_rows(a_qn, a_kn, b_qn, b_kn):
    scale = HEAD_DIM ** -0.5
    scale_a = scale * np.log2(np.e)
    rows = [jnp.tile(b_qn, 2) * scale, jnp.tile(b_kn, 2), jnp.tile(a_qn, 2) * scale_a, jnp.tile(a_kn, 2),
            jnp.ones((LANES,), F32), jnp.full((LANES,), C_KEY_DIM ** -0.5, F32),
            jnp.ones((LANES,), F32), jnp.ones((LANES,), F32)]
    return jnp.stack(rows).astype(F32)


def _trunk(x, seqs, norm1_g, w_in, a_qn, a_kn, b_qn, b_kn, ret_dec_f, ret_dec_b, ret_norm_g,
           w_oa, w_ob, w_oc, w_out, norm2_g, ffn_w1, ffn_w3, ffn_w2,
           moe_router, moe_w1, moe_w3, moe_w2):
    depth = w_in.shape[0]
    tables = _rope_tables(seqs)
    mmat = _head_mean_matrix()
    for l in range(depth):
        proj = _inproj(x, norm1_g[l][None, :], _permute_w_in(w_in[l]))
        bq, bk, bv, aq, ak, cq, ck = _prep(proj, tables, _gain_rows(a_qn[l], a_kn[l], b_qn[l], b_kn[l]), mmat)
        aqt = aq.T
        avt = proj[:, OFF["av"]:OFF["av"] + A_KV].T
        tabs_f = _ret_tables(ret_dec_f[l], C_CHUNK, False)
        tabs_b = _ret_tables(ret_dec_b[l], C_CHUNK, True)
        ng = ret_norm_g[l][None, :].astype(F32)
        o_a, o_c = [], []
        o_b = [[] for _ in range(B_GROUPS)]
        lse_b = [[] for _ in range(B_GROUPS)]
        row_off = 0
        for b, s in seqs:
            o_a.append(_mixer_a(aqt, ak, avt, row_off, b, s))
            for g, (_, dil) in enumerate(B_PATTERNS):
                og, lg = _mixer_b_group(bq, bk, bv, g, dil, row_off, b, s)
                o_b[g].append(og)
                lse_b[g].append(lg)
            o_c.append(_mixer_c(cq, ck, proj, tabs_f, tabs_b, ng, row_off, b, s))
            row_off += b * s
        cat = lambda parts: jnp.concatenate(parts, axis=0)
        x, hn = _merge(jnp.concatenate(o_a, axis=1),[cat(p) for p in o_b], [cat(p) for p in lse_b], cat(o_c), proj, x,
                       w_oa[l].astype(BF16), w_ob[l].astype(BF16), w_oc[l].astype(BF16),
                       w_out[l].astype(BF16), norm2_g[l][None, :], BF16 if l % 2 == 0 else F32)
        i = l // 2
        if l % 2 == 0:
            x = _ffn(hn, x, ffn_w1[i].astype(BF16), ffn_w3[i].astype(BF16), ffn_w2[i].astype(BF16))
        else:
            x = _moe(hn, x, moe_router[i], moe_w1[i].astype(BF16), moe_w3[i].astype(BF16),
                     moe_w2[i].astype(BF16))
    return x


def kernel(x_prompt, x_sample, norm1_g, w_in, a_qn, a_kn, b_qn, b_kn, ret_dec_f, ret_dec_b, ret_norm_g,
           w_oa, w_ob, w_oc, w_out, norm2_g, ffn_w1, ffn_w3, ffn_w2, moe_router, moe_w1, moe_w3, moe_w2):
    seqs = (x_prompt.shape[:2], x_sample.shape[:2])
    x = jnp.concatenate([x_prompt.reshape(-1, D_MODEL), x_sample.reshape(-1, D_MODEL)], axis=0)
    y = _trunk(x, seqs, norm1_g, w_in, a_qn, a_kn, b_qn, b_kn, ret_dec_f, ret_dec_b, ret_norm_g,
               w_oa, w_ob, w_oc, w_out, norm2_g, ffn_w1, ffn_w3, ffn_w2,
               moe_router, moe_w1, moe_w3, moe_w2)
    n_p = x_prompt.shape[0] * x_prompt.shape[1]
    return (y[:n_p].reshape(x_prompt.shape), y[n_p:].reshape(x_sample.shape))
```

```python
import functools

import numpy as np
import jax
import jax.numpy as jnp
from jax import lax
from jax.experimental import pallas as pl
from jax.experimental.pallas import tpu as pltpu

F32 = jnp.float32
BF16 = jnp.bfloat16

D_MODEL = 1024
GRID_W = 64
HEAD_DIM = 64
ROPE_THETA = 10000.0
EPS = 1e-6
A_HEADS = 8
A_KV_HEADS = 2
B_PATTERNS = ((128, 1), (512, 4), (2048, 16))
B_GROUPS = 3
B_HEADS = 8
C_HEADS = 4
C_KEY_DIM = 64
C_VAL_DIM = 128
N_EXPERTS = 8
N_BRANCHES = 3

A_Q = A_HEADS * HEAD_DIM
A_KV = A_KV_HEADS * HEAD_DIM
B_Q = B_GROUPS * B_HEADS * HEAD_DIM
B_KV = B_HEADS * HEAD_DIM
C_QK = C_HEADS * C_KEY_DIM
C_V = C_HEADS * C_VAL_DIM

LANES = 128
VMEM_LIMIT = 56 * 1024 * 1024

_ORIG_SPLITS = (("aq", A_Q), ("ak", A_KV), ("av", A_KV), ("bq", B_Q), ("bk", B_KV), ("bv", B_KV),
                ("cq", C_QK), ("ck", C_QK), ("cv", C_V), ("cg", C_V), ("gl", N_BRANCHES * D_MODEL))
_NEW_ORDER = ("bq", "bk", "aq", "bv", "cv", "cg", "gl", "cq", "ck", "ak", "av")
PROJ_W = 8192


def _layout():
    orig, o = {}, 0
    for name, w in _ORIG_SPLITS:
        orig[name] = (o, w)
        o += w
    new, o = {}, 0
    for name in _NEW_ORDER:
        new[name] = o
        o += orig[name][1]
    return orig, new, o


_ORIG, OFF, _USED_W = _layout()


def _cparams(sem):
    return pltpu.CompilerParams(dimension_semantics=sem, vmem_limit_bytes=VMEM_LIMIT)


def _inproj_kernel(x_ref, g_ref, w_ref, o_ref, xn_ref):
    @pl.when(pl.program_id(1) == 0)
    def _():
        x = x_ref[...]
        ms = jnp.mean(x * x, axis=-1, keepdims=True)
        xn_ref[...] = (x * lax.rsqrt(ms + EPS) * g_ref[...]).astype(BF16)

    o_ref[...] = jnp.dot(xn_ref[...], w_ref[...], preferred_element_type=F32).astype(BF16)


def _inproj(x, g, w):
    n = x.shape[0]
    tm = min(1024, n)
    tn = 1024
    return pl.pallas_call(
        _inproj_kernel,
        grid=(n // tm, PROJ_W // tn),
        in_specs=[pl.BlockSpec((tm, D_MODEL), lambda i, j: (i, 0)),
                  pl.BlockSpec((1, D_MODEL), lambda i, j: (0, 0)),
                  pl.BlockSpec((D_MODEL, tn), lambda i, j: (0, j))],
        out_specs=pl.BlockSpec((tm, tn), lambda i, j: (i, j)),
        out_shape=jax.ShapeDtypeStruct((n, PROJ_W), BF16),
        scratch_shapes=[pltpu.VMEM((tm, D_MODEL), BF16)],
        compiler_params=_cparams(("parallel", "arbitrary")),
        name="inproj",
    )(x, g, w)


def _norm_rot(x, gain, cos, sin, mmat, half, do_norm):
    if do_norm:
        ms = jnp.dot((x * x).astype(BF16), mmat, preferred_element_type=F32)
        x = x * lax.rsqrt(ms + EPS)
    x = x * gain
    lane = lax.broadcasted_iota(jnp.int32, x.shape, 1)
    first = (lane % (2 * half)) < half
    swapped = jnp.where(first, pltpu.roll(x, LANES - half, 1), pltpu.roll(x, half, 1))
    return x * cos + swapped * sin


def _prep_kernel(bqk_ref, aq_ref, bv_ref, cq_ref, ck_ref, ak_ref,
                 cosb_ref, sinb_ref, cosa_ref, sina_ref, gain_ref, mmat_ref,
                 obq_ref, obk_ref, obv_ref, oaq_ref, oak_ref, ocq_ref, ock_ref):
    cosb, sinb = cosb_ref[...], sinb_ref[...]
    cosa, sina = cosa_ref[...], sina_ref[...]
    mmat = mmat_ref[...]
    gains = gain_ref[...]

    def run(src_ref, dst_ref, src_off, width, gain_row, cos, sin, half, do_norm):
        for c in range(width // LANES):
            x = src_ref[:, src_off + c * LANES:src_off + (c + 1) * LANES].astype(F32)
            y = _norm_rot(x, gains[gain_row:gain_row + 1, :], cos, sin, mmat, half, do_norm)
            dst_ref[:, c * LANES:(c + 1) * LANES] = y.astype(dst_ref.dtype)

    run(bqk_ref, obq_ref, 0, B_Q, 0, cosb, sinb, HEAD_DIM // 2, True)
    run(bqk_ref, obk_ref, B_Q, B_KV, 1, cosb, sinb, HEAD_DIM // 2, True)
    run(aq_ref, oaq_ref, 0, A_Q, 2, cosa, sina, HEAD_DIM // 4, True)
    run(ak_ref, oak_ref, 0, A_KV, 3, cosa, sina, HEAD_DIM // 4, True)
    run(cq_ref, ocq_ref, 0, C_QK, 4, cosb, sinb, C_KEY_DIM // 2, False)
    run(ck_ref, ock_ref, 0, C_QK, 5, cosb, sinb, C_KEY_DIM // 2, False)
    obv_ref[...] = bv_ref[...]


def _prep(proj, tables, gains, mmat):
    n = proj.shape[0]
    tm = min(512, n)
    cosb, sinb, cosa, sina = tables

    def col(width, name):
        idx = OFF[name] // width
        return pl.BlockSpec((tm, width), lambda i: (i, idx))

    def tab():
        return pl.BlockSpec((tm, LANES), lambda i: (i, 0))

    def out(width):
        return pl.BlockSpec((tm, width), lambda i: (i, 0))

    widths = (B_Q, B_KV, B_KV, A_Q, A_KV, C_QK, C_QK)
    return pl.pallas_call(
        _prep_kernel,
        grid=(n // tm,),
        in_specs=[col(B_Q + B_KV, "bq"), col(A_Q, "aq"), col(B_KV, "bv"), col(C_QK, "cq"),
                  col(C_QK, "ck"), col(A_KV, "ak"), tab(), tab(), tab(), tab(),
                  pl.BlockSpec((8, LANES), lambda i: (0, 0)),
                  pl.BlockSpec((LANES, LANES), lambda i: (0, 0))],
        out_specs=[out(w) for w in widths],
        out_shape=[jax.ShapeDtypeStruct((n, w), BF16) for w in widths],
        compiler_params=_cparams(("parallel",)),
        name="prep",
    )(proj, proj, proj, proj, proj, proj, cosb, sinb, cosa, sina, gains, mmat)


A_ONES_ROWS = 16


def _attn_a_kernel(qt_ref, k_ref, vt_ref, o_ref, qp_ref, m_ref, acc_ref, *, bq, nk):
    ik = pl.program_id(2)
    rep = A_HEADS // A_KV_HEADS

    @pl.when(ik == 0)
    def _():
        qp_ref[...] = jnp.zeros(qp_ref.shape, BF16)
        for g in range(A_KV_HEADS):
            for r in range(rep):
                h = g * rep + r
                qp_ref[g, g * HEAD_DIM:(g + 1) * HEAD_DIM, r * bq:(r + 1) * bq] = (
                    qt_ref[h * HEAD_DIM:(h + 1) * HEAD_DIM, :])
        m_ref[...] = jnp.full(m_ref.shape, -jnp.inf, F32)
        acc_ref[...] = jnp.zeros(acc_ref.shape, F32)

    k = k_ref[...]
    kv_heads = range(A_KV_HEADS)
    s = [jnp.dot(k, qp_ref[g], preferred_element_type=F32) for g in kv_heads]
    m_prev = [m_ref[g] for g in kv_heads]
    m_new = [jnp.maximum(m_prev[g], jnp.max(s[g], axis=0, keepdims=True)) for g in kv_heads]
    p = [jnp.exp2(s[g] - m_new[g]).astype(BF16) for g in kv_heads]
    ones = jnp.ones((A_ONES_ROWS, k.shape[0]), BF16)
    for g in kv_heads:
        alpha = jnp.exp2(m_prev[g] - m_new[g])
        v_aug = jnp.concatenate([vt_ref[g * HEAD_DIM:(g + 1) * HEAD_DIM, :], ones], axis=0)
        pv = jnp.dot(v_aug, p[g], preferred_element_type=F32)
        acc_ref[g] = alpha * acc_ref[g] + pv
        m_ref[g] = m_new[g]

    @pl.when(ik == nk - 1)
    def _():
        for g in range(A_KV_HEADS):
            acc = acc_ref[g]
            o = acc[:HEAD_DIM] / acc[HEAD_DIM:HEAD_DIM + 1]
            for r in range(rep):
                h = g * rep + r
                o_ref[h * HEAD_DIM:(h + 1) * HEAD_DIM, :] = o[:, r * bq:(r + 1) * bq].astype(o_ref.dtype)


def _mixer_a(aqt, ak, avt, row_off, b, s):
    bq = min(256, s)
    bk = min(512, s)
    nq, nk = s // bq, s // bk
    rep = A_HEADS // A_KV_HEADS
    q0, k0 = row_off // bq, row_off // bk
    kern = functools.partial(_attn_a_kernel, bq=bq, nk=nk)
    return pl.pallas_call(
        kern,
        grid=(b, nq, nk),
        in_specs=[pl.BlockSpec((A_Q, bq), lambda ib, iq, ik: (0, q0 + ib * nq + iq)),
                  pl.BlockSpec((bk, A_KV), lambda ib, iq, ik: (k0 + ib * nk + ik, 0)),
                  pl.BlockSpec((A_KV, bk), lambda ib, iq, ik: (0, k0 + ib * nk + ik))],
        out_specs=pl.BlockSpec((A_Q, bq), lambda ib, iq, ik: (0, ib * nq + iq)),
        out_shape=jax.ShapeDtypeStruct((A_Q, b * s), BF16),
        scratch_shapes=[pltpu.VMEM((A_KV_HEADS, A_KV, rep * bq), BF16),
                        pltpu.VMEM((A_KV_HEADS, 1, rep * bq), F32),
                        pltpu.VMEM((A_KV_HEADS, HEAD_DIM + A_ONES_ROWS, rep * bq), F32)],
        compiler_params=_cparams(("parallel", "parallel", "arbitrary")),
        name="mixer_a",
    )(aqt, ak, avt)


B_BLOCK_Q = 128
B_HALO = 64


def _attn_b_kernel(q_ref, k0_ref, k1_ref, k2_ref, k3_ref, v0_ref, v1_ref, v2_ref, v3_ref,
                   o_ref, lse_ref, *, u_len):
    i = pl.program_id(2)
    kcat = jnp.concatenate([k0_ref[...], k1_ref[...], k2_ref[...], k3_ref[...]], axis=0)
    vcat = jnp.concatenate([v0_ref[...], v1_ref[...], v2_ref[...], v3_ref[...]], axis=0)
    nkeys = B_BLOCK_Q + 2 * B_HALO
    a = lax.broadcasted_iota(jnp.int32, (B_BLOCK_Q, nkeys), 0)
    c = lax.broadcasted_iota(jnp.int32, (B_BLOCK_Q, nkeys), 1)
    key_pos = i * B_BLOCK_Q - B_HALO + c
    valid = (c >= a) & (c <= a + 2 * B_HALO) & (key_pos >= 0) & (key_pos < u_len)
    for h in range(B_HEADS):
        sl = slice(h * HEAD_DIM, (h + 1) * HEAD_DIM)
        s = lax.dot_general(q_ref[:, sl], kcat[:, sl], (((1,), (1,)), ((), ())),
                            preferred_element_type=F32)
        s = jnp.where(valid, s, -1e30)
        m = jnp.max(s, axis=1, keepdims=True)
        e = jnp.exp(s - m)
        den = jnp.sum(e, axis=1, keepdims=True)
        o = jnp.dot(e.astype(BF16), vcat[:, sl], preferred_element_type=F32) / den
        o_ref[:, sl] = o.astype(o_ref.dtype)
        lse_ref[:, sl] = jnp.broadcast_to(m + jnp.log(den), (B_BLOCK_Q, HEAD_DIM))


def _mixer_b_group(bq, bk, bv, g, dil, row_off, b, s):
    n = bq.shape[0]
    u_len = s // dil
    nq = u_len // B_BLOCK_Q
    nkb = u_len // B_HALO
    q_rows0 = row_off // dil // B_BLOCK_Q
    k_rows0 = row_off // dil // B_HALO
    qv = bq.reshape(n // dil, dil * B_Q)
    kv = bk.reshape(n // dil, dil * B_KV)
    vv = bv.reshape(n // dil, dil * B_KV)

    def kspec(t):
        def imap(ib, r, i):
            blk = jnp.clip(2 * i - 1 + t, 0, nkb - 1)
            return (k_rows0 + ib * nkb + blk, r)
        return pl.BlockSpec((B_HALO, B_KV), imap)

    def ospec():
        return pl.BlockSpec((B_BLOCK_Q, B_KV), lambda ib, r, i: (ib * nq + i, r))

    kern = functools.partial(_attn_b_kernel, u_len=u_len)
    o, lse = pl.pallas_call(
        kern,
        grid=(b, dil, nq),
        in_specs=[pl.BlockSpec((B_BLOCK_Q, B_KV), lambda ib, r, i: (q_rows0 + ib * nq + i, r * B_GROUPS + g))]
                 + [kspec(t) for t in range(4)] + [kspec(t) for t in range(4)],
        out_specs=[ospec(), ospec()],
        out_shape=[jax.ShapeDtypeStruct((b * s // dil, dil * B_KV), BF16),
                   jax.ShapeDtypeStruct((b * s // dil, dil * B_KV), F32)],
        compiler_params=_cparams(("parallel", "parallel", "parallel")),
        name=f"mixer_b{g}",
    )(qv, kv, kv, kv, kv, vv, vv, vv, vv)
    return o.reshape(b * s, B_KV), lse.reshape(b * s, B_KV)


def _ret_chunk(q, k, v, dec_ref, xi_ref, zeta_ref, cdec_ref, r_ref):
    outs = []
    for h in range(C_HEADS):
        qh = q[:, h * C_KEY_DIM:(h + 1) * C_KEY_DIM]
        kh = k[:, h * C_KEY_DIM:(h + 1) * C_KEY_DIM]
        vh = v[:, h * C_VAL_DIM:(h + 1) * C_VAL_DIM]
        att = lax.dot_general(qh, kh, (((1,), (1,)), ((), ())), preferred_element_type=F32) * dec_ref[h]
        inner = jnp.dot(att.astype(BF16), vh, preferred_element_type=F32)
        r = r_ref[h]
        cross = jnp.dot(qh, r.astype(BF16), preferred_element_type=F32) * xi_ref[h]
        kz = (kh.astype(F32) * zeta_ref[h]).astype(BF16)
        r_ref[h] = r * cdec_ref[h] + lax.dot_general(kz, vh, (((0,), (0,)), ((), ())),
                                                     preferred_element_type=F32)
        outs.append(inner + cross)
    return jnp.concatenate(outs, axis=1)


def _ret_fwd_kernel(q_ref, k_ref, v_ref, dec_ref, xi_ref, zeta_ref, cdec_ref, o_ref, r_ref):
    @pl.when(pl.program_id(1) == 0)
    def _():
        r_ref[...] = jnp.zeros(r_ref.shape, F32)

    o_ref[...] = _ret_chunk(q_ref[...], k_ref[...], v_ref[...], dec_ref, xi_ref, zeta_ref, cdec_ref, r_ref)


def _ret_bwd_kernel(q_ref, k_ref, v_ref, dec_ref, xi_ref, zeta_ref, cdec_ref, of_ref, gate_ref, ng_ref,
                    o_ref, r_ref):
    @pl.when(pl.program_id(1) == 0)
    def _():
        r_ref[...] = jnp.zeros(r_ref.shape, F32)

    o = of_ref[...] + _ret_chunk(q_ref[...], k_ref[...], v_ref[...], dec_ref, xi_ref, zeta_ref, cdec_ref, r_ref)
    ng = ng_ref[...]
    gate = gate_ref[...].astype(F32)
    for h in range(C_HEADS):
        sl = slice(h * C_VAL_DIM, (h + 1) * C_VAL_DIM)
        oh = o[:, sl]
        mu = jnp.mean(oh, axis=1, keepdims=True)
        var = jnp.mean(jnp.square(oh - mu), axis=1, keepdims=True)
        y = (oh - mu) * lax.rsqrt(var + EPS) * ng[:, sl]
        gh = gate[:, sl]
        o_ref[:, sl] = (gh * jax.nn.sigmoid(gh) * y).astype(o_ref.dtype)


def _ret_tables(dec_param, chunk, strict):
    log_g = -jnp.exp(dec_param.astype(F32))
    j = jnp.arange(chunk, dtype=F32)
    lg = log_g[:, None, None]
    if strict:
        diff = j[None, :] - j[:, None]
        dec = jnp.where((diff > 0)[None], jnp.exp(jnp.maximum(diff, 0.0)[None] * lg), 0.0)
        xi = jnp.exp((chunk - j)[None, :, None] * lg)
        zeta = jnp.exp(j[None, :, None] * lg)
    else:
        diff = j[:, None] - j[None, :]
        dec = jnp.where((diff >= 0)[None], jnp.exp(jnp.maximum(diff, 0.0)[None] * lg), 0.0)
        xi = jnp.exp((j + 1.0)[None, :, None] * lg)
        zeta = jnp.exp((chunk - 1.0 - j)[None, :, None] * lg)
    xi = jnp.broadcast_to(xi, (C_HEADS, chunk, C_VAL_DIM))
    zeta = jnp.broadcast_to(zeta, (C_HEADS, chunk, C_KEY_DIM))
    cdec = jnp.broadcast_to(jnp.exp(chunk * log_g)[:, None, None], (C_HEADS, C_KEY_DIM, C_VAL_DIM))
    return dec, xi, zeta, cdec


C_CHUNK = 128


def _mixer_c(cq, ck, proj, tabs_f, tabs_b, norm_g, row_off, b, s):
    nc = s // C_CHUNK
    r0 = row_off // C_CHUNK
    v_col = OFF["cv"] // C_V
    g_col = OFF["cg"] // C_V

    def fwd_rows(ib, c):
        return r0 + ib * nc + c

    def bwd_rows(ib, c):
        return r0 + ib * nc + (nc - 1 - c)

    def specs(rows):
        def full(shape):
            return pl.BlockSpec(shape, lambda ib, c: (0,) * len(shape))
        return [pl.BlockSpec((C_CHUNK, C_QK), lambda ib, c: (rows(ib, c), 0)),
                pl.BlockSpec((C_CHUNK, C_QK), lambda ib, c: (rows(ib, c), 0)),
                pl.BlockSpec((C_CHUNK, C_V), lambda ib, c: (rows(ib, c), v_col)),
                full((C_HEADS, C_CHUNK, C_CHUNK)), full((C_HEADS, C_CHUNK, C_VAL_DIM)),
                full((C_HEADS, C_CHUNK, C_KEY_DIM)), full((C_HEADS, C_KEY_DIM, C_VAL_DIM))]

    scratch = [pltpu.VMEM((C_HEADS, C_KEY_DIM, C_VAL_DIM), F32)]
    o_f = pl.pallas_call(
        _ret_fwd_kernel,
        grid=(b, nc),
        in_specs=specs(fwd_rows),
        out_specs=pl.BlockSpec((C_CHUNK, C_V), lambda ib, c: (ib * nc + c, 0)),
        out_shape=jax.ShapeDtypeStruct((b * s, C_V), F32),
        scratch_shapes=scratch,
        compiler_params=_cparams(("parallel", "arbitrary")),
        name="ret_fwd",
    )(cq, ck, proj, *tabs_f)
    return pl.pallas_call(
        _ret_bwd_kernel,
        grid=(b, nc),
        in_specs=specs(bwd_rows) + [
            pl.BlockSpec((C_CHUNK, C_V), lambda ib, c: (ib * nc + (nc - 1 - c), 0)),
            pl.BlockSpec((C_CHUNK, C_V), lambda ib, c: (bwd_rows(ib, c), g_col)),
            pl.BlockSpec((1, C_V), lambda ib, c: (0, 0))],
        out_specs=pl.BlockSpec((C_CHUNK, C_V), lambda ib, c: (ib * nc + (nc - 1 - c), 0)),
        out_shape=jax.ShapeDtypeStruct((b * s, C_V), BF16),
        scratch_shapes=scratch,
        compiler_params=_cparams(("parallel", "arbitrary")),
        name="ret_bwd",
    )(cq, ck, proj, *tabs_b, o_f, proj, norm_g)


def _merge_kernel(oa_ref, ob0_ref, ob1_ref, ob2_ref, l0_ref, l1_ref, l2_ref, oc_ref,
                  ga_ref, gb_ref, gc_ref, x_ref, woa_ref, wob_ref, woc_ref, wout_ref, n2_ref,
                  xo_ref, hn_ref):
    l0, l1, l2 = l0_ref[...], l1_ref[...], l2_ref[...]
    mx = jnp.maximum(jnp.maximum(l0, l1), l2)
    e0, e1, e2 = jnp.exp(l0 - mx), jnp.exp(l1 - mx), jnp.exp(l2 - mx)
    ob = (e0 * ob0_ref[...].astype(F32) + e1 * ob1_ref[...].astype(F32)
          + e2 * ob2_ref[...].astype(F32)) / (e0 + e1 + e2)

    def branch(o, w_ref, gate_ref):
        y = jnp.dot(o, w_ref[...], preferred_element_type=F32)
        return jax.nn.sigmoid(gate_ref[...].astype(F32)) * y

    oa = jnp.transpose(oa_ref[...].astype(F32)).astype(BF16)
    merged = (branch(oa, woa_ref, ga_ref) + branch(ob.astype(BF16), wob_ref, gb_ref)
              + branch(oc_ref[...], woc_ref, gc_ref))
    x = x_ref[...] + jnp.dot(merged.astype(BF16), wout_ref[...], preferred_element_type=F32)
    xo_ref[...] = x
    ms = jnp.mean(x * x, axis=-1, keepdims=True)
    hn_ref[...] = (x * lax.rsqrt(ms + EPS) * n2_ref[...]).astype(hn_ref.dtype)


def _merge(o_a, o_b, lse_b, o_c, proj, x, w_oa, w_ob, w_oc, w_out, n2, hn_dtype):
    n = x.shape[0]
    tm = min(256, n)
    gl0 = OFF["gl"] // D_MODEL

    def rows(width):
        return pl.BlockSpec((tm, width), lambda i: (i, 0))

    def gate(k):
        return pl.BlockSpec((tm, D_MODEL), lambda i: (i, gl0 + k))

    def full(r, c):
        return pl.BlockSpec((r, c), lambda i: (0, 0))

    return pl.pallas_call(
        _merge_kernel,
        grid=(n // tm,),
        in_specs=[pl.BlockSpec((A_Q, tm), lambda i: (0, i))] + [rows(B_KV)] * 6 + [rows(C_V), gate(0), gate(1), gate(2), rows(D_MODEL),
                  full(A_Q, D_MODEL), full(B_KV, D_MODEL), full(C_V, D_MODEL), full(D_MODEL, D_MODEL),
                  full(1, D_MODEL)],
        out_specs=[rows(D_MODEL), rows(D_MODEL)],
        out_shape=[jax.ShapeDtypeStruct((n, D_MODEL), F32), jax.ShapeDtypeStruct((n, D_MODEL), hn_dtype)],
        compiler_params=_cparams(("parallel",)),
        name="merge_out",
    )(o_a, *o_b, *lse_b, o_c, proj, proj, proj, x, w_oa, w_ob, w_oc, w_out, n2)


def _ffn_kernel(hn_ref, x_ref, w1_ref, w3_ref, w2_ref, o_ref, acc_ref, *, nf):
    j = pl.program_id(1)
    hn = hn_ref[...]
    a = jnp.dot(hn, w1_ref[...], preferred_element_type=F32)
    g = jnp.dot(hn, w3_ref[...], preferred_element_type=F32)
    h = (a * jax.nn.sigmoid(a) * g).astype(BF16)
    y = jnp.dot(h, w2_ref[...], preferred_element_type=F32)

    @pl.when(j == 0)
    def _():
        acc_ref[...] = x_ref[...] + y

    @pl.when(j > 0)
    def _():
        acc_ref[...] += y

    @pl.when(j == nf - 1)
    def _():
        o_ref[...] = acc_ref[...]


def _ffn(hn, x, w1, w3, w2):
    n = x.shape[0]
    d_ff = w1.shape[1]
    tm = min(512, n)
    tf = d_ff // 2
    nf = d_ff // tf
    return pl.pallas_call(
        functools.partial(_ffn_kernel, nf=nf),
        grid=(n // tm, nf),
        in_specs=[pl.BlockSpec((tm, D_MODEL), lambda i, j: (i, 0)),
                  pl.BlockSpec((tm, D_MODEL), lambda i, j: (i, 0)),
                  pl.BlockSpec((D_MODEL, tf), lambda i, j: (0, j)),
                  pl.BlockSpec((D_MODEL, tf), lambda i, j: (0, j)),
                  pl.BlockSpec((tf, D_MODEL), lambda i, j: (j, 0))],
        out_specs=pl.BlockSpec((tm, D_MODEL), lambda i, j: (i, 0)),
        out_shape=jax.ShapeDtypeStruct((n, D_MODEL), F32),
        scratch_shapes=[pltpu.VMEM((tm, D_MODEL), F32)],
        compiler_params=_cparams(("parallel", "arbitrary")),
        name="ffn",
    )(hn, x, w1, w3, w2)


TOP_K = 2
MOE_TM = 512


def _router_kernel(hn_ref, wr_ref, idx_ref, wts_ref):
    logits = jnp.dot(hn_ref[...], wr_ref[...], preferred_element_type=F32,
                     precision=lax.Precision.HIGHEST)
    col = lax.broadcasted_iota(jnp.int32, logits.shape, 1)
    m1 = jnp.max(logits, axis=1, keepdims=True)
    i1 = jnp.min(jnp.where(logits == m1, col, N_EXPERTS), axis=1, keepdims=True)
    rest = jnp.where(col == i1, -jnp.inf, logits)
    m2 = jnp.max(rest, axis=1, keepdims=True)
    i2 = jnp.min(jnp.where(rest == m2, col, N_EXPERTS), axis=1, keepdims=True)
    e2 = jnp.exp(m2 - m1)
    w1 = 1.0 / (1.0 + e2)
    idx_ref[...] = jnp.concatenate([i1, i2], axis=1)
    wts_ref[...] = jnp.concatenate([w1, e2 * w1], axis=1)


def _router(hn, wr):
    n = hn.shape[0]
    tm = min(1024, n)
    return pl.pallas_call(
        _router_kernel,
        grid=(n // tm,),
        in_specs=[pl.BlockSpec((tm, D_MODEL), lambda i: (i, 0)),
                  pl.BlockSpec((D_MODEL, N_EXPERTS), lambda i: (0, 0))],
        out_specs=[pl.BlockSpec((tm, TOP_K), lambda i: (i, 0)), pl.BlockSpec((tm, TOP_K), lambda i: (i, 0))],
        out_shape=[jax.ShapeDtypeStruct((n, TOP_K), jnp.int32), jax.ShapeDtypeStruct((n, TOP_K), F32)],
        compiler_params=_cparams(("parallel",)),
        name="router",
    )(hn, wr)


def _route(idx, tm):
    n = idx.shape[0]
    e_flat = idx.reshape(-1)
    onehot = (e_flat[:, None] == jnp.arange(N_EXPERTS, dtype=jnp.int32)[None, :]).astype(jnp.int32)
    csum = jnp.cumsum(onehot, axis=0)
    rank = jnp.sum((csum - onehot) * onehot, axis=1)
    gsz = ((csum[-1] + tm - 1) // tm) * tm
    gend = jnp.cumsum(gsz)
    pos = (gend - gsz)[e_flat] + rank
    n_rows = n * TOP_K + N_EXPERTS * tm
    n_tiles = n_rows // tm
    row_token = jnp.zeros((n_rows,), jnp.int32).at[pos].set(jnp.arange(n * TOP_K, dtype=jnp.int32) // TOP_K)
    tile_start = jnp.arange(n_tiles, dtype=jnp.int32) * tm
    tile_expert = jnp.minimum(jnp.sum((tile_start[:, None] >= gend[None, :]).astype(jnp.int32), axis=1),
                              N_EXPERTS - 1)
    n_used = (gend[-1] // tm).astype(jnp.int32).reshape(1)
    return pos.reshape(n, TOP_K), row_token.reshape(n_tiles, 1, tm), tile_expert, n_used


def _row_copy(src_ref, src_row, dst_ref, dst_row, sem):
    return pltpu.make_async_copy(src_ref.at[pl.ds(src_row, 1)], dst_ref.at[pl.ds(dst_row, 1)], sem)


def _expert_kernel(te_ref, nu_ref, idx_ref, nxt_ref, src_ref, w1_ref, w3_ref, w2_ref, y_ref,
                   xs_ref, xb_ref, acc_ref, sem, *, nf, tm):
    i = pl.program_id(0)
    j = pl.program_id(1)
    n_used = nu_ref[0]
    active = i < n_used
    slot = i % 2
    share = tm // nf

    @pl.when((i == 0) & (j == 0))
    def _():
        def issue(r, carry):
            _row_copy(src_ref, idx_ref[0, 0, r], xs_ref.at[0], r, sem.at[0]).start()
            return carry
        lax.fori_loop(0, tm, issue, 0)

    @pl.when(i + 1 < n_used)
    def _():
        for r in range(share):
            row = j * share + r
            _row_copy(src_ref, nxt_ref[0, 0, row], xs_ref.at[1 - slot], row, sem.at[1 - slot]).start()

    @pl.when(active)
    def _():
        @pl.when(j == 0)
        def _():
            def drain(r, carry):
                _row_copy(src_ref, 0, xs_ref.at[slot], 0, sem.at[slot]).wait()
                return carry
            lax.fori_loop(0, tm, drain, 0)
            xb_ref[...] = xs_ref[slot].astype(BF16)

        xb = xb_ref[...]
        a = jnp.dot(xb, w1_ref[0], preferred_element_type=F32)
        g = jnp.dot(xb, w3_ref[0], preferred_element_type=F32)
        h = (a * jax.nn.sigmoid(a) * g).astype(BF16)
        y = jnp.dot(h, w2_ref[0], preferred_element_type=F32)

        @pl.when(j == 0)
        def _():
            acc_ref[...] = y

        @pl.when(j > 0)
        def _():
            acc_ref[...] += y

        @pl.when(j == nf - 1)
        def _():
            y_ref[...] = acc_ref[...]

    @pl.when(jnp.logical_not(active) & (j == nf - 1))
    def _():
        y_ref[...] = jnp.zeros(y_ref.shape, y_ref.dtype)


def _experts(src, row_token, tile_expert, n_used, w1, w3, w2):
    nt, _, tm = row_token.shape
    d_ff = w1.shape[2]
    nf = 4
    tf = d_ff // nf

    def jcol(i, j, nu):
        return jnp.where(i < nu[0], j, nf - 1)

    grid_spec = pltpu.PrefetchScalarGridSpec(
        num_scalar_prefetch=2,
        grid=(nt, nf),
        in_specs=[pl.BlockSpec((1, 1, tm), lambda i, j, te, nu: (i, 0, 0), memory_space=pltpu.SMEM),
                  pl.BlockSpec((1, 1, tm), lambda i, j, te, nu: (jnp.minimum(i + 1, nt - 1), 0, 0),
                               memory_space=pltpu.SMEM),
                  pl.BlockSpec(memory_space=pl.ANY),
                  pl.BlockSpec((1, D_MODEL, tf), lambda i, j, te, nu: (te[i], 0, jcol(i, j, nu))),
                  pl.BlockSpec((1, D_MODEL, tf), lambda i, j, te, nu: (te[i], 0, jcol(i, j, nu))),
                  pl.BlockSpec((1, tf, D_MODEL), lambda i, j, te, nu: (te[i], jcol(i, j, nu), 0))],
        out_specs=pl.BlockSpec((tm, D_MODEL), lambda i, j, te, nu: (i, 0)),
        scratch_shapes=[pltpu.VMEM((2, tm, D_MODEL), src.dtype), pltpu.VMEM((tm, D_MODEL), BF16),
                        pltpu.VMEM((tm, D_MODEL), F32), pltpu.SemaphoreType.DMA((2,))],
    )
    return pl.pallas_call(
        functools.partial(_expert_kernel, nf=nf, tm=tm),
        grid_spec=grid_spec,
        out_shape=jax.ShapeDtypeStruct((nt * tm, D_MODEL), F32),
        compiler_params=_cparams(("arbitrary", "arbitrary")),
        name="moe_experts",
    )(tile_expert, n_used, row_token, row_token, src, w1, w3, w2)


def _combine_kernel(pos_ref, nxt_ref, wts_ref, x_ref, y_ref, o_ref, g_ref, sem, *, tm, nt):
    i = pl.program_id(0)
    slot = i % 2

    def request(p_ref, dst_slot):
        def issue(r, carry):
            for k in range(TOP_K):
                _row_copy(y_ref, p_ref[0, 0, TOP_K * r + k], g_ref.at[dst_slot, k], r, sem.at[dst_slot]).start()
            return carry
        lax.fori_loop(0, tm, issue, 0)

    @pl.when(i == 0)
    def _():
        request(pos_ref, 0)

    @pl.when(i + 1 < nt)
    def _():
        request(nxt_ref, 1 - slot)

    def drain(r, carry):
        _row_copy(y_ref, 0, g_ref.at[slot, 0], 0, sem.at[slot]).wait()
        return carry

    lax.fori_loop(0, TOP_K * tm, drain, 0)
    w = wts_ref[...]
    o_ref[...] = x_ref[...] + w[:, 0:1] * g_ref[slot, 0] + w[:, 1:2] * g_ref[slot, 1]


def _combine(pos, wts, x, y):
    n = x.shape[0]
    tm = min(256, n)
    nt = n // tm
    pos3 = pos.reshape(nt, 1, TOP_K * tm)
    return pl.pallas_call(
        functools.partial(_combine_kernel, tm=tm, nt=nt),
        grid=(nt,),
        in_specs=[pl.BlockSpec((1, 1, TOP_K * tm), lambda i: (i, 0, 0), memory_space=pltpu.SMEM),
                  pl.BlockSpec((1, 1, TOP_K * tm), lambda i: (jnp.minimum(i + 1, nt - 1), 0, 0),
                               memory_space=pltpu.SMEM),
                  pl.BlockSpec((tm, TOP_K), lambda i: (i, 0)),
                  pl.BlockSpec((tm, D_MODEL), lambda i: (i, 0)),
                  pl.BlockSpec(memory_space=pl.ANY)],
        out_specs=pl.BlockSpec((tm, D_MODEL), lambda i: (i, 0)),
        out_shape=jax.ShapeDtypeStruct((n, D_MODEL), F32),
        scratch_shapes=[pltpu.VMEM((2, TOP_K, tm, D_MODEL), F32), pltpu.SemaphoreType.DMA((2,))],
        compiler_params=_cparams(("arbitrary",)),
        name="moe_combine",
    )(pos3, pos3, wts, x, y)


def _moe(hn, x, wr, w1, w3, w2):
    idx, wts = _router(hn, wr)
    pos, row_token, tile_expert, n_used = _route(idx, MOE_TM)
    y = _experts(hn, row_token, tile_expert, n_used, w1, w3, w2)
    return _combine(pos, wts, x, y)


def _permute_w_in(w):
    cols = [w[:, _ORIG[name][0]:_ORIG[name][0] + _ORIG[name][1]] for name in _NEW_ORDER]
    cols.append(jnp.zeros((w.shape[0], PROJ_W - _USED_W), w.dtype))
    return jnp.concatenate(cols, axis=1).astype(BF16)


def _angles(pos, dim):
    inv = ROPE_THETA ** (-jnp.arange(0, dim, 2, dtype=F32) / dim)
    return pos.astype(F32)[:, None] * inv[None, :]


def _rope_tables(seqs):
    pos = jnp.concatenate([jnp.tile(jnp.arange(s), b) for b, s in seqs])
    ang = _angles(pos, HEAD_DIM)
    cosb = jnp.tile(jnp.cos(ang), (1, 4))
    sinb = jnp.tile(jnp.concatenate([-jnp.sin(ang), jnp.sin(ang)], axis=1), (1, 2))
    ar = _angles(pos // GRID_W, HEAD_DIM // 2)
    ac = _angles(pos % GRID_W, HEAD_DIM // 2)
    cosa = jnp.tile(jnp.concatenate([jnp.cos(ar), jnp.cos(ar), jnp.cos(ac), jnp.cos(ac)], axis=1), (1, 2))
    sina = jnp.tile(jnp.concatenate([-jnp.sin(ar), jnp.sin(ar), -jnp.sin(ac), jnp.sin(ac)], axis=1), (1, 2))
    return cosb, sinb, cosa, sina


def _head_mean_matrix():
    blk = np.kron(np.eye(LANES // HEAD_DIM), np.ones((HEAD_DIM, HEAD_DIM))) / HEAD_DIM
    return jnp.asarray(blk, BF16)


def _gain_rows(a_qn, a_kn, b_qn, b_kn):
    scale = HEAD_DIM ** -0.5
    scale_a = scale * np.log2(np.e)
    rows = [jnp.tile(b_qn, 2) * scale, jnp.tile(b_kn, 2), jnp.tile(a_qn, 2) * scale_a, jnp.tile(a_kn, 2),
            jnp.ones((LANES,), F32), jnp.full((LANES,), C_KEY_DIM ** -0.5, F32),
            jnp.ones((LANES,), F32), jnp.ones((LANES,), F32)]
    return jnp.stack(rows).astype(F32)


def _trunk(x, seqs, norm1_g, w_in, a_qn, a_kn, b_qn, b_kn, ret_dec_f, ret_dec_b, ret_norm_g,
           w_oa, w_ob, w_oc, w_out, norm2_g, ffn_w1, ffn_w3, ffn_w2,
           moe_router, moe_w1, moe_w3, moe_w2):
    depth = w_in.shape[0]
    tables = _rope_tables(seqs)
    mmat = _head_mean_matrix()
    for l in range(depth):
        proj = _inproj(x, norm1_g[l][None, :], _permute_w_in(w_in[l]))
        bq, bk, bv, aq, ak, cq, ck = _prep(proj, tables, _gain_rows(a_qn[l], a_kn[l], b_qn[l], b_kn[l]), mmat)
        aqt = aq.T
        avt = proj[:, OFF["av"]:OFF["av"] + A_KV].T
        tabs_f = _ret_tables(ret_dec_f[l], C_CHUNK, False)
        tabs_b = _ret_tables(ret_dec_b[l], C_CHUNK, True)
        ng = ret_norm_g[l][None, :].astype(F32)
        o_a, o_c = [], []
        o_b = [[] for _ in range(B_GROUPS)]
        lse_b = [[] for _ in range(B_GROUPS)]
        row_off = 0
        for b, s in seqs:
            o_a.append(_mixer_a(aqt, ak, avt, row_off, b, s))
            for g, (_, dil) in enumerate(B_PATTERNS):
                og, lg = _mixer_b_group(bq, bk, bv, g, dil, row_off, b, s)
                o_b[g].append(og)
                lse_b[g].append(lg)
            o_c.append(_mixer_c(cq, ck, proj, tabs_f, tabs_b, ng, row_off, b, s))
            row_off += b * s
        cat = lambda parts: jnp.concatenate(parts, axis=0)
        x, hn = _merge(jnp.concatenate(o_a, axis=1),[cat(p) for p in o_b], [cat(p) for p in lse_b], cat(o_c), proj, x,
                       w_oa[l].astype(BF16), w_ob[l].astype(BF16), w_oc[l].astype(BF16),
                       w_out[l].astype(BF16), norm2_g[l][None, :], BF16 if l % 2 == 0 else F32)
        i = l // 2
        if l % 2 == 0:
            x = _ffn(hn, x, ffn_w1[i].astype(BF16), ffn_w3[i].astype(BF16), ffn_w2[i].astype(BF16))
        else:
            x = _moe(hn, x, moe_router[i], moe_w1[i].astype(BF16), moe_w3[i].astype(BF16),
                     moe_w2[i].astype(BF16))
    return x


def kernel(x_prompt, x_sample, norm1_g, w_in, a_qn, a_kn, b_qn, b_kn, ret_dec_f, ret_dec_b, ret_norm_g,
           w_oa, w_ob, w_oc, w_out, norm2_g, ffn_w1, ffn_w3, ffn_w2, moe_router, moe_w1, moe_w3, moe_w2):
    seqs = (x_prompt.shape[:2], x_sample.shape[:2])
    x = jnp.concatenate([x_prompt.reshape(-1, D_MODEL), x_sample.reshape(-1, D_MODEL)], axis=0)
    y = _trunk(x, seqs, norm1_g, w_in, a_qn, a_kn, b_qn, b_kn, ret_dec_f, ret_dec_b, ret_norm_g,
               w_oa, w_ob, w_oc, w_out, norm2_g, ffn_w1, ffn_w3, ffn_w2,
               moe_router, moe_w1, moe_w3, moe_w2)
    n_p = x_prompt.shape[0] * x_prompt.shape[1]
    return (y[:n_p].reshape(x_prompt.shape), y[n_p:].reshape(x_sample.shape))
```

```python
import functools

import numpy as np
import jax
import jax.numpy as jnp
from jax import lax
from jax.experimental import pallas as pl
from jax.experimental.pallas import tpu as pltpu

F32 = jnp.float32
BF16 = jnp.bfloat16

D_MODEL = 1024
GRID_W = 64
HEAD_DIM = 64
ROPE_THETA = 10000.0
EPS = 1e-6
A_HEADS = 8
A_KV_HEADS = 2
B_PATTERNS = ((128, 1), (512, 4), (2048, 16))
B_GROUPS = 3
B_HEADS = 8
C_HEADS = 4
C_KEY_DIM = 64
C_VAL_DIM = 128
N_EXPERTS = 8
N_BRANCHES = 3

A_Q = A_HEADS * HEAD_DIM
A_KV = A_KV_HEADS * HEAD_DIM
B_Q = B_GROUPS * B_HEADS * HEAD_DIM
B_KV = B_HEADS * HEAD_DIM
C_QK = C_HEADS * C_KEY_DIM
C_V = C_HEADS * C_VAL_DIM

LANES = 128
VMEM_LIMIT = 56 * 1024 * 1024

_ORIG_SPLITS = (("aq", A_Q), ("ak", A_KV), ("av", A_KV), ("bq", B_Q), ("bk", B_KV), ("bv", B_KV),
                ("cq", C_QK), ("ck", C_QK), ("cv", C_V), ("cg", C_V), ("gl", N_BRANCHES * D_MODEL))
_NEW_ORDER = ("bq", "bk", "aq", "bv", "cv", "cg", "gl", "cq", "ck", "ak", "av")
PROJ_W = 8192


def _layout():
    orig, o = {}, 0
    for name, w in _ORIG_SPLITS:
        orig[name] = (o, w)
        o += w
    new, o = {}, 0
    for name in _NEW_ORDER:
        new[name] = o
        o += orig[name][1]
    return orig, new, o


_ORIG, OFF, _USED_W = _layout()


def _cparams(sem):
    return pltpu.CompilerParams(dimension_semantics=sem, vmem_limit_bytes=VMEM_LIMIT)


def _seq_call(kern, prev, *, in_specs, out_specs, out_shape, args, **kwargs):
    n_in = len(in_specs)
    if prev is None:
        return pl.pallas_call(kern, in_specs=in_specs, out_specs=out_specs, out_shape=out_shape,
                              **kwargs)(*args)
    n_prev = len(prev)

    def chained(*refs):
        return kern(*refs[:n_in], *refs[n_in + n_prev:])

    return pl.pallas_call(
        chained,
        in_specs=list(in_specs) + [pl.BlockSpec(memory_space=pl.ANY)] * n_prev,
        out_specs=out_specs, out_shape=out_shape,
        input_output_aliases={n_in + k: k for k in range(n_prev)},
        **kwargs)(*args, *prev)


def _inproj_kernel(x_ref, g_ref, w_ref, o_ref, xn_ref):
    @pl.when(pl.program_id(1) == 0)
    def _():
        x = x_ref[...]
        ms = jnp.mean(x * x, axis=-1, keepdims=True)
        xn_ref[...] = (x * lax.rsqrt(ms + EPS) * g_ref[...]).astype(BF16)

    o_ref[...] = jnp.dot(xn_ref[...], w_ref[...], preferred_element_type=F32).astype(BF16)


def _inproj(x, g, w):
    n = x.shape[0]
    tm = min(1024, n)
    tn = 1024
    return pl.pallas_call(
        _inproj_kernel,
        grid=(n // tm, PROJ_W // tn),
        in_specs=[pl.BlockSpec((tm, D_MODEL), lambda i, j: (i, 0)),
                  pl.BlockSpec((1, D_MODEL), lambda i, j: (0, 0)),
                  pl.BlockSpec((D_MODEL, tn), lambda i, j: (0, j))],
        out_specs=pl.BlockSpec((tm, tn), lambda i, j: (i, j)),
        out_shape=jax.ShapeDtypeStruct((n, PROJ_W), BF16),
        scratch_shapes=[pltpu.VMEM((tm, D_MODEL), BF16)],
        compiler_params=_cparams(("parallel", "arbitrary")),
        name="inproj",
    )(x, g, w)


def _norm_rot(x, gain, cos, sin, mmat, half, do_norm):
    if do_norm:
        ms = jnp.dot((x * x).astype(BF16), mmat, preferred_element_type=F32)
        x = x * lax.rsqrt(ms + EPS)
    x = x * gain
    lane = lax.broadcasted_iota(jnp.int32, x.shape, 1)
    first = (lane % (2 * half)) < half
    swapped = jnp.where(first, pltpu.roll(x, LANES - half, 1), pltpu.roll(x, half, 1))
    return x * cos + swapped * sin


def _prep_kernel(bqk_ref, aq_ref, bv_ref, cq_ref, ck_ref, ak_ref,
                 cosb_ref, sinb_ref, cosa_ref, sina_ref, gain_ref, mmat_ref,
                 q0_ref, q1_ref, q2_ref, k0_ref, k1_ref, k2_ref, v0_ref, v1_ref, v2_ref,
                 oaq_ref, oak_ref, ocq_ref, ock_ref, scr_ref):
    cosb, sinb = cosb_ref[...], sinb_ref[...]
    cosa, sina = cosa_ref[...], sina_ref[...]
    mmat = mmat_ref[...]
    gains = gain_ref[...]
    tm = scr_ref.shape[0]

    def prepared(src_ref, src_off, c, gain_row, cos, sin, half, do_norm):
        x = src_ref[:, src_off + c * LANES:src_off + (c + 1) * LANES].astype(F32)
        return _norm_rot(x, gains[gain_row:gain_row + 1, :], cos, sin, mmat, half, do_norm)

    def run(src_ref, dst_ref, width, gain_row, cos, sin, half, do_norm):
        for c in range(width // LANES):
            y = prepared(src_ref, 0, c, gain_row, cos, sin, half, do_norm)
            dst_ref[:, c * LANES:(c + 1) * LANES] = y.astype(dst_ref.dtype)

    def store_dilated(y, dst_ref, dil, c):
        if dil == 1:
            dst_ref[:, c * LANES:(c + 1) * LANES] = y.astype(dst_ref.dtype)
            return
        scr_ref[...] = y
        for r in range(dil):
            rows = scr_ref[pl.ds(r, tm // dil, stride=dil), :]
            dst_ref[:, r * B_KV + c * LANES:r * B_KV + (c + 1) * LANES] = rows.astype(dst_ref.dtype)

    chunks = B_KV // LANES
    for g, (q_ref, (_, dil)) in enumerate(zip((q0_ref, q1_ref, q2_ref), B_PATTERNS)):
        for c in range(chunks):
            y = prepared(bqk_ref, g * B_KV, c, 0, cosb, sinb, HEAD_DIM // 2, True)
            store_dilated(y, q_ref, dil, c)
    for c in range(chunks):
        yk = prepared(bqk_ref, B_Q, c, 1, cosb, sinb, HEAD_DIM // 2, True)
        yv = bv_ref[:, c * LANES:(c + 1) * LANES].astype(F32)
        for (_, dil), k_ref, v_ref in zip(B_PATTERNS, (k0_ref, k1_ref, k2_ref), (v0_ref, v1_ref, v2_ref)):
            store_dilated(yk, k_ref, dil, c)
            store_dilated(yv, v_ref, dil, c)
    run(aq_ref, oaq_ref, A_Q, 2, cosa, sina, HEAD_DIM // 4, True)
    run(ak_ref, oak_ref, A_KV, 3, cosa, sina, HEAD_DIM // 4, True)
    run(cq_ref, ocq_ref, C_QK, 4, cosb, sinb, C_KEY_DIM // 2, False)
    run(ck_ref, ock_ref, C_QK, 5, cosb, sinb, C_KEY_DIM // 2, False)


def _prep(proj, tables, gains, mmat):
    n = proj.shape[0]
    tm = min(512, n)
    cosb, sinb, cosa, sina = tables

    def col(width, name):
        idx = OFF[name] // width
        return pl.BlockSpec((tm, width), lambda i: (i, idx))

    def tab():
        return pl.BlockSpec((tm, LANES), lambda i: (i, 0))

    dils = [dil for _, dil in B_PATTERNS]
    shapes = [(n // dil, dil * B_KV) for dil in dils] * 3 + [(n, A_Q), (n, A_KV), (n, C_QK), (n, C_QK)]
    blocks = [(tm // dil, dil * B_KV) for dil in dils] * 3 + [(tm, A_Q), (tm, A_KV), (tm, C_QK), (tm, C_QK)]
    return pl.pallas_call(
        _prep_kernel,
        grid=(n // tm,),
        in_specs=[col(B_Q + B_KV, "bq"), col(A_Q, "aq"), col(B_KV, "bv"), col(C_QK, "cq"),
                  col(C_QK, "ck"), col(A_KV, "ak"), tab(), tab(), tab(), tab(),
                  pl.BlockSpec((8, LANES), lambda i: (0, 0)),
                  pl.BlockSpec((LANES, LANES), lambda i: (0, 0))],
        out_specs=[pl.BlockSpec(blk, lambda i: (i, 0)) for blk in blocks],
        out_shape=[jax.ShapeDtypeStruct(shp, BF16) for shp in shapes],
        scratch_shapes=[pltpu.VMEM((tm, LANES), F32)],
        compiler_params=_cparams(("parallel",)),
        name="prep",
    )(proj, proj, proj, proj, proj, proj, cosb, sinb, cosa, sina, gains, mmat)


A_ONES_ROWS = 16


def _attn_a_kernel(qt_ref, k_ref, vt_ref, o_ref, qp_ref, m_ref, acc_ref, *, bq, nk):
    ik = pl.program_id(2)
    rep = A_HEADS // A_KV_HEADS

    @pl.when(ik == 0)
    def _():
        qp_ref[...] = jnp.zeros(qp_ref.shape, BF16)
        for g in range(A_KV_HEADS):
            for r in range(rep):
                h = g * rep + r
                qp_ref[g, g * HEAD_DIM:(g + 1) * HEAD_DIM, r * bq:(r + 1) * bq] = (
                    qt_ref[h * HEAD_DIM:(h + 1) * HEAD_DIM, :])
        m_ref[...] = jnp.full(m_ref.shape, -jnp.inf, F32)
        acc_ref[...] = jnp.zeros(acc_ref.shape, F32)

    k = k_ref[...]
    kv_heads = range(A_KV_HEADS)
    s = [jnp.dot(k, qp_ref[g], preferred_element_type=F32) for g in kv_heads]
    m_prev = [m_ref[g] for g in kv_heads]
    m_new = [jnp.maximum(m_prev[g], jnp.max(s[g], axis=0, keepdims=True)) for g in kv_heads]
    p = [jnp.exp2(s[g] - m_new[g]).astype(BF16) for g in kv_heads]
    ones = jnp.ones((A_ONES_ROWS, k.shape[0]), BF16)
    for g in kv_heads:
        alpha = jnp.exp2(m_prev[g] - m_new[g])
        v_aug = jnp.concatenate([vt_ref[g * HEAD_DIM:(g + 1) * HEAD_DIM, :], ones], axis=0)
        pv = jnp.dot(v_aug, p[g], preferred_element_type=F32)
        acc_ref[g] = alpha * acc_ref[g] + pv
        m_ref[g] = m_new[g]

    @pl.when(ik == nk - 1)
    def _():
        for g in range(A_KV_HEADS):
            acc = acc_ref[g]
            o = acc[:HEAD_DIM] / acc[HEAD_DIM:HEAD_DIM + 1]
            for r in range(rep):
                h = g * rep + r
                o_ref[h * HEAD_DIM:(h + 1) * HEAD_DIM, :] = o[:, r * bq:(r + 1) * bq].astype(o_ref.dtype)


def _mixer_a(aqt, ak, avt, prev, row_off, b, s):
    n = ak.shape[0]
    bq = min(256, s)
    bk = min(512, s)
    nq, nk = s // bq, s // bk
    rep = A_HEADS // A_KV_HEADS
    q0, k0 = row_off // bq, row_off // bk
    kern = functools.partial(_attn_a_kernel, bq=bq, nk=nk)
    return _seq_call(
        kern, prev,
        grid=(b, nq, nk),
        in_specs=[pl.BlockSpec((A_Q, bq), lambda ib, iq, ik: (0, q0 + ib * nq + iq)),
                  pl.BlockSpec((bk, A_KV), lambda ib, iq, ik: (k0 + ib * nk + ik, 0)),
                  pl.BlockSpec((A_KV, bk), lambda ib, iq, ik: (0, k0 + ib * nk + ik))],
        out_specs=[pl.BlockSpec((A_Q, bq), lambda ib, iq, ik: (0, q0 + ib * nq + iq))],
        out_shape=[jax.ShapeDtypeStruct((A_Q, n), BF16)],
        scratch_shapes=[pltpu.VMEM((A_KV_HEADS, A_KV, rep * bq), BF16),
                        pltpu.VMEM((A_KV_HEADS, 1, rep * bq), F32),
                        pltpu.VMEM((A_KV_HEADS, HEAD_DIM + A_ONES_ROWS, rep * bq), F32)],
        compiler_params=_cparams(("parallel", "parallel", "arbitrary")),
        name="mixer_a",
        args=(aqt, ak, avt))


B_BLOCK_Q = 128
B_HALO = 64


def _attn_b_kernel(q_ref, k0_ref, k1_ref, k2_ref, k3_ref, v0_ref, v1_ref, v2_ref, v3_ref,
                   o_ref, lse_ref, *, u_len):
    i = pl.program_id(2)
    kcat = jnp.concatenate([k0_ref[...], k1_ref[...], k2_ref[...], k3_ref[...]], axis=0)
    vcat = jnp.concatenate([v0_ref[...], v1_ref[...], v2_ref[...], v3_ref[...]], axis=0)
    nkeys = B_BLOCK_Q + 2 * B_HALO
    a = lax.broadcasted_iota(jnp.int32, (B_BLOCK_Q, nkeys), 0)
    c = lax.broadcasted_iota(jnp.int32, (B_BLOCK_Q, nkeys), 1)
    key_pos = i * B_BLOCK_Q - B_HALO + c
    valid = (c >= a) & (c <= a + 2 * B_HALO) & (key_pos >= 0) & (key_pos < u_len)
    for h in range(B_HEADS):
        sl = slice(h * HEAD_DIM, (h + 1) * HEAD_DIM)
        s = lax.dot_general(q_ref[:, sl], kcat[:, sl], (((1,), (1,)), ((), ())),
                            preferred_element_type=F32)
        s = jnp.where(valid, s, -1e30)
        m = jnp.max(s, axis=1, keepdims=True)
        e = jnp.exp(s - m)
        den = jnp.sum(e, axis=1, keepdims=True)
        o = jnp.dot(e.astype(BF16), vcat[:, sl], preferred_element_type=F32) / den
        o_ref[:, sl] = o.astype(o_ref.dtype)
        lse_ref[:, sl] = jnp.broadcast_to(m + jnp.log(den), (B_BLOCK_Q, HEAD_DIM))


def _mixer_b_group(qd, kd, vd, prev, g, dil, row_off, b, s):
    u_len = s // dil
    nq = u_len // B_BLOCK_Q
    nkb = u_len // B_HALO
    q_rows0 = row_off // dil // B_BLOCK_Q
    k_rows0 = row_off // dil // B_HALO

    def kspec(t):
        def imap(ib, r, i):
            blk = jnp.clip(2 * i - 1 + t, 0, nkb - 1)
            return (k_rows0 + ib * nkb + blk, r)
        return pl.BlockSpec((B_HALO, B_KV), imap)

    def qspec():
        return pl.BlockSpec((B_BLOCK_Q, B_KV), lambda ib, r, i: (q_rows0 + ib * nq + i, r))

    kern = functools.partial(_attn_b_kernel, u_len=u_len)
    return _seq_call(
        kern, prev,
        grid=(b, dil, nq),
        in_specs=[qspec()] + [kspec(t) for t in range(4)] + [kspec(t) for t in range(4)],
        out_specs=[qspec(), qspec()],
        out_shape=[jax.ShapeDtypeStruct(qd.shape, BF16), jax.ShapeDtypeStruct(qd.shape, F32)],
        compiler_params=_cparams(("parallel", "parallel", "parallel")),
        name=f"mixer_b{g}",
        args=(qd, kd, kd, kd, kd, vd, vd, vd, vd))


def _ret_chunk(q, k, v, dec_ref, xi_ref, zeta_ref, cdec_ref, r_ref):
    outs = []
    for h in range(C_HEADS):
        qh = q[:, h * C_KEY_DIM:(h + 1) * C_KEY_DIM]
        kh = k[:, h * C_KEY_DIM:(h + 1) * C_KEY_DIM]
        vh = v[:, h * C_VAL_DIM:(h + 1) * C_VAL_DIM]
        att = lax.dot_general(qh, kh, (((1,), (1,)), ((), ())), preferred_element_type=F32) * dec_ref[h]
        inner = jnp.dot(att.astype(BF16), vh, preferred_element_type=F32)
        r = r_ref[h]
        cross = jnp.dot(qh, r.astype(BF16), preferred_element_type=F32) * xi_ref[h]
        kz = (kh.astype(F32) * zeta_ref[h]).astype(BF16)
        r_ref[h] = r * cdec_ref[h] + lax.dot_general(kz, vh, (((0,), (0,)), ((), ())),
                                                     preferred_element_type=F32)
        outs.append(inner + cross)
    return jnp.concatenate(outs, axis=1)


def _ret_fwd_kernel(q_ref, k_ref, v_ref, dec_ref, xi_ref, zeta_ref, cdec_ref, o_ref, r_ref):
    @pl.when(pl.program_id(1) == 0)
    def _():
        r_ref[...] = jnp.zeros(r_ref.shape, F32)

    o_ref[...] = _ret_chunk(q_ref[...], k_ref[...], v_ref[...], dec_ref, xi_ref, zeta_ref, cdec_ref, r_ref)


def _ret_bwd_kernel(q_ref, k_ref, v_ref, dec_ref, xi_ref, zeta_ref, cdec_ref, of_ref, gate_ref, ng_ref,
                    o_ref, r_ref):
    @pl.when(pl.program_id(1) == 0)
    def _():
        r_ref[...] = jnp.zeros(r_ref.shape, F32)

    o = of_ref[...] + _ret_chunk(q_ref[...], k_ref[...], v_ref[...], dec_ref, xi_ref, zeta_ref, cdec_ref, r_ref)
    ng = ng_ref[...]
    gate = gate_ref[...].astype(F32)
    for h in range(C_HEADS):
        sl = slice(h * C_VAL_DIM, (h + 1) * C_VAL_DIM)
        oh = o[:, sl]
        mu = jnp.mean(oh, axis=1, keepdims=True)
        var = jnp.mean(jnp.square(oh - mu), axis=1, keepdims=True)
        y = (oh - mu) * lax.rsqrt(var + EPS) * ng[:, sl]
        gh = gate[:, sl]
        o_ref[:, sl] = (gh * jax.nn.sigmoid(gh) * y).astype(o_ref.dtype)


def _ret_tables(dec_param, chunk, strict):
    log_g = -jnp.exp(dec_param.astype(F32))
    j = jnp.arange(chunk, dtype=F32)
    lg = log_g[:, None, None]
    if strict:
        diff = j[None, :] - j[:, None]
        dec = jnp.where((diff > 0)[None], jnp.exp(jnp.maximum(diff, 0.0)[None] * lg), 0.0)
        xi = jnp.exp((chunk - j)[None, :, None] * lg)
        zeta = jnp.exp(j[None, :, None] * lg)
    else:
        diff = j[:, None] - j[None, :]
        dec = jnp.where((diff >= 0)[None], jnp.exp(jnp.maximum(diff, 0.0)[None] * lg), 0.0)
        xi = jnp.exp((j + 1.0)[None, :, None] * lg)
        zeta = jnp.exp((chunk - 1.0 - j)[None, :, None] * lg)
    xi = jnp.broadcast_to(xi, (C_HEADS, chunk, C_VAL_DIM))
    zeta = jnp.broadcast_to(zeta, (C_HEADS, chunk, C_KEY_DIM))
    cdec = jnp.broadcast_to(jnp.exp(chunk * log_g)[:, None, None], (C_HEADS, C_KEY_DIM, C_VAL_DIM))
    return dec, xi, zeta, cdec


C_CHUNK = 128


def _mixer_c(cq, ck, proj, tabs_f, tabs_b, norm_g, prev, row_off, b, s):
    n = cq.shape[0]
    prev_f, prev_o = (None, None) if prev is None else ([prev[0]], [prev[1]])
    nc = s // C_CHUNK
    r0 = row_off // C_CHUNK
    v_col = OFF["cv"] // C_V
    g_col = OFF["cg"] // C_V

    def fwd_rows(ib, c):
        return r0 + ib * nc + c

    def bwd_rows(ib, c):
        return r0 + ib * nc + (nc - 1 - c)

    def specs(rows):
        def full(shape):
            return pl.BlockSpec(shape, lambda ib, c: (0,) * len(shape))
        return [pl.BlockSpec((C_CHUNK, C_QK), lambda ib, c: (rows(ib, c), 0)),
                pl.BlockSpec((C_CHUNK, C_QK), lambda ib, c: (rows(ib, c), 0)),
                pl.BlockSpec((C_CHUNK, C_V), lambda ib, c: (rows(ib, c), v_col)),
                full((C_HEADS, C_CHUNK, C_CHUNK)), full((C_HEADS, C_CHUNK, C_VAL_DIM)),
                full((C_HEADS, C_CHUNK, C_KEY_DIM)), full((C_HEADS, C_KEY_DIM, C_VAL_DIM))]

    scratch = [pltpu.VMEM((C_HEADS, C_KEY_DIM, C_VAL_DIM), F32)]
    o_f, = _seq_call(
        _ret_fwd_kernel, prev_f,
        grid=(b, nc),
        in_specs=specs(fwd_rows),
        out_specs=[pl.BlockSpec((C_CHUNK, C_V), lambda ib, c: (fwd_rows(ib, c), 0))],
        out_shape=[jax.ShapeDtypeStruct((n, C_V), F32)],
        scratch_shapes=scratch,
        compiler_params=_cparams(("parallel", "arbitrary")),
        name="ret_fwd",
        args=(cq, ck, proj, *tabs_f))
    o_c, = _seq_call(
        _ret_bwd_kernel, prev_o,
        grid=(b, nc),
        in_specs=specs(bwd_rows) + [
            pl.BlockSpec((C_CHUNK, C_V), lambda ib, c: (bwd_rows(ib, c), 0)),
            pl.BlockSpec((C_CHUNK, C_V), lambda ib, c: (bwd_rows(ib, c), g_col)),
            pl.BlockSpec((1, C_V), lambda ib, c: (0, 0))],
        out_specs=[pl.BlockSpec((C_CHUNK, C_V), lambda ib, c: (bwd_rows(ib, c), 0))],
        out_shape=[jax.ShapeDtypeStruct((n, C_V), BF16)],
        scratch_shapes=scratch,
        compiler_params=_cparams(("parallel", "arbitrary")),
        name="ret_bwd",
        args=(cq, ck, proj, *tabs_b, o_f, proj, norm_g))
    return [o_f, o_c]


def _merge_kernel(oa_ref, ob0_ref, ob1_ref, ob2_ref, l0_ref, l1_ref, l2_ref, oc_ref,
                  ga_ref, gb_ref, gc_ref, x_ref, woa_ref, wob_ref, woc_ref, wout_ref, n2_ref,
                  xo_ref, hn_ref, scr_ref):
    tm = x_ref.shape[0]

    def natural(src_ref, dil, slot):
        if dil == 1:
            return src_ref[...].astype(F32)
        chunks = B_KV // LANES
        for r in range(dil):
            for c in range(chunks):
                col = r * B_KV + c * LANES
                scr_ref[slot, c, pl.ds(r, tm // dil, stride=dil), :] = src_ref[:, col:col + LANES].astype(F32)
        return jnp.concatenate([scr_ref[slot, c] for c in range(chunks)], axis=1)

    dils = [dil for _, dil in B_PATTERNS]
    l0, l1, l2 = [natural(ref, dil, 2 * g) for g, (ref, dil) in enumerate(zip((l0_ref, l1_ref, l2_ref), dils))]
    o0, o1, o2 = [natural(ref, dil, 2 * g + 1)
                  for g, (ref, dil) in enumerate(zip((ob0_ref, ob1_ref, ob2_ref), dils))]
    mx = jnp.maximum(jnp.maximum(l0, l1), l2)
    e0, e1, e2 = jnp.exp(l0 - mx), jnp.exp(l1 - mx), jnp.exp(l2 - mx)
    ob = (e0 * o0 + e1 * o1 + e2 * o2) / (e0 + e1 + e2)

    def branch(o, w_ref, gate_ref):
        y = jnp.dot(o, w_ref[...], preferred_element_type=F32)
        return jax.nn.sigmoid(gate_ref[...].astype(F32)) * y

    oa = jnp.transpose(oa_ref[...].astype(F32)).astype(BF16)
    merged = (branch(oa, woa_ref, ga_ref) + branch(ob.astype(BF16), wob_ref, gb_ref)
              + branch(oc_ref[...], woc_ref, gc_ref))
    x = x_ref[...] + jnp.dot(merged.astype(BF16), wout_ref[...], preferred_element_type=F32)
    xo_ref[...] = x
    ms = jnp.mean(x * x, axis=-1, keepdims=True)
    hn_ref[...] = (x * lax.rsqrt(ms + EPS) * n2_ref[...]).astype(hn_ref.dtype)


def _merge(o_a, o_b, lse_b, o_c, proj, x, w_oa, w_ob, w_oc, w_out, n2, hn_dtype):
    n = x.shape[0]
    tm = min(256, n)
    gl0 = OFF["gl"] // D_MODEL

    def rows(width):
        return pl.BlockSpec((tm, width), lambda i: (i, 0))

    def gate(k):
        return pl.BlockSpec((tm, D_MODEL), lambda i: (i, gl0 + k))

    def full(r, c):
        return pl.BlockSpec((r, c), lambda i: (0, 0))

    grouped = [pl.BlockSpec((tm // dil, dil * B_KV), lambda i: (i, 0)) for _, dil in B_PATTERNS]
    return pl.pallas_call(
        _merge_kernel,
        grid=(n // tm,),
        in_specs=[pl.BlockSpec((A_Q, tm), lambda i: (0, i))] + grouped + grouped
                 + [rows(C_V), gate(0), gate(1), gate(2), rows(D_MODEL),
                    full(A_Q, D_MODEL), full(B_KV, D_MODEL), full(C_V, D_MODEL), full(D_MODEL, D_MODEL),
                    full(1, D_MODEL)],
        out_specs=[rows(D_MODEL), rows(D_MODEL)],
        out_shape=[jax.ShapeDtypeStruct((n, D_MODEL), F32), jax.ShapeDtypeStruct((n, D_MODEL), hn_dtype)],
        scratch_shapes=[pltpu.VMEM((2 * B_GROUPS, B_KV // LANES, tm, LANES), F32)],
        compiler_params=_cparams(("parallel",)),
        name="merge_out",
    )(o_a, *o_b, *lse_b, o_c, proj, proj, proj, x, w_oa, w_ob, w_oc, w_out, n2)


def _ffn_kernel(hn_ref, x_ref, w1_ref, w3_ref, w2_ref, o_ref, acc_ref, *, nf):
    j = pl.program_id(1)
    hn = hn_ref[...]
    a = jnp.dot(hn, w1_ref[...], preferred_element_type=F32)
    g = jnp.dot(hn, w3_ref[...], preferred_element_type=F32)
    h = (a * jax.nn.sigmoid(a) * g).astype(BF16)
    y = jnp.dot(h, w2_ref[...], preferred_element_type=F32)

    @pl.when(j == 0)
    def _():
        acc_ref[...] = x_ref[...] + y

    @pl.when(j > 0)
    def _():
        acc_ref[...] += y

    @pl.when(j == nf - 1)
    def _():
        o_ref[...] = acc_ref[...]


def _ffn(hn, x, w1, w3, w2):
    n = x.shape[0]
    d_ff = w1.shape[1]
    tm = min(512, n)
    tf = d_ff // 2
    nf = d_ff // tf
    return pl.pallas_call(
        functools.partial(_ffn_kernel, nf=nf),
        grid=(n // tm, nf),
        in_specs=[pl.BlockSpec((tm, D_MODEL), lambda i, j: (i, 0)),
                  pl.BlockSpec((tm, D_MODEL), lambda i, j: (i, 0)),
                  pl.BlockSpec((D_MODEL, tf), lambda i, j: (0, j)),
                  pl.BlockSpec((D_MODEL, tf), lambda i, j: (0, j)),
                  pl.BlockSpec((tf, D_MODEL), lambda i, j: (j, 0))],
        out_specs=pl.BlockSpec((tm, D_MODEL), lambda i, j: (i, 0)),
        out_shape=jax.ShapeDtypeStruct((n, D_MODEL), F32),
        scratch_shapes=[pltpu.VMEM((tm, D_MODEL), F32)],
        compiler_params=_cparams(("parallel", "arbitrary")),
        name="ffn",
    )(hn, x, w1, w3, w2)


TOP_K = 2
MOE_TM = 512


def _router_kernel(hn_ref, wr_ref, idx_ref, wts_ref):
    logits = jnp.dot(hn_ref[...], wr_ref[...], preferred_element_type=F32,
                     precision=lax.Precision.HIGHEST)
    col = lax.broadcasted_iota(jnp.int32, logits.shape, 1)
    m1 = jnp.max(logits, axis=1, keepdims=True)
    i1 = jnp.min(jnp.where(logits == m1, col, N_EXPERTS), axis=1, keepdims=True)
    rest = jnp.where(col == i1, -jnp.inf, logits)
    m2 = jnp.max(rest, axis=1, keepdims=True)
    i2 = jnp.min(jnp.where(rest == m2, col, N_EXPERTS), axis=1, keepdims=True)
    e2 = jnp.exp(m2 - m1)
    w1 = 1.0 / (1.0 + e2)
    idx_ref[...] = jnp.concatenate([i1, i2], axis=1)
    wts_ref[...] = jnp.concatenate([w1, e2 * w1], axis=1)


def _router(hn, wr):
    n = hn.shape[0]
    tm = min(1024, n)
    return pl.pallas_call(
        _router_kernel,
        grid=(n // tm,),
        in_specs=[pl.BlockSpec((tm, D_MODEL), lambda i: (i, 0)),
                  pl.BlockSpec((D_MODEL, N_EXPERTS), lambda i: (0, 0))],
        out_specs=[pl.BlockSpec((tm, TOP_K), lambda i: (i, 0)), pl.BlockSpec((tm, TOP_K), lambda i: (i, 0))],
        out_shape=[jax.ShapeDtypeStruct((n, TOP_K), jnp.int32), jax.ShapeDtypeStruct((n, TOP_K), F32)],
        compiler_params=_cparams(("parallel",)),
        name="router",
    )(hn, wr)


def _route(idx, tm):
    n = idx.shape[0]
    e_flat = idx.reshape(-1)
    onehot = (e_flat[:, None] == jnp.arange(N_EXPERTS, dtype=jnp.int32)[None, :]).astype(jnp.int32)
    csum = jnp.cumsum(onehot, axis=0)
    rank = jnp.sum((csum - onehot) * onehot, axis=1)
    gsz = ((csum[-1] + tm - 1) // tm) * tm
    gend = jnp.cumsum(gsz)
    pos = (gend - gsz)[e_flat] + rank
    n_rows = n * TOP_K + N_EXPERTS * tm
    n_tiles = n_rows // tm
    row_token = jnp.zeros((n_rows,), jnp.int32).at[pos].set(jnp.arange(n * TOP_K, dtype=jnp.int32) // TOP_K)
    tile_start = jnp.arange(n_tiles, dtype=jnp.int32) * tm
    tile_expert = jnp.minimum(jnp.sum((tile_start[:, None] >= gend[None, :]).astype(jnp.int32), axis=1),
                              N_EXPERTS - 1)
    n_used = (gend[-1] // tm).astype(jnp.int32).reshape(1)
    return pos.reshape(n, TOP_K), row_token.reshape(n_tiles, 1, tm), tile_expert, n_used


def _row_copy(src_ref, src_row, dst_ref, dst_row, sem):
    return pltpu.make_async_copy(src_ref.at[pl.ds(src_row, 1)], dst_ref.at[pl.ds(dst_row, 1)], sem)


def _expert_kernel(te_ref, nu_ref, idx_ref, nxt_ref, src_ref, w1_ref, w3_ref, w2_ref, y_ref,
                   xs_ref, xb_ref, acc_ref, sem, *, nf, tm):
    i = pl.program_id(0)
    j = pl.program_id(1)
    n_used = nu_ref[0]
    active = i < n_used
    slot = i % 2
    share = tm // nf

    @pl.when((i == 0) & (j == 0))
    def _():
        def issue(r, carry):
            _row_copy(src_ref, idx_ref[0, 0, r], xs_ref.at[0], r, sem.at[0]).start()
            return carry
        lax.fori_loop(0, tm, issue, 0, unroll=8)

    @pl.when(i + 1 < n_used)
    def _():
        for r in range(share):
            row = j * share + r
            _row_copy(src_ref, nxt_ref[0, 0, row], xs_ref.at[1 - slot], row, sem.at[1 - slot]).start()

    @pl.when(active)
    def _():
        @pl.when(j == 0)
        def _():
            pltpu.make_async_copy(src_ref.at[pl.ds(0, tm)], xs_ref.at[slot], sem.at[slot]).wait()
            xb_ref[...] = xs_ref[slot].astype(BF16)

        xb = xb_ref[...]
        a = jnp.dot(xb, w1_ref[0], preferred_element_type=F32)
        g = jnp.dot(xb, w3_ref[0], preferred_element_type=F32)
        h = (a * jax.nn.sigmoid(a) * g).astype(BF16)
        y = jnp.dot(h, w2_ref[0], preferred_element_type=F32)

        @pl.when(j == 0)
        def _():
            acc_ref[...] = y

        @pl.when(j > 0)
        def _():
            acc_ref[...] += y

        @pl.when(j == nf - 1)
        def _():
            y_ref[...] = acc_ref[...]

    @pl.when(jnp.logical_not(active) & (j == nf - 1))
    def _():
        y_ref[...] = jnp.zeros(y_ref.shape, y_ref.dtype)


def _experts(src, row_token, tile_expert, n_used, w1, w3, w2):
    nt, _, tm = row_token.shape
    d_ff = w1.shape[2]
    nf = 4
    tf = d_ff // nf

    def jcol(i, j, nu):
        return jnp.where(i < nu[0], j, nf - 1)

    grid_spec = pltpu.PrefetchScalarGridSpec(
        num_scalar_prefetch=2,
        grid=(nt, nf),
        in_specs=[pl.BlockSpec((1, 1, tm), lambda i, j, te, nu: (i, 0, 0), memory_space=pltpu.SMEM),
                  pl.BlockSpec((1, 1, tm), lambda i, j, te, nu: (jnp.minimum(i + 1, nt - 1), 0, 0),
                               memory_space=pltpu.SMEM),
                  pl.BlockSpec(memory_space=pl.ANY),
                  pl.BlockSpec((1, D_MODEL, tf), lambda i, j, te, nu: (te[i], 0, jcol(i, j, nu))),
                  pl.BlockSpec((1, D_MODEL, tf), lambda i, j, te, nu: (te[i], 0, jcol(i, j, nu))),
                  pl.BlockSpec((1, tf, D_MODEL), lambda i, j, te, nu: (te[i], jcol(i, j, nu), 0))],
        out_specs=pl.BlockSpec((tm, D_MODEL), lambda i, j, te, nu: (i, 0)),
        scratch_shapes=[pltpu.VMEM((2, tm, D_MODEL), src.dtype), pltpu.VMEM((tm, D_MODEL), BF16),
                        pltpu.VMEM((tm, D_MODEL), F32), pltpu.SemaphoreType.DMA((2,))],
    )
    return pl.pallas_call(
        functools.partial(_expert_kernel, nf=nf, tm=tm),
        grid_spec=grid_spec,
        out_shape=jax.ShapeDtypeStruct((nt * tm, D_MODEL), F32),
        compiler_params=_cparams(("arbitrary", "arbitrary")),
        name="moe_experts",
    )(tile_expert, n_used, row_token, row_token, src, w1, w3, w2)


def _combine_kernel(pos_ref, nxt_ref, wts_ref, x_ref, y_ref, o_ref, g_ref, sem, *, tm, nt):
    i = pl.program_id(0)
    slot = i % 2

    def request(p_ref, dst_slot):
        def issue(r, carry):
            for k in range(TOP_K):
                _row_copy(y_ref, p_ref[0, 0, TOP_K * r + k], g_ref.at[dst_slot, k], r, sem.at[dst_slot]).start()
            return carry
        lax.fori_loop(0, tm, issue, 0, unroll=8)

    @pl.when(i == 0)
    def _():
        request(pos_ref, 0)

    @pl.when(i + 1 < nt)
    def _():
        request(nxt_ref, 1 - slot)

    for k in range(TOP_K):
        pltpu.make_async_copy(y_ref.at[pl.ds(0, tm)], g_ref.at[slot, k], sem.at[slot]).wait()
    w = wts_ref[...]
    o_ref[...] = x_ref[...] + w[:, 0:1] * g_ref[slot, 0] + w[:, 1:2] * g_ref[slot, 1]


def _combine(pos, wts, x, y):
    n = x.shape[0]
    tm = min(256, n)
    nt = n // tm
    pos3 = pos.reshape(nt, 1, TOP_K * tm)
    return pl.pallas_call(
        functools.partial(_combine_kernel, tm=tm, nt=nt),
        grid=(nt,),
        in_specs=[pl.BlockSpec((1, 1, TOP_K * tm), lambda i: (i, 0, 0), memory_space=pltpu.SMEM),
                  pl.BlockSpec((1, 1, TOP_K * tm), lambda i: (jnp.minimum(i + 1, nt - 1), 0, 0),
                               memory_space=pltpu.SMEM),
                  pl.BlockSpec((tm, TOP_K), lambda i: (i, 0)),
                  pl.BlockSpec((tm, D_MODEL), lambda i: (i, 0)),
                  pl.BlockSpec(memory_space=pl.ANY)],
        out_specs=pl.BlockSpec((tm, D_MODEL), lambda i: (i, 0)),
        out_shape=jax.ShapeDtypeStruct((n, D_MODEL), F32),
        scratch_shapes=[pltpu.VMEM((2, TOP_K, tm, D_MODEL), F32), pltpu.SemaphoreType.DMA((2,))],
        compiler_params=_cparams(("arbitrary",)),
        name="moe_combine",
    )(pos3, pos3, wts, x, y)


def _moe(hn, x, wr, w1, w3, w2):
    idx, wts = _router(hn, wr)
    pos, row_token, tile_expert, n_used = _route(idx, MOE_TM)
    y = _experts(hn, row_token, tile_expert, n_used, w1, w3, w2)
    return _combine(pos, wts, x, y)


def _permute_w_in(w):
    cols = [w[:, _ORIG[name][0]:_ORIG[name][0] + _ORIG[name][1]] for name in _NEW_ORDER]
    cols.append(jnp.zeros((w.shape[0], PROJ_W - _USED_W), w.dtype))
    return jnp.concatenate(cols, axis=1).astype(BF16)


def _angles(pos, dim):
    inv = ROPE_THETA ** (-jnp.arange(0, dim, 2, dtype=F32) / dim)
    return pos.astype(F32)[:, None] * inv[None, :]


def _rope_tables(seqs):
    pos = jnp.concatenate([jnp.tile(jnp.arange(s), b) for b, s in seqs])
    ang = _angles(pos, HEAD_DIM)
    cosb = jnp.tile(jnp.cos(ang), (1, 4))
    sinb = jnp.tile(jnp.concatenate([-jnp.sin(ang), jnp.sin(ang)], axis=1), (1, 2))
    ar = _angles(pos // GRID_W, HEAD_DIM // 2)
    ac = _angles(pos % GRID_W, HEAD_DIM // 2)
    cosa = jnp.tile(jnp.concatenate([jnp.cos(ar), jnp.cos(ar), jnp.cos(ac), jnp.cos(ac)], axis=1), (1, 2))
    sina = jnp.tile(jnp.concatenate([-jnp.sin(ar), jnp.sin(ar), -jnp.sin(ac), jnp.sin(ac)], axis=1), (1, 2))
    return cosb, sinb, cosa, sina


def _head_mean_matrix():
    blk = np.kron(np.eye(LANES // HEAD_DIM), np.ones((HEAD_DIM, HEAD_DIM))) / HEAD_DIM
    return jnp.asarray(blk, BF16)


def _gain_rows(a_qn, a_kn, b_qn, b_kn):
    scale = HEAD_DIM ** -0.5
    scale_a = scale * np.log2(np.e)
    rows = [jnp.tile(b_qn, 2) * scale, jnp.tile(b_kn, 2), jnp.tile(a_qn, 2) * scale_a, jnp.tile(a_kn, 2),
            jnp.ones((LANES,), F32), jnp.full((LANES,), C_KEY_DIM ** -0.5, F32),
            jnp.ones((LANES,), F32), jnp.ones((LANES,), F32)]
    return jnp.stack(rows).astype(F32)


def _trunk(x, seqs, norm1_g, w_in, a_qn, a_kn, b_qn, b_kn, ret_dec_f, ret_dec_b, ret_norm_g,
           w_oa, w_ob, w_oc, w_out, norm2_g, ffn_w1, ffn_w3, ffn_w2,
           moe_router, moe_w1, moe_w3, moe_w2):
    depth = w_in.shape[0]
    tables = _rope_tables(seqs)
    mmat = _head_mean_matrix()
    for l in range(depth):
        proj = _inproj(x, norm1_g[l][None, :], _permute_w_in(w_in[l]))
        prepped = _prep(proj, tables, _gain_rows(a_qn[l], a_kn[l], b_qn[l], b_kn[l]), mmat)
        bqs, bks, bvs = prepped[0:3], prepped[3:6], prepped[6:9]
        aq, ak, cq, ck = prepped[9:]
        aqt = aq.T
        avt = proj[:, OFF["av"]:OFF["av"] + A_KV].T
        tabs_f = _ret_tables(ret_dec_f[l], C_CHUNK, False)
        tabs_b = _ret_tables(ret_dec_b[l], C_CHUNK, True)
        ng = ret_norm_g[l][None, :].astype(F32)
        res_a, res_c = None, None
        res_b = [None] * B_GROUPS
        row_off = 0
        for b, s in seqs:
            res_a = _mixer_a(aqt, ak, avt, res_a, row_off, b, s)
            for g, (_, dil) in enumerate(B_PATTERNS):
                res_b[g] = _mixer_b_group(bqs[g], bks[g], bvs[g], res_b[g], g, dil, row_off, b, s)
            res_c = _mixer_c(cq, ck, proj, tabs_f, tabs_b, ng, res_c, row_off, b, s)
            row_off += b * s
        x, hn = _merge(res_a[0], [r[0] for r in res_b], [r[1] for r in res_b], res_c[1], proj, x,
                       w_oa[l].astype(BF16), w_ob[l].astype(BF16), w_oc[l].astype(BF16),
                       w_out[l].astype(BF16), norm2_g[l][None, :], BF16 if l % 2 == 0 else F32)
        i = l // 2
        if l % 2 == 0:
            x = _ffn(hn, x, ffn_w1[i].astype(BF16), ffn_w3[i].astype(BF16), ffn_w2[i].astype(BF16))
        else:
            x = _moe(hn, x, moe_router[i], moe_w1[i].astype(BF16), moe_w3[i].astype(BF16),
                     moe_w2[i].astype(BF16))
    return x


def kernel(x_prompt, x_sample, norm1_g, w_in, a_qn, a_kn, b_qn, b_kn, ret_dec_f, ret_dec_b, ret_norm_g,
           w_oa, w_ob, w_oc, w_out, norm2_g, ffn_w1, ffn_w3, ffn_w2, moe_router, moe_w1, moe_w3, moe_w2):
    seqs = (x_prompt.shape[:2], x_sample.shape[:2])
    x = jnp.concatenate([x_prompt.reshape(-1, D_MODEL), x_sample.reshape(-1, D_MODEL)], axis=0)
    y = _trunk(x, seqs, norm1_g, w_in, a_qn, a_kn, b_qn, b_kn, ret_dec_f, ret_dec_b, ret_norm_g,
               w_oa, w_ob, w_oc, w_out, norm2_g, ffn_w1, ffn_w3, ffn_w2,
               moe_router, moe_w1, moe_w3, moe_w2)
    n_p = x_prompt.shape[0] * x_prompt.shape[1]
    return (y[:n_p].reshape(x_prompt.shape), y[n_p:].reshape(x_sample.shape))
```

```python
import functools

import numpy as np
import jax
import jax.numpy as jnp
from jax import lax
from jax.experimental import pallas as pl
from jax.experimental.pallas import tpu as pltpu

F32 = jnp.float32
BF16 = jnp.bfloat16

D_MODEL = 1024
GRID_W = 64
HEAD_DIM = 64
ROPE_THETA = 10000.0
EPS = 1e-6
A_HEADS = 8
A_KV_HEADS = 2
B_PATTERNS = ((128, 1), (512, 4), (2048, 16))
B_GROUPS = 3
B_HEADS = 8
C_HEADS = 4
C_KEY_DIM = 64
C_VAL_DIM = 128
N_EXPERTS = 8
N_BRANCHES = 3

A_Q = A_HEADS * HEAD_DIM
A_KV = A_KV_HEADS * HEAD_DIM
B_Q = B_GROUPS * B_HEADS * HEAD_DIM
B_KV = B_HEADS * HEAD_DIM
C_QK = C_HEADS * C_KEY_DIM
C_V = C_HEADS * C_VAL_DIM

LANES = 128
VMEM_LIMIT = 56 * 1024 * 1024

_ORIG_SPLITS = (("aq", A_Q), ("ak", A_KV), ("av", A_KV), ("bq", B_Q), ("bk", B_KV), ("bv", B_KV),
                ("cq", C_QK), ("ck", C_QK), ("cv", C_V), ("cg", C_V), ("gl", N_BRANCHES * D_MODEL))
_NEW_ORDER = ("bq", "bk", "aq", "bv", "cv", "cg", "gl", "cq", "ck", "ak", "av")
PROJ_W = 8192


def _layout():
    orig, o = {}, 0
    for name, w in _ORIG_SPLITS:
        orig[name] = (o, w)
        o += w
    new, o = {}, 0
    for name in _NEW_ORDER:
        new[name] = o
        o += orig[name][1]
    return orig, new, o


_ORIG, OFF, _USED_W = _layout()


def _cparams(sem):
    return pltpu.CompilerParams(dimension_semantics=sem, vmem_limit_bytes=VMEM_LIMIT)


def _seq_call(kern, prev, *, in_specs, out_specs, out_shape, args, **kwargs):
    n_in = len(in_specs)
    if prev is None:
        return pl.pallas_call(kern, in_specs=in_specs, out_specs=out_specs, out_shape=out_shape,
                              **kwargs)(*args)
    n_prev = len(prev)

    def chained(*refs):
        return kern(*refs[:n_in], *refs[n_in + n_prev:])

    return pl.pallas_call(
        chained,
        in_specs=list(in_specs) + [pl.BlockSpec(memory_space=pl.ANY)] * n_prev,
        out_specs=out_specs, out_shape=out_shape,
        input_output_aliases={n_in + k: k for k in range(n_prev)},
        **kwargs)(*args, *prev)


def _inproj_kernel(x_ref, g_ref, w_ref, o_ref, xn_ref):
    @pl.when(pl.program_id(1) == 0)
    def _():
        x = x_ref[...]
        ms = jnp.mean(x * x, axis=-1, keepdims=True)
        xn_ref[...] = (x * lax.rsqrt(ms + EPS) * g_ref[...]).astype(BF16)

    o_ref[...] = jnp.dot(xn_ref[...], w_ref[...], preferred_element_type=F32).astype(BF16)


def _inproj(x, g, w):
    n = x.shape[0]
    tm = min(1024, n)
    tn = 1024
    return pl.pallas_call(
        _inproj_kernel,
        grid=(n // tm, PROJ_W // tn),
        in_specs=[pl.BlockSpec((tm, D_MODEL), lambda i, j: (i, 0)),
                  pl.BlockSpec((1, D_MODEL), lambda i, j: (0, 0)),
                  pl.BlockSpec((D_MODEL, tn), lambda i, j: (0, j))],
        out_specs=pl.BlockSpec((tm, tn), lambda i, j: (i, j)),
        out_shape=jax.ShapeDtypeStruct((n, PROJ_W), BF16),
        scratch_shapes=[pltpu.VMEM((tm, D_MODEL), BF16)],
        compiler_params=_cparams(("parallel", "arbitrary")),
        name="inproj",
    )(x, g, w)


def _norm_rot(x, gain, cos, sin, mmat, half, do_norm):
    if do_norm:
        ms = jnp.dot((x * x).astype(BF16), mmat, preferred_element_type=F32)
        x = x * lax.rsqrt(ms + EPS)
    x = x * gain
    lane = lax.broadcasted_iota(jnp.int32, x.shape, 1)
    first = (lane % (2 * half)) < half
    swapped = jnp.where(first, pltpu.roll(x, LANES - half, 1), pltpu.roll(x, half, 1))
    return x * cos + swapped * sin


def _prep_kernel(bqk_ref, aq_ref, bv_ref, cq_ref, ck_ref, ak_ref,
                 cosb_ref, sinb_ref, cosa_ref, sina_ref, gain_ref, mmat_ref,
                 q0_ref, q1_ref, q2_ref, k0_ref, k1_ref, k2_ref, v0_ref, v1_ref, v2_ref,
                 oaq_ref, oak_ref, ocq_ref, ock_ref, scr_ref):
    cosb, sinb = cosb_ref[...], sinb_ref[...]
    cosa, sina = cosa_ref[...], sina_ref[...]
    mmat = mmat_ref[...]
    gains = gain_ref[...]
    tm = scr_ref.shape[0]

    def prepared(src_ref, src_off, c, gain_row, cos, sin, half, do_norm):
        x = src_ref[:, src_off + c * LANES:src_off + (c + 1) * LANES].astype(F32)
        return _norm_rot(x, gains[gain_row:gain_row + 1, :], cos, sin, mmat, half, do_norm)

    def run(src_ref, dst_ref, width, gain_row, cos, sin, half, do_norm):
        for c in range(width // LANES):
            y = prepared(src_ref, 0, c, gain_row, cos, sin, half, do_norm)
            dst_ref[:, c * LANES:(c + 1) * LANES] = y.astype(dst_ref.dtype)

    def store_dilated(y, dst_ref, dil, c):
        if dil == 1:
            dst_ref[:, c * LANES:(c + 1) * LANES] = y.astype(dst_ref.dtype)
            return
        scr_ref[...] = y
        for r in range(dil):
            rows = scr_ref[pl.ds(r, tm // dil, stride=dil), :]
            dst_ref[:, r * B_KV + c * LANES:r * B_KV + (c + 1) * LANES] = rows.astype(dst_ref.dtype)

    chunks = B_KV // LANES
    for g, (q_ref, (_, dil)) in enumerate(zip((q0_ref, q1_ref, q2_ref), B_PATTERNS)):
        for c in range(chunks):
            y = prepared(bqk_ref, g * B_KV, c, 0, cosb, sinb, HEAD_DIM // 2, True)
            store_dilated(y, q_ref, dil, c)
    for c in range(chunks):
        yk = prepared(bqk_ref, B_Q, c, 1, cosb, sinb, HEAD_DIM // 2, True)
        yv = bv_ref[:, c * LANES:(c + 1) * LANES].astype(F32)
        for (_, dil), k_ref, v_ref in zip(B_PATTERNS, (k0_ref, k1_ref, k2_ref), (v0_ref, v1_ref, v2_ref)):
            store_dilated(yk, k_ref, dil, c)
            store_dilated(yv, v_ref, dil, c)
    run(aq_ref, oaq_ref, A_Q, 2, cosa, sina, HEAD_DIM // 4, True)
    run(ak_ref, oak_ref, A_KV, 3, cosa, sina, HEAD_DIM // 4, True)
    run(cq_ref, ocq_ref, C_QK, 4, cosb, sinb, C_KEY_DIM // 2, False)
    run(ck_ref, ock_ref, C_QK, 5, cosb, sinb, C_KEY_DIM // 2, False)


def _prep(proj, tables, gains, mmat):
    n = proj.shape[0]
    tm = min(512, n)
    cosb, sinb, cosa, sina = tables

    def col(width, name):
        idx = OFF[name] // width
        return pl.BlockSpec((tm, width), lambda i: (i, idx))

    def tab():
        return pl.BlockSpec((tm, LANES), lambda i: (i, 0))

    dils = [dil for _, dil in B_PATTERNS]
    shapes = [(n // dil, dil * B_KV) for dil in dils] * 3 + [(n, A_Q), (n, A_KV), (n, C_QK), (n, C_QK)]
    blocks = [(tm // dil, dil * B_KV) for dil in dils] * 3 + [(tm, A_Q), (tm, A_KV), (tm, C_QK), (tm, C_QK)]
    return pl.pallas_call(
        _prep_kernel,
        grid=(n // tm,),
        in_specs=[col(B_Q + B_KV, "bq"), col(A_Q, "aq"), col(B_KV, "bv"), col(C_QK, "cq"),
                  col(C_QK, "ck"), col(A_KV, "ak"), tab(), tab(), tab(), tab(),
                  pl.BlockSpec((8, LANES), lambda i: (0, 0)),
                  pl.BlockSpec((LANES, LANES), lambda i: (0, 0))],
        out_specs=[pl.BlockSpec(blk, lambda i: (i, 0)) for blk in blocks],
        out_shape=[jax.ShapeDtypeStruct(shp, BF16) for shp in shapes],
        scratch_shapes=[pltpu.VMEM((tm, LANES), F32)],
        compiler_params=_cparams(("parallel",)),
        name="prep",
    )(proj, proj, proj, proj, proj, proj, cosb, sinb, cosa, sina, gains, mmat)


A_ONES_ROWS = 16


def _attn_a_kernel(qt_ref, k0_ref, kn_ref, vt_ref, o_ref, qp_ref, m_ref, acc_ref, sa_ref, sb_ref, *, bq, nk):
    ik = pl.program_id(2)
    rep = A_HEADS // A_KV_HEADS
    kv_heads = range(A_KV_HEADS)

    @pl.when(ik == 0)
    def _():
        qp_ref[...] = jnp.zeros(qp_ref.shape, BF16)
        for g in kv_heads:
            for r in range(rep):
                h = g * rep + r
                qp_ref[g, g * HEAD_DIM:(g + 1) * HEAD_DIM, r * bq:(r + 1) * bq] = (
                    qt_ref[h * HEAD_DIM:(h + 1) * HEAD_DIM, :])
        m_ref[...] = jnp.full(m_ref.shape, -jnp.inf, F32)
        acc_ref[...] = jnp.zeros(acc_ref.shape, F32)
        for g in kv_heads:
            sa_ref[g] = jnp.dot(k0_ref[...], qp_ref[g], preferred_element_type=F32)

    def step(cur_ref, nxt_ref):
        kn = kn_ref[...]
        ones = jnp.ones((A_ONES_ROWS, kn.shape[0]), BF16)
        for g in kv_heads:
            v_aug = jnp.concatenate([vt_ref[g * HEAD_DIM:(g + 1) * HEAD_DIM, :], ones], axis=0)
            for r in range(rep):
                cols = slice(r * bq, (r + 1) * bq)
                nxt_ref[g, :, cols] = jnp.dot(kn, qp_ref[g, :, cols], preferred_element_type=F32)
                s = cur_ref[g, :, cols]
                m_prev = m_ref[g, :, cols]
                m_new = jnp.maximum(m_prev, jnp.max(s, axis=0, keepdims=True))
                p = jnp.exp2(s - m_new).astype(BF16)
                alpha = jnp.exp2(m_prev - m_new)
                pv = jnp.dot(v_aug, p, preferred_element_type=F32)
                acc_ref[g, :, cols] = alpha * acc_ref[g, :, cols] + pv
                m_ref[g, :, cols] = m_new

    @pl.when(ik % 2 == 0)
    def _():
        step(sa_ref, sb_ref)

    @pl.when(ik % 2 == 1)
    def _():
        step(sb_ref, sa_ref)

    @pl.when(ik == nk - 1)
    def _():
        for g in range(A_KV_HEADS):
            acc = acc_ref[g]
            o = acc[:HEAD_DIM] / acc[HEAD_DIM:HEAD_DIM + 1]
            for r in range(rep):
                h = g * rep + r
                o_ref[h * HEAD_DIM:(h + 1) * HEAD_DIM, :] = o[:, r * bq:(r + 1) * bq].astype(o_ref.dtype)


def _mixer_a(aqt, ak, avt, prev, row_off, b, s):
    n = ak.shape[0]
    bq = min(256, s)
    bk = min(512, s)
    nq, nk = s // bq, s // bk
    rep = A_HEADS // A_KV_HEADS
    q0, k0 = row_off // bq, row_off // bk
    kern = functools.partial(_attn_a_kernel, bq=bq, nk=nk)
    return _seq_call(
        kern, prev,
        grid=(b, nq, nk),
        in_specs=[pl.BlockSpec((A_Q, bq), lambda ib, iq, ik: (0, q0 + ib * nq + iq)),
                  pl.BlockSpec((bk, A_KV), lambda ib, iq, ik: (k0 + ib * nk, 0)),
                  pl.BlockSpec((bk, A_KV), lambda ib, iq, ik: (k0 + ib * nk + jnp.minimum(ik + 1, nk - 1), 0)),
                  pl.BlockSpec((A_KV, bk), lambda ib, iq, ik: (0, k0 + ib * nk + ik))],
        out_specs=[pl.BlockSpec((A_Q, bq), lambda ib, iq, ik: (0, q0 + ib * nq + iq))],
        out_shape=[jax.ShapeDtypeStruct((A_Q, n), BF16)],
        scratch_shapes=[pltpu.VMEM((A_KV_HEADS, A_KV, rep * bq), BF16),
                        pltpu.VMEM((A_KV_HEADS, 1, rep * bq), F32),
                        pltpu.VMEM((A_KV_HEADS, HEAD_DIM + A_ONES_ROWS, rep * bq), F32),
                        pltpu.VMEM((A_KV_HEADS, bk, rep * bq), F32),
                        pltpu.VMEM((A_KV_HEADS, bk, rep * bq), F32)],
        compiler_params=_cparams(("parallel", "parallel", "arbitrary")),
        name="mixer_a",
        args=(aqt, ak, ak, avt))


B_BLOCK_Q = 128
B_HALO = 64


def _attn_b_kernel(q_ref, k0_ref, k1_ref, k2_ref, k3_ref, v0_ref, v1_ref, v2_ref, v3_ref,
                   o_ref, lse_ref, *, u_len):
    i = pl.program_id(2)
    kcat = jnp.concatenate([k0_ref[...], k1_ref[...], k2_ref[...], k3_ref[...]], axis=0)
    vcat = jnp.concatenate([v0_ref[...], v1_ref[...], v2_ref[...], v3_ref[...]], axis=0)
    nkeys = B_BLOCK_Q + 2 * B_HALO
    a = lax.broadcasted_iota(jnp.int32, (B_BLOCK_Q, nkeys), 0)
    c = lax.broadcasted_iota(jnp.int32, (B_BLOCK_Q, nkeys), 1)
    key_pos = i * B_BLOCK_Q - B_HALO + c
    valid = (c >= a) & (c <= a + 2 * B_HALO) & (key_pos >= 0) & (key_pos < u_len)
    for h in range(B_HEADS):
        sl = slice(h * HEAD_DIM, (h + 1) * HEAD_DIM)
        s = lax.dot_general(q_ref[:, sl], kcat[:, sl], (((1,), (1,)), ((), ())),
                            preferred_element_type=F32)
        s = jnp.where(valid, s, -1e30)
        m = jnp.max(s, axis=1, keepdims=True)
        e = jnp.exp(s - m)
        den = jnp.sum(e, axis=1, keepdims=True)
        o = jnp.dot(e.astype(BF16), vcat[:, sl], preferred_element_type=F32) / den
        o_ref[:, sl] = o.astype(o_ref.dtype)
        lse_ref[:, sl] = jnp.broadcast_to(m + jnp.log(den), (B_BLOCK_Q, HEAD_DIM))


def _mixer_b_group(qd, kd, vd, prev, g, dil, row_off, b, s):
    u_len = s // dil
    nq = u_len // B_BLOCK_Q
    nkb = u_len // B_HALO
    q_rows0 = row_off // dil // B_BLOCK_Q
    k_rows0 = row_off // dil // B_HALO

    def kspec(t):
        def imap(ib, r, i):
            blk = jnp.clip(2 * i - 1 + t, 0, nkb - 1)
            return (k_rows0 + ib * nkb + blk, r)
        return pl.BlockSpec((B_HALO, B_KV), imap)

    def qspec():
        return pl.BlockSpec((B_BLOCK_Q, B_KV), lambda ib, r, i: (q_rows0 + ib * nq + i, r))

    kern = functools.partial(_attn_b_kernel, u_len=u_len)
    return _seq_call(
        kern, prev,
        grid=(b, dil, nq),
        in_specs=[qspec()] + [kspec(t) for t in range(4)] + [kspec(t) for t in range(4)],
        out_specs=[qspec(), qspec()],
        out_shape=[jax.ShapeDtypeStruct(qd.shape, BF16), jax.ShapeDtypeStruct(qd.shape, F32)],
        compiler_params=_cparams(("parallel", "parallel", "parallel")),
        name=f"mixer_b{g}",
        args=(qd, kd, kd, kd, kd, vd, vd, vd, vd))


def _ret_chunk(q, k, v, dec_ref, xi_ref, zeta_ref, cdec_ref, r_ref):
    outs = []
    for h in range(C_HEADS):
        qh = q[:, h * C_KEY_DIM:(h + 1) * C_KEY_DIM]
        kh = k[:, h * C_KEY_DIM:(h + 1) * C_KEY_DIM]
        vh = v[:, h * C_VAL_DIM:(h + 1) * C_VAL_DIM]
        att = lax.dot_general(qh, kh, (((1,), (1,)), ((), ())), preferred_element_type=F32) * dec_ref[h]
        inner = jnp.dot(att.astype(BF16), vh, preferred_element_type=F32)
        r = r_ref[h]
        cross = jnp.dot(qh, r.astype(BF16), preferred_element_type=F32) * xi_ref[h]
        kz = (kh.astype(F32) * zeta_ref[h]).astype(BF16)
        r_ref[h] = r * cdec_ref[h] + lax.dot_general(kz, vh, (((0,), (0,)), ((), ())),
                                                     preferred_element_type=F32)
        outs.append(inner + cross)
    return jnp.concatenate(outs, axis=1)


def _ret_fwd_kernel(q_ref, k_ref, v_ref, dec_ref, xi_ref, zeta_ref, cdec_ref, o_ref, r_ref):
    @pl.when(pl.program_id(1) == 0)
    def _():
        r_ref[...] = jnp.zeros(r_ref.shape, F32)

    o_ref[...] = _ret_chunk(q_ref[...], k_ref[...], v_ref[...], dec_ref, xi_ref, zeta_ref, cdec_ref, r_ref)


def _ret_bwd_kernel(q_ref, k_ref, v_ref, dec_ref, xi_ref, zeta_ref, cdec_ref, of_ref, gate_ref, ng_ref,
                    o_ref, r_ref):
    @pl.when(pl.program_id(1) == 0)
    def _():
        r_ref[...] = jnp.zeros(r_ref.shape, F32)

    o = of_ref[...] + _ret_chunk(q_ref[...], k_ref[...], v_ref[...], dec_ref, xi_ref, zeta_ref, cdec_ref, r_ref)
    ng = ng_ref[...]
    gate = gate_ref[...].astype(F32)
    for h in range(C_HEADS):
        sl = slice(h * C_VAL_DIM, (h + 1) * C_VAL_DIM)
        oh = o[:, sl]
        mu = jnp.mean(oh, axis=1, keepdims=True)
        var = jnp.mean(jnp.square(oh - mu), axis=1, keepdims=True)
        y = (oh - mu) * lax.rsqrt(var + EPS) * ng[:, sl]
        gh = gate[:, sl]
        o_ref[:, sl] = (gh * jax.nn.sigmoid(gh) * y).astype(o_ref.dtype)


def _ret_tables(dec_param, chunk, strict):
    log_g = -jnp.exp(dec_param.astype(F32))
    j = jnp.arange(chunk, dtype=F32)
    lg = log_g[:, None, None]
    if strict:
        diff = j[None, :] - j[:, None]
        dec = jnp.where((diff > 0)[None], jnp.exp(jnp.maximum(diff, 0.0)[None] * lg), 0.0)
        xi = jnp.exp((chunk - j)[None, :, None] * lg)
        zeta = jnp.exp(j[None, :, None] * lg)
    else:
        diff = j[:, None] - j[None, :]
        dec = jnp.where((diff >= 0)[None], jnp.exp(jnp.maximum(diff, 0.0)[None] * lg), 0.0)
        xi = jnp.exp((j + 1.0)[None, :, None] * lg)
        zeta = jnp.exp((chunk - 1.0 - j)[None, :, None] * lg)
    xi = jnp.broadcast_to(xi, (C_HEADS, chunk, C_VAL_DIM))
    zeta = jnp.broadcast_to(zeta, (C_HEADS, chunk, C_KEY_DIM))
    cdec = jnp.broadcast_to(jnp.exp(chunk * log_g)[:, None, None], (C_HEADS, C_KEY_DIM, C_VAL_DIM))
    return dec, xi, zeta, cdec


C_CHUNK = 128


def _mixer_c(cq, ck, proj, tabs_f, tabs_b, norm_g, prev, row_off, b, s):
    n = cq.shape[0]
    prev_f, prev_o = (None, None) if prev is None else ([prev[0]], [prev[1]])
    nc = s // C_CHUNK
    r0 = row_off // C_CHUNK
    v_col = OFF["cv"] // C_V
    g_col = OFF["cg"] // C_V

    def fwd_rows(ib, c):
        return r0 + ib * nc + c

    def bwd_rows(ib, c):
        return r0 + ib * nc + (nc - 1 - c)

    def specs(rows):
        def full(shape):
            return pl.BlockSpec(shape, lambda ib, c: (0,) * len(shape))
        return [pl.BlockSpec((C_CHUNK, C_QK), lambda ib, c: (rows(ib, c), 0)),
                pl.BlockSpec((C_CHUNK, C_QK), lambda ib, c: (rows(ib, c), 0)),
                pl.BlockSpec((C_CHUNK, C_V), lambda ib, c: (rows(ib, c), v_col)),
                full((C_HEADS, C_CHUNK, C_CHUNK)), full((C_HEADS, C_CHUNK, C_VAL_DIM)),
                full((C_HEADS, C_CHUNK, C_KEY_DIM)), full((C_HEADS, C_KEY_DIM, C_VAL_DIM))]

    scratch = [pltpu.VMEM((C_HEADS, C_KEY_DIM, C_VAL_DIM), F32)]
    o_f, = _seq_call(
        _ret_fwd_kernel, prev_f,
        grid=(b, nc),
        in_specs=specs(fwd_rows),
        out_specs=[pl.BlockSpec((C_CHUNK, C_V), lambda ib, c: (fwd_rows(ib, c), 0))],
        out_shape=[jax.ShapeDtypeStruct((n, C_V), F32)],
        scratch_shapes=scratch,
        compiler_params=_cparams(("parallel", "arbitrary")),
        name="ret_fwd",
        args=(cq, ck, proj, *tabs_f))
    o_c, = _seq_call(
        _ret_bwd_kernel, prev_o,
        grid=(b, nc),
        in_specs=specs(bwd_rows) + [
            pl.BlockSpec((C_CHUNK, C_V), lambda ib, c: (bwd_rows(ib, c), 0)),
            pl.BlockSpec((C_CHUNK, C_V), lambda ib, c: (bwd_rows(ib, c), g_col)),
            pl.BlockSpec((1, C_V), lambda ib, c: (0, 0))],
        out_specs=[pl.BlockSpec((C_CHUNK, C_V), lambda ib, c: (bwd_rows(ib, c), 0))],
        out_shape=[jax.ShapeDtypeStruct((n, C_V), BF16)],
        scratch_shapes=scratch,
        compiler_params=_cparams(("parallel", "arbitrary")),
        name="ret_bwd",
        args=(cq, ck, proj, *tabs_b, o_f, proj, norm_g))
    return [o_f, o_c]


def _merge_kernel(oa_ref, ob0_ref, ob1_ref, ob2_ref, l0_ref, l1_ref, l2_ref, oc_ref,
                  ga_ref, gb_ref, gc_ref, x_ref, woa_ref, wob_ref, woc_ref, wout_ref, n2_ref,
                  xo_ref, hn_ref, scr_ref):
    tm = x_ref.shape[0]

    def natural(src_ref, dil, slot):
        if dil == 1:
            return src_ref[...].astype(F32)
        chunks = B_KV // LANES
        for r in range(dil):
            for c in range(chunks):
                col = r * B_KV + c * LANES
                scr_ref[slot, c, pl.ds(r, tm // dil, stride=dil), :] = src_ref[:, col:col + LANES].astype(F32)
        return jnp.concatenate([scr_ref[slot, c] for c in range(chunks)], axis=1)

    dils = [dil for _, dil in B_PATTERNS]
    l0, l1, l2 = [natural(ref, dil, 2 * g) for g, (ref, dil) in enumerate(zip((l0_ref, l1_ref, l2_ref), dils))]
    o0, o1, o2 = [natural(ref, dil, 2 * g + 1)
                  for g, (ref, dil) in enumerate(zip((ob0_ref, ob1_ref, ob2_ref), dils))]
    mx = jnp.maximum(jnp.maximum(l0, l1), l2)
    e0, e1, e2 = jnp.exp(l0 - mx), jnp.exp(l1 - mx), jnp.exp(l2 - mx)
    ob = (e0 * o0 + e1 * o1 + e2 * o2) / (e0 + e1 + e2)

    def branch(o, w_ref, gate_ref):
        y = jnp.dot(o, w_ref[...], preferred_element_type=F32)
        return jax.nn.sigmoid(gate_ref[...].astype(F32)) * y

    oa = jnp.transpose(oa_ref[...].astype(F32)).astype(BF16)
    merged = (branch(oa, woa_ref, ga_ref) + branch(ob.astype(BF16), wob_ref, gb_ref)
              + branch(oc_ref[...], woc_ref, gc_ref))
    x = x_ref[...] + jnp.dot(merged.astype(BF16), wout_ref[...], preferred_element_type=F32)
    xo_ref[...] = x
    ms = jnp.mean(x * x, axis=-1, keepdims=True)
    hn_ref[...] = (x * lax.rsqrt(ms + EPS) * n2_ref[...]).astype(hn_ref.dtype)


def _merge(o_a, o_b, lse_b, o_c, proj, x, w_oa, w_ob, w_oc, w_out, n2, hn_dtype):
    n = x.shape[0]
    tm = min(256, n)
    gl0 = OFF["gl"] // D_MODEL

    def rows(width):
        return pl.BlockSpec((tm, width), lambda i: (i, 0))

    def gate(k):
        return pl.BlockSpec((tm, D_MODEL), lambda i: (i, gl0 + k))

    def full(r, c):
        return pl.BlockSpec((r, c), lambda i: (0, 0))

    grouped = [pl.BlockSpec((tm // dil, dil * B_KV), lambda i: (i, 0)) for _, dil in B_PATTERNS]
    return pl.pallas_call(
        _merge_kernel,
        grid=(n // tm,),
        in_specs=[pl.BlockSpec((A_Q, tm), lambda i: (0, i))] + grouped + grouped
                 + [rows(C_V), gate(0), gate(1), gate(2), rows(D_MODEL),
                    full(A_Q, D_MODEL), full(B_KV, D_MODEL), full(C_V, D_MODEL), full(D_MODEL, D_MODEL),
                    full(1, D_MODEL)],
        out_specs=[rows(D_MODEL), rows(D_MODEL)],
        out_shape=[jax.ShapeDtypeStruct((n, D_MODEL), F32), jax.ShapeDtypeStruct((n, D_MODEL), hn_dtype)],
        scratch_shapes=[pltpu.VMEM((2 * B_GROUPS, B_KV // LANES, tm, LANES), F32)],
        compiler_params=_cparams(("parallel",)),
        name="merge_out",
    )(o_a, *o_b, *lse_b, o_c, proj, proj, proj, x, w_oa, w_ob, w_oc, w_out, n2)


def _ffn_kernel(hn_ref, x_ref, w1_ref, w3_ref, w2_ref, o_ref, acc_ref, *, nf):
    j = pl.program_id(1)
    hn = hn_ref[...]
    a = jnp.dot(hn, w1_ref[...], preferred_element_type=F32)
    g = jnp.dot(hn, w3_ref[...], preferred_element_type=F32)
    h = (a * jax.nn.sigmoid(a) * g).astype(BF16)
    y = jnp.dot(h, w2_ref[...], preferred_element_type=F32)

    @pl.when(j == 0)
    def _():
        acc_ref[...] = x_ref[...] + y

    @pl.when(j > 0)
    def _():
        acc_ref[...] += y

    @pl.when(j == nf - 1)
    def _():
        o_ref[...] = acc_ref[...]


def _ffn(hn, x, w1, w3, w2):
    n = x.shape[0]
    d_ff = w1.shape[1]
    tm = min(512, n)
    tf = d_ff // 2
    nf = d_ff // tf
    return pl.pallas_call(
        functools.partial(_ffn_kernel, nf=nf),
        grid=(n // tm, nf),
        in_specs=[pl.BlockSpec((tm, D_MODEL), lambda i, j: (i, 0)),
                  pl.BlockSpec((tm, D_MODEL), lambda i, j: (i, 0)),
                  pl.BlockSpec((D_MODEL, tf), lambda i, j: (0, j)),
                  pl.BlockSpec((D_MODEL, tf), lambda i, j: (0, j)),
                  pl.BlockSpec((tf, D_MODEL), lambda i, j: (j, 0))],
        out_specs=pl.BlockSpec((tm, D_MODEL), lambda i, j: (i, 0)),
        out_shape=jax.ShapeDtypeStruct((n, D_MODEL), F32),
        scratch_shapes=[pltpu.VMEM((tm, D_MODEL), F32)],
        compiler_params=_cparams(("parallel", "arbitrary")),
        name="ffn",
    )(hn, x, w1, w3, w2)


TOP_K = 2
MOE_TM = 512


def _router_kernel(hn_ref, wr_ref, idx_ref, wts_ref):
    logits = jnp.dot(hn_ref[...], wr_ref[...], preferred_element_type=F32,
                     precision=lax.Precision.HIGHEST)
    col = lax.broadcasted_iota(jnp.int32, logits.shape, 1)
    m1 = jnp.max(logits, axis=1, keepdims=True)
    i1 = jnp.min(jnp.where(logits == m1, col, N_EXPERTS), axis=1, keepdims=True)
    rest = jnp.where(col == i1, -jnp.inf, logits)
    m2 = jnp.max(rest, axis=1, keepdims=True)
    i2 = jnp.min(jnp.where(rest == m2, col, N_EXPERTS), axis=1, keepdims=True)
    e2 = jnp.exp(m2 - m1)
    w1 = 1.0 / (1.0 + e2)
    idx_ref[...] = jnp.concatenate([i1, i2], axis=1)
    wts_ref[...] = jnp.concatenate([w1, e2 * w1], axis=1)


def _router(hn, wr):
    n = hn.shape[0]
    tm = min(1024, n)
    return pl.pallas_call(
        _router_kernel,
        grid=(n // tm,),
        in_specs=[pl.BlockSpec((tm, D_MODEL), lambda i: (i, 0)),
                  pl.BlockSpec((D_MODEL, N_EXPERTS), lambda i: (0, 0))],
        out_specs=[pl.BlockSpec((tm, TOP_K), lambda i: (i, 0)), pl.BlockSpec((tm, TOP_K), lambda i: (i, 0))],
        out_shape=[jax.ShapeDtypeStruct((n, TOP_K), jnp.int32), jax.ShapeDtypeStruct((n, TOP_K), F32)],
        compiler_params=_cparams(("parallel",)),
        name="router",
    )(hn, wr)


def _route(idx, tm):
    n = idx.shape[0]
    e_flat = idx.reshape(-1)
    onehot = (e_flat[:, None] == jnp.arange(N_EXPERTS, dtype=jnp.int32)[None, :]).astype(jnp.int32)
    csum = jnp.cumsum(onehot, axis=0)
    rank = jnp.sum((csum - onehot) * onehot, axis=1)
    gsz = ((csum[-1] + tm - 1) // tm) * tm
    gend = jnp.cumsum(gsz)
    pos = (gend - gsz)[e_flat] + rank
    n_rows = n * TOP_K + N_EXPERTS * tm
    n_tiles = n_rows // tm
    row_token = jnp.zeros((n_rows,), jnp.int32).at[pos].set(jnp.arange(n * TOP_K, dtype=jnp.int32) // TOP_K)
    tile_start = jnp.arange(n_tiles, dtype=jnp.int32) * tm
    tile_expert = jnp.minimum(jnp.sum((tile_start[:, None] >= gend[None, :]).astype(jnp.int32), axis=1),
                              N_EXPERTS - 1)
    n_used = (gend[-1] // tm).astype(jnp.int32).reshape(1)
    return pos.reshape(n, TOP_K), row_token.reshape(n_tiles, 1, tm), tile_expert, n_used


def _row_copy(src_ref, src_row, dst_ref, dst_row, sem):
    return pltpu.make_async_copy(src_ref.at[pl.ds(src_row, 1)], dst_ref.at[pl.ds(dst_row, 1)], sem)


def _expert_kernel(te_ref, nu_ref, idx_ref, nxt_ref, src_ref, w1_ref, w3_ref, w2_ref, y_ref,
                   xs_ref, xb_ref, acc_ref, sem, *, nf, tm):
    i = pl.program_id(0)
    j = pl.program_id(1)
    n_used = nu_ref[0]
    active = i < n_used
    slot = i % 2
    share = tm // nf

    @pl.when((i == 0) & (j == 0))
    def _():
        def issue(r, carry):
            _row_copy(src_ref, idx_ref[0, 0, r], xs_ref.at[0], r, sem.at[0]).start()
            return carry
        lax.fori_loop(0, tm, issue, 0, unroll=8)

    @pl.when(i + 1 < n_used)
    def _():
        for r in range(share):
            row = j * share + r
            _row_copy(src_ref, nxt_ref[0, 0, row], xs_ref.at[1 - slot], row, sem.at[1 - slot]).start()

    @pl.when(active)
    def _():
        @pl.when(j == 0)
        def _():
            pltpu.make_async_copy(src_ref.at[pl.ds(0, tm)], xs_ref.at[slot], sem.at[slot]).wait()
            xb_ref[...] = xs_ref[slot].astype(BF16)

        xb = xb_ref[...]
        a = jnp.dot(xb, w1_ref[0], preferred_element_type=F32)
        g = jnp.dot(xb, w3_ref[0], preferred_element_type=F32)
        h = (a * jax.nn.sigmoid(a) * g).astype(BF16)
        y = jnp.dot(h, w2_ref[0], preferred_element_type=F32)

        @pl.when(j == 0)
        def _():
            acc_ref[...] = y

        @pl.when(j > 0)
        def _():
            acc_ref[...] += y

        @pl.when(j == nf - 1)
        def _():
            y_ref[...] = acc_ref[...]

    @pl.when(jnp.logical_not(active) & (j == nf - 1))
    def _():
        y_ref[...] = jnp.zeros(y_ref.shape, y_ref.dtype)


def _experts(src, row_token, tile_expert, n_used, w1, w3, w2):
    nt, _, tm = row_token.shape
    d_ff = w1.shape[2]
    nf = 4
    tf = d_ff // nf

    def jcol(i, j, nu):
        return jnp.where(i < nu[0], j, nf - 1)

    grid_spec = pltpu.PrefetchScalarGridSpec(
        num_scalar_prefetch=2,
        grid=(nt, nf),
        in_specs=[pl.BlockSpec((1, 1, tm), lambda i, j, te, nu: (i, 0, 0), memory_space=pltpu.SMEM),
                  pl.BlockSpec((1, 1, tm), lambda i, j, te, nu: (jnp.minimum(i + 1, nt - 1), 0, 0),
                               memory_space=pltpu.SMEM),
                  pl.BlockSpec(memory_space=pl.ANY),
                  pl.BlockSpec((1, D_MODEL, tf), lambda i, j, te, nu: (te[i], 0, jcol(i, j, nu))),
                  pl.BlockSpec((1, D_MODEL, tf), lambda i, j, te, nu: (te[i], 0, jcol(i, j, nu))),
                  pl.BlockSpec((1, tf, D_MODEL), lambda i, j, te, nu: (te[i], jcol(i, j, nu), 0))],
        out_specs=pl.BlockSpec((tm, D_MODEL), lambda i, j, te, nu: (i, 0)),
        scratch_shapes=[pltpu.VMEM((2, tm, D_MODEL), src.dtype), pltpu.VMEM((tm, D_MODEL), BF16),
                        pltpu.VMEM((tm, D_MODEL), F32), pltpu.SemaphoreType.DMA((2,))],
    )
    return pl.pallas_call(
        functools.partial(_expert_kernel, nf=nf, tm=tm),
        grid_spec=grid_spec,
        out_shape=jax.ShapeDtypeStruct((nt * tm, D_MODEL), F32),
        compiler_params=_cparams(("arbitrary", "arbitrary")),
        name="moe_experts",
    )(tile_expert, n_used, row_token, row_token, src, w1, w3, w2)


def _combine_kernel(pos_ref, nxt_ref, wts_ref, x_ref, y_ref, o_ref, g_ref, sem, *, tm, nt):
    i = pl.program_id(0)
    slot = i % 2

    def request(p_ref, dst_slot):
        def issue(r, carry):
            for k in range(TOP_K):
                _row_copy(y_ref, p_ref[0, 0, TOP_K * r + k], g_ref.at[dst_slot, k], r, sem.at[dst_slot]).start()
            return carry
        lax.fori_loop(0, tm, issue, 0, unroll=8)

    @pl.when(i == 0)
    def _():
        request(pos_ref, 0)

    @pl.when(i + 1 < nt)
    def _():
        request(nxt_ref, 1 - slot)

    for k in range(TOP_K):
        pltpu.make_async_copy(y_ref.at[pl.ds(0, tm)], g_ref.at[slot, k], sem.at[slot]).wait()
    w = wts_ref[...]
    o_ref[...] = x_ref[...] + w[:, 0:1] * g_ref[slot, 0] + w[:, 1:2] * g_ref[slot, 1]


def _combine(pos, wts, x, y):
    n = x.shape[0]
    tm = min(256, n)
    nt = n // tm
    pos3 = pos.reshape(nt, 1, TOP_K * tm)
    return pl.pallas_call(
        functools.partial(_combine_kernel, tm=tm, nt=nt),
        grid=(nt,),
        in_specs=[pl.BlockSpec((1, 1, TOP_K * tm), lambda i: (i, 0, 0), memory_space=pltpu.SMEM),
                  pl.BlockSpec((1, 1, TOP_K * tm), lambda i: (jnp.minimum(i + 1, nt - 1), 0, 0),
                               memory_space=pltpu.SMEM),
                  pl.BlockSpec((tm, TOP_K), lambda i: (i, 0)),
                  pl.BlockSpec((tm, D_MODEL), lambda i: (i, 0)),
                  pl.BlockSpec(memory_space=pl.ANY)],
        out_specs=pl.BlockSpec((tm, D_MODEL), lambda i: (i, 0)),
        out_shape=jax.ShapeDtypeStruct((n, D_MODEL), F32),
        scratch_shapes=[pltpu.VMEM((2, TOP_K, tm, D_MODEL), F32), pltpu.SemaphoreType.DMA((2,))],
        compiler_params=_cparams(("arbitrary",)),
        name="moe_combine",
    )(pos3, pos3, wts, x, y)


def _moe(hn, x, wr, w1, w3, w2):
    idx, wts = _router(hn, wr)
    pos, row_token, tile_expert, n_used = _route(idx, MOE_TM)
    y = _experts(hn, row_token, tile_expert, n_used, w1, w3, w2)
    return _combine(pos, wts, x, y)


def _permute_w_in(w):
    cols = [w[:, _ORIG[name][0]:_ORIG[name][0] + _ORIG[name][1]] for name in _NEW_ORDER]
    cols.append(jnp.zeros((w.shape[0], PROJ_W - _USED_W), w.dtype))
    return jnp.concatenate(cols, axis=1).astype(BF16)


def _angles(pos, dim):
    inv = ROPE_THETA ** (-jnp.arange(0, dim, 2, dtype=F32) / dim)
    return pos.astype(F32)[:, None] * inv[None, :]


def _rope_tables(seqs):
    pos = jnp.concatenate([jnp.tile(jnp.arange(s), b) for b, s in seqs])
    ang = _angles(pos, HEAD_DIM)
    cosb = jnp.tile(jnp.cos(ang), (1, 4))
    sinb = jnp.tile(jnp.concatenate([-jnp.sin(ang), jnp.sin(ang)], axis=1), (1, 2))
    ar = _angles(pos // GRID_W, HEAD_DIM // 2)
    ac = _angles(pos % GRID_W, HEAD_DIM // 2)
    cosa = jnp.tile(jnp.concatenate([jnp.cos(ar), jnp.cos(ar), jnp.cos(ac), jnp.cos(ac)], axis=1), (1, 2))
    sina = jnp.tile(jnp.concatenate([-jnp.sin(ar), jnp.sin(ar), -jnp.sin(ac), jnp.sin(ac)], axis=1), (1, 2))
    return cosb, sinb, cosa, sina


def _head_mean_matrix():
    blk = np.kron(np.eye(LANES // HEAD_DIM), np.ones((HEAD_DIM, HEAD_DIM))) / HEAD_DIM
    return jnp.asarray(blk, BF16)


def _gain_rows(a_qn, a_kn, b_qn, b_kn):
    scale = HEAD_DIM ** -0.5
    scale_a = scale * np.log2(np.e)
    rows = [jnp.tile(b_qn, 2) * scale, jnp.tile(b_kn, 2), jnp.tile(a_qn, 2) * scale_a, jnp.tile(a_kn, 2),
            jnp.ones((LANES,), F32), jnp.full((LANES,), C_KEY_DIM ** -0.5, F32),
            jnp.ones((LANES,), F32), jnp.ones((LANES,), F32)]
    return jnp.stack(rows).astype(F32)


def _trunk(x, seqs, norm1_g, w_in, a_qn, a_kn, b_qn, b_kn, ret_dec_f, ret_dec_b, ret_norm_g,
           w_oa, w_ob, w_oc, w_out, norm2_g, ffn_w1, ffn_w3, ffn_w2,
           moe_router, moe_w1, moe_w3, moe_w2):
    depth = w_in.shape[0]
    tables = _rope_tables(seqs)
    mmat = _head_mean_matrix()
    for l in range(depth):
        proj = _inproj(x, norm1_g[l][None, :], _permute_w_in(w_in[l]))
        prepped = _prep(proj, tables, _gain_rows(a_qn[l], a_kn[l], b_qn[l], b_kn[l]), mmat)
        bqs, bks, bvs = prepped[0:3], prepped[3:6], prepped[6:9]
        aq, ak, cq, ck = prepped[9:]
        aqt = aq.T
        avt = proj[:, OFF["av"]:OFF["av"] + A_KV].T
        tabs_f = _ret_tables(ret_dec_f[l], C_CHUNK, False)
        tabs_b = _ret_tables(ret_dec_b[l], C_CHUNK, True)
        ng = ret_norm_g[l][None, :].astype(F32)
        res_a, res_c = None, None
        res_b = [None] * B_GROUPS
        row_off = 0
        for b, s in seqs:
            res_a = _mixer_a(aqt, ak, avt, res_a, row_off, b, s)
            for g, (_, dil) in enumerate(B_PATTERNS):
                res_b[g] = _mixer_b_group(bqs[g], bks[g], bvs[g], res_b[g], g, dil, row_off, b, s)
            res_c = _mixer_c(cq, ck, proj, tabs_f, tabs_b, ng, res_c, row_off, b, s)
            row_off += b * s
        x, hn = _merge(res_a[0], [r[0] for r in res_b], [r[1] for r in res_b], res_c[1], proj, x,
                       w_oa[l].astype(BF16), w_ob[l].astype(BF16), w_oc[l].astype(BF16),
                       w_out[l].astype(BF16), norm2_g[l][None, :], BF16 if l % 2 == 0 else F32)
        i = l // 2
        if l % 2 == 0:
            x = _ffn(hn, x, ffn_w1[i].astype(BF16), ffn_w3[i].astype(BF16), ffn_w2[i].astype(BF16))
        else:
            x = _moe(hn, x, moe_router[i], moe_w1[i].astype(BF16), moe_w3[i].astype(BF16),
                     moe_w2[i].astype(BF16))
    return x


def kernel(x_prompt, x_sample, norm1_g, w_in, a_qn, a_kn, b_qn, b_kn, ret_dec_f, ret_dec_b, ret_norm_g,
           w_oa, w_ob, w_oc, w_out, norm2_g, ffn_w1, ffn_w3, ffn_w2, moe_router, moe_w1, moe_w3, moe_w2):
    seqs = (x_prompt.shape[:2], x_sample.shape[:2])
    x = jnp.concatenate([x_prompt.reshape(-1, D_MODEL), x_sample.reshape(-1, D_MODEL)], axis=0)
    y = _trunk(x, seqs, norm1_g, w_in, a_qn, a_kn, b_qn, b_kn, ret_dec_f, ret_dec_b, ret_norm_g,
               w_oa, w_ob, w_oc, w_out, norm2_g, ffn_w1, ffn_w3, ffn_w2,
               moe_router, moe_w1, moe_w3, moe_w2)
    n_p = x_prompt.shape[0] * x_prompt.shape[1]
    return (y[:n_p].reshape(x_prompt.shape), y[n_p:].reshape(x_sample.shape))
```

```python
import functools

import numpy as np
import jax
import jax.numpy as jnp
from jax import lax
from jax.experimental import pallas as pl
from jax.experimental.pallas import tpu as pltpu

F32 = jnp.float32
BF16 = jnp.bfloat16

D_MODEL = 1024
GRID_W = 64
HEAD_DIM = 64
ROPE_THETA = 10000.0
EPS = 1e-6
A_HEADS = 8
A_KV_HEADS = 2
B_PATTERNS = ((128, 1), (512, 4), (2048, 16))
B_GROUPS = 3
B_HEADS = 8
C_HEADS = 4
C_KEY_DIM = 64
C_VAL_DIM = 128
N_EXPERTS = 8
N_BRANCHES = 3

A_Q = A_HEADS * HEAD_DIM
A_KV = A_KV_HEADS * HEAD_DIM
B_Q = B_GROUPS * B_HEADS * HEAD_DIM
B_KV = B_HEADS * HEAD_DIM
C_QK = C_HEADS * C_KEY_DIM
C_V = C_HEADS * C_VAL_DIM

LANES = 128
VMEM_LIMIT = 56 * 1024 * 1024

_ORIG_SPLITS = (("aq", A_Q), ("ak", A_KV), ("av", A_KV), ("bq", B_Q), ("bk", B_KV), ("bv", B_KV),
                ("cq", C_QK), ("ck", C_QK), ("cv", C_V), ("cg", C_V), ("gl", N_BRANCHES * D_MODEL))
_NEW_ORDER = ("bq", "bk", "aq", "bv", "cv", "cg", "gl", "cq", "ck", "ak", "av")
PROJ_W = 8192


def _layout():
    orig, o = {}, 0
    for name, w in _ORIG_SPLITS:
        orig[name] = (o, w)
        o += w
    new, o = {}, 0
    for name in _NEW_ORDER:
        new[name] = o
        o += orig[name][1]
    return orig, new, o


_ORIG, OFF, _USED_W = _layout()


def _cparams(sem):
    return pltpu.CompilerParams(dimension_semantics=sem, vmem_limit_bytes=VMEM_LIMIT)


def _seq_call(kern, prev, *, in_specs, out_specs, out_shape, args, **kwargs):
    n_in = len(in_specs)
    if prev is None:
        return pl.pallas_call(kern, in_specs=in_specs, out_specs=out_specs, out_shape=out_shape,
                              **kwargs)(*args)
    n_prev = len(prev)

    def chained(*refs):
        return kern(*refs[:n_in], *refs[n_in + n_prev:])

    return pl.pallas_call(
        chained,
        in_specs=list(in_specs) + [pl.BlockSpec(memory_space=pl.ANY)] * n_prev,
        out_specs=out_specs, out_shape=out_shape,
        input_output_aliases={n_in + k: k for k in range(n_prev)},
        **kwargs)(*args, *prev)


def _inproj_kernel(x_ref, g_ref, w_ref, o_ref, xn_ref):
    @pl.when(pl.program_id(1) == 0)
    def _():
        x = x_ref[...]
        ms = jnp.mean(x * x, axis=-1, keepdims=True)
        xn_ref[...] = (x * lax.rsqrt(ms + EPS) * g_ref[...]).astype(BF16)

    o_ref[...] = jnp.dot(xn_ref[...], w_ref[...], preferred_element_type=F32).astype(BF16)


def _inproj(x, g, w):
    n = x.shape[0]
    tm = min(1024, n)
    tn = 1024
    return pl.pallas_call(
        _inproj_kernel,
        grid=(n // tm, PROJ_W // tn),
        in_specs=[pl.BlockSpec((tm, D_MODEL), lambda i, j: (i, 0)),
                  pl.BlockSpec((1, D_MODEL), lambda i, j: (0, 0)),
                  pl.BlockSpec((D_MODEL, tn), lambda i, j: (0, j))],
        out_specs=pl.BlockSpec((tm, tn), lambda i, j: (i, j)),
        out_shape=jax.ShapeDtypeStruct((n, PROJ_W), BF16),
        scratch_shapes=[pltpu.VMEM((tm, D_MODEL), BF16)],
        compiler_params=_cparams(("parallel", "arbitrary")),
        name="inproj",
    )(x, g, w)


def _norm_rot(x, gain, cos, sin, mmat, half, do_norm):
    if do_norm:
        ms = jnp.dot((x * x).astype(BF16), mmat, preferred_element_type=F32)
        x = x * lax.rsqrt(ms + EPS)
    x = x * gain
    lane = lax.broadcasted_iota(jnp.int32, x.shape, 1)
    first = (lane % (2 * half)) < half
    swapped = jnp.where(first, pltpu.roll(x, LANES - half, 1), pltpu.roll(x, half, 1))
    return x * cos + swapped * sin


def _prep_kernel(bqk_ref, aq_ref, bv_ref, cq_ref, ck_ref, ak_ref,
                 cosb_ref, sinb_ref, cosa_ref, sina_ref, gain_ref, mmat_ref,
                 q0_ref, q1_ref, q2_ref, k0_ref, k1_ref, k2_ref, v0_ref, v1_ref, v2_ref,
                 oaq_ref, oak_ref, ocq_ref, ock_ref, scr_ref):
    cosb, sinb = cosb_ref[...], sinb_ref[...]
    cosa, sina = cosa_ref[...], sina_ref[...]
    mmat = mmat_ref[...]
    gains = gain_ref[...]
    tm = scr_ref.shape[0]

    def prepared(src_ref, src_off, c, gain_row, cos, sin, half, do_norm):
        x = src_ref[:, src_off + c * LANES:src_off + (c + 1) * LANES].astype(F32)
        return _norm_rot(x, gains[gain_row:gain_row + 1, :], cos, sin, mmat, half, do_norm)

    def run(src_ref, dst_ref, width, gain_row, cos, sin, half, do_norm):
        for c in range(width // LANES):
            y = prepared(src_ref, 0, c, gain_row, cos, sin, half, do_norm)
            dst_ref[:, c * LANES:(c + 1) * LANES] = y.astype(dst_ref.dtype)

    def store_dilated(y, dst_ref, dil, c):
        if dil == 1:
            dst_ref[:, c * LANES:(c + 1) * LANES] = y.astype(dst_ref.dtype)
            return
        scr_ref[...] = y
        for r in range(dil):
            rows = scr_ref[pl.ds(r, tm // dil, stride=dil), :]
            dst_ref[:, r * B_KV + c * LANES:r * B_KV + (c + 1) * LANES] = rows.astype(dst_ref.dtype)

    chunks = B_KV // LANES
    for g, (q_ref, (_, dil)) in enumerate(zip((q0_ref, q1_ref, q2_ref), B_PATTERNS)):
        for c in range(chunks):
            y = prepared(bqk_ref, g * B_KV, c, 0, cosb, sinb, HEAD_DIM // 2, True)
            store_dilated(y, q_ref, dil, c)
    for c in range(chunks):
        yk = prepared(bqk_ref, B_Q, c, 1, cosb, sinb, HEAD_DIM // 2, True)
        yv = bv_ref[:, c * LANES:(c + 1) * LANES].astype(F32)
        for (_, dil), k_ref, v_ref in zip(B_PATTERNS, (k0_ref, k1_ref, k2_ref), (v0_ref, v1_ref, v2_ref)):
            store_dilated(yk, k_ref, dil, c)
            store_dilated(yv, v_ref, dil, c)
    run(aq_ref, oaq_ref, A_Q, 2, cosa, sina, HEAD_DIM // 4, True)
    run(ak_ref, oak_ref, A_KV, 3, cosa, sina, HEAD_DIM // 4, True)
    run(cq_ref, ocq_ref, C_QK, 4, cosb, sinb, C_KEY_DIM // 2, False)
    run(ck_ref, ock_ref, C_QK, 5, cosb, sinb, C_KEY_DIM // 2, False)


def _prep(proj, tables, gains, mmat):
    n = proj.shape[0]
    tm = min(512, n)
    cosb, sinb, cosa, sina = tables

    def col(width, name):
        idx = OFF[name] // width
        return pl.BlockSpec((tm, width), lambda i: (i, idx))

    def tab():
        return pl.BlockSpec((tm, LANES), lambda i: (i, 0))

    dils = [dil for _, dil in B_PATTERNS]
    shapes = [(n // dil, dil * B_KV) for dil in dils] * 3 + [(n, A_Q), (n, A_KV), (n, C_QK), (n, C_QK)]
    blocks = [(tm // dil, dil * B_KV) for dil in dils] * 3 + [(tm, A_Q), (tm, A_KV), (tm, C_QK), (tm, C_QK)]
    return pl.pallas_call(
        _prep_kernel,
        grid=(n // tm,),
        in_specs=[col(B_Q + B_KV, "bq"), col(A_Q, "aq"), col(B_KV, "bv"), col(C_QK, "cq"),
                  col(C_QK, "ck"), col(A_KV, "ak"), tab(), tab(), tab(), tab(),
                  pl.BlockSpec((8, LANES), lambda i: (0, 0)),
                  pl.BlockSpec((LANES, LANES), lambda i: (0, 0))],
        out_specs=[pl.BlockSpec(blk, lambda i: (i, 0)) for blk in blocks],
        out_shape=[jax.ShapeDtypeStruct(shp, BF16) for shp in shapes],
        scratch_shapes=[pltpu.VMEM((tm, LANES), F32)],
        compiler_params=_cparams(("parallel",)),
        name="prep",
    )(proj, proj, proj, proj, proj, proj, cosb, sinb, cosa, sina, gains, mmat)


A_ONES_ROWS = 16


def _attn_a_kernel(qt_ref, k0_ref, kn_ref, vt_ref, o_ref, qp_ref, m_ref, acc_ref, sa_ref, sb_ref, *, bq, nk):
    ik = pl.program_id(2)
    rep = A_HEADS // A_KV_HEADS
    kv_heads = range(A_KV_HEADS)

    @pl.when(ik == 0)
    def _():
        qp_ref[...] = jnp.zeros(qp_ref.shape, BF16)
        for g in kv_heads:
            for r in range(rep):
                h = g * rep + r
                qp_ref[g, g * HEAD_DIM:(g + 1) * HEAD_DIM, r * bq:(r + 1) * bq] = (
                    qt_ref[h * HEAD_DIM:(h + 1) * HEAD_DIM, :])
        m_ref[...] = jnp.full(m_ref.shape, -jnp.inf, F32)
        acc_ref[...] = jnp.zeros(acc_ref.shape, F32)
        for g in kv_heads:
            sa_ref[g] = jnp.dot(k0_ref[...], qp_ref[g], preferred_element_type=F32)

    def step(cur_ref, nxt_ref):
        kn = kn_ref[...]
        ones = jnp.ones((A_ONES_ROWS, kn.shape[0]), BF16)
        for g in kv_heads:
            v_aug = jnp.concatenate([vt_ref[g * HEAD_DIM:(g + 1) * HEAD_DIM, :], ones], axis=0)
            for r in range(rep):
                cols = slice(r * bq, (r + 1) * bq)
                nxt_ref[g, :, cols] = jnp.dot(kn, qp_ref[g, :, cols], preferred_element_type=F32)
                s = cur_ref[g, :, cols]
                m_prev = m_ref[g, :, cols]
                m_new = jnp.maximum(m_prev, jnp.max(s, axis=0, keepdims=True))
                p = jnp.exp2(s - m_new).astype(BF16)
                alpha = jnp.exp2(m_prev - m_new)
                pv = jnp.dot(v_aug, p, preferred_element_type=F32)
                acc_ref[g, :, cols] = alpha * acc_ref[g, :, cols] + pv
                m_ref[g, :, cols] = m_new

    @pl.when(ik % 2 == 0)
    def _():
        step(sa_ref, sb_ref)

    @pl.when(ik % 2 == 1)
    def _():
        step(sb_ref, sa_ref)

    @pl.when(ik == nk - 1)
    def _():
        for g in range(A_KV_HEADS):
            acc = acc_ref[g]
            o = acc[:HEAD_DIM] / acc[HEAD_DIM:HEAD_DIM + 1]
            for r in range(rep):
                h = g * rep + r
                o_ref[h * HEAD_DIM:(h + 1) * HEAD_DIM, :] = o[:, r * bq:(r + 1) * bq].astype(o_ref.dtype)


def _mixer_a(aqt, ak, avt, prev, row_off, b, s):
    n = ak.shape[0]
    bq = min(256, s)
    bk = min(512, s)
    nq, nk = s // bq, s // bk
    rep = A_HEADS // A_KV_HEADS
    q0, k0 = row_off // bq, row_off // bk
    kern = functools.partial(_attn_a_kernel, bq=bq, nk=nk)
    return _seq_call(
        kern, prev,
        grid=(b, nq, nk),
        in_specs=[pl.BlockSpec((A_Q, bq), lambda ib, iq, ik: (0, q0 + ib * nq + iq)),
                  pl.BlockSpec((bk, A_KV), lambda ib, iq, ik: (k0 + ib * nk, 0)),
                  pl.BlockSpec((bk, A_KV), lambda ib, iq, ik: (k0 + ib * nk + jnp.minimum(ik + 1, nk - 1), 0)),
                  pl.BlockSpec((A_KV, bk), lambda ib, iq, ik: (0, k0 + ib * nk + ik))],
        out_specs=[pl.BlockSpec((A_Q, bq), lambda ib, iq, ik: (0, q0 + ib * nq + iq))],
        out_shape=[jax.ShapeDtypeStruct((A_Q, n), BF16)],
        scratch_shapes=[pltpu.VMEM((A_KV_HEADS, A_KV, rep * bq), BF16),
                        pltpu.VMEM((A_KV_HEADS, 1, rep * bq), F32),
                        pltpu.VMEM((A_KV_HEADS, HEAD_DIM + A_ONES_ROWS, rep * bq), F32),
                        pltpu.VMEM((A_KV_HEADS, bk, rep * bq), F32),
                        pltpu.VMEM((A_KV_HEADS, bk, rep * bq), F32)],
        compiler_params=_cparams(("parallel", "parallel", "arbitrary")),
        name="mixer_a",
        args=(aqt, ak, ak, avt))


B_BLOCK_Q = 128
B_HALO = 64


def _attn_b_kernel(q_ref, k0_ref, k1_ref, k2_ref, k3_ref, v0_ref, v1_ref, v2_ref, v3_ref,
                   o_ref, lse_ref, *, u_len):
    i = pl.program_id(2)
    kcat = jnp.concatenate([k0_ref[...], k1_ref[...], k2_ref[...], k3_ref[...]], axis=0)
    vcat = jnp.concatenate([v0_ref[...], v1_ref[...], v2_ref[...], v3_ref[...]], axis=0)
    nkeys = B_BLOCK_Q + 2 * B_HALO
    a = lax.broadcasted_iota(jnp.int32, (B_BLOCK_Q, nkeys), 0)
    c = lax.broadcasted_iota(jnp.int32, (B_BLOCK_Q, nkeys), 1)
    key_pos = i * B_BLOCK_Q - B_HALO + c
    valid = (c >= a) & (c <= a + 2 * B_HALO) & (key_pos >= 0) & (key_pos < u_len)
    left = lax.broadcasted_iota(jnp.int32, (B_BLOCK_Q, LANES), 1) < HEAD_DIM
    zero = jnp.zeros((B_BLOCK_Q, LANES), BF16)
    for pair in range(B_HEADS // 2):
        sl = slice(pair * LANES, (pair + 1) * LANES)
        q2, k2, v2 = q_ref[:, sl], kcat[:, sl], vcat[:, sl]
        outs, lses = [], []
        for own in (left, jnp.logical_not(left)):
            s = lax.dot_general(jnp.where(own, q2, zero), k2, (((1,), (1,)), ((), ())),
                                preferred_element_type=F32)
            s = jnp.where(valid, s, -1e30)
            m = jnp.max(s, axis=1, keepdims=True)
            e = jnp.exp(s - m)
            den = jnp.sum(e, axis=1, keepdims=True)
            outs.append(jnp.dot(e.astype(BF16), v2, preferred_element_type=F32) / den)
            lses.append(m + jnp.log(den))
        o_ref[:, sl] = jnp.where(left, outs[0], outs[1]).astype(o_ref.dtype)
        lse_ref[:, sl] = jnp.where(left, lses[0], lses[1])


def _mixer_b_group(qd, kd, vd, prev, g, dil, row_off, b, s):
    u_len = s // dil
    nq = u_len // B_BLOCK_Q
    nkb = u_len // B_HALO
    q_rows0 = row_off // dil // B_BLOCK_Q
    k_rows0 = row_off // dil // B_HALO

    def kspec(t):
        def imap(ib, r, i):
            blk = jnp.clip(2 * i - 1 + t, 0, nkb - 1)
            return (k_rows0 + ib * nkb + blk, r)
        return pl.BlockSpec((B_HALO, B_KV), imap)

    def qspec():
        return pl.BlockSpec((B_BLOCK_Q, B_KV), lambda ib, r, i: (q_rows0 + ib * nq + i, r))

    kern = functools.partial(_attn_b_kernel, u_len=u_len)
    return _seq_call(
        kern, prev,
        grid=(b, dil, nq),
        in_specs=[qspec()] + [kspec(t) for t in range(4)] + [kspec(t) for t in range(4)],
        out_specs=[qspec(), qspec()],
        out_shape=[jax.ShapeDtypeStruct(qd.shape, BF16), jax.ShapeDtypeStruct(qd.shape, F32)],
        compiler_params=_cparams(("parallel", "parallel", "parallel")),
        name=f"mixer_b{g}",
        args=(qd, kd, kd, kd, kd, vd, vd, vd, vd))


def _ret_chunk(q, k, v, dec_ref, xi_ref, zeta_ref, cdec_ref, r_ref):
    outs = []
    for h in range(C_HEADS):
        qh = q[:, h * C_KEY_DIM:(h + 1) * C_KEY_DIM]
        kh = k[:, h * C_KEY_DIM:(h + 1) * C_KEY_DIM]
        vh = v[:, h * C_VAL_DIM:(h + 1) * C_VAL_DIM]
        att = lax.dot_general(qh, kh, (((1,), (1,)), ((), ())), preferred_element_type=F32) * dec_ref[h]
        inner = jnp.dot(att.astype(BF16), vh, preferred_element_type=F32)
        r = r_ref[h]
        cross = jnp.dot(qh, r.astype(BF16), preferred_element_type=F32) * xi_ref[h]
        kz = (kh.astype(F32) * zeta_ref[h]).astype(BF16)
        r_ref[h] = r * cdec_ref[h] + lax.dot_general(kz, vh, (((0,), (0,)), ((), ())),
                                                     preferred_element_type=F32)
        outs.append(inner + cross)
    return jnp.concatenate(outs, axis=1)


def _ret_fwd_kernel(q_ref, k_ref, v_ref, dec_ref, xi_ref, zeta_ref, cdec_ref, o_ref, r_ref):
    @pl.when(pl.program_id(1) == 0)
    def _():
        r_ref[...] = jnp.zeros(r_ref.shape, F32)

    o_ref[...] = _ret_chunk(q_ref[...], k_ref[...], v_ref[...], dec_ref, xi_ref, zeta_ref, cdec_ref, r_ref)


def _ret_bwd_kernel(q_ref, k_ref, v_ref, dec_ref, xi_ref, zeta_ref, cdec_ref, of_ref, gate_ref, ng_ref,
                    o_ref, r_ref):
    @pl.when(pl.program_id(1) == 0)
    def _():
        r_ref[...] = jnp.zeros(r_ref.shape, F32)

    o = of_ref[...] + _ret_chunk(q_ref[...], k_ref[...], v_ref[...], dec_ref, xi_ref, zeta_ref, cdec_ref, r_ref)
    ng = ng_ref[...]
    gate = gate_ref[...].astype(F32)
    for h in range(C_HEADS):
        sl = slice(h * C_VAL_DIM, (h + 1) * C_VAL_DIM)
        oh = o[:, sl]
        mu = jnp.mean(oh, axis=1, keepdims=True)
        var = jnp.mean(jnp.square(oh - mu), axis=1, keepdims=True)
        y = (oh - mu) * lax.rsqrt(var + EPS) * ng[:, sl]
        gh = gate[:, sl]
        o_ref[:, sl] = (gh * jax.nn.sigmoid(gh) * y).astype(o_ref.dtype)


def _ret_tables(dec_param, chunk, strict):
    log_g = -jnp.exp(dec_param.astype(F32))
    j = jnp.arange(chunk, dtype=F32)
    lg = log_g[:, None, None]
    if strict:
        diff = j[None, :] - j[:, None]
        dec = jnp.where((diff > 0)[None], jnp.exp(jnp.maximum(diff, 0.0)[None] * lg), 0.0)
        xi = jnp.exp((chunk - j)[None, :, None] * lg)
        zeta = jnp.exp(j[None, :, None] * lg)
    else:
        diff = j[:, None] - j[None, :]
        dec = jnp.where((diff >= 0)[None], jnp.exp(jnp.maximum(diff, 0.0)[None] * lg), 0.0)
        xi = jnp.exp((j + 1.0)[None, :, None] * lg)
        zeta = jnp.exp((chunk - 1.0 - j)[None, :, None] * lg)
    xi = jnp.broadcast_to(xi, (C_HEADS, chunk, C_VAL_DIM))
    zeta = jnp.broadcast_to(zeta, (C_HEADS, chunk, C_KEY_DIM))
    cdec = jnp.broadcast_to(jnp.exp(chunk * log_g)[:, None, None], (C_HEADS, C_KEY_DIM, C_VAL_DIM))
    return dec, xi, zeta, cdec


C_CHUNK = 128


def _mixer_c(cq, ck, proj, tabs_f, tabs_b, norm_g, prev, row_off, b, s):
    n = cq.shape[0]
    prev_f, prev_o = (None, None) if prev is None else ([prev[0]], [prev[1]])
    nc = s // C_CHUNK
    r0 = row_off // C_CHUNK
    v_col = OFF["cv"] // C_V
    g_col = OFF["cg"] // C_V

    def fwd_rows(ib, c):
        return r0 + ib * nc + c

    def bwd_rows(ib, c):
        return r0 + ib * nc + (nc - 1 - c)

    def specs(rows):
        def full(shape):
            return pl.BlockSpec(shape, lambda ib, c: (0,) * len(shape))
        return [pl.BlockSpec((C_CHUNK, C_QK), lambda ib, c: (rows(ib, c), 0)),
                pl.BlockSpec((C_CHUNK, C_QK), lambda ib, c: (rows(ib, c), 0)),
                pl.BlockSpec((C_CHUNK, C_V), lambda ib, c: (rows(ib, c), v_col)),
                full((C_HEADS, C_CHUNK, C_CHUNK)), full((C_HEADS, C_CHUNK, C_VAL_DIM)),
                full((C_HEADS, C_CHUNK, C_KEY_DIM)), full((C_HEADS, C_KEY_DIM, C_VAL_DIM))]

    scratch = [pltpu.VMEM((C_HEADS, C_KEY_DIM, C_VAL_DIM), F32)]
    o_f, = _seq_call(
        _ret_fwd_kernel, prev_f,
        grid=(b, nc),
        in_specs=specs(fwd_rows),
        out_specs=[pl.BlockSpec((C_CHUNK, C_V), lambda ib, c: (fwd_rows(ib, c), 0))],
        out_shape=[jax.ShapeDtypeStruct((n, C_V), F32)],
        scratch_shapes=scratch,
        compiler_params=_cparams(("parallel", "arbitrary")),
        name="ret_fwd",
        args=(cq, ck, proj, *tabs_f))
    o_c, = _seq_call(
        _ret_bwd_kernel, prev_o,
        grid=(b, nc),
        in_specs=specs(bwd_rows) + [
            pl.BlockSpec((C_CHUNK, C_V), lambda ib, c: (bwd_rows(ib, c), 0)),
            pl.BlockSpec((C_CHUNK, C_V), lambda ib, c: (bwd_rows(ib, c), g_col)),
            pl.BlockSpec((1, C_V), lambda ib, c: (0, 0))],
        out_specs=[pl.BlockSpec((C_CHUNK, C_V), lambda ib, c: (bwd_rows(ib, c), 0))],
        out_shape=[jax.ShapeDtypeStruct((n, C_V), BF16)],
        scratch_shapes=scratch,
        compiler_params=_cparams(("parallel", "arbitrary")),
        name="ret_bwd",
        args=(cq, ck, proj, *tabs_b, o_f, proj, norm_g))
    return [o_f, o_c]


def _merge_kernel(oa_ref, ob0_ref, ob1_ref, ob2_ref, l0_ref, l1_ref, l2_ref, oc_ref,
                  ga_ref, gb_ref, gc_ref, x_ref, woa_ref, wob_ref, woc_ref, wout_ref, n2_ref,
                  xo_ref, hn_ref, scr_ref):
    tm = x_ref.shape[0]

    def natural(src_ref, dil, slot):
        if dil == 1:
            return src_ref[...].astype(F32)
        chunks = B_KV // LANES
        for r in range(dil):
            for c in range(chunks):
                col = r * B_KV + c * LANES
                scr_ref[slot, c, pl.ds(r, tm // dil, stride=dil), :] = src_ref[:, col:col + LANES].astype(F32)
        return jnp.concatenate([scr_ref[slot, c] for c in range(chunks)], axis=1)

    dils = [dil for _, dil in B_PATTERNS]
    l0, l1, l2 = [natural(ref, dil, 2 * g) for g, (ref, dil) in enumerate(zip((l0_ref, l1_ref, l2_ref), dils))]
    o0, o1, o2 = [natural(ref, dil, 2 * g + 1)
                  for g, (ref, dil) in enumerate(zip((ob0_ref, ob1_ref, ob2_ref), dils))]
    mx = jnp.maximum(jnp.maximum(l0, l1), l2)
    e0, e1, e2 = jnp.exp(l0 - mx), jnp.exp(l1 - mx), jnp.exp(l2 - mx)
    ob = (e0 * o0 + e1 * o1 + e2 * o2) / (e0 + e1 + e2)

    def branch(o, w_ref, gate_ref):
        y = jnp.dot(o, w_ref[...], preferred_element_type=F32)
        return jax.nn.sigmoid(gate_ref[...].astype(F32)) * y

    oa = jnp.transpose(oa_ref[...].astype(F32)).astype(BF16)
    merged = (branch(oa, woa_ref, ga_ref) + branch(ob.astype(BF16), wob_ref, gb_ref)
              + branch(oc_ref[...], woc_ref, gc_ref))
    x = x_ref[...] + jnp.dot(merged.astype(BF16), wout_ref[...], preferred_element_type=F32)
    xo_ref[...] = x
    ms = jnp.mean(x * x, axis=-1, keepdims=True)
    hn_ref[...] = (x * lax.rsqrt(ms + EPS) * n2_ref[...]).astype(hn_ref.dtype)


def _merge(o_a, o_b, lse_b, o_c, proj, x, w_oa, w_ob, w_oc, w_out, n2, hn_dtype):
    n = x.shape[0]
    tm = min(256, n)
    gl0 = OFF["gl"] // D_MODEL

    def rows(width):
        return pl.BlockSpec((tm, width), lambda i: (i, 0))

    def gate(k):
        return pl.BlockSpec((tm, D_MODEL), lambda i: (i, gl0 + k))

    def full(r, c):
        return pl.BlockSpec((r, c), lambda i: (0, 0))

    grouped = [pl.BlockSpec((tm // dil, dil * B_KV), lambda i: (i, 0)) for _, dil in B_PATTERNS]
    return pl.pallas_call(
        _merge_kernel,
        grid=(n // tm,),
        in_specs=[pl.BlockSpec((A_Q, tm), lambda i: (0, i))] + grouped + grouped
                 + [rows(C_V), gate(0), gate(1), gate(2), rows(D_MODEL),
                    full(A_Q, D_MODEL), full(B_KV, D_MODEL), full(C_V, D_MODEL), full(D_MODEL, D_MODEL),
                    full(1, D_MODEL)],
        out_specs=[rows(D_MODEL), rows(D_MODEL)],
        out_shape=[jax.ShapeDtypeStruct((n, D_MODEL), F32), jax.ShapeDtypeStruct((n, D_MODEL), hn_dtype)],
        scratch_shapes=[pltpu.VMEM((2 * B_GROUPS, B_KV // LANES, tm, LANES), F32)],
        compiler_params=_cparams(("parallel",)),
        name="merge_out",
    )(o_a, *o_b, *lse_b, o_c, proj, proj, proj, x, w_oa, w_ob, w_oc, w_out, n2)


def _ffn_kernel(hn_ref, x_ref, w1_ref, w3_ref, w2_ref, o_ref, acc_ref, *, nf):
    j = pl.program_id(1)
    hn = hn_ref[...]
    a = jnp.dot(hn, w1_ref[...], preferred_element_type=F32)
    g = jnp.dot(hn, w3_ref[...], preferred_element_type=F32)
    h = (a * jax.nn.sigmoid(a) * g).astype(BF16)
    y = jnp.dot(h, w2_ref[...], preferred_element_type=F32)

    @pl.when(j == 0)
    def _():
        acc_ref[...] = x_ref[...] + y

    @pl.when(j > 0)
    def _():
        acc_ref[...] += y

    @pl.when(j == nf - 1)
    def _():
        o_ref[...] = acc_ref[...]


def _ffn(hn, x, w1, w3, w2):
    n = x.shape[0]
    d_ff = w1.shape[1]
    tm = min(512, n)
    tf = d_ff // 2
    nf = d_ff // tf
    return pl.pallas_call(
        functools.partial(_ffn_kernel, nf=nf),
        grid=(n // tm, nf),
        in_specs=[pl.BlockSpec((tm, D_MODEL), lambda i, j: (i, 0)),
                  pl.BlockSpec((tm, D_MODEL), lambda i, j: (i, 0)),
                  pl.BlockSpec((D_MODEL, tf), lambda i, j: (0, j)),
                  pl.BlockSpec((D_MODEL, tf), lambda i, j: (0, j)),
                  pl.BlockSpec((tf, D_MODEL), lambda i, j: (j, 0))],
        out_specs=pl.BlockSpec((tm, D_MODEL), lambda i, j: (i, 0)),
        out_shape=jax.ShapeDtypeStruct((n, D_MODEL), F32),
        scratch_shapes=[pltpu.VMEM((tm, D_MODEL), F32)],
        compiler_params=_cparams(("parallel", "arbitrary")),
        name="ffn",
    )(hn, x, w1, w3, w2)


TOP_K = 2
MOE_TM = 512


def _router_kernel(hn_ref, wr_ref, idx_ref, wts_ref):
    logits = jnp.dot(hn_ref[...], wr_ref[...], preferred_element_type=F32,
                     precision=lax.Precision.HIGHEST)
    col = lax.broadcasted_iota(jnp.int32, logits.shape, 1)
    m1 = jnp.max(logits, axis=1, keepdims=True)
    i1 = jnp.min(jnp.where(logits == m1, col, N_EXPERTS), axis=1, keepdims=True)
    rest = jnp.where(col == i1, -jnp.inf, logits)
    m2 = jnp.max(rest, axis=1, keepdims=True)
    i2 = jnp.min(jnp.where(rest == m2, col, N_EXPERTS), axis=1, keepdims=True)
    e2 = jnp.exp(m2 - m1)
    w1 = 1.0 / (1.0 + e2)
    idx_ref[...] = jnp.concatenate([i1, i2], axis=1)
    wts_ref[...] = jnp.concatenate([w1, e2 * w1], axis=1)


def _router(hn, wr):
    n = hn.shape[0]
    tm = min(1024, n)
    return pl.pallas_call(
        _router_kernel,
        grid=(n // tm,),
        in_specs=[pl.BlockSpec((tm, D_MODEL), lambda i: (i, 0)),
                  pl.BlockSpec((D_MODEL, N_EXPERTS), lambda i: (0, 0))],
        out_specs=[pl.BlockSpec((tm, TOP_K), lambda i: (i, 0)), pl.BlockSpec((tm, TOP_K), lambda i: (i, 0))],
        out_shape=[jax.ShapeDtypeStruct((n, TOP_K), jnp.int32), jax.ShapeDtypeStruct((n, TOP_K), F32)],
        compiler_params=_cparams(("parallel",)),
        name="router",
    )(hn, wr)


def _route(idx, tm):
    n = idx.shape[0]
    e_flat = idx.reshape(-1)
    onehot = (e_flat[:, None] == jnp.arange(N_EXPERTS, dtype=jnp.int32)[None, :]).astype(jnp.int32)
    csum = jnp.cumsum(onehot, axis=0)
    rank = jnp.sum((csum - onehot) * onehot, axis=1)
    gsz = ((csum[-1] + tm - 1) // tm) * tm
    gend = jnp.cumsum(gsz)
    pos = (gend - gsz)[e_flat] + rank
    n_rows = n * TOP_K + N_EXPERTS * tm
    n_tiles = n_rows // tm
    row_token = jnp.zeros((n_rows,), jnp.int32).at[pos].set(jnp.arange(n * TOP_K, dtype=jnp.int32) // TOP_K)
    tile_start = jnp.arange(n_tiles, dtype=jnp.int32) * tm
    tile_expert = jnp.minimum(jnp.sum((tile_start[:, None] >= gend[None, :]).astype(jnp.int32), axis=1),
                              N_EXPERTS - 1)
    n_used = (gend[-1] // tm).astype(jnp.int32).reshape(1)
    return pos.reshape(n, TOP_K), row_token.reshape(n_tiles, 1, tm), tile_expert, n_used


def _row_copy(src_ref, src_row, dst_ref, dst_row, sem):
    return pltpu.make_async_copy(src_ref.at[pl.ds(src_row, 1)], dst_ref.at[pl.ds(dst_row, 1)], sem)


def _expert_kernel(te_ref, nu_ref, idx_ref, nxt_ref, src_ref, w1_ref, w3_ref, w2_ref, y_ref,
                   xs_ref, xb_ref, acc_ref, sem, *, nf, tm):
    i = pl.program_id(0)
    j = pl.program_id(1)
    n_used = nu_ref[0]
    active = i < n_used
    slot = i % 2
    share = tm // nf

    @pl.when((i == 0) & (j == 0))
    def _():
        def issue(r, carry):
            _row_copy(src_ref, idx_ref[0, 0, r], xs_ref.at[0], r, sem.at[0]).start()
            return carry
        lax.fori_loop(0, tm, issue, 0, unroll=8)

    @pl.when((j == 0) & (i <= n_used))
    def _():
        pltpu.make_async_copy(src_ref.at[pl.ds(0, tm)], xs_ref.at[slot], sem.at[slot]).wait()

    @pl.when(active)
    def _():
        @pl.when(j == 0)
        def _():
            xb_ref[...] = xs_ref[slot].astype(BF16)

        for r in range(share):
            row = j * share + r
            _row_copy(src_ref, nxt_ref[0, 0, row], xs_ref.at[1 - slot], row, sem.at[1 - slot]).start()

        xb = xb_ref[...]
        a = jnp.dot(xb, w1_ref[0], preferred_element_type=F32)
        g = jnp.dot(xb, w3_ref[0], preferred_element_type=F32)
        h = (a * jax.nn.sigmoid(a) * g).astype(BF16)
        y = jnp.dot(h, w2_ref[0], preferred_element_type=F32)

        @pl.when(j == 0)
        def _():
            acc_ref[...] = y

        @pl.when(j > 0)
        def _():
            acc_ref[...] += y

        @pl.when(j == nf - 1)
        def _():
            y_ref[...] = acc_ref[...]

    @pl.when(jnp.logical_not(active) & (j == nf - 1))
    def _():
        y_ref[...] = jnp.zeros(y_ref.shape, y_ref.dtype)


def _experts(src, row_token, tile_expert, n_used, w1, w3, w2):
    nt, _, tm = row_token.shape
    d_ff = w1.shape[2]
    nf = 2
    tf = d_ff // nf

    def jcol(i, j, nu):
        return jnp.where(i < nu[0], j, nf - 1)

    grid_spec = pltpu.PrefetchScalarGridSpec(
        num_scalar_prefetch=2,
        grid=(nt, nf),
        in_specs=[pl.BlockSpec((1, 1, tm), lambda i, j, te, nu: (i, 0, 0), memory_space=pltpu.SMEM),
                  pl.BlockSpec((1, 1, tm), lambda i, j, te, nu: (jnp.minimum(i + 1, nt - 1), 0, 0),
                               memory_space=pltpu.SMEM),
                  pl.BlockSpec(memory_space=pl.ANY),
                  pl.BlockSpec((1, D_MODEL, tf), lambda i, j, te, nu: (te[i], 0, jcol(i, j, nu))),
                  pl.BlockSpec((1, D_MODEL, tf), lambda i, j, te, nu: (te[i], 0, jcol(i, j, nu))),
                  pl.BlockSpec((1, tf, D_MODEL), lambda i, j, te, nu: (te[i], jcol(i, j, nu), 0))],
        out_specs=pl.BlockSpec((tm, D_MODEL), lambda i, j, te, nu: (i, 0)),
        scratch_shapes=[pltpu.VMEM((2, tm, D_MODEL), src.dtype), pltpu.VMEM((tm, D_MODEL), BF16),
                        pltpu.VMEM((tm, D_MODEL), F32), pltpu.SemaphoreType.DMA((2,))],
    )
    return pl.pallas_call(
        functools.partial(_expert_kernel, nf=nf, tm=tm),
        grid_spec=grid_spec,
        out_shape=jax.ShapeDtypeStruct((nt * tm, D_MODEL), F32),
        compiler_params=_cparams(("arbitrary", "arbitrary")),
        name="moe_experts",
    )(tile_expert, n_used, row_token, row_token, src, w1, w3, w2)


def _combine_kernel(pos_ref, nxt_ref, wts_ref, x_ref, y_ref, o_ref, g_ref, sem, *, tm, nt):
    i = pl.program_id(0)
    slot = i % 2

    def request(p_ref, dst_slot):
        def issue(r, carry):
            for k in range(TOP_K):
                _row_copy(y_ref, p_ref[0, 0, TOP_K * r + k], g_ref.at[dst_slot, k], r, sem.at[dst_slot]).start()
            return carry
        lax.fori_loop(0, tm, issue, 0, unroll=8)

    @pl.when(i == 0)
    def _():
        request(pos_ref, 0)

    @pl.when(i + 1 < nt)
    def _():
        request(nxt_ref, 1 - slot)

    for k in range(TOP_K):
        pltpu.make_async_copy(y_ref.at[pl.ds(0, tm)], g_ref.at[slot, k], sem.at[slot]).wait()
    w = wts_ref[...]
    o_ref[...] = x_ref[...] + w[:, 0:1] * g_ref[slot, 0] + w[:, 1:2] * g_ref[slot, 1]


def _combine(pos, wts, x, y):
    n = x.shape[0]
    tm = min(256, n)
    nt = n // tm
    pos3 = pos.reshape(nt, 1, TOP_K * tm)
    return pl.pallas_call(
        functools.partial(_combine_kernel, tm=tm, nt=nt),
        grid=(nt,),
        in_specs=[pl.BlockSpec((1, 1, TOP_K * tm), lambda i: (i, 0, 0), memory_space=pltpu.SMEM),
                  pl.BlockSpec((1, 1, TOP_K * tm), lambda i: (jnp.minimum(i + 1, nt - 1), 0, 0),
                               memory_space=pltpu.SMEM),
                  pl.BlockSpec((tm, TOP_K), lambda i: (i, 0)),
                  pl.BlockSpec((tm, D_MODEL), lambda i: (i, 0)),
                  pl.BlockSpec(memory_space=pl.ANY)],
        out_specs=pl.BlockSpec((tm, D_MODEL), lambda i: (i, 0)),
        out_shape=jax.ShapeDtypeStruct((n, D_MODEL), F32),
        scratch_shapes=[pltpu.VMEM((2, TOP_K, tm, D_MODEL), F32), pltpu.SemaphoreType.DMA((2,))],
        compiler_params=_cparams(("arbitrary",)),
        name="moe_combine",
    )(pos3, pos3, wts, x, y)


def _moe(hn, x, wr, w1, w3, w2):
    idx, wts = _router(hn, wr)
    pos, row_token, tile_expert, n_used = _route(idx, MOE_TM)
    y = _experts(hn, row_token, tile_expert, n_used, w1, w3, w2)
    return _combine(pos, wts, x, y)


def _permute_w_in(w):
    cols = [w[:, _ORIG[name][0]:_ORIG[name][0] + _ORIG[name][1]] for name in _NEW_ORDER]
    cols.append(jnp.zeros((w.shape[0], PROJ_W - _USED_W), w.dtype))
    return jnp.concatenate(cols, axis=1).astype(BF16)


def _angles(pos, dim):
    inv = ROPE_THETA ** (-jnp.arange(0, dim, 2, dtype=F32) / dim)
    return pos.astype(F32)[:, None] * inv[None, :]


def _rope_tables(seqs):
    pos = jnp.concatenate([jnp.tile(jnp.arange(s), b) for b, s in seqs])
    ang = _angles(pos, HEAD_DIM)
    cosb = jnp.tile(jnp.cos(ang), (1, 4))
    sinb = jnp.tile(jnp.concatenate([-jnp.sin(ang), jnp.sin(ang)], axis=1), (1, 2))
    ar = _angles(pos // GRID_W, HEAD_DIM // 2)
    ac = _angles(pos % GRID_W, HEAD_DIM // 2)
    cosa = jnp.tile(jnp.concatenate([jnp.cos(ar), jnp.cos(ar), jnp.cos(ac), jnp.cos(ac)], axis=1), (1, 2))
    sina = jnp.tile(jnp.concatenate([-jnp.sin(ar), jnp.sin(ar), -jnp.sin(ac), jnp.sin(ac)], axis=1), (1, 2))
    return cosb, sinb, cosa, sina


def _head_mean_matrix():
    blk = np.kron(np.eye(LANES // HEAD_DIM), np.ones((HEAD_DIM, HEAD_DIM))) / HEAD_DIM
    return jnp.asarray(blk, BF16)


def _gain_rows(a_qn, a_kn, b_qn, b_kn):
    scale = HEAD_DIM ** -0.5
    scale_a = scale * np.log2(np.e)
    rows = [jnp.tile(b_qn, 2) * scale, jnp.tile(b_kn, 2), jnp.tile(a_qn, 2) * scale_a, jnp.tile(a_kn, 2),
            jnp.ones((LANES,), F32), jnp.full((LANES,), C_KEY_DIM ** -0.5, F32),
            jnp.ones((LANES,), F32), jnp.ones((LANES,), F32)]
    return jnp.stack(rows).astype(F32)


def _trunk(x, seqs, norm1_g, w_in, a_qn, a_kn, b_qn, b_kn, ret_dec_f, ret_dec_b, ret_norm_g,
           w_oa, w_ob, w_oc, w_out, norm2_g, ffn_w1, ffn_w3, ffn_w2,
           moe_router, moe_w1, moe_w3, moe_w2):
    depth = w_in.shape[0]
    tables = _rope_tables(seqs)
    mmat = _head_mean_matrix()
    for l in range(depth):
        proj = _inproj(x, norm1_g[l][None, :], _permute_w_in(w_in[l]))
        prepped = _prep(proj, tables, _gain_rows(a_qn[l], a_kn[l], b_qn[l], b_kn[l]), mmat)
        bqs, bks, bvs = prepped[0:3], prepped[3:6], prepped[6:9]
        aq, ak, cq, ck = prepped[9:]
        aqt = aq.T
        avt = proj[:, OFF["av"]:OFF["av"] + A_KV].T
        tabs_f = _ret_tables(ret_dec_f[l], C_CHUNK, False)
        tabs_b = _ret_tables(ret_dec_b[l], C_CHUNK, True)
        ng = ret_norm_g[l][None, :].astype(F32)
        res_a, res_c = None, None
        res_b = [None] * B_GROUPS
        row_off = 0
        for b, s in seqs:
            res_a = _mixer_a(aqt, ak, avt, res_a, row_off, b, s)
            for g, (_, dil) in enumerate(B_PATTERNS):
                res_b[g] = _mixer_b_group(bqs[g], bks[g], bvs[g], res_b[g], g, dil, row_off, b, s)
            res_c = _mixer_c(cq, ck, proj, tabs_f, tabs_b, ng, res_c, row_off, b, s)
            row_off += b * s
        x, hn = _merge(res_a[0], [r[0] for r in res_b], [r[1] for r in res_b], res_c[1], proj, x,
                       w_oa[l].astype(BF16), w_ob[l].astype(BF16), w_oc[l].astype(BF16),
                       w_out[l].astype(BF16), norm2_g[l][None, :], BF16 if l % 2 == 0 else F32)
        i = l // 2
        if l % 2 == 0:
            x = _ffn(hn, x, ffn_w1[i].astype(BF16), ffn_w3[i].astype(BF16), ffn_w2[i].astype(BF16))
        else:
            x = _moe(hn, x, moe_router[i], moe_w1[i].astype(BF16), moe_w3[i].astype(BF16),
                     moe_w2[i].astype(BF16))
    return x


def kernel(x_prompt, x_sample, norm1_g, w_in, a_qn, a_kn, b_qn, b_kn, ret_dec_f, ret_dec_b, ret_norm_g,
           w_oa, w_ob, w_oc, w_out, norm2_g, ffn_w1, ffn_w3, ffn_w2, moe_router, moe_w1, moe_w3, moe_w2):
    seqs = (x_prompt.shape[:2], x_sample.shape[:2])
    x = jnp.concatenate([x_prompt.reshape(-1, D_MODEL), x_sample.reshape(-1, D_MODEL)], axis=0)
    y = _trunk(x, seqs, norm1_g, w_in, a_qn, a_kn, b_qn, b_kn, ret_dec_f, ret_dec_b, ret_norm_g,
               w_oa, w_ob, w_oc, w_out, norm2_g, ffn_w1, ffn_w3, ffn_w2,
               moe_router, moe_w1, moe_w3, moe_w2)
    n_p = x_prompt.shape[0] * x_prompt.shape[1]
    return (y[:n_p].reshape(x_prompt.shape), y[n_p:].reshape(x_sample.shape))
```

```python
import functools

import numpy as np
import jax
import jax.numpy as jnp
from jax import lax
from jax.experimental import pallas as pl
from jax.experimental.pallas import tpu as pltpu

F32 = jnp.float32
BF16 = jnp.bfloat16

D_MODEL = 1024
GRID_W = 64
HEAD_DIM = 64
ROPE_THETA = 10000.0
EPS = 1e-6
A_HEADS = 8
A_KV_HEADS = 2
B_PATTERNS = ((128, 1), (512, 4), (2048, 16))
B_GROUPS = 3
B_HEADS = 8
C_HEADS = 4
C_KEY_DIM = 64
C_VAL_DIM = 128
N_EXPERTS = 8
N_BRANCHES = 3

A_Q = A_HEADS * HEAD_DIM
A_KV = A_KV_HEADS * HEAD_DIM
B_Q = B_GROUPS * B_HEADS * HEAD_DIM
B_KV = B_HEADS * HEAD_DIM
C_QK = C_HEADS * C_KEY_DIM
C_V = C_HEADS * C_VAL_DIM

LANES = 128
VMEM_LIMIT = 56 * 1024 * 1024

_ORIG_SPLITS = (("aq", A_Q), ("ak", A_KV), ("av", A_KV), ("bq", B_Q), ("bk", B_KV), ("bv", B_KV),
                ("cq", C_QK), ("ck", C_QK), ("cv", C_V), ("cg", C_V), ("gl", N_BRANCHES * D_MODEL))
_NEW_ORDER = ("bq", "bk", "aq", "bv", "cv", "cg", "gl", "cq", "ck", "ak", "av")
PROJ_W = 8192


def _layout():
    orig, o = {}, 0
    for name, w in _ORIG_SPLITS:
        orig[name] = (o, w)
        o += w
    new, o = {}, 0
    for name in _NEW_ORDER:
        new[name] = o
        o += orig[name][1]
    return orig, new, o


_ORIG, OFF, _USED_W = _layout()


def _cparams(sem):
    return pltpu.CompilerParams(dimension_semantics=sem, vmem_limit_bytes=VMEM_LIMIT)


def _seq_call(kern, prev, *, in_specs, out_specs, out_shape, args, **kwargs):
    n_in = len(in_specs)
    if prev is None:
        return pl.pallas_call(kern, in_specs=in_specs, out_specs=out_specs, out_shape=out_shape,
                              **kwargs)(*args)
    n_prev = len(prev)

    def chained(*refs):
        return kern(*refs[:n_in], *refs[n_in + n_prev:])

    return pl.pallas_call(
        chained,
        in_specs=list(in_specs) + [pl.BlockSpec(memory_space=pl.ANY)] * n_prev,
        out_specs=out_specs, out_shape=out_shape,
        input_output_aliases={n_in + k: k for k in range(n_prev)},
        **kwargs)(*args, *prev)


def _inproj_kernel(x_ref, g_ref, w_ref, o_ref, xn_ref):
    @pl.when(pl.program_id(1) == 0)
    def _():
        x = x_ref[...]
        ms = jnp.mean(x * x, axis=-1, keepdims=True)
        xn_ref[...] = (x * lax.rsqrt(ms + EPS) * g_ref[...]).astype(BF16)

    o_ref[...] = jnp.dot(xn_ref[...], w_ref[...], preferred_element_type=F32).astype(BF16)


def _inproj(x, g, w):
    n = x.shape[0]
    tm = min(1024, n)
    tn = 1024
    return pl.pallas_call(
        _inproj_kernel,
        grid=(n // tm, PROJ_W // tn),
        in_specs=[pl.BlockSpec((tm, D_MODEL), lambda i, j: (i, 0)),
                  pl.BlockSpec((1, D_MODEL), lambda i, j: (0, 0)),
                  pl.BlockSpec((D_MODEL, tn), lambda i, j: (0, j))],
        out_specs=pl.BlockSpec((tm, tn), lambda i, j: (i, j)),
        out_shape=jax.ShapeDtypeStruct((n, PROJ_W), BF16),
        scratch_shapes=[pltpu.VMEM((tm, D_MODEL), BF16)],
        compiler_params=_cparams(("parallel", "arbitrary")),
        name="inproj",
    )(x, g, w)


def _norm_rot(x, gain, cos, sin, mmat, half, do_norm):
    if do_norm:
        ms = jnp.dot((x * x).astype(BF16), mmat, preferred_element_type=F32)
        x = x * lax.rsqrt(ms + EPS)
    x = x * gain
    lane = lax.broadcasted_iota(jnp.int32, x.shape, 1)
    first = (lane % (2 * half)) < half
    swapped = jnp.where(first, pltpu.roll(x, LANES - half, 1), pltpu.roll(x, half, 1))
    return x * cos + swapped * sin


def _prep_kernel(bqk_ref, aq_ref, bv_ref, cq_ref, ck_ref, ak_ref,
                 cosb_ref, sinb_ref, cosa_ref, sina_ref, gain_ref, mmat_ref,
                 q0_ref, q1_ref, q2_ref, k0_ref, k1_ref, k2_ref, v0_ref, v1_ref, v2_ref,
                 oaq_ref, oak_ref, ocq_ref, ock_ref, scr_ref):
    cosb, sinb = cosb_ref[...], sinb_ref[...]
    cosa, sina = cosa_ref[...], sina_ref[...]
    mmat = mmat_ref[...]
    gains = gain_ref[...]
    tm = scr_ref.shape[0]

    def prepared(src_ref, src_off, c, gain_row, cos, sin, half, do_norm):
        x = src_ref[:, src_off + c * LANES:src_off + (c + 1) * LANES].astype(F32)
        return _norm_rot(x, gains[gain_row:gain_row + 1, :], cos, sin, mmat, half, do_norm)

    def run(src_ref, dst_ref, width, gain_row, cos, sin, half, do_norm):
        for c in range(width // LANES):
            y = prepared(src_ref, 0, c, gain_row, cos, sin, half, do_norm)
            dst_ref[:, c * LANES:(c + 1) * LANES] = y.astype(dst_ref.dtype)

    def store_dilated(y, dst_ref, dil, c):
        if dil == 1:
            dst_ref[:, c * LANES:(c + 1) * LANES] = y.astype(dst_ref.dtype)
            return
        scr_ref[...] = y
        for r in range(dil):
            rows = scr_ref[pl.ds(r, tm // dil, stride=dil), :]
            dst_ref[:, r * B_KV + c * LANES:r * B_KV + (c + 1) * LANES] = rows.astype(dst_ref.dtype)

    chunks = B_KV // LANES
    for g, (q_ref, (_, dil)) in enumerate(zip((q0_ref, q1_ref, q2_ref), B_PATTERNS)):
        for c in range(chunks):
            y = prepared(bqk_ref, g * B_KV, c, 0, cosb, sinb, HEAD_DIM // 2, True)
            store_dilated(y, q_ref, dil, c)
    for c in range(chunks):
        yk = prepared(bqk_ref, B_Q, c, 1, cosb, sinb, HEAD_DIM // 2, True)
        yv = bv_ref[:, c * LANES:(c + 1) * LANES].astype(F32)
        for (_, dil), k_ref, v_ref in zip(B_PATTERNS, (k0_ref, k1_ref, k2_ref), (v0_ref, v1_ref, v2_ref)):
            store_dilated(yk, k_ref, dil, c)
            store_dilated(yv, v_ref, dil, c)
    run(aq_ref, oaq_ref, A_Q, 2, cosa, sina, HEAD_DIM // 4, True)
    run(ak_ref, oak_ref, A_KV, 3, cosa, sina, HEAD_DIM // 4, True)
    run(cq_ref, ocq_ref, C_QK, 4, cosb, sinb, C_KEY_DIM // 2, False)
    run(ck_ref, ock_ref, C_QK, 5, cosb, sinb, C_KEY_DIM // 2, False)


def _prep(proj, tables, gains, mmat):
    n = proj.shape[0]
    tm = min(512, n)
    cosb, sinb, cosa, sina = tables

    def col(width, name):
        idx = OFF[name] // width
        return pl.BlockSpec((tm, width), lambda i: (i, idx))

    def tab():
        return pl.BlockSpec((tm, LANES), lambda i: (i, 0))

    dils = [dil for _, dil in B_PATTERNS]
    shapes = [(n // dil, dil * B_KV) for dil in dils] * 3 + [(n, A_Q), (n, A_KV), (n, C_QK), (n, C_QK)]
    blocks = [(tm // dil, dil * B_KV) for dil in dils] * 3 + [(tm, A_Q), (tm, A_KV), (tm, C_QK), (tm, C_QK)]
    return pl.pallas_call(
        _prep_kernel,
        grid=(n // tm,),
        in_specs=[col(B_Q + B_KV, "bq"), col(A_Q, "aq"), col(B_KV, "bv"), col(C_QK, "cq"),
                  col(C_QK, "ck"), col(A_KV, "ak"), tab(), tab(), tab(), tab(),
                  pl.BlockSpec((8, LANES), lambda i: (0, 0)),
                  pl.BlockSpec((LANES, LANES), lambda i: (0, 0))],
        out_specs=[pl.BlockSpec(blk, lambda i: (i, 0)) for blk in blocks],
        out_shape=[jax.ShapeDtypeStruct(shp, BF16) for shp in shapes],
        scratch_shapes=[pltpu.VMEM((tm, LANES), F32)],
        compiler_params=_cparams(("parallel",)),
        name="prep",
    )(proj, proj, proj, proj, proj, proj, cosb, sinb, cosa, sina, gains, mmat)


A_ONES_ROWS = 16


def _attn_a_kernel(qt_ref, k0_ref, kn_ref, vt_ref, o_ref, qp_ref, m_ref, acc_ref, sa_ref, sb_ref, *, bq, nk):
    ik = pl.program_id(2)
    rep = A_HEADS // A_KV_HEADS
    kv_heads = range(A_KV_HEADS)

    @pl.when(ik == 0)
    def _():
        qp_ref[...] = jnp.zeros(qp_ref.shape, BF16)
        for g in kv_heads:
            for r in range(rep):
                h = g * rep + r
                qp_ref[g, g * HEAD_DIM:(g + 1) * HEAD_DIM, r * bq:(r + 1) * bq] = (
                    qt_ref[h * HEAD_DIM:(h + 1) * HEAD_DIM, :])
        m_ref[...] = jnp.full(m_ref.shape, -jnp.inf, F32)
        acc_ref[...] = jnp.zeros(acc_ref.shape, F32)
        for g in kv_heads:
            sa_ref[g] = jnp.dot(k0_ref[...], qp_ref[g], preferred_element_type=F32)

    def step(cur_ref, nxt_ref):
        kn = kn_ref[...]
        ones = jnp.ones((A_ONES_ROWS, kn.shape[0]), BF16)
        for g in kv_heads:
            v_aug = jnp.concatenate([vt_ref[g * HEAD_DIM:(g + 1) * HEAD_DIM, :], ones], axis=0)
            for r in range(rep):
                cols = slice(r * bq, (r + 1) * bq)
                nxt_ref[g, :, cols] = jnp.dot(kn, qp_ref[g, :, cols], preferred_element_type=F32)
                s = cur_ref[g, :, cols]
                m_prev = m_ref[g, :, cols]
                m_new = jnp.maximum(m_prev, jnp.max(s, axis=0, keepdims=True))
                p = jnp.exp2(s - m_new).astype(BF16)
                alpha = jnp.exp2(m_prev - m_new)
                pv = jnp.dot(v_aug, p, preferred_element_type=F32)
                acc_ref[g, :, cols] = alpha * acc_ref[g, :, cols] + pv
                m_ref[g, :, cols] = m_new

    @pl.when(ik % 2 == 0)
    def _():
        step(sa_ref, sb_ref)

    @pl.when(ik % 2 == 1)
    def _():
        step(sb_ref, sa_ref)

    @pl.when(ik == nk - 1)
    def _():
        for g in range(A_KV_HEADS):
            acc = acc_ref[g]
            o = acc[:HEAD_DIM] / acc[HEAD_DIM:HEAD_DIM + 1]
            for r in range(rep):
                h = g * rep + r
                o_ref[h * HEAD_DIM:(h + 1) * HEAD_DIM, :] = o[:, r * bq:(r + 1) * bq].astype(o_ref.dtype)


def _mixer_a(aqt, ak, avt, prev, row_off, b, s):
    n = ak.shape[0]
    bq = min(512, s)
    bk = min(512, s)
    nq, nk = s // bq, s // bk
    rep = A_HEADS // A_KV_HEADS
    q0, k0 = row_off // bq, row_off // bk
    kern = functools.partial(_attn_a_kernel, bq=bq, nk=nk)
    return _seq_call(
        kern, prev,
        grid=(b, nq, nk),
        in_specs=[pl.BlockSpec((A_Q, bq), lambda ib, iq, ik: (0, q0 + ib * nq + iq)),
                  pl.BlockSpec((bk, A_KV), lambda ib, iq, ik: (k0 + ib * nk, 0)),
                  pl.BlockSpec((bk, A_KV), lambda ib, iq, ik: (k0 + ib * nk + jnp.minimum(ik + 1, nk - 1), 0)),
                  pl.BlockSpec((A_KV, bk), lambda ib, iq, ik: (0, k0 + ib * nk + ik))],
        out_specs=[pl.BlockSpec((A_Q, bq), lambda ib, iq, ik: (0, q0 + ib * nq + iq))],
        out_shape=[jax.ShapeDtypeStruct((A_Q, n), BF16)],
        scratch_shapes=[pltpu.VMEM((A_KV_HEADS, A_KV, rep * bq), BF16),
                        pltpu.VMEM((A_KV_HEADS, 1, rep * bq), F32),
                        pltpu.VMEM((A_KV_HEADS, HEAD_DIM + A_ONES_ROWS, rep * bq), F32),
                        pltpu.VMEM((A_KV_HEADS, bk, rep * bq), F32),
                        pltpu.VMEM((A_KV_HEADS, bk, rep * bq), F32)],
        compiler_params=_cparams(("parallel", "parallel", "arbitrary")),
        name="mixer_a",
        args=(aqt, ak, ak, avt))


B_BLOCK_Q = 128
B_HALO = 64


def _attn_b_kernel(q_ref, k0_ref, k1_ref, k2_ref, k3_ref, v0_ref, v1_ref, v2_ref, v3_ref,
                   o_ref, lse_ref, *, u_len):
    i = pl.program_id(2)
    kcat = jnp.concatenate([k0_ref[...], k1_ref[...], k2_ref[...], k3_ref[...]], axis=0)
    vcat = jnp.concatenate([v0_ref[...], v1_ref[...], v2_ref[...], v3_ref[...]], axis=0)
    nkeys = B_BLOCK_Q + 2 * B_HALO
    a = lax.broadcasted_iota(jnp.int32, (B_BLOCK_Q, nkeys), 0)
    c = lax.broadcasted_iota(jnp.int32, (B_BLOCK_Q, nkeys), 1)
    key_pos = i * B_BLOCK_Q - B_HALO + c
    valid = (c >= a) & (c <= a + 2 * B_HALO) & (key_pos >= 0) & (key_pos < u_len)
    left = lax.broadcasted_iota(jnp.int32, (B_BLOCK_Q, LANES), 1) < HEAD_DIM
    zero = jnp.zeros((B_BLOCK_Q, LANES), BF16)
    for pair in range(B_HEADS // 2):
        sl = slice(pair * LANES, (pair + 1) * LANES)
        q2, k2, v2 = q_ref[:, sl], kcat[:, sl], vcat[:, sl]
        outs, lses = [], []
        for own in (left, jnp.logical_not(left)):
            s = lax.dot_general(jnp.where(own, q2, zero), k2, (((1,), (1,)), ((), ())),
                                preferred_element_type=F32)
            s = jnp.where(valid, s, -1e30)
            m = jnp.max(s, axis=1, keepdims=True)
            e = jnp.exp(s - m)
            den = jnp.sum(e, axis=1, keepdims=True)
            outs.append(jnp.dot(e.astype(BF16), v2, preferred_element_type=F32) / den)
            lses.append(m + jnp.log(den))
        o_ref[:, sl] = jnp.where(left, outs[0], outs[1]).astype(o_ref.dtype)
        lse_ref[:, sl] = jnp.where(left, lses[0], lses[1])


def _mixer_b_group(qd, kd, vd, prev, g, dil, row_off, b, s):
    u_len = s // dil
    nq = u_len // B_BLOCK_Q
    nkb = u_len // B_HALO
    q_rows0 = row_off // dil // B_BLOCK_Q
    k_rows0 = row_off // dil // B_HALO

    def kspec(t):
        def imap(ib, r, i):
            blk = jnp.clip(2 * i - 1 + t, 0, nkb - 1)
            return (k_rows0 + ib * nkb + blk, r)
        return pl.BlockSpec((B_HALO, B_KV), imap)

    def qspec():
        return pl.BlockSpec((B_BLOCK_Q, B_KV), lambda ib, r, i: (q_rows0 + ib * nq + i, r))

    kern = functools.partial(_attn_b_kernel, u_len=u_len)
    return _seq_call(
        kern, prev,
        grid=(b, dil, nq),
        in_specs=[qspec()] + [kspec(t) for t in range(4)] + [kspec(t) for t in range(4)],
        out_specs=[qspec(), qspec()],
        out_shape=[jax.ShapeDtypeStruct(qd.shape, BF16), jax.ShapeDtypeStruct(qd.shape, F32)],
        compiler_params=_cparams(("parallel", "parallel", "parallel")),
        name=f"mixer_b{g}",
        args=(qd, kd, kd, kd, kd, vd, vd, vd, vd))


def _ret_chunk(q, k, v, dec_ref, xi_ref, zeta_ref, cdec_ref, r_ref):
    outs = []
    for h in range(C_HEADS):
        qh = q[:, h * C_KEY_DIM:(h + 1) * C_KEY_DIM]
        kh = k[:, h * C_KEY_DIM:(h + 1) * C_KEY_DIM]
        vh = v[:, h * C_VAL_DIM:(h + 1) * C_VAL_DIM]
        att = lax.dot_general(qh, kh, (((1,), (1,)), ((), ())), preferred_element_type=F32) * dec_ref[h]
        inner = jnp.dot(att.astype(BF16), vh, preferred_element_type=F32)
        r = r_ref[h]
        cross = jnp.dot(qh, r.astype(BF16), preferred_element_type=F32) * xi_ref[h]
        kz = (kh.astype(F32) * zeta_ref[h]).astype(BF16)
        r_ref[h] = r * cdec_ref[h] + lax.dot_general(kz, vh, (((0,), (0,)), ((), ())),
                                                     preferred_element_type=F32)
        outs.append(inner + cross)
    return jnp.concatenate(outs, axis=1)


def _ret_fwd_kernel(q_ref, k_ref, v_ref, dec_ref, xi_ref, zeta_ref, cdec_ref, o_ref, r_ref):
    @pl.when(pl.program_id(1) == 0)
    def _():
        r_ref[...] = jnp.zeros(r_ref.shape, F32)

    o_ref[...] = _ret_chunk(q_ref[...], k_ref[...], v_ref[...], dec_ref, xi_ref, zeta_ref, cdec_ref, r_ref)


def _ret_bwd_kernel(q_ref, k_ref, v_ref, dec_ref, xi_ref, zeta_ref, cdec_ref, of_ref, gate_ref, ng_ref,
                    o_ref, r_ref):
    @pl.when(pl.program_id(1) == 0)
    def _():
        r_ref[...] = jnp.zeros(r_ref.shape, F32)

    o = of_ref[...] + _ret_chunk(q_ref[...], k_ref[...], v_ref[...], dec_ref, xi_ref, zeta_ref, cdec_ref, r_ref)
    ng = ng_ref[...]
    gate = gate_ref[...].astype(F32)
    for h in range(C_HEADS):
        sl = slice(h * C_VAL_DIM, (h + 1) * C_VAL_DIM)
        oh = o[:, sl]
        mu = jnp.mean(oh, axis=1, keepdims=True)
        var = jnp.mean(jnp.square(oh - mu), axis=1, keepdims=True)
        y = (oh - mu) * lax.rsqrt(var + EPS) * ng[:, sl]
        gh = gate[:, sl]
        o_ref[:, sl] = (gh * jax.nn.sigmoid(gh) * y).astype(o_ref.dtype)


def _ret_tables(dec_param, chunk, strict):
    log_g = -jnp.exp(dec_param.astype(F32))
    j = jnp.arange(chunk, dtype=F32)
    lg = log_g[:, None, None]
    if strict:
        diff = j[None, :] - j[:, None]
        dec = jnp.where((diff > 0)[None], jnp.exp(jnp.maximum(diff, 0.0)[None] * lg), 0.0)
        xi = jnp.exp((chunk - j)[None, :, None] * lg)
        zeta = jnp.exp(j[None, :, None] * lg)
    else:
        diff = j[:, None] - j[None, :]
        dec = jnp.where((diff >= 0)[None], jnp.exp(jnp.maximum(diff, 0.0)[None] * lg), 0.0)
        xi = jnp.exp((j + 1.0)[None, :, None] * lg)
        zeta = jnp.exp((chunk - 1.0 - j)[None, :, None] * lg)
    xi = jnp.broadcast_to(xi, (C_HEADS, chunk, C_VAL_DIM))
    zeta = jnp.broadcast_to(zeta, (C_HEADS, chunk, C_KEY_DIM))
    cdec = jnp.broadcast_to(jnp.exp(chunk * log_g)[:, None, None], (C_HEADS, C_KEY_DIM, C_VAL_DIM))
    return dec, xi, zeta, cdec


C_CHUNK = 512


def _mixer_c(cq, ck, proj, tabs_f, tabs_b, norm_g, prev, row_off, b, s):
    n = cq.shape[0]
    prev_f, prev_o = (None, None) if prev is None else ([prev[0]], [prev[1]])
    nc = s // C_CHUNK
    r0 = row_off // C_CHUNK
    v_col = OFF["cv"] // C_V
    g_col = OFF["cg"] // C_V

    def fwd_rows(ib, c):
        return r0 + ib * nc + c

    def bwd_rows(ib, c):
        return r0 + ib * nc + (nc - 1 - c)

    def specs(rows):
        def full(shape):
            return pl.BlockSpec(shape, lambda ib, c: (0,) * len(shape))
        return [pl.BlockSpec((C_CHUNK, C_QK), lambda ib, c: (rows(ib, c), 0)),
                pl.BlockSpec((C_CHUNK, C_QK), lambda ib, c: (rows(ib, c), 0)),
                pl.BlockSpec((C_CHUNK, C_V), lambda ib, c: (rows(ib, c), v_col)),
                full((C_HEADS, C_CHUNK, C_CHUNK)), full((C_HEADS, C_CHUNK, C_VAL_DIM)),
                full((C_HEADS, C_CHUNK, C_KEY_DIM)), full((C_HEADS, C_KEY_DIM, C_VAL_DIM))]

    scratch = [pltpu.VMEM((C_HEADS, C_KEY_DIM, C_VAL_DIM), F32)]
    o_f, = _seq_call(
        _ret_fwd_kernel, prev_f,
        grid=(b, nc),
        in_specs=specs(fwd_rows),
        out_specs=[pl.BlockSpec((C_CHUNK, C_V), lambda ib, c: (fwd_rows(ib, c), 0))],
        out_shape=[jax.ShapeDtypeStruct((n, C_V), F32)],
        scratch_shapes=scratch,
        compiler_params=_cparams(("parallel", "arbitrary")),
        name="ret_fwd",
        args=(cq, ck, proj, *tabs_f))
    o_c, = _seq_call(
        _ret_bwd_kernel, prev_o,
        grid=(b, nc),
        in_specs=specs(bwd_rows) + [
            pl.BlockSpec((C_CHUNK, C_V), lambda ib, c: (bwd_rows(ib, c), 0)),
            pl.BlockSpec((C_CHUNK, C_V), lambda ib, c: (bwd_rows(ib, c), g_col)),
            pl.BlockSpec((1, C_V), lambda ib, c: (0, 0))],
        out_specs=[pl.BlockSpec((C_CHUNK, C_V), lambda ib, c: (bwd_rows(ib, c), 0))],
        out_shape=[jax.ShapeDtypeStruct((n, C_V), BF16)],
        scratch_shapes=scratch,
        compiler_params=_cparams(("parallel", "arbitrary")),
        name="ret_bwd",
        args=(cq, ck, proj, *tabs_b, o_f, proj, norm_g))
    return [o_f, o_c]


def _merge_kernel(oa_ref, ob0_ref, ob1_ref, ob2_ref, l0_ref, l1_ref, l2_ref, oc_ref,
                  ga_ref, gb_ref, gc_ref, x_ref, woa_ref, wob_ref, woc_ref, wout_ref, n2_ref,
                  xo_ref, hn_ref, scr_ref):
    tm = x_ref.shape[0]

    def natural(src_ref, dil, slot):
        if dil == 1:
            return src_ref[...].astype(F32)
        chunks = B_KV // LANES
        for r in range(dil):
            for c in range(chunks):
                col = r * B_KV + c * LANES
                scr_ref[slot, c, pl.ds(r, tm // dil, stride=dil), :] = src_ref[:, col:col + LANES].astype(F32)
        return jnp.concatenate([scr_ref[slot, c] for c in range(chunks)], axis=1)

    dils = [dil for _, dil in B_PATTERNS]
    l0, l1, l2 = [natural(ref, dil, 2 * g) for g, (ref, dil) in enumerate(zip((l0_ref, l1_ref, l2_ref), dils))]
    o0, o1, o2 = [natural(ref, dil, 2 * g + 1)
                  for g, (ref, dil) in enumerate(zip((ob0_ref, ob1_ref, ob2_ref), dils))]
    mx = jnp.maximum(jnp.maximum(l0, l1), l2)
    e0, e1, e2 = jnp.exp(l0 - mx), jnp.exp(l1 - mx), jnp.exp(l2 - mx)
    ob = (e0 * o0 + e1 * o1 + e2 * o2) / (e0 + e1 + e2)

    def branch(o, w_ref, gate_ref):
        y = jnp.dot(o, w_ref[...], preferred_element_type=F32)
        return jax.nn.sigmoid(gate_ref[...].astype(F32)) * y

    oa = jnp.transpose(oa_ref[...].astype(F32)).astype(BF16)
    merged = (branch(oa, woa_ref, ga_ref) + branch(ob.astype(BF16), wob_ref, gb_ref)
              + branch(oc_ref[...], woc_ref, gc_ref))
    x = x_ref[...] + jnp.dot(merged.astype(BF16), wout_ref[...], preferred_element_type=F32)
    xo_ref[...] = x
    ms = jnp.mean(x * x, axis=-1, keepdims=True)
    hn_ref[...] = (x * lax.rsqrt(ms + EPS) * n2_ref[...]).astype(hn_ref.dtype)


def _merge(o_a, o_b, lse_b, o_c, proj, x, w_oa, w_ob, w_oc, w_out, n2, hn_dtype):
    n = x.shape[0]
    tm = min(512, n)
    gl0 = OFF["gl"] // D_MODEL

    def rows(width):
        return pl.BlockSpec((tm, width), lambda i: (i, 0))

    def gate(k):
        return pl.BlockSpec((tm, D_MODEL), lambda i: (i, gl0 + k))

    def full(r, c):
        return pl.BlockSpec((r, c), lambda i: (0, 0))

    grouped = [pl.BlockSpec((tm // dil, dil * B_KV), lambda i: (i, 0)) for _, dil in B_PATTERNS]
    return pl.pallas_call(
        _merge_kernel,
        grid=(n // tm,),
        in_specs=[pl.BlockSpec((A_Q, tm), lambda i: (0, i))] + grouped + grouped
                 + [rows(C_V), gate(0), gate(1), gate(2), rows(D_MODEL),
                    full(A_Q, D_MODEL), full(B_KV, D_MODEL), full(C_V, D_MODEL), full(D_MODEL, D_MODEL),
                    full(1, D_MODEL)],
        out_specs=[rows(D_MODEL), rows(D_MODEL)],
        out_shape=[jax.ShapeDtypeStruct((n, D_MODEL), F32), jax.ShapeDtypeStruct((n, D_MODEL), hn_dtype)],
        scratch_shapes=[pltpu.VMEM((2 * B_GROUPS, B_KV // LANES, tm, LANES), F32)],
        compiler_params=_cparams(("parallel",)),
        name="merge_out",
    )(o_a, *o_b, *lse_b, o_c, proj, proj, proj, x, w_oa, w_ob, w_oc, w_out, n2)


def _ffn_kernel(hn_ref, x_ref, w1_ref, w3_ref, w2_ref, o_ref, acc_ref, *, nf):
    j = pl.program_id(1)
    hn = hn_ref[...]
    a = jnp.dot(hn, w1_ref[...], preferred_element_type=F32)
    g = jnp.dot(hn, w3_ref[...], preferred_element_type=F32)
    h = (a * jax.nn.sigmoid(a) * g).astype(BF16)
    y = jnp.dot(h, w2_ref[...], preferred_element_type=F32)

    @pl.when(j == 0)
    def _():
        acc_ref[...] = x_ref[...] + y

    @pl.when(j > 0)
    def _():
        acc_ref[...] += y

    @pl.when(j == nf - 1)
    def _():
        o_ref[...] = acc_ref[...]


def _ffn(hn, x, w1, w3, w2):
    n = x.shape[0]
    d_ff = w1.shape[1]
    tm = min(512, n)
    tf = d_ff // 2
    nf = d_ff // tf
    return pl.pallas_call(
        functools.partial(_ffn_kernel, nf=nf),
        grid=(n // tm, nf),
        in_specs=[pl.BlockSpec((tm, D_MODEL), lambda i, j: (i, 0)),
                  pl.BlockSpec((tm, D_MODEL), lambda i, j: (i, 0)),
                  pl.BlockSpec((D_MODEL, tf), lambda i, j: (0, j)),
                  pl.BlockSpec((D_MODEL, tf), lambda i, j: (0, j)),
                  pl.BlockSpec((tf, D_MODEL), lambda i, j: (j, 0))],
        out_specs=pl.BlockSpec((tm, D_MODEL), lambda i, j: (i, 0)),
        out_shape=jax.ShapeDtypeStruct((n, D_MODEL), F32),
        scratch_shapes=[pltpu.VMEM((tm, D_MODEL), F32)],
        compiler_params=_cparams(("parallel", "arbitrary")),
        name="ffn",
    )(hn, x, w1, w3, w2)


TOP_K = 2
MOE_TM = 512


def _router_kernel(hn_ref, wr_ref, idx_ref, wts_ref):
    logits = jnp.dot(hn_ref[...], wr_ref[...], preferred_element_type=F32,
                     precision=lax.Precision.HIGHEST)
    col = lax.broadcasted_iota(jnp.int32, logits.shape, 1)
    m1 = jnp.max(logits, axis=1, keepdims=True)
    i1 = jnp.min(jnp.where(logits == m1, col, N_EXPERTS), axis=1, keepdims=True)
    rest = jnp.where(col == i1, -jnp.inf, logits)
    m2 = jnp.max(rest, axis=1, keepdims=True)
    i2 = jnp.min(jnp.where(rest == m2, col, N_EXPERTS), axis=1, keepdims=True)
    e2 = jnp.exp(m2 - m1)
    w1 = 1.0 / (1.0 + e2)
    idx_ref[...] = jnp.concatenate([i1, i2], axis=1)
    wts_ref[...] = jnp.concatenate([w1, e2 * w1], axis=1)


def _router(hn, wr):
    n = hn.shape[0]
    tm = min(1024, n)
    return pl.pallas_call(
        _router_kernel,
        grid=(n // tm,),
        in_specs=[pl.BlockSpec((tm, D_MODEL), lambda i: (i, 0)),
                  pl.BlockSpec((D_MODEL, N_EXPERTS), lambda i: (0, 0))],
        out_specs=[pl.BlockSpec((tm, TOP_K), lambda i: (i, 0)), pl.BlockSpec((tm, TOP_K), lambda i: (i, 0))],
        out_shape=[jax.ShapeDtypeStruct((n, TOP_K), jnp.int32), jax.ShapeDtypeStruct((n, TOP_K), F32)],
        compiler_params=_cparams(("parallel",)),
        name="router",
    )(hn, wr)


def _route(idx, tm):
    n = idx.shape[0]
    e_flat = idx.reshape(-1)
    onehot = (e_flat[:, None] == jnp.arange(N_EXPERTS, dtype=jnp.int32)[None, :]).astype(jnp.int32)
    csum = jnp.cumsum(onehot, axis=0)
    rank = jnp.sum((csum - onehot) * onehot, axis=1)
    gsz = ((csum[-1] + tm - 1) // tm) * tm
    gend = jnp.cumsum(gsz)
    pos = (gend - gsz)[e_flat] + rank
    n_rows = n * TOP_K + N_EXPERTS * tm
    n_tiles = n_rows // tm
    row_token = jnp.zeros((n_rows,), jnp.int32).at[pos].set(jnp.arange(n * TOP_K, dtype=jnp.int32) // TOP_K)
    tile_start = jnp.arange(n_tiles, dtype=jnp.int32) * tm
    tile_expert = jnp.minimum(jnp.sum((tile_start[:, None] >= gend[None, :]).astype(jnp.int32), axis=1),
                              N_EXPERTS - 1)
    n_used = (gend[-1] // tm).astype(jnp.int32).reshape(1)
    return pos.reshape(n, TOP_K), row_token.reshape(n_tiles, 1, tm), tile_expert, n_used


def _row_copy(src_ref, src_row, dst_ref, dst_row, sem):
    return pltpu.make_async_copy(src_ref.at[pl.ds(src_row, 1)], dst_ref.at[pl.ds(dst_row, 1)], sem)


def _expert_kernel(te_ref, nu_ref, idx_ref, nxt_ref, src_ref, w1_ref, w3_ref, w2_ref, y_ref,
                   xs_ref, xb_ref, acc_ref, sem, *, nf, tm):
    i = pl.program_id(0)
    j = pl.program_id(1)
    n_used = nu_ref[0]
    active = i < n_used
    slot = i % 2
    share = tm // nf

    @pl.when((i == 0) & (j == 0))
    def _():
        def issue(r, carry):
            _row_copy(src_ref, idx_ref[0, 0, r], xs_ref.at[0], r, sem.at[0]).start()
            return carry
        lax.fori_loop(0, tm, issue, 0, unroll=8)

    @pl.when((j == 0) & (i <= n_used))
    def _():
        pltpu.make_async_copy(src_ref.at[pl.ds(0, tm)], xs_ref.at[slot], sem.at[slot]).wait()

    @pl.when(active)
    def _():
        @pl.when(j == 0)
        def _():
            xb_ref[...] = xs_ref[slot].astype(BF16)

        for r in range(share):
            row = j * share + r
            _row_copy(src_ref, nxt_ref[0, 0, row], xs_ref.at[1 - slot], row, sem.at[1 - slot]).start()

        xb = xb_ref[...]
        a = jnp.dot(xb, w1_ref[0], preferred_element_type=F32)
        g = jnp.dot(xb, w3_ref[0], preferred_element_type=F32)
        h = (a * jax.nn.sigmoid(a) * g).astype(BF16)
        y = jnp.dot(h, w2_ref[0], preferred_element_type=F32)

        @pl.when(j == 0)
        def _():
            acc_ref[...] = y

        @pl.when(j > 0)
        def _():
            acc_ref[...] += y

        @pl.when(j == nf - 1)
        def _():
            y_ref[...] = acc_ref[...]

    @pl.when(jnp.logical_not(active) & (j == nf - 1))
    def _():
        y_ref[...] = jnp.zeros(y_ref.shape, y_ref.dtype)


def _experts(src, row_token, tile_expert, n_used, w1, w3, w2):
    nt, _, tm = row_token.shape
    d_ff = w1.shape[2]
    nf = 2
    tf = d_ff // nf

    def jcol(i, j, nu):
        return jnp.where(i < nu[0], j, nf - 1)

    grid_spec = pltpu.PrefetchScalarGridSpec(
        num_scalar_prefetch=2,
        grid=(nt, nf),
        in_specs=[pl.BlockSpec((1, 1, tm), lambda i, j, te, nu: (i, 0, 0), memory_space=pltpu.SMEM),
                  pl.BlockSpec((1, 1, tm), lambda i, j, te, nu: (jnp.minimum(i + 1, nt - 1), 0, 0),
                               memory_space=pltpu.SMEM),
                  pl.BlockSpec(memory_space=pl.ANY),
                  pl.BlockSpec((1, D_MODEL, tf), lambda i, j, te, nu: (te[i], 0, jcol(i, j, nu))),
                  pl.BlockSpec((1, D_MODEL, tf), lambda i, j, te, nu: (te[i], 0, jcol(i, j, nu))),
                  pl.BlockSpec((1, tf, D_MODEL), lambda i, j, te, nu: (te[i], jcol(i, j, nu), 0))],
        out_specs=pl.BlockSpec((tm, D_MODEL), lambda i, j, te, nu: (i, 0)),
        scratch_shapes=[pltpu.VMEM((2, tm, D_MODEL), src.dtype), pltpu.VMEM((tm, D_MODEL), BF16),
                        pltpu.VMEM((tm, D_MODEL), F32), pltpu.SemaphoreType.DMA((2,))],
    )
    return pl.pallas_call(
        functools.partial(_expert_kernel, nf=nf, tm=tm),
        grid_spec=grid_spec,
        out_shape=jax.ShapeDtypeStruct((nt * tm, D_MODEL), F32),
        compiler_params=_cparams(("arbitrary", "arbitrary")),
        name="moe_experts",
    )(tile_expert, n_used, row_token, row_token, src, w1, w3, w2)


def _combine_kernel(pos_ref, nxt_ref, wts_ref, x_ref, y_ref, o_ref, g_ref, sem, *, tm, nt):
    i = pl.program_id(0)
    slot = i % 2

    def request(p_ref, dst_slot):
        def issue(r, carry):
            for k in range(TOP_K):
                _row_copy(y_ref, p_ref[0, 0, TOP_K * r + k], g_ref.at[dst_slot, k], r, sem.at[dst_slot]).start()
            return carry
        lax.fori_loop(0, tm, issue, 0, unroll=8)

    @pl.when(i == 0)
    def _():
        request(pos_ref, 0)

    @pl.when(i + 1 < nt)
    def _():
        request(nxt_ref, 1 - slot)

    for k in range(TOP_K):
        pltpu.make_async_copy(y_ref.at[pl.ds(0, tm)], g_ref.at[slot, k], sem.at[slot]).wait()
    w = wts_ref[...]
    o_ref[...] = x_ref[...] + w[:, 0:1] * g_ref[slot, 0] + w[:, 1:2] * g_ref[slot, 1]


def _combine(pos, wts, x, y):
    n = x.shape[0]
    tm = min(256, n)
    nt = n // tm
    pos3 = pos.reshape(nt, 1, TOP_K * tm)
    return pl.pallas_call(
        functools.partial(_combine_kernel, tm=tm, nt=nt),
        grid=(nt,),
        in_specs=[pl.BlockSpec((1, 1, TOP_K * tm), lambda i: (i, 0, 0), memory_space=pltpu.SMEM),
                  pl.BlockSpec((1, 1, TOP_K * tm), lambda i: (jnp.minimum(i + 1, nt - 1), 0, 0),
                               memory_space=pltpu.SMEM),
                  pl.BlockSpec((tm, TOP_K), lambda i: (i, 0)),
                  pl.BlockSpec((tm, D_MODEL), lambda i: (i, 0)),
                  pl.BlockSpec(memory_space=pl.ANY)],
        out_specs=pl.BlockSpec((tm, D_MODEL), lambda i: (i, 0)),
        out_shape=jax.ShapeDtypeStruct((n, D_MODEL), F32),
        scratch_shapes=[pltpu.VMEM((2, TOP_K, tm, D_MODEL), F32), pltpu.SemaphoreType.DMA((2,))],
        compiler_params=_cparams(("arbitrary",)),
        name="moe_combine",
    )(pos3, pos3, wts, x, y)


def _moe(hn, x, wr, w1, w3, w2):
    idx, wts = _router(hn, wr)
    pos, row_token, tile_expert, n_used = _route(idx, MOE_TM)
    y = _experts(hn, row_token, tile_expert, n_used, w1, w3, w2)
    return _combine(pos, wts, x, y)


def _permute_w_in(w):
    cols = [w[:, _ORIG[name][0]:_ORIG[name][0] + _ORIG[name][1]] for name in _NEW_ORDER]
    cols.append(jnp.zeros((w.shape[0], PROJ_W - _USED_W), w.dtype))
    return jnp.concatenate(cols, axis=1).astype(BF16)


def _angles(pos, dim):
    inv = ROPE_THETA ** (-jnp.arange(0, dim, 2, dtype=F32) / dim)
    return pos.astype(F32)[:, None] * inv[None, :]


def _rope_tables(seqs):
    pos = jnp.concatenate([jnp.tile(jnp.arange(s), b) for b, s in seqs])
    ang = _angles(pos, HEAD_DIM)
    cosb = jnp.tile(jnp.cos(ang), (1, 4))
    sinb = jnp.tile(jnp.concatenate([-jnp.sin(ang), jnp.sin(ang)], axis=1), (1, 2))
    ar = _angles(pos // GRID_W, HEAD_DIM // 2)
    ac = _angles(pos % GRID_W, HEAD_DIM // 2)
    cosa = jnp.tile(jnp.concatenate([jnp.cos(ar), jnp.cos(ar), jnp.cos(ac), jnp.cos(ac)], axis=1), (1, 2))
    sina = jnp.tile(jnp.concatenate([-jnp.sin(ar), jnp.sin(ar), -jnp.sin(ac), jnp.sin(ac)], axis=1), (1, 2))
    return cosb, sinb, cosa, sina


def _head_mean_matrix():
    blk = np.kron(np.eye(LANES // HEAD_DIM), np.ones((HEAD_DIM, HEAD_DIM))) / HEAD_DIM
    return jnp.asarray(blk, BF16)


def _gain_rows(a_qn, a_kn, b_qn, b_kn):
    scale = HEAD_DIM ** -0.5
    scale_a = scale * np.log2(np.e)
    rows = [jnp.tile(b_qn, 2) * scale, jnp.tile(b_kn, 2), jnp.tile(a_qn, 2) * scale_a, jnp.tile(a_kn, 2),
            jnp.ones((LANES,), F32), jnp.full((LANES,), C_KEY_DIM ** -0.5, F32),
            jnp.ones((LANES,), F32), jnp.ones((LANES,), F32)]
    return jnp.stack(rows).astype(F32)


def _trunk(x, seqs, norm1_g, w_in, a_qn, a_kn, b_qn, b_kn, ret_dec_f, ret_dec_b, ret_norm_g,
           w_oa, w_ob, w_oc, w_out, norm2_g, ffn_w1, ffn_w3, ffn_w2,
           moe_router, moe_w1, moe_w3, moe_w2):
    depth = w_in.shape[0]
    tables = _rope_tables(seqs)
    mmat = _head_mean_matrix()
    for l in range(depth):
        proj = _inproj(x, norm1_g[l][None, :], _permute_w_in(w_in[l]))
        prepped = _prep(proj, tables, _gain_rows(a_qn[l], a_kn[l], b_qn[l], b_kn[l]), mmat)
        bqs, bks, bvs = prepped[0:3], prepped[3:6], prepped[6:9]
        aq, ak, cq, ck = prepped[9:]
        aqt = aq.T
        avt = proj[:, OFF["av"]:OFF["av"] + A_KV].T
        tabs_f = _ret_tables(ret_dec_f[l], C_CHUNK, False)
        tabs_b = _ret_tables(ret_dec_b[l], C_CHUNK, True)
        ng = ret_norm_g[l][None, :].astype(F32)
        res_a, res_c = None, None
        res_b = [None] * B_GROUPS
        row_off = 0
        for b, s in seqs:
            res_a = _mixer_a(aqt, ak, avt, res_a, row_off, b, s)
            for g, (_, dil) in enumerate(B_PATTERNS):
                res_b[g] = _mixer_b_group(bqs[g], bks[g], bvs[g], res_b[g], g, dil, row_off, b, s)
            res_c = _mixer_c(cq, ck, proj, tabs_f, tabs_b, ng, res_c, row_off, b, s)
            row_off += b * s
        x, hn = _merge(res_a[0], [r[0] for r in res_b], [r[1] for r in res_b], res_c[1], proj, x,
                       w_oa[l].astype(BF16), w_ob[l].astype(BF16), w_oc[l].astype(BF16),
                       w_out[l].astype(BF16), norm2_g[l][None, :], BF16 if l % 2 == 0 else F32)
        i = l // 2
        if l % 2 == 0:
            x = _ffn(hn, x, ffn_w1[i].astype(BF16), ffn_w3[i].astype(BF16), ffn_w2[i].astype(BF16))
        else:
            x = _moe(hn, x, moe_router[i], moe_w1[i].astype(BF16), moe_w3[i].astype(BF16),
                     moe_w2[i].astype(BF16))
    return x


def kernel(x_prompt, x_sample, norm1_g, w_in, a_qn, a_kn, b_qn, b_kn, ret_dec_f, ret_dec_b, ret_norm_g,
           w_oa, w_ob, w_oc, w_out, norm2_g, ffn_w1, ffn_w3, ffn_w2, moe_router, moe_w1, moe_w3, moe_w2):
    seqs = (x_prompt.shape[:2], x_sample.shape[:2])
    x = jnp.concatenate([x_prompt.reshape(-1, D_MODEL), x_sample.reshape(-1, D_MODEL)], axis=0)
    y = _trunk(x, seqs, norm1_g, w_in, a_qn, a_kn, b_qn, b_kn, ret_dec_f, ret_dec_b, ret_norm_g,
               w_oa, w_ob, w_oc, w_out, norm2_g, ffn_w1, ffn_w3, ffn_w2,
               moe_router, moe_w1, moe_w3, moe_w2)
    n_p = x_prompt.shape[0] * x_prompt.shape[1]
    return (y[:n_p].reshape(x_prompt.shape), y[n_p:].reshape(x_sample.shape))
```

```python
import functools

import numpy as np
import jax
import jax.numpy as jnp
from jax import lax
from jax.experimental import pallas as pl
from jax.experimental.pallas import tpu as pltpu

F32 = jnp.float32
BF16 = jnp.bfloat16

D_MODEL = 1024
GRID_W = 64
HEAD_DIM = 64
ROPE_THETA = 10000.0
EPS = 1e-6
A_HEADS = 8
A_KV_HEADS = 2
B_PATTERNS = ((128, 1), (512, 4), (2048, 16))
B_GROUPS = 3
B_HEADS = 8
C_HEADS = 4
C_KEY_DIM = 64
C_VAL_DIM = 128
N_EXPERTS = 8
N_BRANCHES = 3

A_Q = A_HEADS * HEAD_DIM
A_KV = A_KV_HEADS * HEAD_DIM
B_Q = B_GROUPS * B_HEADS * HEAD_DIM
B_KV = B_HEADS * HEAD_DIM
C_QK = C_HEADS * C_KEY_DIM
C_V = C_HEADS * C_VAL_DIM

LANES = 128
VMEM_LIMIT = 56 * 1024 * 1024

_ORIG_SPLITS = (("aq", A_Q), ("ak", A_KV), ("av", A_KV), ("bq", B_Q), ("bk", B_KV), ("bv", B_KV),
                ("cq", C_QK), ("ck", C_QK), ("cv", C_V), ("cg", C_V), ("gl", N_BRANCHES * D_MODEL))
_NEW_ORDER = ("bq", "bk", "aq", "bv", "cv", "cg", "gl", "cq", "ck", "ak", "av")
PROJ_W = 8192


def _layout():
    orig, o = {}, 0
    for name, w in _ORIG_SPLITS:
        orig[name] = (o, w)
        o += w
    new, o = {}, 0
    for name in _NEW_ORDER:
        new[name] = o
        o += orig[name][1]
    return orig, new, o


_ORIG, OFF, _USED_W = _layout()


def _cparams(sem):
    return pltpu.CompilerParams(dimension_semantics=sem, vmem_limit_bytes=VMEM_LIMIT)


def _seq_call(kern, prev, *, in_specs, out_specs, out_shape, args, **kwargs):
    n_in = len(in_specs)
    if prev is None:
        return pl.pallas_call(kern, in_specs=in_specs, out_specs=out_specs, out_shape=out_shape,
                              **kwargs)(*args)
    n_prev = len(prev)

    def chained(*refs):
        return kern(*refs[:n_in], *refs[n_in + n_prev:])

    return pl.pallas_call(
        chained,
        in_specs=list(in_specs) + [pl.BlockSpec(memory_space=pl.ANY)] * n_prev,
        out_specs=out_specs, out_shape=out_shape,
        input_output_aliases={n_in + k: k for k in range(n_prev)},
        **kwargs)(*args, *prev)


def _inproj_kernel(x_ref, g_ref, w_ref, o_ref, xn_ref):
    @pl.when(pl.program_id(1) == 0)
    def _():
        x = x_ref[...]
        ms = jnp.mean(x * x, axis=-1, keepdims=True)
        xn_ref[...] = (x * lax.rsqrt(ms + EPS) * g_ref[...]).astype(BF16)

    o_ref[...] = jnp.dot(xn_ref[...], w_ref[...], preferred_element_type=F32).astype(BF16)


def _inproj(x, g, w):
    n = x.shape[0]
    tm = min(1024, n)
    tn = 1024
    return pl.pallas_call(
        _inproj_kernel,
        grid=(n // tm, PROJ_W // tn),
        in_specs=[pl.BlockSpec((tm, D_MODEL), lambda i, j: (i, 0)),
                  pl.BlockSpec((1, D_MODEL), lambda i, j: (0, 0)),
                  pl.BlockSpec((D_MODEL, tn), lambda i, j: (0, j))],
        out_specs=pl.BlockSpec((tm, tn), lambda i, j: (i, j)),
        out_shape=jax.ShapeDtypeStruct((n, PROJ_W), BF16),
        scratch_shapes=[pltpu.VMEM((tm, D_MODEL), BF16)],
        compiler_params=_cparams(("parallel", "arbitrary")),
        name="inproj",
    )(x, g, w)


def _norm_rot(x, gain, cos, sin, mmat, half, do_norm):
    if do_norm:
        ms = jnp.dot((x * x).astype(BF16), mmat, preferred_element_type=F32)
        x = x * lax.rsqrt(ms + EPS)
    x = x * gain
    lane = lax.broadcasted_iota(jnp.int32, x.shape, 1)
    first = (lane % (2 * half)) < half
    swapped = jnp.where(first, pltpu.roll(x, LANES - half, 1), pltpu.roll(x, half, 1))
    return x * cos + swapped * sin


def _prep_kernel(bqk_ref, aq_ref, bv_ref, cq_ref, ck_ref, ak_ref,
                 cosb_ref, sinb_ref, cosa_ref, sina_ref, gain_ref, mmat_ref,
                 q0_ref, q1_ref, q2_ref, k0_ref, k1_ref, k2_ref, v0_ref, v1_ref, v2_ref,
                 oaq_ref, oak_ref, ocq_ref, ock_ref, scr_ref):
    cosb, sinb = cosb_ref[...], sinb_ref[...]
    cosa, sina = cosa_ref[...], sina_ref[...]
    mmat = mmat_ref[...]
    gains = gain_ref[...]
    tm = scr_ref.shape[0]

    def prepared(src_ref, src_off, c, gain_row, cos, sin, half, do_norm):
        x = src_ref[:, src_off + c * LANES:src_off + (c + 1) * LANES].astype(F32)
        return _norm_rot(x, gains[gain_row:gain_row + 1, :], cos, sin, mmat, half, do_norm)

    def run(src_ref, dst_ref, width, gain_row, cos, sin, half, do_norm):
        for c in range(width // LANES):
            y = prepared(src_ref, 0, c, gain_row, cos, sin, half, do_norm)
            dst_ref[:, c * LANES:(c + 1) * LANES] = y.astype(dst_ref.dtype)

    def store_dilated(y, dst_ref, dil, c):
        if dil == 1:
            dst_ref[:, c * LANES:(c + 1) * LANES] = y.astype(dst_ref.dtype)
            return
        scr_ref[...] = y
        for r in range(dil):
            rows = scr_ref[pl.ds(r, tm // dil, stride=dil), :]
            dst_ref[:, r * B_KV + c * LANES:r * B_KV + (c + 1) * LANES] = rows.astype(dst_ref.dtype)

    chunks = B_KV // LANES
    for g, (q_ref, (_, dil)) in enumerate(zip((q0_ref, q1_ref, q2_ref), B_PATTERNS)):
        for c in range(chunks):
            y = prepared(bqk_ref, g * B_KV, c, 0, cosb, sinb, HEAD_DIM // 2, True)
            store_dilated(y, q_ref, dil, c)
    for c in range(chunks):
        yk = prepared(bqk_ref, B_Q, c, 1, cosb, sinb, HEAD_DIM // 2, True)
        yv = bv_ref[:, c * LANES:(c + 1) * LANES].astype(F32)
        for (_, dil), k_ref, v_ref in zip(B_PATTERNS, (k0_ref, k1_ref, k2_ref), (v0_ref, v1_ref, v2_ref)):
            store_dilated(yk, k_ref, dil, c)
            store_dilated(yv, v_ref, dil, c)
    run(aq_ref, oaq_ref, A_Q, 2, cosa, sina, HEAD_DIM // 4, True)
    run(ak_ref, oak_ref, A_KV, 3, cosa, sina, HEAD_DIM // 4, True)
    run(cq_ref, ocq_ref, C_QK, 4, cosb, sinb, C_KEY_DIM // 2, False)
    run(ck_ref, ock_ref, C_QK, 5, cosb, sinb, C_KEY_DIM // 2, False)


def _prep(proj, tables, gains, mmat):
    n = proj.shape[0]
    tm = min(512, n)
    cosb, sinb, cosa, sina = tables

    def col(width, name):
        idx = OFF[name] // width
        return pl.BlockSpec((tm, width), lambda i: (i, idx))

    def tab():
        return pl.BlockSpec((tm, LANES), lambda i: (i, 0))

    dils = [dil for _, dil in B_PATTERNS]
    shapes = [(n // dil, dil * B_KV) for dil in dils] * 3 + [(n, A_Q), (n, A_KV), (n, C_QK), (n, C_QK)]
    blocks = [(tm // dil, dil * B_KV) for dil in dils] * 3 + [(tm, A_Q), (tm, A_KV), (tm, C_QK), (tm, C_QK)]
    return pl.pallas_call(
        _prep_kernel,
        grid=(n // tm,),
        in_specs=[col(B_Q + B_KV, "bq"), col(A_Q, "aq"), col(B_KV, "bv"), col(C_QK, "cq"),
                  col(C_QK, "ck"), col(A_KV, "ak"), tab(), tab(), tab(), tab(),
                  pl.BlockSpec((8, LANES), lambda i: (0, 0)),
                  pl.BlockSpec((LANES, LANES), lambda i: (0, 0))],
        out_specs=[pl.BlockSpec(blk, lambda i: (i, 0)) for blk in blocks],
        out_shape=[jax.ShapeDtypeStruct(shp, BF16) for shp in shapes],
        scratch_shapes=[pltpu.VMEM((tm, LANES), F32)],
        compiler_params=_cparams(("parallel",)),
        name="prep",
    )(proj, proj, proj, proj, proj, proj, cosb, sinb, cosa, sina, gains, mmat)


A_ONES_ROWS = 16


def _attn_a_kernel(qt_ref, k0_ref, kn_ref, vt_ref, o_ref, qp_ref, m_ref, acc_ref, sa_ref, sb_ref, *, bq, nk):
    ik = pl.program_id(2)
    rep = A_HEADS // A_KV_HEADS
    kv_heads = range(A_KV_HEADS)

    @pl.when(ik == 0)
    def _():
        qp_ref[...] = jnp.zeros(qp_ref.shape, BF16)
        for g in kv_heads:
            for r in range(rep):
                h = g * rep + r
                qp_ref[g, g * HEAD_DIM:(g + 1) * HEAD_DIM, r * bq:(r + 1) * bq] = (
                    qt_ref[h * HEAD_DIM:(h + 1) * HEAD_DIM, :])
        m_ref[...] = jnp.full(m_ref.shape, -jnp.inf, F32)
        acc_ref[...] = jnp.zeros(acc_ref.shape, F32)
        for g in kv_heads:
            sa_ref[g] = jnp.dot(k0_ref[...], qp_ref[g], preferred_element_type=F32)

    def step(cur_ref, nxt_ref, has_next):
        kn = kn_ref[...]
        ones = jnp.ones((A_ONES_ROWS, kn.shape[0]), BF16)
        for g in kv_heads:
            v_aug = jnp.concatenate([vt_ref[g * HEAD_DIM:(g + 1) * HEAD_DIM, :], ones], axis=0)
            for r in range(rep):
                cols = slice(r * bq, (r + 1) * bq)
                if has_next:
                    nxt_ref[g, :, cols] = jnp.dot(kn, qp_ref[g, :, cols], preferred_element_type=F32)
                s = cur_ref[g, :, cols]
                m_prev = m_ref[g, :, cols]
                m_new = jnp.maximum(m_prev, jnp.max(s, axis=0, keepdims=True))
                p = jnp.exp2(s - m_new).astype(BF16)
                alpha = jnp.exp2(m_prev - m_new)
                pv = jnp.dot(v_aug, p, preferred_element_type=F32)
                acc_ref[g, :, cols] = alpha * acc_ref[g, :, cols] + pv
                m_ref[g, :, cols] = m_new

    last = ik == nk - 1
    for parity, cur_ref, nxt_ref in ((0, sa_ref, sb_ref), (1, sb_ref, sa_ref)):
        for has_next in (True, False):
            @pl.when((ik % 2 == parity) & (last != has_next))
            def _(cur_ref=cur_ref, nxt_ref=nxt_ref, has_next=has_next):
                step(cur_ref, nxt_ref, has_next)

    @pl.when(last)
    def _():
        for g in range(A_KV_HEADS):
            acc = acc_ref[g]
            o = acc[:HEAD_DIM] / acc[HEAD_DIM:HEAD_DIM + 1]
            for r in range(rep):
                h = g * rep + r
                o_ref[h * HEAD_DIM:(h + 1) * HEAD_DIM, :] = o[:, r * bq:(r + 1) * bq].astype(o_ref.dtype)


def _mixer_a(aqt, ak, avt, prev, row_off, b, s):
    n = ak.shape[0]
    bq = min(512, s)
    bk = min(512, s)
    nq, nk = s // bq, s // bk
    rep = A_HEADS // A_KV_HEADS
    q0, k0 = row_off // bq, row_off // bk
    kern = functools.partial(_attn_a_kernel, bq=bq, nk=nk)
    return _seq_call(
        kern, prev,
        grid=(b, nq, nk),
        in_specs=[pl.BlockSpec((A_Q, bq), lambda ib, iq, ik: (0, q0 + ib * nq + iq)),
                  pl.BlockSpec((bk, A_KV), lambda ib, iq, ik: (k0 + ib * nk, 0)),
                  pl.BlockSpec((bk, A_KV), lambda ib, iq, ik: (k0 + ib * nk + jnp.minimum(ik + 1, nk - 1), 0)),
                  pl.BlockSpec((A_KV, bk), lambda ib, iq, ik: (0, k0 + ib * nk + ik))],
        out_specs=[pl.BlockSpec((A_Q, bq), lambda ib, iq, ik: (0, q0 + ib * nq + iq))],
        out_shape=[jax.ShapeDtypeStruct((A_Q, n), BF16)],
        scratch_shapes=[pltpu.VMEM((A_KV_HEADS, A_KV, rep * bq), BF16),
                        pltpu.VMEM((A_KV_HEADS, 1, rep * bq), F32),
                        pltpu.VMEM((A_KV_HEADS, HEAD_DIM + A_ONES_ROWS, rep * bq), F32),
                        pltpu.VMEM((A_KV_HEADS, bk, rep * bq), F32),
                        pltpu.VMEM((A_KV_HEADS, bk, rep * bq), F32)],
        compiler_params=_cparams(("parallel", "parallel", "arbitrary")),
        name="mixer_a",
        args=(aqt, ak, ak, avt))


B_BLOCK_Q = 128
B_HALO = 64


def _attn_b_kernel(q_ref, k0_ref, k1_ref, k2_ref, k3_ref, v0_ref, v1_ref, v2_ref, v3_ref,
                   o_ref, lse_ref, *, u_len):
    i = pl.program_id(2)
    kcat = jnp.concatenate([k0_ref[...], k1_ref[...], k2_ref[...], k3_ref[...]], axis=0)
    vcat = jnp.concatenate([v0_ref[...], v1_ref[...], v2_ref[...], v3_ref[...]], axis=0)
    nkeys = B_BLOCK_Q + 2 * B_HALO
    a = lax.broadcasted_iota(jnp.int32, (B_BLOCK_Q, nkeys), 0)
    c = lax.broadcasted_iota(jnp.int32, (B_BLOCK_Q, nkeys), 1)
    key_pos = i * B_BLOCK_Q - B_HALO + c
    valid = (c >= a) & (c <= a + 2 * B_HALO) & (key_pos >= 0) & (key_pos < u_len)
    left = lax.broadcasted_iota(jnp.int32, (B_BLOCK_Q, LANES), 1) < HEAD_DIM
    zero = jnp.zeros((B_BLOCK_Q, LANES), BF16)
    heads = [(pair, own) for pair in range(B_HEADS // 2) for own in (left, jnp.logical_not(left))]
    lanes = [slice(pair * LANES, (pair + 1) * LANES) for pair in range(B_HEADS // 2)]
    s = [lax.dot_general(jnp.where(own, q_ref[:, lanes[pair]], zero), kcat[:, lanes[pair]],
                         (((1,), (1,)), ((), ())), preferred_element_type=F32) for pair, own in heads]
    s = [jnp.where(valid, sh, -1e30) for sh in s]
    m = [jnp.max(sh, axis=1, keepdims=True) for sh in s]
    e = [jnp.exp(sh - mh) for sh, mh in zip(s, m)]
    den = [jnp.sum(eh, axis=1, keepdims=True) for eh in e]
    pv = [jnp.dot(eh.astype(BF16), vcat[:, lanes[pair]], preferred_element_type=F32)
          for eh, (pair, _) in zip(e, heads)]
    outs = [pvh / dh for pvh, dh in zip(pv, den)]
    lses = [mh + jnp.log(dh) for mh, dh in zip(m, den)]
    for pair in range(B_HEADS // 2):
        o_ref[:, lanes[pair]] = jnp.where(left, outs[2 * pair], outs[2 * pair + 1]).astype(o_ref.dtype)
        lse_ref[:, lanes[pair]] = jnp.where(left, lses[2 * pair], lses[2 * pair + 1])


def _mixer_b_group(qd, kd, vd, prev, g, dil, row_off, b, s):
    u_len = s // dil
    nq = u_len // B_BLOCK_Q
    nkb = u_len // B_HALO
    q_rows0 = row_off // dil // B_BLOCK_Q
    k_rows0 = row_off // dil // B_HALO

    def kspec(t):
        def imap(ib, r, i):
            blk = jnp.clip(2 * i - 1 + t, 0, nkb - 1)
            return (k_rows0 + ib * nkb + blk, r)
        return pl.BlockSpec((B_HALO, B_KV), imap)

    def qspec():
        return pl.BlockSpec((B_BLOCK_Q, B_KV), lambda ib, r, i: (q_rows0 + ib * nq + i, r))

    kern = functools.partial(_attn_b_kernel, u_len=u_len)
    return _seq_call(
        kern, prev,
        grid=(b, dil, nq),
        in_specs=[qspec()] + [kspec(t) for t in range(4)] + [kspec(t) for t in range(4)],
        out_specs=[qspec(), qspec()],
        out_shape=[jax.ShapeDtypeStruct(qd.shape, BF16), jax.ShapeDtypeStruct(qd.shape, F32)],
        compiler_params=_cparams(("parallel", "parallel", "parallel")),
        name=f"mixer_b{g}",
        args=(qd, kd, kd, kd, kd, vd, vd, vd, vd))


def _ret_chunk(q, k, v, dec_ref, xi_ref, zeta_ref, cdec_ref, r_ref):
    outs = []
    for h in range(C_HEADS):
        qh = q[:, h * C_KEY_DIM:(h + 1) * C_KEY_DIM]
        kh = k[:, h * C_KEY_DIM:(h + 1) * C_KEY_DIM]
        vh = v[:, h * C_VAL_DIM:(h + 1) * C_VAL_DIM]
        att = lax.dot_general(qh, kh, (((1,), (1,)), ((), ())), preferred_element_type=F32) * dec_ref[h]
        inner = jnp.dot(att.astype(BF16), vh, preferred_element_type=F32)
        r = r_ref[h]
        cross = jnp.dot(qh, r.astype(BF16), preferred_element_type=F32) * xi_ref[h]
        kz = (kh.astype(F32) * zeta_ref[h]).astype(BF16)
        r_ref[h] = r * cdec_ref[h] + lax.dot_general(kz, vh, (((0,), (0,)), ((), ())),
                                                     preferred_element_type=F32)
        outs.append(inner + cross)
    return jnp.concatenate(outs, axis=1)


def _ret_fwd_kernel(q_ref, k_ref, v_ref, dec_ref, xi_ref, zeta_ref, cdec_ref, o_ref, r_ref):
    @pl.when(pl.program_id(1) == 0)
    def _():
        r_ref[...] = jnp.zeros(r_ref.shape, F32)

    o_ref[...] = _ret_chunk(q_ref[...], k_ref[...], v_ref[...], dec_ref, xi_ref, zeta_ref, cdec_ref, r_ref)


def _ret_bwd_kernel(q_ref, k_ref, v_ref, dec_ref, xi_ref, zeta_ref, cdec_ref, of_ref, gate_ref, ng_ref,
                    o_ref, r_ref):
    @pl.when(pl.program_id(1) == 0)
    def _():
        r_ref[...] = jnp.zeros(r_ref.shape, F32)

    o = of_ref[...] + _ret_chunk(q_ref[...], k_ref[...], v_ref[...], dec_ref, xi_ref, zeta_ref, cdec_ref, r_ref)
    ng = ng_ref[...]
    gate = gate_ref[...].astype(F32)
    for h in range(C_HEADS):
        sl = slice(h * C_VAL_DIM, (h + 1) * C_VAL_DIM)
        oh = o[:, sl]
        mu = jnp.mean(oh, axis=1, keepdims=True)
        var = jnp.mean(jnp.square(oh - mu), axis=1, keepdims=True)
        y = (oh - mu) * lax.rsqrt(var + EPS) * ng[:, sl]
        gh = gate[:, sl]
        o_ref[:, sl] = (gh * jax.nn.sigmoid(gh) * y).astype(o_ref.dtype)


def _ret_tables(dec_param, chunk, strict):
    log_g = -jnp.exp(dec_param.astype(F32))
    j = jnp.arange(chunk, dtype=F32)
    lg = log_g[:, None, None]
    if strict:
        diff = j[None, :] - j[:, None]
        dec = jnp.where((diff > 0)[None], jnp.exp(jnp.maximum(diff, 0.0)[None] * lg), 0.0)
        xi = jnp.exp((chunk - j)[None, :, None] * lg)
        zeta = jnp.exp(j[None, :, None] * lg)
    else:
        diff = j[:, None] - j[None, :]
        dec = jnp.where((diff >= 0)[None], jnp.exp(jnp.maximum(diff, 0.0)[None] * lg), 0.0)
        xi = jnp.exp((j + 1.0)[None, :, None] * lg)
        zeta = jnp.exp((chunk - 1.0 - j)[None, :, None] * lg)
    xi = jnp.broadcast_to(xi, (C_HEADS, chunk, C_VAL_DIM))
    zeta = jnp.broadcast_to(zeta, (C_HEADS, chunk, C_KEY_DIM))
    cdec = jnp.broadcast_to(jnp.exp(chunk * log_g)[:, None, None], (C_HEADS, C_KEY_DIM, C_VAL_DIM))
    return dec, xi, zeta, cdec


C_CHUNK = 512


def _mixer_c(cq, ck, proj, tabs_f, tabs_b, norm_g, prev, row_off, b, s):
    n = cq.shape[0]
    prev_f, prev_o = (None, None) if prev is None else ([prev[0]], [prev[1]])
    nc = s // C_CHUNK
    r0 = row_off // C_CHUNK
    v_col = OFF["cv"] // C_V
    g_col = OFF["cg"] // C_V

    def fwd_rows(ib, c):
        return r0 + ib * nc + c

    def bwd_rows(ib, c):
        return r0 + ib * nc + (nc - 1 - c)

    def specs(rows):
        def full(shape):
            return pl.BlockSpec(shape, lambda ib, c: (0,) * len(shape))
        return [pl.BlockSpec((C_CHUNK, C_QK), lambda ib, c: (rows(ib, c), 0)),
                pl.BlockSpec((C_CHUNK, C_QK), lambda ib, c: (rows(ib, c), 0)),
                pl.BlockSpec((C_CHUNK, C_V), lambda ib, c: (rows(ib, c), v_col)),
                full((C_HEADS, C_CHUNK, C_CHUNK)), full((C_HEADS, C_CHUNK, C_VAL_DIM)),
                full((C_HEADS, C_CHUNK, C_KEY_DIM)), full((C_HEADS, C_KEY_DIM, C_VAL_DIM))]

    scratch = [pltpu.VMEM((C_HEADS, C_KEY_DIM, C_VAL_DIM), F32)]
    o_f, = _seq_call(
        _ret_fwd_kernel, prev_f,
        grid=(b, nc),
        in_specs=specs(fwd_rows),
        out_specs=[pl.BlockSpec((C_CHUNK, C_V), lambda ib, c: (fwd_rows(ib, c), 0))],
        out_shape=[jax.ShapeDtypeStruct((n, C_V), F32)],
        scratch_shapes=scratch,
        compiler_params=_cparams(("parallel", "arbitrary")),
        name="ret_fwd",
        args=(cq, ck, proj, *tabs_f))
    o_c, = _seq_call(
        _ret_bwd_kernel, prev_o,
        grid=(b, nc),
        in_specs=specs(bwd_rows) + [
            pl.BlockSpec((C_CHUNK, C_V), lambda ib, c: (bwd_rows(ib, c), 0)),
            pl.BlockSpec((C_CHUNK, C_V), lambda ib, c: (bwd_rows(ib, c), g_col)),
            pl.BlockSpec((1, C_V), lambda ib, c: (0, 0))],
        out_specs=[pl.BlockSpec((C_CHUNK, C_V), lambda ib, c: (bwd_rows(ib, c), 0))],
        out_shape=[jax.ShapeDtypeStruct((n, C_V), BF16)],
        scratch_shapes=scratch,
        compiler_params=_cparams(("parallel", "arbitrary")),
        name="ret_bwd",
        args=(cq, ck, proj, *tabs_b, o_f, proj, norm_g))
    return [o_f, o_c]


def _merge_kernel(oa_ref, ob0_ref, ob1_ref, ob2_ref, l0_ref, l1_ref, l2_ref, oc_ref,
                  ga_ref, gb_ref, gc_ref, x_ref, woa_ref, wob_ref, woc_ref, wout_ref, n2_ref,
                  xo_ref, hn_ref, scr_ref):
    tm = x_ref.shape[0]

    def natural(src_ref, dil, slot):
        if dil == 1:
            return src_ref[...].astype(F32)
        chunks = B_KV // LANES
        for r in range(dil):
            for c in range(chunks):
                col = r * B_KV + c * LANES
                scr_ref[slot, c, pl.ds(r, tm // dil, stride=dil), :] = src_ref[:, col:col + LANES].astype(F32)
        return jnp.concatenate([scr_ref[slot, c] for c in range(chunks)], axis=1)

    dils = [dil for _, dil in B_PATTERNS]
    l0, l1, l2 = [natural(ref, dil, 2 * g) for g, (ref, dil) in enumerate(zip((l0_ref, l1_ref, l2_ref), dils))]
    o0, o1, o2 = [natural(ref, dil, 2 * g + 1)
                  for g, (ref, dil) in enumerate(zip((ob0_ref, ob1_ref, ob2_ref), dils))]
    mx = jnp.maximum(jnp.maximum(l0, l1), l2)
    e0, e1, e2 = jnp.exp(l0 - mx), jnp.exp(l1 - mx), jnp.exp(l2 - mx)
    ob = (e0 * o0 + e1 * o1 + e2 * o2) / (e0 + e1 + e2)

    def branch(o, w_ref, gate_ref):
        y = jnp.dot(o, w_ref[...], preferred_element_type=F32)
        return jax.nn.sigmoid(gate_ref[...].astype(F32)) * y

    oa = jnp.transpose(oa_ref[...].astype(F32)).astype(BF16)
    merged = (branch(oa, woa_ref, ga_ref) + branch(ob.astype(BF16), wob_ref, gb_ref)
              + branch(oc_ref[...], woc_ref, gc_ref))
    x = x_ref[...] + jnp.dot(merged.astype(BF16), wout_ref[...], preferred_element_type=F32)
    xo_ref[...] = x
    ms = jnp.mean(x * x, axis=-1, keepdims=True)
    hn_ref[...] = (x * lax.rsqrt(ms + EPS) * n2_ref[...]).astype(hn_ref.dtype)


def _merge(o_a, o_b, lse_b, o_c, proj, x, w_oa, w_ob, w_oc, w_out, n2, hn_dtype):
    n = x.shape[0]
    tm = min(512, n)
    gl0 = OFF["gl"] // D_MODEL

    def rows(width):
        return pl.BlockSpec((tm, width), lambda i: (i, 0))

    def gate(k):
        return pl.BlockSpec((tm, D_MODEL), lambda i: (i, gl0 + k))

    def full(r, c):
        return pl.BlockSpec((r, c), lambda i: (0, 0))

    grouped = [pl.BlockSpec((tm // dil, dil * B_KV), lambda i: (i, 0)) for _, dil in B_PATTERNS]
    return pl.pallas_call(
        _merge_kernel,
        grid=(n // tm,),
        in_specs=[pl.BlockSpec((A_Q, tm), lambda i: (0, i))] + grouped + grouped
                 + [rows(C_V), gate(0), gate(1), gate(2), rows(D_MODEL),
                    full(A_Q, D_MODEL), full(B_KV, D_MODEL), full(C_V, D_MODEL), full(D_MODEL, D_MODEL),
                    full(1, D_MODEL)],
        out_specs=[rows(D_MODEL), rows(D_MODEL)],
        out_shape=[jax.ShapeDtypeStruct((n, D_MODEL), F32), jax.ShapeDtypeStruct((n, D_MODEL), hn_dtype)],
        scratch_shapes=[pltpu.VMEM((2 * B_GROUPS, B_KV // LANES, tm, LANES), F32)],
        compiler_params=_cparams(("parallel",)),
        name="merge_out",
    )(o_a, *o_b, *lse_b, o_c, proj, proj, proj, x, w_oa, w_ob, w_oc, w_out, n2)


def _ffn_kernel(hn_ref, x_ref, w1_ref, w3_ref, w2_ref, o_ref, acc_ref, *, nf):
    j = pl.program_id(1)
    hn = hn_ref[...]
    a = jnp.dot(hn, w1_ref[...], preferred_element_type=F32)
    g = jnp.dot(hn, w3_ref[...], preferred_element_type=F32)
    h = (a * jax.nn.sigmoid(a) * g).astype(BF16)
    y = jnp.dot(h, w2_ref[...], preferred_element_type=F32)

    @pl.when(j == 0)
    def _():
        acc_ref[...] = x_ref[...] + y

    @pl.when(j > 0)
    def _():
        acc_ref[...] += y

    @pl.when(j == nf - 1)
    def _():
        o_ref[...] = acc_ref[...]


def _ffn(hn, x, w1, w3, w2):
    n = x.shape[0]
    d_ff = w1.shape[1]
    tm = min(512, n)
    tf = d_ff // 2
    nf = d_ff // tf
    return pl.pallas_call(
        functools.partial(_ffn_kernel, nf=nf),
        grid=(n // tm, nf),
        in_specs=[pl.BlockSpec((tm, D_MODEL), lambda i, j: (i, 0)),
                  pl.BlockSpec((tm, D_MODEL), lambda i, j: (i, 0)),
                  pl.BlockSpec((D_MODEL, tf), lambda i, j: (0, j)),
                  pl.BlockSpec((D_MODEL, tf), lambda i, j: (0, j)),
                  pl.BlockSpec((tf, D_MODEL), lambda i, j: (j, 0))],
        out_specs=pl.BlockSpec((tm, D_MODEL), lambda i, j: (i, 0)),
        out_shape=jax.ShapeDtypeStruct((n, D_MODEL), F32),
        scratch_shapes=[pltpu.VMEM((tm, D_MODEL), F32)],
        compiler_params=_cparams(("parallel", "arbitrary")),
        name="ffn",
    )(hn, x, w1, w3, w2)


TOP_K = 2
MOE_TM = 512


def _router_kernel(hn_ref, wr_ref, idx_ref, wts_ref):
    logits = jnp.dot(hn_ref[...], wr_ref[...], preferred_element_type=F32,
                     precision=lax.Precision.HIGHEST)
    col = lax.broadcasted_iota(jnp.int32, logits.shape, 1)
    m1 = jnp.max(logits, axis=1, keepdims=True)
    i1 = jnp.min(jnp.where(logits == m1, col, N_EXPERTS), axis=1, keepdims=True)
    rest = jnp.where(col == i1, -jnp.inf, logits)
    m2 = jnp.max(rest, axis=1, keepdims=True)
    i2 = jnp.min(jnp.where(rest == m2, col, N_EXPERTS), axis=1, keepdims=True)
    e2 = jnp.exp(m2 - m1)
    w1 = 1.0 / (1.0 + e2)
    idx_ref[...] = jnp.concatenate([i1, i2], axis=1)
    wts_ref[...] = jnp.concatenate([w1, e2 * w1], axis=1)


def _router(hn, wr):
    n = hn.shape[0]
    tm = min(1024, n)
    return pl.pallas_call(
        _router_kernel,
        grid=(n // tm,),
        in_specs=[pl.BlockSpec((tm, D_MODEL), lambda i: (i, 0)),
                  pl.BlockSpec((D_MODEL, N_EXPERTS), lambda i: (0, 0))],
        out_specs=[pl.BlockSpec((tm, TOP_K), lambda i: (i, 0)), pl.BlockSpec((tm, TOP_K), lambda i: (i, 0))],
        out_shape=[jax.ShapeDtypeStruct((n, TOP_K), jnp.int32), jax.ShapeDtypeStruct((n, TOP_K), F32)],
        compiler_params=_cparams(("parallel",)),
        name="router",
    )(hn, wr)


def _route(idx, tm):
    n = idx.shape[0]
    e_flat = idx.reshape(-1)
    onehot = (e_flat[:, None] == jnp.arange(N_EXPERTS, dtype=jnp.int32)[None, :]).astype(jnp.int32)
    csum = jnp.cumsum(onehot, axis=0)
    rank = jnp.sum((csum - onehot) * onehot, axis=1)
    gsz = ((csum[-1] + tm - 1) // tm) * tm
    gend = jnp.cumsum(gsz)
    pos = (gend - gsz)[e_flat] + rank
    n_rows = n * TOP_K + N_EXPERTS * tm
    n_tiles = n_rows // tm
    row_token = jnp.zeros((n_rows,), jnp.int32).at[pos].set(jnp.arange(n * TOP_K, dtype=jnp.int32) // TOP_K)
    tile_start = jnp.arange(n_tiles, dtype=jnp.int32) * tm
    tile_expert = jnp.minimum(jnp.sum((tile_start[:, None] >= gend[None, :]).astype(jnp.int32), axis=1),
                              N_EXPERTS - 1)
    n_used = (gend[-1] // tm).astype(jnp.int32).reshape(1)
    return pos.reshape(n, TOP_K), row_token.reshape(n_tiles, 1, tm), tile_expert, n_used


def _row_copy(src_ref, src_row, dst_ref, dst_row, sem):
    return pltpu.make_async_copy(src_ref.at[pl.ds(src_row, 1)], dst_ref.at[pl.ds(dst_row, 1)], sem)


def _expert_kernel(te_ref, nu_ref, idx_ref, nxt_ref, src_ref, w1_ref, w3_ref, w2_ref, y_ref,
                   xs_ref, xb_ref, acc_ref, sem, *, nf, tm):
    i = pl.program_id(0)
    j = pl.program_id(1)
    n_used = nu_ref[0]
    active = i < n_used
    slot = i % 2
    share = tm // nf

    @pl.when((i == 0) & (j == 0))
    def _():
        def issue(r, carry):
            _row_copy(src_ref, idx_ref[0, 0, r], xs_ref.at[0], r, sem.at[0]).start()
            return carry
        lax.fori_loop(0, tm, issue, 0, unroll=8)

    @pl.when((j == 0) & (i <= n_used))
    def _():
        pltpu.make_async_copy(src_ref.at[pl.ds(0, tm)], xs_ref.at[slot], sem.at[slot]).wait()

    @pl.when(active)
    def _():
        @pl.when(j == 0)
        def _():
            xb_ref[...] = xs_ref[slot].astype(BF16)

        for r in range(share):
            row = j * share + r
            _row_copy(src_ref, nxt_ref[0, 0, row], xs_ref.at[1 - slot], row, sem.at[1 - slot]).start()

        xb = xb_ref[...]
        a = jnp.dot(xb, w1_ref[0], preferred_element_type=F32)
        g = jnp.dot(xb, w3_ref[0], preferred_element_type=F32)
        h = (a * jax.nn.sigmoid(a) * g).astype(BF16)
        y = jnp.dot(h, w2_ref[0], preferred_element_type=F32)

        @pl.when(j == 0)
        def _():
            acc_ref[...] = y

        @pl.when(j > 0)
        def _():
            acc_ref[...] += y

        @pl.when(j == nf - 1)
        def _():
            y_ref[...] = acc_ref[...]

    @pl.when(jnp.logical_not(active) & (j == nf - 1))
    def _():
        y_ref[...] = jnp.zeros(y_ref.shape, y_ref.dtype)


def _experts(src, row_token, tile_expert, n_used, w1, w3, w2):
    nt, _, tm = row_token.shape
    d_ff = w1.shape[2]
    nf = 2
    tf = d_ff // nf

    def jcol(i, j, nu):
        return jnp.where(i < nu[0], j, nf - 1)

    grid_spec = pltpu.PrefetchScalarGridSpec(
        num_scalar_prefetch=2,
        grid=(nt, nf),
        in_specs=[pl.BlockSpec((1, 1, tm), lambda i, j, te, nu: (i, 0, 0), memory_space=pltpu.SMEM),
                  pl.BlockSpec((1, 1, tm), lambda i, j, te, nu: (jnp.minimum(i + 1, nt - 1), 0, 0),
                               memory_space=pltpu.SMEM),
                  pl.BlockSpec(memory_space=pl.ANY),
                  pl.BlockSpec((1, D_MODEL, tf), lambda i, j, te, nu: (te[i], 0, jcol(i, j, nu))),
                  pl.BlockSpec((1, D_MODEL, tf), lambda i, j, te, nu: (te[i], 0, jcol(i, j, nu))),
                  pl.BlockSpec((1, tf, D_MODEL), lambda i, j, te, nu: (te[i], jcol(i, j, nu), 0))],
        out_specs=pl.BlockSpec((tm, D_MODEL), lambda i, j, te, nu: (i, 0)),
        scratch_shapes=[pltpu.VMEM((2, tm, D_MODEL), src.dtype), pltpu.VMEM((tm, D_MODEL), BF16),
                        pltpu.VMEM((tm, D_MODEL), F32), pltpu.SemaphoreType.DMA((2,))],
    )
    return pl.pallas_call(
        functools.partial(_expert_kernel, nf=nf, tm=tm),
        grid_spec=grid_spec,
        out_shape=jax.ShapeDtypeStruct((nt * tm, D_MODEL), F32),
        compiler_params=_cparams(("arbitrary", "arbitrary")),
        name="moe_experts",
    )(tile_expert, n_used, row_token, row_token, src, w1, w3, w2)


def _combine_kernel(pos_ref, nxt_ref, wts_ref, x_ref, y_ref, o_ref, g_ref, sem, *, tm, nt):
    i = pl.program_id(0)
    slot = i % 2

    def request(p_ref, dst_slot):
        def issue(r, carry):
            for k in range(TOP_K):
                _row_copy(y_ref, p_ref[0, 0, TOP_K * r + k], g_ref.at[dst_slot, k], r, sem.at[dst_slot]).start()
            return carry
        lax.fori_loop(0, tm, issue, 0, unroll=8)

    @pl.when(i == 0)
    def _():
        request(pos_ref, 0)

    @pl.when(i + 1 < nt)
    def _():
        request(nxt_ref, 1 - slot)

    for k in range(TOP_K):
        pltpu.make_async_copy(y_ref.at[pl.ds(0, tm)], g_ref.at[slot, k], sem.at[slot]).wait()
    w = wts_ref[...]
    o_ref[...] = x_ref[...] + w[:, 0:1] * g_ref[slot, 0] + w[:, 1:2] * g_ref[slot, 1]


def _combine(pos, wts, x, y):
    n = x.shape[0]
    tm = min(256, n)
    nt = n // tm
    pos3 = pos.reshape(nt, 1, TOP_K * tm)
    return pl.pallas_call(
        functools.partial(_combine_kernel, tm=tm, nt=nt),
        grid=(nt,),
        in_specs=[pl.BlockSpec((1, 1, TOP_K * tm), lambda i: (i, 0, 0), memory_space=pltpu.SMEM),
                  pl.BlockSpec((1, 1, TOP_K * tm), lambda i: (jnp.minimum(i + 1, nt - 1), 0, 0),
                               memory_space=pltpu.SMEM),
                  pl.BlockSpec((tm, TOP_K), lambda i: (i, 0)),
                  pl.BlockSpec((tm, D_MODEL), lambda i: (i, 0)),
                  pl.BlockSpec(memory_space=pl.ANY)],
        out_specs=pl.BlockSpec((tm, D_MODEL), lambda i: (i, 0)),
        out_shape=jax.ShapeDtypeStruct((n, D_MODEL), F32),
        scratch_shapes=[pltpu.VMEM((2, TOP_K, tm, D_MODEL), F32), pltpu.SemaphoreType.DMA((2,))],
        compiler_params=_cparams(("arbitrary",)),
        name="moe_combine",
    )(pos3, pos3, wts, x, y)


def _moe(hn, x, wr, w1, w3, w2):
    idx, wts = _router(hn, wr)
    pos, row_token, tile_expert, n_used = _route(idx, MOE_TM)
    y = _experts(hn, row_token, tile_expert, n_used, w1, w3, w2)
    return _combine(pos, wts, x, y)


def _permute_w_in(w):
    cols = [w[:, _ORIG[name][0]:_ORIG[name][0] + _ORIG[name][1]] for name in _NEW_ORDER]
    cols.append(jnp.zeros((w.shape[0], PROJ_W - _USED_W), w.dtype))
    return jnp.concatenate(cols, axis=1).astype(BF16)


def _angles(pos, dim):
    inv = ROPE_THETA ** (-jnp.arange(0, dim, 2, dtype=F32) / dim)
    return pos.astype(F32)[:, None] * inv[None, :]


def _rope_tables(seqs):
    pos = jnp.concatenate([jnp.tile(jnp.arange(s), b) for b, s in seqs])
    ang = _angles(pos, HEAD_DIM)
    cosb = jnp.tile(jnp.cos(ang), (1, 4))
    sinb = jnp.tile(jnp.concatenate([-jnp.sin(ang), jnp.sin(ang)], axis=1), (1, 2))
    ar = _angles(pos // GRID_W, HEAD_DIM // 2)
    ac = _angles(pos % GRID_W, HEAD_DIM // 2)
    cosa = jnp.tile(jnp.concatenate([jnp.cos(ar), jnp.cos(ar), jnp.cos(ac), jnp.cos(ac)], axis=1), (1, 2))
    sina = jnp.tile(jnp.concatenate([-jnp.sin(ar), jnp.sin(ar), -jnp.sin(ac), jnp.sin(ac)], axis=1), (1, 2))
    return cosb, sinb, cosa, sina


def _head_mean_matrix():
    blk = np.kron(np.eye(LANES // HEAD_DIM), np.ones((HEAD_DIM, HEAD_DIM))) / HEAD_DIM
    return jnp.asarray(blk, BF16)


def _gain_rows(a_qn, a_kn, b_qn, b_kn):
    scale = HEAD_DIM ** -0.5
    scale_a = scale * np.log2(np.e)
    rows = [jnp.tile(b_qn, 2) * scale, jnp.tile(b_kn, 2), jnp.tile(a_qn, 2) * scale_a, jnp.tile(a_kn, 2),
            jnp.ones((LANES,), F32), jnp.full((LANES,), C_KEY_DIM ** -0.5, F32),
            jnp.ones((LANES,), F32), jnp.ones((LANES,), F32)]
    return jnp.stack(rows).astype(F32)


def _trunk(x, seqs, norm1_g, w_in, a_qn, a_kn, b_qn, b_kn, ret_dec_f, ret_dec_b, ret_norm_g,
           w_oa, w_ob, w_oc, w_out, norm2_g, ffn_w1, ffn_w3, ffn_w2,
           moe_router, moe_w1, moe_w3, moe_w2):
    depth = w_in.shape[0]
    tables = _rope_tables(seqs)
    mmat = _head_mean_matrix()
    for l in range(depth):
        proj = _inproj(x, norm1_g[l][None, :], _permute_w_in(w_in[l]))
        prepped = _prep(proj, tables, _gain_rows(a_qn[l], a_kn[l], b_qn[l], b_kn[l]), mmat)
        bqs, bks, bvs = prepped[0:3], prepped[3:6], prepped[6:9]
        aq, ak, cq, ck = prepped[9:]
        aqt = aq.T
        avt = proj[:, OFF["av"]:OFF["av"] + A_KV].T
        tabs_f = _ret_tables(ret_dec_f[l], C_CHUNK, False)
        tabs_b = _ret_tables(ret_dec_b[l], C_CHUNK, True)
        ng = ret_norm_g[l][None, :].astype(F32)
        res_a, res_c = None, None
        res_b = [None] * B_GROUPS
        row_off = 0
        for b, s in seqs:
            res_a = _mixer_a(aqt, ak, avt, res_a, row_off, b, s)
            for g, (_, dil) in enumerate(B_PATTERNS):
                res_b[g] = _mixer_b_group(bqs[g], bks[g], bvs[g], res_b[g], g, dil, row_off, b, s)
            res_c = _mixer_c(cq, ck, proj, tabs_f, tabs_b, ng, res_c, row_off, b, s)
            row_off += b * s
        x, hn = _merge(res_a[0], [r[0] for r in res_b], [r[1] for r in res_b], res_c[1], proj, x,
                       w_oa[l].astype(BF16), w_ob[l].astype(BF16), w_oc[l].astype(BF16),
                       w_out[l].astype(BF16), norm2_g[l][None, :], BF16 if l % 2 == 0 else F32)
        i = l // 2
        if l % 2 == 0:
            x = _ffn(hn, x, ffn_w1[i].astype(BF16), ffn_w3[i].astype(BF16), ffn_w2[i].astype(BF16))
        else:
            x = _moe(hn, x, moe_router[i], moe_w1[i].astype(BF16), moe_w3[i].astype(BF16),
                     moe_w2[i].astype(BF16))
    return x


def kernel(x_prompt, x_sample, norm1_g, w_in, a_qn, a_kn, b_qn, b_kn, ret_dec_f, ret_dec_b, ret_norm_g,
           w_oa, w_ob, w_oc, w_out, norm2_g, ffn_w1, ffn_w3, ffn_w2, moe_router, moe_w1, moe_w3, moe_w2):
    seqs = (x_prompt.shape[:2], x_sample.shape[:2])
    x = jnp.concatenate([x_prompt.reshape(-1, D_MODEL), x_sample.reshape(-1, D_MODEL)], axis=0)
    y = _trunk(x, seqs, norm1_g, w_in, a_qn, a_kn, b_qn, b_kn, ret_dec_f, ret_dec_b, ret_norm_g,
               w_oa, w_ob, w_oc, w_out, norm2_g, ffn_w1, ffn_w3, ffn_w2,
               moe_router, moe_w1, moe_w3, moe_w2)
    n_p = x_prompt.shape[0] * x_prompt.shape[1]
    return (y[:n_p].reshape(x_prompt.shape), y[n_p:].reshape(x_sample.shape))
```

```python
import functools

import numpy as np
import jax
import jax.numpy as jnp
from jax import lax
from jax.experimental import pallas as pl
from jax.experimental.pallas import tpu as pltpu

F32 = jnp.float32
BF16 = jnp.bfloat16

D_MODEL = 1024
GRID_W = 64
HEAD_DIM = 64
ROPE_THETA = 10000.0
EPS = 1e-6
A_HEADS = 8
A_KV_HEADS = 2
B_PATTERNS = ((128, 1), (512, 4), (2048, 16))
B_GROUPS = 3
B_HEADS = 8
C_HEADS = 4
C_KEY_DIM = 64
C_VAL_DIM = 128
N_EXPERTS = 8
N_BRANCHES = 3

A_Q = A_HEADS * HEAD_DIM
A_KV = A_KV_HEADS * HEAD_DIM
B_Q = B_GROUPS * B_HEADS * HEAD_DIM
B_KV = B_HEADS * HEAD_DIM
C_QK = C_HEADS * C_KEY_DIM
C_V = C_HEADS * C_VAL_DIM

LANES = 128
VMEM_LIMIT = 56 * 1024 * 1024

_ORIG_SPLITS = (("aq", A_Q), ("ak", A_KV), ("av", A_KV), ("bq", B_Q), ("bk", B_KV), ("bv", B_KV),
                ("cq", C_QK), ("ck", C_QK), ("cv", C_V), ("cg", C_V), ("gl", N_BRANCHES * D_MODEL))
_NEW_ORDER = ("bq", "bk", "aq", "bv", "cv", "cg", "gl", "cq", "ck", "ak", "av")
PROJ_W = 8192


def _layout():
    orig, o = {}, 0
    for name, w in _ORIG_SPLITS:
        orig[name] = (o, w)
        o += w
    new, o = {}, 0
    for name in _NEW_ORDER:
        new[name] = o
        o += orig[name][1]
    return orig, new, o


_ORIG, OFF, _USED_W = _layout()


def _cparams(sem):
    return pltpu.CompilerParams(dimension_semantics=sem, vmem_limit_bytes=VMEM_LIMIT)


def _seq_call(kern, prev, *, in_specs, out_specs, out_shape, args, **kwargs):
    n_in = len(in_specs)
    if prev is None:
        return pl.pallas_call(kern, in_specs=in_specs, out_specs=out_specs, out_shape=out_shape,
                              **kwargs)(*args)
    n_prev = len(prev)

    def chained(*refs):
        return kern(*refs[:n_in], *refs[n_in + n_prev:])

    return pl.pallas_call(
        chained,
        in_specs=list(in_specs) + [pl.BlockSpec(memory_space=pl.ANY)] * n_prev,
        out_specs=out_specs, out_shape=out_shape,
        input_output_aliases={n_in + k: k for k in range(n_prev)},
        **kwargs)(*args, *prev)


def _inproj_kernel(x_ref, g_ref, w_ref, o_ref, xn_ref):
    @pl.when(pl.program_id(1) == 0)
    def _():
        x = x_ref[...]
        ms = jnp.mean(x * x, axis=-1, keepdims=True)
        xn_ref[...] = (x * lax.rsqrt(ms + EPS) * g_ref[...]).astype(BF16)

    o_ref[...] = jnp.dot(xn_ref[...], w_ref[...], preferred_element_type=F32).astype(BF16)


def _inproj(x, g, w):
    n = x.shape[0]
    tm = min(1024, n)
    tn = 1024
    return pl.pallas_call(
        _inproj_kernel,
        grid=(n // tm, PROJ_W // tn),
        in_specs=[pl.BlockSpec((tm, D_MODEL), lambda i, j: (i, 0)),
                  pl.BlockSpec((1, D_MODEL), lambda i, j: (0, 0)),
                  pl.BlockSpec((D_MODEL, tn), lambda i, j: (0, j))],
        out_specs=pl.BlockSpec((tm, tn), lambda i, j: (i, j)),
        out_shape=jax.ShapeDtypeStruct((n, PROJ_W), BF16),
        scratch_shapes=[pltpu.VMEM((tm, D_MODEL), BF16)],
        compiler_params=_cparams(("parallel", "arbitrary")),
        name="inproj",
    )(x, g, w)


def _norm_rot(x, gain, cos, sin, mmat, half, do_norm):
    if do_norm:
        ms = jnp.dot((x * x).astype(BF16), mmat, preferred_element_type=F32)
        x = x * lax.rsqrt(ms + EPS)
    x = x * gain
    lane = lax.broadcasted_iota(jnp.int32, x.shape, 1)
    first = (lane % (2 * half)) < half
    swapped = jnp.where(first, pltpu.roll(x, LANES - half, 1), pltpu.roll(x, half, 1))
    return x * cos + swapped * sin


def _prep_kernel(bqk_ref, aq_ref, bv_ref, cq_ref, ck_ref, ak_ref,
                 cosb_ref, sinb_ref, cosa_ref, sina_ref, gain_ref, mmat_ref,
                 q0_ref, q1_ref, q2_ref, k0_ref, k1_ref, k2_ref, v0_ref, v1_ref, v2_ref,
                 oaq_ref, oak_ref, ocq_ref, ock_ref, scr_ref):
    cosb, sinb = cosb_ref[...], sinb_ref[...]
    cosa, sina = cosa_ref[...], sina_ref[...]
    mmat = mmat_ref[...]
    gains = gain_ref[...]
    tm = scr_ref.shape[0]

    def prepared(src_ref, src_off, c, gain_row, cos, sin, half, do_norm):
        x = src_ref[:, src_off + c * LANES:src_off + (c + 1) * LANES].astype(F32)
        return _norm_rot(x, gains[gain_row:gain_row + 1, :], cos, sin, mmat, half, do_norm)

    def run(src_ref, dst_ref, width, gain_row, cos, sin, half, do_norm):
        for c in range(width // LANES):
            y = prepared(src_ref, 0, c, gain_row, cos, sin, half, do_norm)
            dst_ref[:, c * LANES:(c + 1) * LANES] = y.astype(dst_ref.dtype)

    def store_dilated(y, dst_ref, dil, c):
        if dil == 1:
            dst_ref[:, c * LANES:(c + 1) * LANES] = y.astype(dst_ref.dtype)
            return
        scr_ref[...] = y
        for r in range(dil):
            rows = scr_ref[pl.ds(r, tm // dil, stride=dil), :]
            dst_ref[:, r * B_KV + c * LANES:r * B_KV + (c + 1) * LANES] = rows.astype(dst_ref.dtype)

    chunks = B_KV // LANES
    for g, (q_ref, (_, dil)) in enumerate(zip((q0_ref, q1_ref, q2_ref), B_PATTERNS)):
        for c in range(chunks):
            y = prepared(bqk_ref, g * B_KV, c, 0, cosb, sinb, HEAD_DIM // 2, True)
            store_dilated(y, q_ref, dil, c)
    for c in range(chunks):
        yk = prepared(bqk_ref, B_Q, c, 1, cosb, sinb, HEAD_DIM // 2, True)
        yv = bv_ref[:, c * LANES:(c + 1) * LANES].astype(F32)
        for (_, dil), k_ref, v_ref in zip(B_PATTERNS, (k0_ref, k1_ref, k2_ref), (v0_ref, v1_ref, v2_ref)):
            store_dilated(yk, k_ref, dil, c)
            store_dilated(yv, v_ref, dil, c)
    run(aq_ref, oaq_ref, A_Q, 2, cosa, sina, HEAD_DIM // 4, True)
    run(ak_ref, oak_ref, A_KV, 3, cosa, sina, HEAD_DIM // 4, True)
    run(cq_ref, ocq_ref, C_QK, 4, cosb, sinb, C_KEY_DIM // 2, False)
    run(ck_ref, ock_ref, C_QK, 5, cosb, sinb, C_KEY_DIM // 2, False)


def _prep(proj, tables, gains, mmat):
    n = proj.shape[0]
    tm = min(512, n)
    cosb, sinb, cosa, sina = tables

    def col(width, name):
        idx = OFF[name] // width
        return pl.BlockSpec((tm, width), lambda i: (i, idx))

    def tab():
        return pl.BlockSpec((tm, LANES), lambda i: (i, 0))

    dils = [dil for _, dil in B_PATTERNS]
    shapes = [(n // dil, dil * B_KV) for dil in dils] * 3 + [(n, A_Q), (n, A_KV), (n, C_QK), (n, C_QK)]
    blocks = [(tm // dil, dil * B_KV) for dil in dils] * 3 + [(tm, A_Q), (tm, A_KV), (tm, C_QK), (tm, C_QK)]
    return pl.pallas_call(
        _prep_kernel,
        grid=(n // tm,),
        in_specs=[col(B_Q + B_KV, "bq"), col(A_Q, "aq"), col(B_KV, "bv"), col(C_QK, "cq"),
                  col(C_QK, "ck"), col(A_KV, "ak"), tab(), tab(), tab(), tab(),
                  pl.BlockSpec((8, LANES), lambda i: (0, 0)),
                  pl.BlockSpec((LANES, LANES), lambda i: (0, 0))],
        out_specs=[pl.BlockSpec(blk, lambda i: (i, 0)) for blk in blocks],
        out_shape=[jax.ShapeDtypeStruct(shp, BF16) for shp in shapes],
        scratch_shapes=[pltpu.VMEM((tm, LANES), F32)],
        compiler_params=_cparams(("parallel",)),
        name="prep",
    )(proj, proj, proj, proj, proj, proj, cosb, sinb, cosa, sina, gains, mmat)


A_ONES_ROWS = 16


def _attn_a_kernel(small_ref, qt_ref, k0_ref, kn_ref, vt_ref, o_ref, qp_ref, m_ref, acc_ref, sa_ref, sb_ref,
                   *, bq, nk):
    ik = pl.program_id(2)
    rep = A_HEADS // A_KV_HEADS
    kv_heads = range(A_KV_HEADS)

    @pl.when(ik == 0)
    def _():
        qp_ref[...] = jnp.zeros(qp_ref.shape, BF16)
        for g in kv_heads:
            for r in range(rep):
                h = g * rep + r
                qp_ref[g, g * HEAD_DIM:(g + 1) * HEAD_DIM, r * bq:(r + 1) * bq] = (
                    qt_ref[h * HEAD_DIM:(h + 1) * HEAD_DIM, :])
        m_ref[...] = jnp.full(m_ref.shape, -jnp.inf, F32)
        acc_ref[...] = jnp.zeros(acc_ref.shape, F32)
        for g in kv_heads:
            sa_ref[g] = jnp.dot(k0_ref[...], qp_ref[g], preferred_element_type=F32)

    def step(cur_ref, nxt_ref, has_next, small):
        kn = kn_ref[...]
        ones = jnp.ones((A_ONES_ROWS, kn.shape[0]), BF16)
        for g in kv_heads:
            v_aug = jnp.concatenate([vt_ref[g * HEAD_DIM:(g + 1) * HEAD_DIM, :], ones], axis=0)
            for r in range(rep):
                cols = slice(r * bq, (r + 1) * bq)
                if has_next:
                    nxt_ref[g, :, cols] = jnp.dot(kn, qp_ref[g, :, cols], preferred_element_type=F32)
                s = cur_ref[g, :, cols]
                if small:
                    pv = jnp.dot(v_aug, jnp.exp2(s).astype(BF16), preferred_element_type=F32)
                    acc_ref[g, :, cols] += pv
                    continue
                m_prev = m_ref[g, :, cols]
                m_new = jnp.maximum(m_prev, jnp.max(s, axis=0, keepdims=True))
                p = jnp.exp2(s - m_new).astype(BF16)
                alpha = jnp.exp2(m_prev - m_new)
                pv = jnp.dot(v_aug, p, preferred_element_type=F32)
                acc_ref[g, :, cols] = alpha * acc_ref[g, :, cols] + pv
                m_ref[g, :, cols] = m_new

    last = ik == nk - 1
    is_small = small_ref[0] != 0
    for parity, cur_ref, nxt_ref in ((0, sa_ref, sb_ref), (1, sb_ref, sa_ref)):
        for has_next in (True, False):
            for small in (True, False):
                @pl.when((ik % 2 == parity) & (last != has_next) & (is_small == small))
                def _(cur_ref=cur_ref, nxt_ref=nxt_ref, has_next=has_next, small=small):
                    step(cur_ref, nxt_ref, has_next, small)

    @pl.when(last)
    def _():
        for g in range(A_KV_HEADS):
            acc = acc_ref[g]
            o = acc[:HEAD_DIM] / acc[HEAD_DIM:HEAD_DIM + 1]
            for r in range(rep):
                h = g * rep + r
                o_ref[h * HEAD_DIM:(h + 1) * HEAD_DIM, :] = o[:, r * bq:(r + 1) * bq].astype(o_ref.dtype)


A_SMALL_SCORE = 60.0


def _a_scores_small(a_qn, a_kn):
    bound = (HEAD_DIM ** 0.5) * np.log2(np.e) * 1.02 * jnp.max(jnp.abs(a_qn)) * jnp.max(jnp.abs(a_kn))
    return (bound < A_SMALL_SCORE).astype(jnp.int32).reshape(1)


def _mixer_a(small, aqt, ak, avt, prev, row_off, b, s):
    n = ak.shape[0]
    bq = min(512, s)
    bk = min(512, s)
    nq, nk = s // bq, s // bk
    rep = A_HEADS // A_KV_HEADS
    q0, k0 = row_off // bq, row_off // bk
    kern = functools.partial(_attn_a_kernel, bq=bq, nk=nk)
    return _seq_call(
        kern, prev,
        grid=(b, nq, nk),
        in_specs=[pl.BlockSpec(memory_space=pltpu.SMEM),
                  pl.BlockSpec((A_Q, bq), lambda ib, iq, ik: (0, q0 + ib * nq + iq)),
                  pl.BlockSpec((bk, A_KV), lambda ib, iq, ik: (k0 + ib * nk, 0)),
                  pl.BlockSpec((bk, A_KV), lambda ib, iq, ik: (k0 + ib * nk + jnp.minimum(ik + 1, nk - 1), 0)),
                  pl.BlockSpec((A_KV, bk), lambda ib, iq, ik: (0, k0 + ib * nk + ik))],
        out_specs=[pl.BlockSpec((A_Q, bq), lambda ib, iq, ik: (0, q0 + ib * nq + iq))],
        out_shape=[jax.ShapeDtypeStruct((A_Q, n), BF16)],
        scratch_shapes=[pltpu.VMEM((A_KV_HEADS, A_KV, rep * bq), BF16),
                        pltpu.VMEM((A_KV_HEADS, 1, rep * bq), F32),
                        pltpu.VMEM((A_KV_HEADS, HEAD_DIM + A_ONES_ROWS, rep * bq), F32),
                        pltpu.VMEM((A_KV_HEADS, bk, rep * bq), F32),
                        pltpu.VMEM((A_KV_HEADS, bk, rep * bq), F32)],
        compiler_params=_cparams(("parallel", "parallel", "arbitrary")),
        name="mixer_a",
        args=(small, aqt, ak, ak, avt))


B_BLOCK_Q = 128
B_HALO = 64


def _attn_b_kernel(q_ref, k0_ref, k1_ref, k2_ref, k3_ref, v0_ref, v1_ref, v2_ref, v3_ref,
                   o_ref, lse_ref, *, u_len):
    i = pl.program_id(2)
    kcat = jnp.concatenate([k0_ref[...], k1_ref[...], k2_ref[...], k3_ref[...]], axis=0)
    vcat = jnp.concatenate([v0_ref[...], v1_ref[...], v2_ref[...], v3_ref[...]], axis=0)
    nkeys = B_BLOCK_Q + 2 * B_HALO
    a = lax.broadcasted_iota(jnp.int32, (B_BLOCK_Q, nkeys), 0)
    c = lax.broadcasted_iota(jnp.int32, (B_BLOCK_Q, nkeys), 1)
    key_pos = i * B_BLOCK_Q - B_HALO + c
    valid = (c >= a) & (c <= a + 2 * B_HALO) & (key_pos >= 0) & (key_pos < u_len)
    left = lax.broadcasted_iota(jnp.int32, (B_BLOCK_Q, LANES), 1) < HEAD_DIM
    zero = jnp.zeros((B_BLOCK_Q, LANES), BF16)
    heads = [(pair, own) for pair in range(B_HEADS // 2) for own in (left, jnp.logical_not(left))]
    lanes = [slice(pair * LANES, (pair + 1) * LANES) for pair in range(B_HEADS // 2)]
    s = [lax.dot_general(jnp.where(own, q_ref[:, lanes[pair]], zero), kcat[:, lanes[pair]],
                         (((1,), (1,)), ((), ())), preferred_element_type=F32) for pair, own in heads]
    s = [jnp.where(valid, sh, -1e30) for sh in s]
    m = [jnp.max(sh, axis=1, keepdims=True) for sh in s]
    e = [jnp.exp(sh - mh) for sh, mh in zip(s, m)]
    den = [jnp.sum(eh, axis=1, keepdims=True) for eh in e]
    pv = [jnp.dot(eh.astype(BF16), vcat[:, lanes[pair]], preferred_element_type=F32)
          for eh, (pair, _) in zip(e, heads)]
    outs = [pvh / dh for pvh, dh in zip(pv, den)]
    lses = [mh + jnp.log(dh) for mh, dh in zip(m, den)]
    for pair in range(B_HEADS // 2):
        o_ref[:, lanes[pair]] = jnp.where(left, outs[2 * pair], outs[2 * pair + 1]).astype(o_ref.dtype)
        lse_ref[:, lanes[pair]] = jnp.where(left, lses[2 * pair], lses[2 * pair + 1])


def _mixer_b_group(qd, kd, vd, prev, g, dil, row_off, b, s):
    u_len = s // dil
    nq = u_len // B_BLOCK_Q
    nkb = u_len // B_HALO
    q_rows0 = row_off // dil // B_BLOCK_Q
    k_rows0 = row_off // dil // B_HALO

    def kspec(t):
        def imap(ib, r, i):
            blk = jnp.clip(2 * i - 1 + t, 0, nkb - 1)
            return (k_rows0 + ib * nkb + blk, r)
        return pl.BlockSpec((B_HALO, B_KV), imap)

    def qspec():
        return pl.BlockSpec((B_BLOCK_Q, B_KV), lambda ib, r, i: (q_rows0 + ib * nq + i, r))

    kern = functools.partial(_attn_b_kernel, u_len=u_len)
    return _seq_call(
        kern, prev,
        grid=(b, dil, nq),
        in_specs=[qspec()] + [kspec(t) for t in range(4)] + [kspec(t) for t in range(4)],
        out_specs=[qspec(), qspec()],
        out_shape=[jax.ShapeDtypeStruct(qd.shape, BF16), jax.ShapeDtypeStruct(qd.shape, F32)],
        compiler_params=_cparams(("parallel", "parallel", "parallel")),
        name=f"mixer_b{g}",
        args=(qd, kd, kd, kd, kd, vd, vd, vd, vd))


def _ret_chunk(q, k, v, dec_ref, xi_ref, zeta_ref, cdec_ref, r_ref):
    outs = []
    for h in range(C_HEADS):
        qh = q[:, h * C_KEY_DIM:(h + 1) * C_KEY_DIM]
        kh = k[:, h * C_KEY_DIM:(h + 1) * C_KEY_DIM]
        vh = v[:, h * C_VAL_DIM:(h + 1) * C_VAL_DIM]
        att = lax.dot_general(qh, kh, (((1,), (1,)), ((), ())), preferred_element_type=F32) * dec_ref[h]
        inner = jnp.dot(att.astype(BF16), vh, preferred_element_type=F32)
        r = r_ref[h]
        cross = jnp.dot(qh, r.astype(BF16), preferred_element_type=F32) * xi_ref[h]
        kz = (kh.astype(F32) * zeta_ref[h]).astype(BF16)
        r_ref[h] = r * cdec_ref[h] + lax.dot_general(kz, vh, (((0,), (0,)), ((), ())),
                                                     preferred_element_type=F32)
        outs.append(inner + cross)
    return jnp.concatenate(outs, axis=1)


def _ret_fwd_kernel(q_ref, k_ref, v_ref, dec_ref, xi_ref, zeta_ref, cdec_ref, o_ref, r_ref):
    @pl.when(pl.program_id(1) == 0)
    def _():
        r_ref[...] = jnp.zeros(r_ref.shape, F32)

    o_ref[...] = _ret_chunk(q_ref[...], k_ref[...], v_ref[...], dec_ref, xi_ref, zeta_ref, cdec_ref, r_ref)


def _ret_bwd_kernel(q_ref, k_ref, v_ref, dec_ref, xi_ref, zeta_ref, cdec_ref, of_ref, gate_ref, ng_ref,
                    o_ref, r_ref):
    @pl.when(pl.program_id(1) == 0)
    def _():
        r_ref[...] = jnp.zeros(r_ref.shape, F32)

    o = of_ref[...] + _ret_chunk(q_ref[...], k_ref[...], v_ref[...], dec_ref, xi_ref, zeta_ref, cdec_ref, r_ref)
    ng = ng_ref[...]
    gate = gate_ref[...].astype(F32)
    for h in range(C_HEADS):
        sl = slice(h * C_VAL_DIM, (h + 1) * C_VAL_DIM)
        oh = o[:, sl]
        mu = jnp.mean(oh, axis=1, keepdims=True)
        var = jnp.mean(jnp.square(oh - mu), axis=1, keepdims=True)
        y = (oh - mu) * lax.rsqrt(var + EPS) * ng[:, sl]
        gh = gate[:, sl]
        o_ref[:, sl] = (gh * jax.nn.sigmoid(gh) * y).astype(o_ref.dtype)


def _ret_tables(dec_param, chunk, strict):
    log_g = -jnp.exp(dec_param.astype(F32))
    j = jnp.arange(chunk, dtype=F32)
    lg = log_g[:, None, None]
    if strict:
        diff = j[None, :] - j[:, None]
        dec = jnp.where((diff > 0)[None], jnp.exp(jnp.maximum(diff, 0.0)[None] * lg), 0.0)
        xi = jnp.exp((chunk - j)[None, :, None] * lg)
        zeta = jnp.exp(j[None, :, None] * lg)
    else:
        diff = j[:, None] - j[None, :]
        dec = jnp.where((diff >= 0)[None], jnp.exp(jnp.maximum(diff, 0.0)[None] * lg), 0.0)
        xi = jnp.exp((j + 1.0)[None, :, None] * lg)
        zeta = jnp.exp((chunk - 1.0 - j)[None, :, None] * lg)
    xi = jnp.broadcast_to(xi, (C_HEADS, chunk, C_VAL_DIM))
    zeta = jnp.broadcast_to(zeta, (C_HEADS, chunk, C_KEY_DIM))
    cdec = jnp.broadcast_to(jnp.exp(chunk * log_g)[:, None, None], (C_HEADS, C_KEY_DIM, C_VAL_DIM))
    return dec, xi, zeta, cdec


C_CHUNK = 512


def _mixer_c(cq, ck, proj, tabs_f, tabs_b, norm_g, prev, row_off, b, s):
    n = cq.shape[0]
    prev_f, prev_o = (None, None) if prev is None else ([prev[0]], [prev[1]])
    nc = s // C_CHUNK
    r0 = row_off // C_CHUNK
    v_col = OFF["cv"] // C_V
    g_col = OFF["cg"] // C_V

    def fwd_rows(ib, c):
        return r0 + ib * nc + c

    def bwd_rows(ib, c):
        return r0 + ib * nc + (nc - 1 - c)

    def specs(rows):
        def full(shape):
            return pl.BlockSpec(shape, lambda ib, c: (0,) * len(shape))
        return [pl.BlockSpec((C_CHUNK, C_QK), lambda ib, c: (rows(ib, c), 0)),
                pl.BlockSpec((C_CHUNK, C_QK), lambda ib, c: (rows(ib, c), 0)),
                pl.BlockSpec((C_CHUNK, C_V), lambda ib, c: (rows(ib, c), v_col)),
                full((C_HEADS, C_CHUNK, C_CHUNK)), full((C_HEADS, C_CHUNK, C_VAL_DIM)),
                full((C_HEADS, C_CHUNK, C_KEY_DIM)), full((C_HEADS, C_KEY_DIM, C_VAL_DIM))]

    scratch = [pltpu.VMEM((C_HEADS, C_KEY_DIM, C_VAL_DIM), F32)]
    o_f, = _seq_call(
        _ret_fwd_kernel, prev_f,
        grid=(b, nc),
        in_specs=specs(fwd_rows),
        out_specs=[pl.BlockSpec((C_CHUNK, C_V), lambda ib, c: (fwd_rows(ib, c), 0))],
        out_shape=[jax.ShapeDtypeStruct((n, C_V), F32)],
        scratch_shapes=scratch,
        compiler_params=_cparams(("parallel", "arbitrary")),
        name="ret_fwd",
        args=(cq, ck, proj, *tabs_f))
    o_c, = _seq_call(
        _ret_bwd_kernel, prev_o,
        grid=(b, nc),
        in_specs=specs(bwd_rows) + [
            pl.BlockSpec((C_CHUNK, C_V), lambda ib, c: (bwd_rows(ib, c), 0)),
            pl.BlockSpec((C_CHUNK, C_V), lambda ib, c: (bwd_rows(ib, c), g_col)),
            pl.BlockSpec((1, C_V), lambda ib, c: (0, 0))],
        out_specs=[pl.BlockSpec((C_CHUNK, C_V), lambda ib, c: (bwd_rows(ib, c), 0))],
        out_shape=[jax.ShapeDtypeStruct((n, C_V), BF16)],
        scratch_shapes=scratch,
        compiler_params=_cparams(("parallel", "arbitrary")),
        name="ret_bwd",
        args=(cq, ck, proj, *tabs_b, o_f, proj, norm_g))
    return [o_f, o_c]


def _merge_kernel(oa_ref, ob0_ref, ob1_ref, ob2_ref, l0_ref, l1_ref, l2_ref, oc_ref,
                  ga_ref, gb_ref, gc_ref, x_ref, woa_ref, wob_ref, woc_ref, wout_ref, n2_ref,
                  xo_ref, hn_ref, scr_ref):
    tm = x_ref.shape[0]

    def natural(src_ref, dil, slot):
        if dil == 1:
            return src_ref[...].astype(F32)
        chunks = B_KV // LANES
        for r in range(dil):
            for c in range(chunks):
                col = r * B_KV + c * LANES
                scr_ref[slot, c, pl.ds(r, tm // dil, stride=dil), :] = src_ref[:, col:col + LANES].astype(F32)
        return jnp.concatenate([scr_ref[slot, c] for c in range(chunks)], axis=1)

    dils = [dil for _, dil in B_PATTERNS]
    l0, l1, l2 = [natural(ref, dil, 2 * g) for g, (ref, dil) in enumerate(zip((l0_ref, l1_ref, l2_ref), dils))]
    o0, o1, o2 = [natural(ref, dil, 2 * g + 1)
                  for g, (ref, dil) in enumerate(zip((ob0_ref, ob1_ref, ob2_ref), dils))]
    mx = jnp.maximum(jnp.maximum(l0, l1), l2)
    e0, e1, e2 = jnp.exp(l0 - mx), jnp.exp(l1 - mx), jnp.exp(l2 - mx)
    ob = (e0 * o0 + e1 * o1 + e2 * o2) / (e0 + e1 + e2)

    def branch(o, w_ref, gate_ref):
        y = jnp.dot(o, w_ref[...], preferred_element_type=F32)
        return jax.nn.sigmoid(gate_ref[...].astype(F32)) * y

    oa = jnp.transpose(oa_ref[...].astype(F32)).astype(BF16)
    merged = (branch(oa, woa_ref, ga_ref) + branch(ob.astype(BF16), wob_ref, gb_ref)
              + branch(oc_ref[...], woc_ref, gc_ref))
    x = x_ref[...] + jnp.dot(merged.astype(BF16), wout_ref[...], preferred_element_type=F32)
    xo_ref[...] = x
    ms = jnp.mean(x * x, axis=-1, keepdims=True)
    hn_ref[...] = (x * lax.rsqrt(ms + EPS) * n2_ref[...]).astype(hn_ref.dtype)


def _merge(o_a, o_b, lse_b, o_c, proj, x, w_oa, w_ob, w_oc, w_out, n2, hn_dtype):
    n = x.shape[0]
    tm = min(512, n)
    gl0 = OFF["gl"] // D_MODEL

    def rows(width):
        return pl.BlockSpec((tm, width), lambda i: (i, 0))

    def gate(k):
        return pl.BlockSpec((tm, D_MODEL), lambda i: (i, gl0 + k))

    def full(r, c):
        return pl.BlockSpec((r, c), lambda i: (0, 0))

    grouped = [pl.BlockSpec((tm // dil, dil * B_KV), lambda i: (i, 0)) for _, dil in B_PATTERNS]
    return pl.pallas_call(
        _merge_kernel,
        grid=(n // tm,),
        in_specs=[pl.BlockSpec((A_Q, tm), lambda i: (0, i))] + grouped + grouped
                 + [rows(C_V), gate(0), gate(1), gate(2), rows(D_MODEL),
                    full(A_Q, D_MODEL), full(B_KV, D_MODEL), full(C_V, D_MODEL), full(D_MODEL, D_MODEL),
                    full(1, D_MODEL)],
        out_specs=[rows(D_MODEL), rows(D_MODEL)],
        out_shape=[jax.ShapeDtypeStruct((n, D_MODEL), F32), jax.ShapeDtypeStruct((n, D_MODEL), hn_dtype)],
        scratch_shapes=[pltpu.VMEM((2 * B_GROUPS, B_KV // LANES, tm, LANES), F32)],
        compiler_params=_cparams(("parallel",)),
        name="merge_out",
    )(o_a, *o_b, *lse_b, o_c, proj, proj, proj, x, w_oa, w_ob, w_oc, w_out, n2)


def _ffn_kernel(hn_ref, x_ref, w1_ref, w3_ref, w2_ref, o_ref, acc_ref, *, nf):
    j = pl.program_id(1)
    hn = hn_ref[...]
    a = jnp.dot(hn, w1_ref[...], preferred_element_type=F32)
    g = jnp.dot(hn, w3_ref[...], preferred_element_type=F32)
    h = (a * jax.nn.sigmoid(a) * g).astype(BF16)
    y = jnp.dot(h, w2_ref[...], preferred_element_type=F32)

    @pl.when(j == 0)
    def _():
        acc_ref[...] = x_ref[...] + y

    @pl.when(j > 0)
    def _():
        acc_ref[...] += y

    @pl.when(j == nf - 1)
    def _():
        o_ref[...] = acc_ref[...]


def _ffn(hn, x, w1, w3, w2):
    n = x.shape[0]
    d_ff = w1.shape[1]
    tm = min(512, n)
    tf = d_ff // 2
    nf = d_ff // tf
    return pl.pallas_call(
        functools.partial(_ffn_kernel, nf=nf),
        grid=(n // tm, nf),
        in_specs=[pl.BlockSpec((tm, D_MODEL), lambda i, j: (i, 0)),
                  pl.BlockSpec((tm, D_MODEL), lambda i, j: (i, 0)),
                  pl.BlockSpec((D_MODEL, tf), lambda i, j: (0, j)),
                  pl.BlockSpec((D_MODEL, tf), lambda i, j: (0, j)),
                  pl.BlockSpec((tf, D_MODEL), lambda i, j: (j, 0))],
        out_specs=pl.BlockSpec((tm, D_MODEL), lambda i, j: (i, 0)),
        out_shape=jax.ShapeDtypeStruct((n, D_MODEL), F32),
        scratch_shapes=[pltpu.VMEM((tm, D_MODEL), F32)],
        compiler_params=_cparams(("parallel", "arbitrary")),
        name="ffn",
    )(hn, x, w1, w3, w2)


TOP_K = 2
MOE_TM = 512


def _router_kernel(hn_ref, wr_ref, idx_ref, wts_ref):
    logits = jnp.dot(hn_ref[...], wr_ref[...], preferred_element_type=F32,
                     precision=lax.Precision.HIGHEST)
    col = lax.broadcasted_iota(jnp.int32, logits.shape, 1)
    m1 = jnp.max(logits, axis=1, keepdims=True)
    i1 = jnp.min(jnp.where(logits == m1, col, N_EXPERTS), axis=1, keepdims=True)
    rest = jnp.where(col == i1, -jnp.inf, logits)
    m2 = jnp.max(rest, axis=1, keepdims=True)
    i2 = jnp.min(jnp.where(rest == m2, col, N_EXPERTS), axis=1, keepdims=True)
    e2 = jnp.exp(m2 - m1)
    w1 = 1.0 / (1.0 + e2)
    idx_ref[...] = jnp.concatenate([i1, i2], axis=1)
    wts_ref[...] = jnp.concatenate([w1, e2 * w1], axis=1)


def _router(hn, wr):
    n = hn.shape[0]
    tm = min(1024, n)
    return pl.pallas_call(
        _router_kernel,
        grid=(n // tm,),
        in_specs=[pl.BlockSpec((tm, D_MODEL), lambda i: (i, 0)),
                  pl.BlockSpec((D_MODEL, N_EXPERTS), lambda i: (0, 0))],
        out_specs=[pl.BlockSpec((tm, TOP_K), lambda i: (i, 0)), pl.BlockSpec((tm, TOP_K), lambda i: (i, 0))],
        out_shape=[jax.ShapeDtypeStruct((n, TOP_K), jnp.int32), jax.ShapeDtypeStruct((n, TOP_K), F32)],
        compiler_params=_cparams(("parallel",)),
        name="router",
    )(hn, wr)


def _route(idx, tm):
    n = idx.shape[0]
    e_flat = idx.reshape(-1)
    onehot = (e_flat[:, None] == jnp.arange(N_EXPERTS, dtype=jnp.int32)[None, :]).astype(jnp.int32)
    csum = jnp.cumsum(onehot, axis=0)
    rank = jnp.sum((csum - onehot) * onehot, axis=1)
    gsz = ((csum[-1] + tm - 1) // tm) * tm
    gend = jnp.cumsum(gsz)
    pos = (gend - gsz)[e_flat] + rank
    n_rows = n * TOP_K + N_EXPERTS * tm
    n_tiles = n_rows // tm
    row_token = jnp.zeros((n_rows,), jnp.int32).at[pos].set(jnp.arange(n * TOP_K, dtype=jnp.int32) // TOP_K)
    tile_start = jnp.arange(n_tiles, dtype=jnp.int32) * tm
    tile_expert = jnp.minimum(jnp.sum((tile_start[:, None] >= gend[None, :]).astype(jnp.int32), axis=1),
                              N_EXPERTS - 1)
    n_used = (gend[-1] // tm).astype(jnp.int32).reshape(1)
    return pos.reshape(n, TOP_K), row_token.reshape(n_tiles, 1, tm), tile_expert, n_used


def _row_copy(src_ref, src_row, dst_ref, dst_row, sem):
    return pltpu.make_async_copy(src_ref.at[pl.ds(src_row, 1)], dst_ref.at[pl.ds(dst_row, 1)], sem)


def _expert_kernel(te_ref, nu_ref, idx_ref, nxt_ref, src_ref, w1_ref, w3_ref, w2_ref, y_ref,
                   xs_ref, xb_ref, acc_ref, sem, *, nf, tm):
    i = pl.program_id(0)
    j = pl.program_id(1)
    n_used = nu_ref[0]
    active = i < n_used
    slot = i % 2
    share = tm // nf

    @pl.when((i == 0) & (j == 0))
    def _():
        def issue(r, carry):
            _row_copy(src_ref, idx_ref[0, 0, r], xs_ref.at[0], r, sem.at[0]).start()
            return carry
        lax.fori_loop(0, tm, issue, 0, unroll=8)

    @pl.when((j == 0) & (i <= n_used))
    def _():
        pltpu.make_async_copy(src_ref.at[pl.ds(0, tm)], xs_ref.at[slot], sem.at[slot]).wait()

    @pl.when(active)
    def _():
        @pl.when(j == 0)
        def _():
            xb_ref[...] = xs_ref[slot].astype(BF16)

        for r in range(share):
            row = j * share + r
            _row_copy(src_ref, nxt_ref[0, 0, row], xs_ref.at[1 - slot], row, sem.at[1 - slot]).start()

        xb = xb_ref[...]
        a = jnp.dot(xb, w1_ref[0], preferred_element_type=F32)
        g = jnp.dot(xb, w3_ref[0], preferred_element_type=F32)
        h = (a * jax.nn.sigmoid(a) * g).astype(BF16)
        y = jnp.dot(h, w2_ref[0], preferred_element_type=F32)

        @pl.when(j == 0)
        def _():
            acc_ref[...] = y

        @pl.when(j > 0)
        def _():
            acc_ref[...] += y

        @pl.when(j == nf - 1)
        def _():
            y_ref[...] = acc_ref[...]

    @pl.when(jnp.logical_not(active) & (j == nf - 1))
    def _():
        y_ref[...] = jnp.zeros(y_ref.shape, y_ref.dtype)


def _experts(src, row_token, tile_expert, n_used, w1, w3, w2):
    nt, _, tm = row_token.shape
    d_ff = w1.shape[2]
    nf = 2
    tf = d_ff // nf

    def jcol(i, j, nu):
        return jnp.where(i < nu[0], j, nf - 1)

    grid_spec = pltpu.PrefetchScalarGridSpec(
        num_scalar_prefetch=2,
        grid=(nt, nf),
        in_specs=[pl.BlockSpec((1, 1, tm), lambda i, j, te, nu: (i, 0, 0), memory_space=pltpu.SMEM),
                  pl.BlockSpec((1, 1, tm), lambda i, j, te, nu: (jnp.minimum(i + 1, nt - 1), 0, 0),
                               memory_space=pltpu.SMEM),
                  pl.BlockSpec(memory_space=pl.ANY),
                  pl.BlockSpec((1, D_MODEL, tf), lambda i, j, te, nu: (te[i], 0, jcol(i, j, nu))),
                  pl.BlockSpec((1, D_MODEL, tf), lambda i, j, te, nu: (te[i], 0, jcol(i, j, nu))),
                  pl.BlockSpec((1, tf, D_MODEL), lambda i, j, te, nu: (te[i], jcol(i, j, nu), 0))],
        out_specs=pl.BlockSpec((tm, D_MODEL), lambda i, j, te, nu: (i, 0)),
        scratch_shapes=[pltpu.VMEM((2, tm, D_MODEL), src.dtype), pltpu.VMEM((tm, D_MODEL), BF16),
                        pltpu.VMEM((tm, D_MODEL), F32), pltpu.SemaphoreType.DMA((2,))],
    )
    return pl.pallas_call(
        functools.partial(_expert_kernel, nf=nf, tm=tm),
        grid_spec=grid_spec,
        out_shape=jax.ShapeDtypeStruct((nt * tm, D_MODEL), F32),
        compiler_params=_cparams(("arbitrary", "arbitrary")),
        name="moe_experts",
    )(tile_expert, n_used, row_token, row_token, src, w1, w3, w2)


def _combine_kernel(pos_ref, nxt_ref, wts_ref, x_ref, y_ref, o_ref, g_ref, sem, *, tm, nt):
    i = pl.program_id(0)
    slot = i % 2

    def request(p_ref, dst_slot):
        def issue(r, carry):
            for k in range(TOP_K):
                _row_copy(y_ref, p_ref[0, 0, TOP_K * r + k], g_ref.at[dst_slot, k], r, sem.at[dst_slot]).start()
            return carry
        lax.fori_loop(0, tm, issue, 0, unroll=8)

    @pl.when(i == 0)
    def _():
        request(pos_ref, 0)

    @pl.when(i + 1 < nt)
    def _():
        request(nxt_ref, 1 - slot)

    for k in range(TOP_K):
        pltpu.make_async_copy(y_ref.at[pl.ds(0, tm)], g_ref.at[slot, k], sem.at[slot]).wait()
    w = wts_ref[...]
    o_ref[...] = x_ref[...] + w[:, 0:1] * g_ref[slot, 0] + w[:, 1:2] * g_ref[slot, 1]


def _combine(pos, wts, x, y):
    n = x.shape[0]
    tm = min(256, n)
    nt = n // tm
    pos3 = pos.reshape(nt, 1, TOP_K * tm)
    return pl.pallas_call(
        functools.partial(_combine_kernel, tm=tm, nt=nt),
        grid=(nt,),
        in_specs=[pl.BlockSpec((1, 1, TOP_K * tm), lambda i: (i, 0, 0), memory_space=pltpu.SMEM),
                  pl.BlockSpec((1, 1, TOP_K * tm), lambda i: (jnp.minimum(i + 1, nt - 1), 0, 0),
                               memory_space=pltpu.SMEM),
                  pl.BlockSpec((tm, TOP_K), lambda i: (i, 0)),
                  pl.BlockSpec((tm, D_MODEL), lambda i: (i, 0)),
                  pl.BlockSpec(memory_space=pl.ANY)],
        out_specs=pl.BlockSpec((tm, D_MODEL), lambda i: (i, 0)),
        out_shape=jax.ShapeDtypeStruct((n, D_MODEL), F32),
        scratch_shapes=[pltpu.VMEM((2, TOP_K, tm, D_MODEL), F32), pltpu.SemaphoreType.DMA((2,))],
        compiler_params=_cparams(("arbitrary",)),
        name="moe_combine",
    )(pos3, pos3, wts, x, y)


def _moe(hn, x, wr, w1, w3, w2):
    idx, wts = _router(hn, wr)
    pos, row_token, tile_expert, n_used = _route(idx, MOE_TM)
    y = _experts(hn, row_token, tile_expert, n_used, w1, w3, w2)
    return _combine(pos, wts, x, y)


def _permute_w_in(w):
    cols = [w[:, _ORIG[name][0]:_ORIG[name][0] + _ORIG[name][1]] for name in _NEW_ORDER]
    cols.append(jnp.zeros((w.shape[0], PROJ_W - _USED_W), w.dtype))
    return jnp.concatenate(cols, axis=1).astype(BF16)


def _angles(pos, dim):
    inv = ROPE_THETA ** (-jnp.arange(0, dim, 2, dtype=F32) / dim)
    return pos.astype(F32)[:, None] * inv[None, :]


def _rope_tables(seqs):
    pos = jnp.concatenate([jnp.tile(jnp.arange(s), b) for b, s in seqs])
    ang = _angles(pos, HEAD_DIM)
    cosb = jnp.tile(jnp.cos(ang), (1, 4))
    sinb = jnp.tile(jnp.concatenate([-jnp.sin(ang), jnp.sin(ang)], axis=1), (1, 2))
    ar = _angles(pos // GRID_W, HEAD_DIM // 2)
    ac = _angles(pos % GRID_W, HEAD_DIM // 2)
    cosa = jnp.tile(jnp.concatenate([jnp.cos(ar), jnp.cos(ar), jnp.cos(ac), jnp.cos(ac)], axis=1), (1, 2))
    sina = jnp.tile(jnp.concatenate([-jnp.sin(ar), jnp.sin(ar), -jnp.sin(ac), jnp.sin(ac)], axis=1), (1, 2))
    return cosb, sinb, cosa, sina


def _head_mean_matrix():
    blk = np.kron(np.eye(LANES // HEAD_DIM), np.ones((HEAD_DIM, HEAD_DIM))) / HEAD_DIM
    return jnp.asarray(blk, BF16)


def _gain_rows(a_qn, a_kn, b_qn, b_kn):
    scale = HEAD_DIM ** -0.5
    scale_a = scale * np.log2(np.e)
    rows = [jnp.tile(b_qn, 2) * scale, jnp.tile(b_kn, 2), jnp.tile(a_qn, 2) * scale_a, jnp.tile(a_kn, 2),
            jnp.ones((LANES,), F32), jnp.full((LANES,), C_KEY_DIM ** -0.5, F32),
            jnp.ones((LANES,), F32), jnp.ones((LANES,), F32)]
    return jnp.stack(rows).astype(F32)


def _trunk(x, seqs, norm1_g, w_in, a_qn, a_kn, b_qn, b_kn, ret_dec_f, ret_dec_b, ret_norm_g,
           w_oa, w_ob, w_oc, w_out, norm2_g, ffn_w1, ffn_w3, ffn_w2,
           moe_router, moe_w1, moe_w3, moe_w2):
    depth = w_in.shape[0]
    tables = _rope_tables(seqs)
    mmat = _head_mean_matrix()
    for l in range(depth):
        proj = _inproj(x, norm1_g[l][None, :], _permute_w_in(w_in[l]))
        prepped = _prep(proj, tables, _gain_rows(a_qn[l], a_kn[l], b_qn[l], b_kn[l]), mmat)
        bqs, bks, bvs = prepped[0:3], prepped[3:6], prepped[6:9]
        aq, ak, cq, ck = prepped[9:]
        aqt = aq.T
        avt = proj[:, OFF["av"]:OFF["av"] + A_KV].T
        tabs_f = _ret_tables(ret_dec_f[l], C_CHUNK, False)
        tabs_b = _ret_tables(ret_dec_b[l], C_CHUNK, True)
        ng = ret_norm_g[l][None, :].astype(F32)
        res_a, res_c = None, None
        res_b = [None] * B_GROUPS
        row_off = 0
        for b, s in seqs:
            res_a = _mixer_a(_a_scores_small(a_qn[l], a_kn[l]), aqt, ak, avt, res_a, row_off, b, s)
            for g, (_, dil) in enumerate(B_PATTERNS):
                res_b[g] = _mixer_b_group(bqs[g], bks[g], bvs[g], res_b[g], g, dil, row_off, b, s)
            res_c = _mixer_c(cq, ck, proj, tabs_f, tabs_b, ng, res_c, row_off, b, s)
            row_off += b * s
        x, hn = _merge(res_a[0], [r[0] for r in res_b], [r[1] for r in res_b], res_c[1], proj, x,
                       w_oa[l].astype(BF16), w_ob[l].astype(BF16), w_oc[l].astype(BF16),
                       w_out[l].astype(BF16), norm2_g[l][None, :], BF16 if l % 2 == 0 else F32)
        i = l // 2
        if l % 2 == 0:
            x = _ffn(hn, x, ffn_w1[i].astype(BF16), ffn_w3[i].astype(BF16), ffn_w2[i].astype(BF16))
        else:
            x = _moe(hn, x, moe_router[i], moe_w1[i].astype(BF16), moe_w3[i].astype(BF16),
                     moe_w2[i].astype(BF16))
    return x


def kernel(x_prompt, x_sample, norm1_g, w_in, a_qn, a_kn, b_qn, b_kn, ret_dec_f, ret_dec_b, ret_norm_g,
           w_oa, w_ob, w_oc, w_out, norm2_g, ffn_w1, ffn_w3, ffn_w2, moe_router, moe_w1, moe_w3, moe_w2):
    seqs = (x_prompt.shape[:2], x_sample.shape[:2])
    x = jnp.concatenate([x_prompt.reshape(-1, D_MODEL), x_sample.reshape(-1, D_MODEL)], axis=0)
    y = _trunk(x, seqs, norm1_g, w_in, a_qn, a_kn, b_qn, b_kn, ret_dec_f, ret_dec_b, ret_norm_g,
               w_oa, w_ob, w_oc, w_out, norm2_g, ffn_w1, ffn_w3, ffn_w2,
               moe_router, moe_w1, moe_w3, moe_w2)
    n_p = x_prompt.shape[0] * x_prompt.shape[1]
    return (y[:n_p].reshape(x_prompt.shape), y[n_p:].reshape(x_sample.shape))
```

```python
import functools

import numpy as np
import jax
import jax.numpy as jnp
from jax import lax
from jax.experimental import pallas as pl
from jax.experimental.pallas import tpu as pltpu

F32 = jnp.float32
BF16 = jnp.bfloat16

D_MODEL = 1024
GRID_W = 64
HEAD_DIM = 64
ROPE_THETA = 10000.0
EPS = 1e-6
A_HEADS = 8
A_KV_HEADS = 2
B_PATTERNS = ((128, 1), (512, 4), (2048, 16))
B_GROUPS = 3
B_HEADS = 8
C_HEADS = 4
C_KEY_DIM = 64
C_VAL_DIM = 128
N_EXPERTS = 8
N_BRANCHES = 3

A_Q = A_HEADS * HEAD_DIM
A_KV = A_KV_HEADS * HEAD_DIM
B_Q = B_GROUPS * B_HEADS * HEAD_DIM
B_KV = B_HEADS * HEAD_DIM
C_QK = C_HEADS * C_KEY_DIM
C_V = C_HEADS * C_VAL_DIM

LANES = 128
VMEM_LIMIT = 56 * 1024 * 1024

_ORIG_SPLITS = (("aq", A_Q), ("ak", A_KV), ("av", A_KV), ("bq", B_Q), ("bk", B_KV), ("bv", B_KV),
                ("cq", C_QK), ("ck", C_QK), ("cv", C_V), ("cg", C_V), ("gl", N_BRANCHES * D_MODEL))
_NEW_ORDER = ("bq", "bk", "aq", "bv", "cv", "cg", "gl", "cq", "ck", "ak", "av")
PROJ_W = 8192


def _layout():
    orig, o = {}, 0
    for name, w in _ORIG_SPLITS:
        orig[name] = (o, w)
        o += w
    new, o = {}, 0
    for name in _NEW_ORDER:
        new[name] = o
        o += orig[name][1]
    return orig, new, o


_ORIG, OFF, _USED_W = _layout()


def _cparams(sem):
    return pltpu.CompilerParams(dimension_semantics=sem, vmem_limit_bytes=VMEM_LIMIT)


def _seq_call(kern, prev, *, in_specs, out_specs, out_shape, args, **kwargs):
    n_in = len(in_specs)
    if prev is None:
        return pl.pallas_call(kern, in_specs=in_specs, out_specs=out_specs, out_shape=out_shape,
                              **kwargs)(*args)
    n_prev = len(prev)

    def chained(*refs):
        return kern(*refs[:n_in], *refs[n_in + n_prev:])

    return pl.pallas_call(
        chained,
        in_specs=list(in_specs) + [pl.BlockSpec(memory_space=pl.ANY)] * n_prev,
        out_specs=out_specs, out_shape=out_shape,
        input_output_aliases={n_in + k: k for k in range(n_prev)},
        **kwargs)(*args, *prev)


def _inproj_kernel(x_ref, g_ref, w_ref, o_ref, xn_ref):
    @pl.when(pl.program_id(1) == 0)
    def _():
        x = x_ref[...]
        ms = jnp.mean(x * x, axis=-1, keepdims=True)
        xn_ref[...] = (x * lax.rsqrt(ms + EPS) * g_ref[...]).astype(BF16)

    o_ref[...] = jnp.dot(xn_ref[...], w_ref[...], preferred_element_type=F32).astype(BF16)


def _inproj(x, g, w):
    n = x.shape[0]
    tm = min(2048, n)
    tn = 1024
    return pl.pallas_call(
        _inproj_kernel,
        grid=(n // tm, PROJ_W // tn),
        in_specs=[pl.BlockSpec((tm, D_MODEL), lambda i, j: (i, 0)),
                  pl.BlockSpec((1, D_MODEL), lambda i, j: (0, 0)),
                  pl.BlockSpec((D_MODEL, tn), lambda i, j: (0, j))],
        out_specs=pl.BlockSpec((tm, tn), lambda i, j: (i, j)),
        out_shape=jax.ShapeDtypeStruct((n, PROJ_W), BF16),
        scratch_shapes=[pltpu.VMEM((tm, D_MODEL), BF16)],
        compiler_params=_cparams(("parallel", "arbitrary")),
        name="inproj",
    )(x, g, w)


def _norm_rot(x, gain, cos, sin, mmat, half, do_norm):
    if do_norm:
        ms = jnp.dot((x * x).astype(BF16), mmat, preferred_element_type=F32)
        x = x * lax.rsqrt(ms + EPS)
    x = x * gain
    lane = lax.broadcasted_iota(jnp.int32, x.shape, 1)
    first = (lane % (2 * half)) < half
    swapped = jnp.where(first, pltpu.roll(x, LANES - half, 1), pltpu.roll(x, half, 1))
    return x * cos + swapped * sin


def _prep_kernel(bqk_ref, aq_ref, bv_ref, cq_ref, ck_ref, ak_ref,
                 cosb_ref, sinb_ref, cosa_ref, sina_ref, gain_ref, mmat_ref,
                 q0_ref, q1_ref, q2_ref, k0_ref, k1_ref, k2_ref, v0_ref, v1_ref, v2_ref,
                 oaq_ref, oak_ref, ocq_ref, ock_ref, scr_ref):
    cosb, sinb = cosb_ref[...], sinb_ref[...]
    cosa, sina = cosa_ref[...], sina_ref[...]
    mmat = mmat_ref[...]
    gains = gain_ref[...]
    tm = scr_ref.shape[0]

    def prepared(src_ref, src_off, c, gain_row, cos, sin, half, do_norm):
        x = src_ref[:, src_off + c * LANES:src_off + (c + 1) * LANES].astype(F32)
        return _norm_rot(x, gains[gain_row:gain_row + 1, :], cos, sin, mmat, half, do_norm)

    def run(src_ref, dst_ref, width, gain_row, cos, sin, half, do_norm):
        for c in range(width // LANES):
            y = prepared(src_ref, 0, c, gain_row, cos, sin, half, do_norm)
            dst_ref[:, c * LANES:(c + 1) * LANES] = y.astype(dst_ref.dtype)

    def store_dilated(y, dst_ref, dil, c):
        if dil == 1:
            dst_ref[:, c * LANES:(c + 1) * LANES] = y.astype(dst_ref.dtype)
            return
        scr_ref[...] = y
        for r in range(dil):
            rows = scr_ref[pl.ds(r, tm // dil, stride=dil), :]
            dst_ref[:, r * B_KV + c * LANES:r * B_KV + (c + 1) * LANES] = rows.astype(dst_ref.dtype)

    chunks = B_KV // LANES
    for g, (q_ref, (_, dil)) in enumerate(zip((q0_ref, q1_ref, q2_ref), B_PATTERNS)):
        for c in range(chunks):
            y = prepared(bqk_ref, g * B_KV, c, 0, cosb, sinb, HEAD_DIM // 2, True)
            store_dilated(y, q_ref, dil, c)
    for c in range(chunks):
        yk = prepared(bqk_ref, B_Q, c, 1, cosb, sinb, HEAD_DIM // 2, True)
        yv = bv_ref[:, c * LANES:(c + 1) * LANES].astype(F32)
        for (_, dil), k_ref, v_ref in zip(B_PATTERNS, (k0_ref, k1_ref, k2_ref), (v0_ref, v1_ref, v2_ref)):
            store_dilated(yk, k_ref, dil, c)
            store_dilated(yv, v_ref, dil, c)
    run(aq_ref, oaq_ref, A_Q, 2, cosa, sina, HEAD_DIM // 4, True)
    run(ak_ref, oak_ref, A_KV, 3, cosa, sina, HEAD_DIM // 4, True)
    run(cq_ref, ocq_ref, C_QK, 4, cosb, sinb, C_KEY_DIM // 2, False)
    run(ck_ref, ock_ref, C_QK, 5, cosb, sinb, C_KEY_DIM // 2, False)


def _prep(proj, tables, gains, mmat):
    n = proj.shape[0]
    tm = min(512, n)
    cosb, sinb, cosa, sina = tables

    def col(width, name):
        idx = OFF[name] // width
        return pl.BlockSpec((tm, width), lambda i: (i, idx))

    def tab():
        return pl.BlockSpec((tm, LANES), lambda i: (i, 0))

    dils = [dil for _, dil in B_PATTERNS]
    shapes = [(n // dil, dil * B_KV) for dil in dils] * 3 + [(n, A_Q), (n, A_KV), (n, C_QK), (n, C_QK)]
    blocks = [(tm // dil, dil * B_KV) for dil in dils] * 3 + [(tm, A_Q), (tm, A_KV), (tm, C_QK), (tm, C_QK)]
    return pl.pallas_call(
        _prep_kernel,
        grid=(n // tm,),
        in_specs=[col(B_Q + B_KV, "bq"), col(A_Q, "aq"), col(B_KV, "bv"), col(C_QK, "cq"),
                  col(C_QK, "ck"), col(A_KV, "ak"), tab(), tab(), tab(), tab(),
                  pl.BlockSpec((8, LANES), lambda i: (0, 0)),
                  pl.BlockSpec((LANES, LANES), lambda i: (0, 0))],
        out_specs=[pl.BlockSpec(blk, lambda i: (i, 0)) for blk in blocks],
        out_shape=[jax.ShapeDtypeStruct(shp, BF16) for shp in shapes],
        scratch_shapes=[pltpu.VMEM((tm, LANES), F32)],
        compiler_params=_cparams(("parallel",)),
        name="prep",
    )(proj, proj, proj, proj, proj, proj, cosb, sinb, cosa, sina, gains, mmat)


A_ONES_ROWS = 16


def _attn_a_kernel(small_ref, qt_ref, k0_ref, kn_ref, vt_ref, o_ref, qp_ref, m_ref, acc_ref, sa_ref, sb_ref,
                   *, bq, nk):
    ik = pl.program_id(2)
    rep = A_HEADS // A_KV_HEADS
    kv_heads = range(A_KV_HEADS)

    @pl.when(ik == 0)
    def _():
        qp_ref[...] = jnp.zeros(qp_ref.shape, BF16)
        for g in kv_heads:
            for r in range(rep):
                h = g * rep + r
                qp_ref[g, g * HEAD_DIM:(g + 1) * HEAD_DIM, r * bq:(r + 1) * bq] = (
                    qt_ref[h * HEAD_DIM:(h + 1) * HEAD_DIM, :])
        m_ref[...] = jnp.full(m_ref.shape, -jnp.inf, F32)
        acc_ref[...] = jnp.zeros(acc_ref.shape, F32)
        for g in kv_heads:
            sa_ref[g] = jnp.dot(k0_ref[...], qp_ref[g], preferred_element_type=F32)

    def step(cur_ref, nxt_ref, has_next, small):
        kn = kn_ref[...]
        ones = jnp.ones((A_ONES_ROWS, kn.shape[0]), BF16)
        for g in kv_heads:
            v_aug = jnp.concatenate([vt_ref[g * HEAD_DIM:(g + 1) * HEAD_DIM, :], ones], axis=0)
            for r in range(rep):
                cols = slice(r * bq, (r + 1) * bq)
                if has_next:
                    nxt_ref[g, :, cols] = jnp.dot(kn, qp_ref[g, :, cols], preferred_element_type=F32)
                s = cur_ref[g, :, cols]
                if small:
                    pv = jnp.dot(v_aug, jnp.exp2(s).astype(BF16), preferred_element_type=F32)
                    acc_ref[g, :, cols] += pv
                    continue
                m_prev = m_ref[g, :, cols]
                m_new = jnp.maximum(m_prev, jnp.max(s, axis=0, keepdims=True))
                p = jnp.exp2(s - m_new).astype(BF16)
                alpha = jnp.exp2(m_prev - m_new)
                pv = jnp.dot(v_aug, p, preferred_element_type=F32)
                acc_ref[g, :, cols] = alpha * acc_ref[g, :, cols] + pv
                m_ref[g, :, cols] = m_new

    last = ik == nk - 1
    is_small = small_ref[0] != 0
    for parity, cur_ref, nxt_ref in ((0, sa_ref, sb_ref), (1, sb_ref, sa_ref)):
        for has_next in (True, False):
            for small in (True, False):
                @pl.when((ik % 2 == parity) & (last != has_next) & (is_small == small))
                def _(cur_ref=cur_ref, nxt_ref=nxt_ref, has_next=has_next, small=small):
                    step(cur_ref, nxt_ref, has_next, small)

    @pl.when(last)
    def _():
        for g in range(A_KV_HEADS):
            acc = acc_ref[g]
            o = acc[:HEAD_DIM] / acc[HEAD_DIM:HEAD_DIM + 1]
            for r in range(rep):
                h = g * rep + r
                o_ref[h * HEAD_DIM:(h + 1) * HEAD_DIM, :] = o[:, r * bq:(r + 1) * bq].astype(o_ref.dtype)


A_SMALL_SCORE = 60.0


def _a_scores_small(a_qn, a_kn):
    bound = (HEAD_DIM ** 0.5) * np.log2(np.e) * 1.02 * jnp.max(jnp.abs(a_qn)) * jnp.max(jnp.abs(a_kn))
    return (bound < A_SMALL_SCORE).astype(jnp.int32).reshape(1)


def _mixer_a(small, aqt, ak, avt, prev, row_off, b, s):
    n = ak.shape[0]
    bq = min(512, s)
    bk = min(512, s)
    nq, nk = s // bq, s // bk
    rep = A_HEADS // A_KV_HEADS
    q0, k0 = row_off // bq, row_off // bk
    kern = functools.partial(_attn_a_kernel, bq=bq, nk=nk)
    return _seq_call(
        kern, prev,
        grid=(b, nq, nk),
        in_specs=[pl.BlockSpec(memory_space=pltpu.SMEM),
                  pl.BlockSpec((A_Q, bq), lambda ib, iq, ik: (0, q0 + ib * nq + iq)),
                  pl.BlockSpec((bk, A_KV), lambda ib, iq, ik: (k0 + ib * nk, 0)),
                  pl.BlockSpec((bk, A_KV), lambda ib, iq, ik: (k0 + ib * nk + jnp.minimum(ik + 1, nk - 1), 0)),
                  pl.BlockSpec((A_KV, bk), lambda ib, iq, ik: (0, k0 + ib * nk + ik))],
        out_specs=[pl.BlockSpec((A_Q, bq), lambda ib, iq, ik: (0, q0 + ib * nq + iq))],
        out_shape=[jax.ShapeDtypeStruct((A_Q, n), BF16)],
        scratch_shapes=[pltpu.VMEM((A_KV_HEADS, A_KV, rep * bq), BF16),
                        pltpu.VMEM((A_KV_HEADS, 1, rep * bq), F32),
                        pltpu.VMEM((A_KV_HEADS, HEAD_DIM + A_ONES_ROWS, rep * bq), F32),
                        pltpu.VMEM((A_KV_HEADS, bk, rep * bq), F32),
                        pltpu.VMEM((A_KV_HEADS, bk, rep * bq), F32)],
        compiler_params=_cparams(("parallel", "parallel", "arbitrary")),
        name="mixer_a",
        args=(small, aqt, ak, ak, avt))


B_BLOCK_Q = 128
B_HALO = 64


def _attn_b_kernel(q_ref, k0_ref, k1_ref, k2_ref, k3_ref, v0_ref, v1_ref, v2_ref, v3_ref,
                   o_ref, lse_ref, *, u_len):
    i = pl.program_id(2)
    kcat = jnp.concatenate([k0_ref[...], k1_ref[...], k2_ref[...], k3_ref[...]], axis=0)
    vcat = jnp.concatenate([v0_ref[...], v1_ref[...], v2_ref[...], v3_ref[...]], axis=0)
    nkeys = B_BLOCK_Q + 2 * B_HALO
    a = lax.broadcasted_iota(jnp.int32, (B_BLOCK_Q, nkeys), 0)
    c = lax.broadcasted_iota(jnp.int32, (B_BLOCK_Q, nkeys), 1)
    key_pos = i * B_BLOCK_Q - B_HALO + c
    valid = (c >= a) & (c <= a + 2 * B_HALO) & (key_pos >= 0) & (key_pos < u_len)
    left = lax.broadcasted_iota(jnp.int32, (B_BLOCK_Q, LANES), 1) < HEAD_DIM
    zero = jnp.zeros((B_BLOCK_Q, LANES), BF16)
    heads = [(pair, own) for pair in range(B_HEADS // 2) for own in (left, jnp.logical_not(left))]
    lanes = [slice(pair * LANES, (pair + 1) * LANES) for pair in range(B_HEADS // 2)]
    s = [lax.dot_general(jnp.where(own, q_ref[:, lanes[pair]], zero), kcat[:, lanes[pair]],
                         (((1,), (1,)), ((), ())), preferred_element_type=F32) for pair, own in heads]
    s = [jnp.where(valid, sh, -1e30) for sh in s]
    m = [jnp.max(sh, axis=1, keepdims=True) for sh in s]
    e = [jnp.exp(sh - mh) for sh, mh in zip(s, m)]
    den = [jnp.sum(eh, axis=1, keepdims=True) for eh in e]
    pv = [jnp.dot(eh.astype(BF16), vcat[:, lanes[pair]], preferred_element_type=F32)
          for eh, (pair, _) in zip(e, heads)]
    outs = [pvh / dh for pvh, dh in zip(pv, den)]
    lses = [mh + jnp.log(dh) for mh, dh in zip(m, den)]
    for pair in range(B_HEADS // 2):
        o_ref[:, lanes[pair]] = jnp.where(left, outs[2 * pair], outs[2 * pair + 1]).astype(o_ref.dtype)
        lse_ref[:, lanes[pair]] = jnp.where(left, lses[2 * pair], lses[2 * pair + 1])


def _mixer_b_group(qd, kd, vd, prev, g, dil, row_off, b, s):
    u_len = s // dil
    nq = u_len // B_BLOCK_Q
    nkb = u_len // B_HALO
    q_rows0 = row_off // dil // B_BLOCK_Q
    k_rows0 = row_off // dil // B_HALO

    def kspec(t):
        def imap(ib, r, i):
            blk = jnp.clip(2 * i - 1 + t, 0, nkb - 1)
            return (k_rows0 + ib * nkb + blk, r)
        return pl.BlockSpec((B_HALO, B_KV), imap)

    def qspec():
        return pl.BlockSpec((B_BLOCK_Q, B_KV), lambda ib, r, i: (q_rows0 + ib * nq + i, r))

    kern = functools.partial(_attn_b_kernel, u_len=u_len)
    return _seq_call(
        kern, prev,
        grid=(b, dil, nq),
        in_specs=[qspec()] + [kspec(t) for t in range(4)] + [kspec(t) for t in range(4)],
        out_specs=[qspec(), qspec()],
        out_shape=[jax.ShapeDtypeStruct(qd.shape, BF16), jax.ShapeDtypeStruct(qd.shape, F32)],
        compiler_params=_cparams(("parallel", "parallel", "parallel")),
        name=f"mixer_b{g}",
        args=(qd, kd, kd, kd, kd, vd, vd, vd, vd))


def _ret_chunk(q, k, v, dec_ref, xi_ref, zeta_ref, cdec_ref, r_ref):
    outs = []
    for h in range(C_HEADS):
        qh = q[:, h * C_KEY_DIM:(h + 1) * C_KEY_DIM]
        kh = k[:, h * C_KEY_DIM:(h + 1) * C_KEY_DIM]
        vh = v[:, h * C_VAL_DIM:(h + 1) * C_VAL_DIM]
        att = lax.dot_general(qh, kh, (((1,), (1,)), ((), ())), preferred_element_type=F32) * dec_ref[h]
        inner = jnp.dot(att.astype(BF16), vh, preferred_element_type=F32)
        r = r_ref[h]
        cross = jnp.dot(qh, r.astype(BF16), preferred_element_type=F32) * xi_ref[h]
        kz = (kh.astype(F32) * zeta_ref[h]).astype(BF16)
        r_ref[h] = r * cdec_ref[h] + lax.dot_general(kz, vh, (((0,), (0,)), ((), ())),
                                                     preferred_element_type=F32)
        outs.append(inner + cross)
    return jnp.concatenate(outs, axis=1)


def _ret_fwd_kernel(q_ref, k_ref, v_ref, dec_ref, xi_ref, zeta_ref, cdec_ref, o_ref, r_ref):
    @pl.when(pl.program_id(1) == 0)
    def _():
        r_ref[...] = jnp.zeros(r_ref.shape, F32)

    o_ref[...] = _ret_chunk(q_ref[...], k_ref[...], v_ref[...], dec_ref, xi_ref, zeta_ref, cdec_ref, r_ref)


def _ret_bwd_kernel(q_ref, k_ref, v_ref, dec_ref, xi_ref, zeta_ref, cdec_ref, of_ref, gate_ref, ng_ref,
                    o_ref, r_ref):
    @pl.when(pl.program_id(1) == 0)
    def _():
        r_ref[...] = jnp.zeros(r_ref.shape, F32)

    o = of_ref[...] + _ret_chunk(q_ref[...], k_ref[...], v_ref[...], dec_ref, xi_ref, zeta_ref, cdec_ref, r_ref)
    ng = ng_ref[...]
    gate = gate_ref[...].astype(F32)
    for h in range(C_HEADS):
        sl = slice(h * C_VAL_DIM, (h + 1) * C_VAL_DIM)
        oh = o[:, sl]
        mu = jnp.mean(oh, axis=1, keepdims=True)
        var = jnp.mean(jnp.square(oh - mu), axis=1, keepdims=True)
        y = (oh - mu) * lax.rsqrt(var + EPS) * ng[:, sl]
        gh = gate[:, sl]
        o_ref[:, sl] = (gh * jax.nn.sigmoid(gh) * y).astype(o_ref.dtype)


def _ret_tables(dec_param, chunk, strict):
    log_g = -jnp.exp(dec_param.astype(F32))
    j = jnp.arange(chunk, dtype=F32)
    lg = log_g[:, None, None]
    if strict:
        diff = j[None, :] - j[:, None]
        dec = jnp.where((diff > 0)[None], jnp.exp(jnp.maximum(diff, 0.0)[None] * lg), 0.0)
        xi = jnp.exp((chunk - j)[None, :, None] * lg)
        zeta = jnp.exp(j[None, :, None] * lg)
    else:
        diff = j[:, None] - j[None, :]
        dec = jnp.where((diff >= 0)[None], jnp.exp(jnp.maximum(diff, 0.0)[None] * lg), 0.0)
        xi = jnp.exp((j + 1.0)[None, :, None] * lg)
        zeta = jnp.exp((chunk - 1.0 - j)[None, :, None] * lg)
    xi = jnp.broadcast_to(xi, (C_HEADS, chunk, C_VAL_DIM))
    zeta = jnp.broadcast_to(zeta, (C_HEADS, chunk, C_KEY_DIM))
    cdec = jnp.broadcast_to(jnp.exp(chunk * log_g)[:, None, None], (C_HEADS, C_KEY_DIM, C_VAL_DIM))
    return dec, xi, zeta, cdec


C_CHUNK = 512


def _mixer_c(cq, ck, proj, tabs_f, tabs_b, norm_g, prev, row_off, b, s):
    n = cq.shape[0]
    prev_f, prev_o = (None, None) if prev is None else ([prev[0]], [prev[1]])
    nc = s // C_CHUNK
    r0 = row_off // C_CHUNK
    v_col = OFF["cv"] // C_V
    g_col = OFF["cg"] // C_V

    def fwd_rows(ib, c):
        return r0 + ib * nc + c

    def bwd_rows(ib, c):
        return r0 + ib * nc + (nc - 1 - c)

    def specs(rows):
        def full(shape):
            return pl.BlockSpec(shape, lambda ib, c: (0,) * len(shape))
        return [pl.BlockSpec((C_CHUNK, C_QK), lambda ib, c: (rows(ib, c), 0)),
                pl.BlockSpec((C_CHUNK, C_QK), lambda ib, c: (rows(ib, c), 0)),
                pl.BlockSpec((C_CHUNK, C_V), lambda ib, c: (rows(ib, c), v_col)),
                full((C_HEADS, C_CHUNK, C_CHUNK)), full((C_HEADS, C_CHUNK, C_VAL_DIM)),
                full((C_HEADS, C_CHUNK, C_KEY_DIM)), full((C_HEADS, C_KEY_DIM, C_VAL_DIM))]

    scratch = [pltpu.VMEM((C_HEADS, C_KEY_DIM, C_VAL_DIM), F32)]
    o_f, = _seq_call(
        _ret_fwd_kernel, prev_f,
        grid=(b, nc),
        in_specs=specs(fwd_rows),
        out_specs=[pl.BlockSpec((C_CHUNK, C_V), lambda ib, c: (fwd_rows(ib, c), 0))],
        out_shape=[jax.ShapeDtypeStruct((n, C_V), F32)],
        scratch_shapes=scratch,
        compiler_params=_cparams(("parallel", "arbitrary")),
        name="ret_fwd",
        args=(cq, ck, proj, *tabs_f))
    o_c, = _seq_call(
        _ret_bwd_kernel, prev_o,
        grid=(b, nc),
        in_specs=specs(bwd_rows) + [
            pl.BlockSpec((C_CHUNK, C_V), lambda ib, c: (bwd_rows(ib, c), 0)),
            pl.BlockSpec((C_CHUNK, C_V), lambda ib, c: (bwd_rows(ib, c), g_col)),
            pl.BlockSpec((1, C_V), lambda ib, c: (0, 0))],
        out_specs=[pl.BlockSpec((C_CHUNK, C_V), lambda ib, c: (bwd_rows(ib, c), 0))],
        out_shape=[jax.ShapeDtypeStruct((n, C_V), BF16)],
        scratch_shapes=scratch,
        compiler_params=_cparams(("parallel", "arbitrary")),
        name="ret_bwd",
        args=(cq, ck, proj, *tabs_b, o_f, proj, norm_g))
    return [o_f, o_c]


def _merge_kernel(oa_ref, ob0_ref, ob1_ref, ob2_ref, l0_ref, l1_ref, l2_ref, oc_ref,
                  ga_ref, gb_ref, gc_ref, x_ref, woa_ref, wob_ref, woc_ref, wout_ref, n2_ref,
                  xo_ref, hn_ref, scr_ref):
    tm = x_ref.shape[0]

    def natural(src_ref, dil, slot):
        if dil == 1:
            return src_ref[...].astype(F32)
        chunks = B_KV // LANES
        for r in range(dil):
            for c in range(chunks):
                col = r * B_KV + c * LANES
                scr_ref[slot, c, pl.ds(r, tm // dil, stride=dil), :] = src_ref[:, col:col + LANES].astype(F32)
        return jnp.concatenate([scr_ref[slot, c] for c in range(chunks)], axis=1)

    dils = [dil for _, dil in B_PATTERNS]
    l0, l1, l2 = [natural(ref, dil, 2 * g) for g, (ref, dil) in enumerate(zip((l0_ref, l1_ref, l2_ref), dils))]
    o0, o1, o2 = [natural(ref, dil, 2 * g + 1)
                  for g, (ref, dil) in enumerate(zip((ob0_ref, ob1_ref, ob2_ref), dils))]
    mx = jnp.maximum(jnp.maximum(l0, l1), l2)
    e0, e1, e2 = jnp.exp(l0 - mx), jnp.exp(l1 - mx), jnp.exp(l2 - mx)
    ob = (e0 * o0 + e1 * o1 + e2 * o2) / (e0 + e1 + e2)

    def branch(o, w_ref, gate_ref):
        y = jnp.dot(o, w_ref[...], preferred_element_type=F32)
        return jax.nn.sigmoid(gate_ref[...].astype(F32)) * y

    oa = jnp.transpose(oa_ref[...].astype(F32)).astype(BF16)
    merged = (branch(oa, woa_ref, ga_ref) + branch(ob.astype(BF16), wob_ref, gb_ref)
              + branch(oc_ref[...], woc_ref, gc_ref))
    x = x_ref[...] + jnp.dot(merged.astype(BF16), wout_ref[...], preferred_element_type=F32)
    xo_ref[...] = x
    ms = jnp.mean(x * x, axis=-1, keepdims=True)
    hn_ref[...] = (x * lax.rsqrt(ms + EPS) * n2_ref[...]).astype(hn_ref.dtype)


def _merge(o_a, o_b, lse_b, o_c, proj, x, w_oa, w_ob, w_oc, w_out, n2, hn_dtype):
    n = x.shape[0]
    tm = min(512, n)
    gl0 = OFF["gl"] // D_MODEL

    def rows(width):
        return pl.BlockSpec((tm, width), lambda i: (i, 0))

    def gate(k):
        return pl.BlockSpec((tm, D_MODEL), lambda i: (i, gl0 + k))

    def full(r, c):
        return pl.BlockSpec((r, c), lambda i: (0, 0))

    grouped = [pl.BlockSpec((tm // dil, dil * B_KV), lambda i: (i, 0)) for _, dil in B_PATTERNS]
    return pl.pallas_call(
        _merge_kernel,
        grid=(n // tm,),
        in_specs=[pl.BlockSpec((A_Q, tm), lambda i: (0, i))] + grouped + grouped
                 + [rows(C_V), gate(0), gate(1), gate(2), rows(D_MODEL),
                    full(A_Q, D_MODEL), full(B_KV, D_MODEL), full(C_V, D_MODEL), full(D_MODEL, D_MODEL),
                    full(1, D_MODEL)],
        out_specs=[rows(D_MODEL), rows(D_MODEL)],
        out_shape=[jax.ShapeDtypeStruct((n, D_MODEL), F32), jax.ShapeDtypeStruct((n, D_MODEL), hn_dtype)],
        scratch_shapes=[pltpu.VMEM((2 * B_GROUPS, B_KV // LANES, tm, LANES), F32)],
        compiler_params=_cparams(("parallel",)),
        name="merge_out",
    )(o_a, *o_b, *lse_b, o_c, proj, proj, proj, x, w_oa, w_ob, w_oc, w_out, n2)


def _ffn_kernel(hn_ref, x_ref, w1_ref, w3_ref, w2_ref, o_ref):
    j = pl.program_id(1)
    hn = hn_ref[...]
    a = jnp.dot(hn, w1_ref[...], preferred_element_type=F32)
    g = jnp.dot(hn, w3_ref[...], preferred_element_type=F32)
    h = (a * jax.nn.sigmoid(a) * g).astype(BF16)
    y = jnp.dot(h, w2_ref[...], preferred_element_type=F32)

    @pl.when(j == 0)
    def _():
        o_ref[...] = x_ref[...] + y

    @pl.when(j > 0)
    def _():
        o_ref[...] += y


def _ffn(hn, x, w1, w3, w2):
    n = x.shape[0]
    d_ff = w1.shape[1]
    tm = min(1024, n)
    tf = d_ff // 2
    nf = d_ff // tf
    return pl.pallas_call(
        _ffn_kernel,
        grid=(n // tm, nf),
        in_specs=[pl.BlockSpec((tm, D_MODEL), lambda i, j: (i, 0)),
                  pl.BlockSpec((tm, D_MODEL), lambda i, j: (i, 0)),
                  pl.BlockSpec((D_MODEL, tf), lambda i, j: (0, j)),
                  pl.BlockSpec((D_MODEL, tf), lambda i, j: (0, j)),
                  pl.BlockSpec((tf, D_MODEL), lambda i, j: (j, 0))],
        out_specs=pl.BlockSpec((tm, D_MODEL), lambda i, j: (i, 0)),
        out_shape=jax.ShapeDtypeStruct((n, D_MODEL), F32),
        compiler_params=_cparams(("parallel", "arbitrary")),
        name="ffn",
    )(hn, x, w1, w3, w2)


TOP_K = 2
MOE_TM = 1024


def _router_kernel(hn_ref, wr_ref, idx_ref, wts_ref):
    logits = jnp.dot(hn_ref[...], wr_ref[...], preferred_element_type=F32,
                     precision=lax.Precision.HIGHEST)
    col = lax.broadcasted_iota(jnp.int32, logits.shape, 1)
    m1 = jnp.max(logits, axis=1, keepdims=True)
    i1 = jnp.min(jnp.where(logits == m1, col, N_EXPERTS), axis=1, keepdims=True)
    rest = jnp.where(col == i1, -jnp.inf, logits)
    m2 = jnp.max(rest, axis=1, keepdims=True)
    i2 = jnp.min(jnp.where(rest == m2, col, N_EXPERTS), axis=1, keepdims=True)
    e2 = jnp.exp(m2 - m1)
    w1 = 1.0 / (1.0 + e2)
    idx_ref[...] = jnp.concatenate([i1, i2], axis=1)
    wts_ref[...] = jnp.concatenate([w1, e2 * w1], axis=1)


def _router(hn, wr):
    n = hn.shape[0]
    tm = min(1024, n)
    return pl.pallas_call(
        _router_kernel,
        grid=(n // tm,),
        in_specs=[pl.BlockSpec((tm, D_MODEL), lambda i: (i, 0)),
                  pl.BlockSpec((D_MODEL, N_EXPERTS), lambda i: (0, 0))],
        out_specs=[pl.BlockSpec((tm, TOP_K), lambda i: (i, 0)), pl.BlockSpec((tm, TOP_K), lambda i: (i, 0))],
        out_shape=[jax.ShapeDtypeStruct((n, TOP_K), jnp.int32), jax.ShapeDtypeStruct((n, TOP_K), F32)],
        compiler_params=_cparams(("parallel",)),
        name="router",
    )(hn, wr)


def _route(idx, tm):
    n = idx.shape[0]
    e_flat = idx.reshape(-1)
    onehot = (e_flat[:, None] == jnp.arange(N_EXPERTS, dtype=jnp.int32)[None, :]).astype(jnp.int32)
    csum = jnp.cumsum(onehot, axis=0)
    rank = jnp.sum((csum - onehot) * onehot, axis=1)
    gsz = ((csum[-1] + tm - 1) // tm) * tm
    gend = jnp.cumsum(gsz)
    pos = (gend - gsz)[e_flat] + rank
    n_rows = n * TOP_K + N_EXPERTS * tm
    n_tiles = n_rows // tm
    row_token = jnp.zeros((n_rows,), jnp.int32).at[pos].set(jnp.arange(n * TOP_K, dtype=jnp.int32) // TOP_K)
    tile_start = jnp.arange(n_tiles, dtype=jnp.int32) * tm
    tile_expert = jnp.minimum(jnp.sum((tile_start[:, None] >= gend[None, :]).astype(jnp.int32), axis=1),
                              N_EXPERTS - 1)
    n_used = (gend[-1] // tm).astype(jnp.int32).reshape(1)
    return pos.reshape(n, TOP_K), row_token.reshape(n_tiles, 1, tm), tile_expert, n_used


def _row_copy(src_ref, src_row, dst_ref, dst_row, sem):
    return pltpu.make_async_copy(src_ref.at[pl.ds(src_row, 1)], dst_ref.at[pl.ds(dst_row, 1)], sem)


def _expert_kernel(te_ref, nu_ref, idx_ref, nxt_ref, src_ref, w1_ref, w3_ref, w2_ref, y_ref,
                   xs_ref, xb_ref, acc_ref, sem, *, nf, tm):
    i = pl.program_id(0)
    j = pl.program_id(1)
    n_used = nu_ref[0]
    active = i < n_used
    slot = i % 2
    share = tm // nf

    @pl.when((i == 0) & (j == 0))
    def _():
        def issue(r, carry):
            _row_copy(src_ref, idx_ref[0, 0, r], xs_ref.at[0], r, sem.at[0]).start()
            return carry
        lax.fori_loop(0, tm, issue, 0, unroll=8)

    @pl.when((j == 0) & (i <= n_used))
    def _():
        pltpu.make_async_copy(src_ref.at[pl.ds(0, tm)], xs_ref.at[slot], sem.at[slot]).wait()

    @pl.when(active)
    def _():
        @pl.when(j == 0)
        def _():
            xb_ref[...] = xs_ref[slot].astype(BF16)

        for r in range(share):
            row = j * share + r
            _row_copy(src_ref, nxt_ref[0, 0, row], xs_ref.at[1 - slot], row, sem.at[1 - slot]).start()

        xb = xb_ref[...]
        a = jnp.dot(xb, w1_ref[0], preferred_element_type=F32)
        g = jnp.dot(xb, w3_ref[0], preferred_element_type=F32)
        h = (a * jax.nn.sigmoid(a) * g).astype(BF16)
        y = jnp.dot(h, w2_ref[0], preferred_element_type=F32)

        @pl.when(j == 0)
        def _():
            acc_ref[...] = y

        @pl.when(j > 0)
        def _():
            acc_ref[...] += y

        @pl.when(j == nf - 1)
        def _():
            y_ref[...] = acc_ref[...]

    @pl.when(jnp.logical_not(active) & (j == nf - 1))
    def _():
        y_ref[...] = jnp.zeros(y_ref.shape, y_ref.dtype)


def _experts(src, row_token, tile_expert, n_used, w1, w3, w2):
    nt, _, tm = row_token.shape
    d_ff = w1.shape[2]
    nf = 4
    tf = d_ff // nf

    def jcol(i, j, nu):
        return jnp.where(i < nu[0], j, nf - 1)

    grid_spec = pltpu.PrefetchScalarGridSpec(
        num_scalar_prefetch=2,
        grid=(nt, nf),
        in_specs=[pl.BlockSpec((1, 1, tm), lambda i, j, te, nu: (i, 0, 0), memory_space=pltpu.SMEM),
                  pl.BlockSpec((1, 1, tm), lambda i, j, te, nu: (jnp.minimum(i + 1, nt - 1), 0, 0),
                               memory_space=pltpu.SMEM),
                  pl.BlockSpec(memory_space=pl.ANY),
                  pl.BlockSpec((1, D_MODEL, tf), lambda i, j, te, nu: (te[i], 0, jcol(i, j, nu))),
                  pl.BlockSpec((1, D_MODEL, tf), lambda i, j, te, nu: (te[i], 0, jcol(i, j, nu))),
                  pl.BlockSpec((1, tf, D_MODEL), lambda i, j, te, nu: (te[i], jcol(i, j, nu), 0))],
        out_specs=pl.BlockSpec((tm, D_MODEL), lambda i, j, te, nu: (i, 0)),
        scratch_shapes=[pltpu.VMEM((2, tm, D_MODEL), src.dtype), pltpu.VMEM((tm, D_MODEL), BF16),
                        pltpu.VMEM((tm, D_MODEL), F32), pltpu.SemaphoreType.DMA((2,))],
    )
    return pl.pallas_call(
        functools.partial(_expert_kernel, nf=nf, tm=tm),
        grid_spec=grid_spec,
        out_shape=jax.ShapeDtypeStruct((nt * tm, D_MODEL), F32),
        compiler_params=_cparams(("arbitrary", "arbitrary")),
        name="moe_experts",
    )(tile_expert, n_used, row_token, row_token, src, w1, w3, w2)


def _combine_kernel(pos_ref, nxt_ref, wts_ref, x_ref, y_ref, o_ref, g_ref, sem, *, tm, nt):
    i = pl.program_id(0)
    slot = i % 2

    def request(p_ref, dst_slot):
        def issue(r, carry):
            for k in range(TOP_K):
                _row_copy(y_ref, p_ref[0, 0, TOP_K * r + k], g_ref.at[dst_slot, k], r, sem.at[dst_slot]).start()
            return carry
        lax.fori_loop(0, tm, issue, 0, unroll=8)

    @pl.when(i == 0)
    def _():
        request(pos_ref, 0)

    @pl.when(i + 1 < nt)
    def _():
        request(nxt_ref, 1 - slot)

    for k in range(TOP_K):
        pltpu.make_async_copy(y_ref.at[pl.ds(0, tm)], g_ref.at[slot, k], sem.at[slot]).wait()
    w = wts_ref[...]
    o_ref[...] = x_ref[...] + w[:, 0:1] * g_ref[slot, 0] + w[:, 1:2] * g_ref[slot, 1]


def _combine(pos, wts, x, y):
    n = x.shape[0]
    tm = min(256, n)
    nt = n // tm
    pos3 = pos.reshape(nt, 1, TOP_K * tm)
    return pl.pallas_call(
        functools.partial(_combine_kernel, tm=tm, nt=nt),
        grid=(nt,),
        in_specs=[pl.BlockSpec((1, 1, TOP_K * tm), lambda i: (i, 0, 0), memory_space=pltpu.SMEM),
                  pl.BlockSpec((1, 1, TOP_K * tm), lambda i: (jnp.minimum(i + 1, nt - 1), 0, 0),
                               memory_space=pltpu.SMEM),
                  pl.BlockSpec((tm, TOP_K), lambda i: (i, 0)),
                  pl.BlockSpec((tm, D_MODEL), lambda i: (i, 0)),
                  pl.BlockSpec(memory_space=pl.ANY)],
        out_specs=pl.BlockSpec((tm, D_MODEL), lambda i: (i, 0)),
        out_shape=jax.ShapeDtypeStruct((n, D_MODEL), F32),
        scratch_shapes=[pltpu.VMEM((2, TOP_K, tm, D_MODEL), F32), pltpu.SemaphoreType.DMA((2,))],
        compiler_params=_cparams(("arbitrary",)),
        name="moe_combine",
    )(pos3, pos3, wts, x, y)


def _moe(hn, x, wr, w1, w3, w2):
    idx, wts = _router(hn, wr)
    pos, row_token, tile_expert, n_used = _route(idx, MOE_TM)
    y = _experts(hn, row_token, tile_expert, n_used, w1, w3, w2)
    return _combine(pos, wts, x, y)


def _permute_w_in(w):
    cols = [w[:, _ORIG[name][0]:_ORIG[name][0] + _ORIG[name][1]] for name in _NEW_ORDER]
    cols.append(jnp.zeros((w.shape[0], PROJ_W - _USED_W), w.dtype))
    return jnp.concatenate(cols, axis=1).astype(BF16)


def _angles(pos, dim):
    inv = ROPE_THETA ** (-jnp.arange(0, dim, 2, dtype=F32) / dim)
    return pos.astype(F32)[:, None] * inv[None, :]


def _rope_tables(seqs):
    pos = jnp.concatenate([jnp.tile(jnp.arange(s), b) for b, s in seqs])
    ang = _angles(pos, HEAD_DIM)
    cosb = jnp.tile(jnp.cos(ang), (1, 4))
    sinb = jnp.tile(jnp.concatenate([-jnp.sin(ang), jnp.sin(ang)], axis=1), (1, 2))
    ar = _angles(pos // GRID_W, HEAD_DIM // 2)
    ac = _angles(pos % GRID_W, HEAD_DIM // 2)
    cosa = jnp.tile(jnp.concatenate([jnp.cos(ar), jnp.cos(ar), jnp.cos(ac), jnp.cos(ac)], axis=1), (1, 2))
    sina = jnp.tile(jnp.concatenate([-jnp.sin(ar), jnp.sin(ar), -jnp.sin(ac), jnp.sin(ac)], axis=1), (1, 2))
    return cosb, sinb, cosa, sina


def _head_mean_matrix():
    blk = np.kron(np.eye(LANES // HEAD_DIM), np.ones((HEAD_DIM, HEAD_DIM))) / HEAD_DIM
    return jnp.asarray(blk, BF16)


def _gain_rows(a_qn, a_kn, b_qn, b_kn):
    scale = HEAD_DIM ** -0.5
    scale_a = scale * np.log2(np.e)
    rows = [jnp.tile(b_qn, 2) * scale, jnp.tile(b_kn, 2), jnp.tile(a_qn, 2) * scale_a, jnp.tile(a_kn, 2),
            jnp.ones((LANES,), F32), jnp.full((LANES,), C_KEY_DIM ** -0.5, F32),
            jnp.ones((LANES,), F32), jnp.ones((LANES,), F32)]
    return jnp.stack(rows).astype(F32)


def _trunk(x, seqs, norm1_g, w_in, a_qn, a_kn, b_qn, b_kn, ret_dec_f, ret_dec_b, ret_norm_g,
           w_oa, w_ob, w_oc, w_out, norm2_g, ffn_w1, ffn_w3, ffn_w2,
           moe_router, moe_w1, moe_w3, moe_w2):
    depth = w_in.shape[0]
    tables = _rope_tables(seqs)
    mmat = _head_mean_matrix()
    for l in range(depth):
        proj = _inproj(x, norm1_g[l][None, :], _permute_w_in(w_in[l]))
        prepped = _prep(proj, tables, _gain_rows(a_qn[l], a_kn[l], b_qn[l], b_kn[l]), mmat)
        bqs, bks, bvs = prepped[0:3], prepped[3:6], prepped[6:9]
        aq, ak, cq, ck = prepped[9:]
        aqt = aq.T
        avt = proj[:, OFF["av"]:OFF["av"] + A_KV].T
        tabs_f = _ret_tables(ret_dec_f[l], C_CHUNK, False)
        tabs_b = _ret_tables(ret_dec_b[l], C_CHUNK, True)
        ng = ret_norm_g[l][None, :].astype(F32)
        res_a, res_c = None, None
        res_b = [None] * B_GROUPS
        row_off = 0
        for b, s in seqs:
            res_a = _mixer_a(_a_scores_small(a_qn[l], a_kn[l]), aqt, ak, avt, res_a, row_off, b, s)
            for g, (_, dil) in enumerate(B_PATTERNS):
                res_b[g] = _mixer_b_group(bqs[g], bks[g], bvs[g], res_b[g], g, dil, row_off, b, s)
            res_c = _mixer_c(cq, ck, proj, tabs_f, tabs_b, ng, res_c, row_off, b, s)
            row_off += b * s
        x, hn = _merge(res_a[0], [r[0] for r in res_b], [r[1] for r in res_b], res_c[1], proj, x,
                       w_oa[l].astype(BF16), w_ob[l].astype(BF16), w_oc[l].astype(BF16),
                       w_out[l].astype(BF16), norm2_g[l][None, :], BF16 if l % 2 == 0 else F32)
        i = l // 2
        if l % 2 == 0:
            x = _ffn(hn, x, ffn_w1[i].astype(BF16), ffn_w3[i].astype(BF16), ffn_w2[i].astype(BF16))
        else:
            x = _moe(hn, x, moe_router[i], moe_w1[i].astype(BF16), moe_w3[i].astype(BF16),
                     moe_w2[i].astype(BF16))
    return x


def kernel(x_prompt, x_sample, norm1_g, w_in, a_qn, a_kn, b_qn, b_kn, ret_dec_f, ret_dec_b, ret_norm_g,
           w_oa, w_ob, w_oc, w_out, norm2_g, ffn_w1, ffn_w3, ffn_w2, moe_router, moe_w1, moe_w3, moe_w2):
    seqs = (x_prompt.shape[:2], x_sample.shape[:2])
    x = jnp.concatenate([x_prompt.reshape(-1, D_MODEL), x_sample.reshape(-1, D_MODEL)], axis=0)
    y = _trunk(x, seqs, norm1_g, w_in, a_qn, a_kn, b_qn, b_kn, ret_dec_f, ret_dec_b, ret_norm_g,
               w_oa, w_ob, w_oc, w_out, norm2_g, ffn_w1, ffn_w3, ffn_w2,
               moe_router, moe_w1, moe_w3, moe_w2)
    n_p = x_prompt.shape[0] * x_prompt.shape[1]
    return (y[:n_p].reshape(x_prompt.shape), y[n_p:].reshape(x_sample.shape))
```

```python
import functools

import numpy as np
import jax
import jax.numpy as jnp
from jax import lax
from jax.experimental import pallas as pl
from jax.experimental.pallas import tpu as pltpu

F32 = jnp.float32
BF16 = jnp.bfloat16

D_MODEL = 1024
GRID_W = 64
HEAD_DIM = 64
ROPE_THETA = 10000.0
EPS = 1e-6
A_HEADS = 8
A_KV_HEADS = 2
B_PATTERNS = ((128, 1), (512, 4), (2048, 16))
B_GROUPS = 3
B_HEADS = 8
C_HEADS = 4
C_KEY_DIM = 64
C_VAL_DIM = 128
N_EXPERTS = 8
N_BRANCHES = 3

A_Q = A_HEADS * HEAD_DIM
A_KV = A_KV_HEADS * HEAD_DIM
B_Q = B_GROUPS * B_HEADS * HEAD_DIM
B_KV = B_HEADS * HEAD_DIM
C_QK = C_HEADS * C_KEY_DIM
C_V = C_HEADS * C_VAL_DIM

LANES = 128
VMEM_LIMIT = 56 * 1024 * 1024

_ORIG_SPLITS = (("aq", A_Q), ("ak", A_KV), ("av", A_KV), ("bq", B_Q), ("bk", B_KV), ("bv", B_KV),
                ("cq", C_QK), ("ck", C_QK), ("cv", C_V), ("cg", C_V), ("gl", N_BRANCHES * D_MODEL))
_NEW_ORDER = ("bq", "bk", "aq", "bv", "cv", "cg", "gl", "cq", "ck", "ak", "av")
PROJ_W = 8192


def _layout():
    orig, o = {}, 0
    for name, w in _ORIG_SPLITS:
        orig[name] = (o, w)
        o += w
    new, o = {}, 0
    for name in _NEW_ORDER:
        new[name] = o
        o += orig[name][1]
    return orig, new, o


_ORIG, OFF, _USED_W = _layout()


def _cparams(sem):
    return pltpu.CompilerParams(dimension_semantics=sem, vmem_limit_bytes=VMEM_LIMIT)


def _seq_call(kern, prev, *, in_specs, out_specs, out_shape, args, **kwargs):
    n_in = len(in_specs)
    if prev is None:
        return pl.pallas_call(kern, in_specs=in_specs, out_specs=out_specs, out_shape=out_shape,
                              **kwargs)(*args)
    n_prev = len(prev)

    def chained(*refs):
        return kern(*refs[:n_in], *refs[n_in + n_prev:])

    return pl.pallas_call(
        chained,
        in_specs=list(in_specs) + [pl.BlockSpec(memory_space=pl.ANY)] * n_prev,
        out_specs=out_specs, out_shape=out_shape,
        input_output_aliases={n_in + k: k for k in range(n_prev)},
        **kwargs)(*args, *prev)


def _inproj_kernel(x_ref, g_ref, w_ref, o_ref, xn_ref):
    @pl.when(pl.program_id(1) == 0)
    def _():
        x = x_ref[...]
        ms = jnp.mean(x * x, axis=-1, keepdims=True)
        xn_ref[...] = (x * lax.rsqrt(ms + EPS) * g_ref[...]).astype(BF16)

    o_ref[...] = jnp.dot(xn_ref[...], w_ref[...], preferred_element_type=F32).astype(BF16)


def _inproj(x, g, w):
    n = x.shape[0]
    tm = min(2048, n)
    tn = 1024
    return pl.pallas_call(
        _inproj_kernel,
        grid=(n // tm, PROJ_W // tn),
        in_specs=[pl.BlockSpec((tm, D_MODEL), lambda i, j: (i, 0)),
                  pl.BlockSpec((1, D_MODEL), lambda i, j: (0, 0)),
                  pl.BlockSpec((D_MODEL, tn), lambda i, j: (0, j))],
        out_specs=pl.BlockSpec((tm, tn), lambda i, j: (i, j)),
        out_shape=jax.ShapeDtypeStruct((n, PROJ_W), BF16),
        scratch_shapes=[pltpu.VMEM((tm, D_MODEL), BF16)],
        compiler_params=_cparams(("parallel", "arbitrary")),
        name="inproj",
    )(x, g, w)


def _norm_rot(x, gain, cos, sin, mmat, half, do_norm):
    if do_norm:
        ms = jnp.dot((x * x).astype(BF16), mmat, preferred_element_type=F32)
        x = x * lax.rsqrt(ms + EPS)
    x = x * gain
    lane = lax.broadcasted_iota(jnp.int32, x.shape, 1)
    first = (lane % (2 * half)) < half
    swapped = jnp.where(first, pltpu.roll(x, LANES - half, 1), pltpu.roll(x, half, 1))
    return x * cos + swapped * sin


def _prep_kernel(bqk_ref, aq_ref, bv_ref, cq_ref, ck_ref, ak_ref,
                 cosb_ref, sinb_ref, cosa_ref, sina_ref, gain_ref, mmat_ref,
                 q0_ref, q1_ref, q2_ref, k0_ref, k1_ref, k2_ref, v0_ref, v1_ref, v2_ref,
                 oaq_ref, oak_ref, ocq_ref, ock_ref, scr_ref):
    cosb, sinb = cosb_ref[...], sinb_ref[...]
    cosa, sina = cosa_ref[...], sina_ref[...]
    mmat = mmat_ref[...]
    gains = gain_ref[...]
    tm = scr_ref.shape[0]

    def prepared(src_ref, src_off, c, gain_row, cos, sin, half, do_norm):
        x = src_ref[:, src_off + c * LANES:src_off + (c + 1) * LANES].astype(F32)
        return _norm_rot(x, gains[gain_row:gain_row + 1, :], cos, sin, mmat, half, do_norm)

    def run(src_ref, dst_ref, width, gain_row, cos, sin, half, do_norm):
        for c in range(width // LANES):
            y = prepared(src_ref, 0, c, gain_row, cos, sin, half, do_norm)
            dst_ref[:, c * LANES:(c + 1) * LANES] = y.astype(dst_ref.dtype)

    def store_dilated(y, dst_ref, dil, c):
        if dil == 1:
            dst_ref[:, c * LANES:(c + 1) * LANES] = y.astype(dst_ref.dtype)
            return
        scr_ref[...] = y
        for r in range(dil):
            rows = scr_ref[pl.ds(r, tm // dil, stride=dil), :]
            dst_ref[:, r * B_KV + c * LANES:r * B_KV + (c + 1) * LANES] = rows.astype(dst_ref.dtype)

    chunks = B_KV // LANES
    for g, (q_ref, (_, dil)) in enumerate(zip((q0_ref, q1_ref, q2_ref), B_PATTERNS)):
        for c in range(chunks):
            y = prepared(bqk_ref, g * B_KV, c, 0, cosb, sinb, HEAD_DIM // 2, True)
            store_dilated(y, q_ref, dil, c)
    for c in range(chunks):
        yk = prepared(bqk_ref, B_Q, c, 1, cosb, sinb, HEAD_DIM // 2, True)
        yv = bv_ref[:, c * LANES:(c + 1) * LANES].astype(F32)
        for (_, dil), k_ref, v_ref in zip(B_PATTERNS, (k0_ref, k1_ref, k2_ref), (v0_ref, v1_ref, v2_ref)):
            store_dilated(yk, k_ref, dil, c)
            store_dilated(yv, v_ref, dil, c)
    run(aq_ref, oaq_ref, A_Q, 2, cosa, sina, HEAD_DIM // 4, True)
    run(ak_ref, oak_ref, A_KV, 3, cosa, sina, HEAD_DIM // 4, True)
    run(cq_ref, ocq_ref, C_QK, 4, cosb, sinb, C_KEY_DIM // 2, False)
    run(ck_ref, ock_ref, C_QK, 5, cosb, sinb, C_KEY_DIM // 2, False)


def _prep(proj, tables, gains, mmat):
    n = proj.shape[0]
    tm = min(512, n)
    cosb, sinb, cosa, sina = tables

    def col(width, name):
        idx = OFF[name] // width
        return pl.BlockSpec((tm, width), lambda i: (i, idx))

    def tab():
        return pl.BlockSpec((tm, LANES), lambda i: (i, 0))

    dils = [dil for _, dil in B_PATTERNS]
    shapes = [(n // dil, dil * B_KV) for dil in dils] * 3 + [(n, A_Q), (n, A_KV), (n, C_QK), (n, C_QK)]
    blocks = [(tm // dil, dil * B_KV) for dil in dils] * 3 + [(tm, A_Q), (tm, A_KV), (tm, C_QK), (tm, C_QK)]
    return pl.pallas_call(
        _prep_kernel,
        grid=(n // tm,),
        in_specs=[col(B_Q + B_KV, "bq"), col(A_Q, "aq"), col(B_KV, "bv"), col(C_QK, "cq"),
                  col(C_QK, "ck"), col(A_KV, "ak"), tab(), tab(), tab(), tab(),
                  pl.BlockSpec((8, LANES), lambda i: (0, 0)),
                  pl.BlockSpec((LANES, LANES), lambda i: (0, 0))],
        out_specs=[pl.BlockSpec(blk, lambda i: (i, 0)) for blk in blocks],
        out_shape=[jax.ShapeDtypeStruct(shp, BF16) for shp in shapes],
        scratch_shapes=[pltpu.VMEM((tm, LANES), F32)],
        compiler_params=_cparams(("parallel",)),
        name="prep",
    )(proj, proj, proj, proj, proj, proj, cosb, sinb, cosa, sina, gains, mmat)


A_ONES_ROWS = 16


def _attn_a_kernel(small_ref, qt_ref, k0_ref, kn_ref, vt_ref, o_ref, qp_ref, m_ref, acc_ref, sa_ref, sb_ref,
                   *, bq, nk):
    ik = pl.program_id(2)
    rep = A_HEADS // A_KV_HEADS
    kv_heads = range(A_KV_HEADS)

    @pl.when(ik == 0)
    def _():
        qp_ref[...] = jnp.zeros(qp_ref.shape, BF16)
        for g in kv_heads:
            for r in range(rep):
                h = g * rep + r
                qp_ref[g, g * HEAD_DIM:(g + 1) * HEAD_DIM, r * bq:(r + 1) * bq] = (
                    qt_ref[h * HEAD_DIM:(h + 1) * HEAD_DIM, :])
        m_ref[...] = jnp.full(m_ref.shape, -jnp.inf, F32)
        acc_ref[...] = jnp.zeros(acc_ref.shape, F32)
        for g in kv_heads:
            sa_ref[g] = jnp.dot(k0_ref[...], qp_ref[g], preferred_element_type=F32)

    def step(cur_ref, nxt_ref, has_next, small):
        kn = kn_ref[...]
        ones = jnp.ones((A_ONES_ROWS, kn.shape[0]), BF16)
        for g in kv_heads:
            v_aug = jnp.concatenate([vt_ref[g * HEAD_DIM:(g + 1) * HEAD_DIM, :], ones], axis=0)
            for r in range(rep):
                cols = slice(r * bq, (r + 1) * bq)
                if has_next:
                    nxt_ref[g, :, cols] = jnp.dot(kn, qp_ref[g, :, cols], preferred_element_type=F32)
                s = cur_ref[g, :, cols]
                if small:
                    pv = jnp.dot(v_aug, jnp.exp2(s).astype(BF16), preferred_element_type=F32)
                    acc_ref[g, :, cols] += pv
                    continue
                m_prev = m_ref[g, :, cols]
                m_new = jnp.maximum(m_prev, jnp.max(s, axis=0, keepdims=True))
                p = jnp.exp2(s - m_new).astype(BF16)
                alpha = jnp.exp2(m_prev - m_new)
                pv = jnp.dot(v_aug, p, preferred_element_type=F32)
                acc_ref[g, :, cols] = alpha * acc_ref[g, :, cols] + pv
                m_ref[g, :, cols] = m_new

    last = ik == nk - 1
    is_small = small_ref[0] != 0
    for parity, cur_ref, nxt_ref in ((0, sa_ref, sb_ref), (1, sb_ref, sa_ref)):
        for has_next in (True, False):
            for small in (True, False):
                @pl.when((ik % 2 == parity) & (last != has_next) & (is_small == small))
                def _(cur_ref=cur_ref, nxt_ref=nxt_ref, has_next=has_next, small=small):
                    step(cur_ref, nxt_ref, has_next, small)

    @pl.when(last)
    def _():
        for g in range(A_KV_HEADS):
            acc = acc_ref[g]
            o = acc[:HEAD_DIM] / acc[HEAD_DIM:HEAD_DIM + 1]
            for r in range(rep):
                h = g * rep + r
                o_ref[h * HEAD_DIM:(h + 1) * HEAD_DIM, :] = o[:, r * bq:(r + 1) * bq].astype(o_ref.dtype)


A_SMALL_SCORE = 60.0


def _a_scores_small(a_qn, a_kn):
    bound = (HEAD_DIM ** 0.5) * np.log2(np.e) * 1.02 * jnp.max(jnp.abs(a_qn)) * jnp.max(jnp.abs(a_kn))
    return (bound < A_SMALL_SCORE).astype(jnp.int32).reshape(1)


def _mixer_a(small, aqt, ak, avt, prev, row_off, b, s):
    n = ak.shape[0]
    bq = min(512, s)
    bk = min(512, s)
    nq, nk = s // bq, s // bk
    rep = A_HEADS // A_KV_HEADS
    q0, k0 = row_off // bq, row_off // bk
    kern = functools.partial(_attn_a_kernel, bq=bq, nk=nk)
    return _seq_call(
        kern, prev,
        grid=(b, nq, nk),
        in_specs=[pl.BlockSpec(memory_space=pltpu.SMEM),
                  pl.BlockSpec((A_Q, bq), lambda ib, iq, ik: (0, q0 + ib * nq + iq)),
                  pl.BlockSpec((bk, A_KV), lambda ib, iq, ik: (k0 + ib * nk, 0)),
                  pl.BlockSpec((bk, A_KV), lambda ib, iq, ik: (k0 + ib * nk + jnp.minimum(ik + 1, nk - 1), 0)),
                  pl.BlockSpec((A_KV, bk), lambda ib, iq, ik: (0, k0 + ib * nk + ik))],
        out_specs=[pl.BlockSpec((A_Q, bq), lambda ib, iq, ik: (0, q0 + ib * nq + iq))],
        out_shape=[jax.ShapeDtypeStruct((A_Q, n), BF16)],
        scratch_shapes=[pltpu.VMEM((A_KV_HEADS, A_KV, rep * bq), BF16),
                        pltpu.VMEM((A_KV_HEADS, 1, rep * bq), F32),
                        pltpu.VMEM((A_KV_HEADS, HEAD_DIM + A_ONES_ROWS, rep * bq), F32),
                        pltpu.VMEM((A_KV_HEADS, bk, rep * bq), F32),
                        pltpu.VMEM((A_KV_HEADS, bk, rep * bq), F32)],
        compiler_params=_cparams(("parallel", "parallel", "arbitrary")),
        name="mixer_a",
        args=(small, aqt, ak, ak, avt))


B_BLOCK_Q = 128
B_HALO = 64


def _attn_b_kernel(q_ref, k0_ref, k1_ref, k2_ref, k3_ref, v0_ref, v1_ref, v2_ref, v3_ref,
                   o_ref, lse_ref, *, u_len):
    i = pl.program_id(2)
    kcat = jnp.concatenate([k0_ref[...], k1_ref[...], k2_ref[...], k3_ref[...]], axis=0)
    vcat = jnp.concatenate([v0_ref[...], v1_ref[...], v2_ref[...], v3_ref[...]], axis=0)
    nkeys = B_BLOCK_Q + 2 * B_HALO
    a = lax.broadcasted_iota(jnp.int32, (B_BLOCK_Q, nkeys), 0)
    c = lax.broadcasted_iota(jnp.int32, (B_BLOCK_Q, nkeys), 1)
    key_pos = i * B_BLOCK_Q - B_HALO + c
    valid = (c >= a) & (c <= a + 2 * B_HALO) & (key_pos >= 0) & (key_pos < u_len)
    left = lax.broadcasted_iota(jnp.int32, (B_BLOCK_Q, LANES), 1) < HEAD_DIM
    zero = jnp.zeros((B_BLOCK_Q, LANES), BF16)
    heads = [(pair, own) for pair in range(B_HEADS // 2) for own in (left, jnp.logical_not(left))]
    lanes = [slice(pair * LANES, (pair + 1) * LANES) for pair in range(B_HEADS // 2)]
    s = [lax.dot_general(jnp.where(own, q_ref[:, lanes[pair]], zero), kcat[:, lanes[pair]],
                         (((1,), (1,)), ((), ())), preferred_element_type=F32) for pair, own in heads]
    s = [jnp.where(valid, sh, -1e30) for sh in s]
    m = [jnp.max(sh, axis=1, keepdims=True) for sh in s]
    e = [jnp.exp(sh - mh) for sh, mh in zip(s, m)]
    den = [jnp.sum(eh, axis=1, keepdims=True) for eh in e]
    pv = [jnp.dot(eh.astype(BF16), vcat[:, lanes[pair]], preferred_element_type=F32)
          for eh, (pair, _) in zip(e, heads)]
    outs = [pvh / dh for pvh, dh in zip(pv, den)]
    lses = [mh + jnp.log(dh) for mh, dh in zip(m, den)]
    for pair in range(B_HEADS // 2):
        o_ref[:, lanes[pair]] = jnp.where(left, outs[2 * pair], outs[2 * pair + 1]).astype(o_ref.dtype)
        lse_ref[:, lanes[pair]] = jnp.where(left, lses[2 * pair], lses[2 * pair + 1])


def _mixer_b_group(qd, kd, vd, prev, g, dil, row_off, b, s):
    u_len = s // dil
    nq = u_len // B_BLOCK_Q
    nkb = u_len // B_HALO
    q_rows0 = row_off // dil // B_BLOCK_Q
    k_rows0 = row_off // dil // B_HALO

    def kspec(t):
        def imap(ib, r, i):
            blk = jnp.clip(2 * i - 1 + t, 0, nkb - 1)
            return (k_rows0 + ib * nkb + blk, r)
        return pl.BlockSpec((B_HALO, B_KV), imap)

    def qspec():
        return pl.BlockSpec((B_BLOCK_Q, B_KV), lambda ib, r, i: (q_rows0 + ib * nq + i, r))

    kern = functools.partial(_attn_b_kernel, u_len=u_len)
    return _seq_call(
        kern, prev,
        grid=(b, dil, nq),
        in_specs=[qspec()] + [kspec(t) for t in range(4)] + [kspec(t) for t in range(4)],
        out_specs=[qspec(), qspec()],
        out_shape=[jax.ShapeDtypeStruct(qd.shape, BF16), jax.ShapeDtypeStruct(qd.shape, F32)],
        compiler_params=_cparams(("parallel", "parallel", "parallel")),
        name=f"mixer_b{g}",
        args=(qd, kd, kd, kd, kd, vd, vd, vd, vd))


def _ret_chunk(q, k, v, dec_ref, xi_ref, zeta_ref, cdec_ref, r_ref):
    outs = []
    for h in range(C_HEADS):
        qh = q[:, h * C_KEY_DIM:(h + 1) * C_KEY_DIM]
        kh = k[:, h * C_KEY_DIM:(h + 1) * C_KEY_DIM]
        vh = v[:, h * C_VAL_DIM:(h + 1) * C_VAL_DIM]
        att = lax.dot_general(qh, kh, (((1,), (1,)), ((), ())), preferred_element_type=F32) * dec_ref[h]
        inner = jnp.dot(att.astype(BF16), vh, preferred_element_type=F32)
        r = r_ref[h]
        cross = jnp.dot(qh, r.astype(BF16), preferred_element_type=F32) * xi_ref[h]
        kz = (kh.astype(F32) * zeta_ref[h]).astype(BF16)
        r_ref[h] = r * cdec_ref[h] + lax.dot_general(kz, vh, (((0,), (0,)), ((), ())),
                                                     preferred_element_type=F32)
        outs.append(inner + cross)
    return jnp.concatenate(outs, axis=1)


def _ret_fwd_kernel(q_ref, k_ref, v_ref, dec_ref, xi_ref, zeta_ref, cdec_ref, o_ref, r_ref):
    @pl.when(pl.program_id(1) == 0)
    def _():
        r_ref[...] = jnp.zeros(r_ref.shape, F32)

    o_ref[...] = _ret_chunk(q_ref[...], k_ref[...], v_ref[...], dec_ref, xi_ref, zeta_ref, cdec_ref, r_ref)


def _ret_bwd_kernel(q_ref, k_ref, v_ref, dec_ref, xi_ref, zeta_ref, cdec_ref, of_ref, gate_ref, ng_ref,
                    o_ref, r_ref):
    @pl.when(pl.program_id(1) == 0)
    def _():
        r_ref[...] = jnp.zeros(r_ref.shape, F32)

    o = of_ref[...] + _ret_chunk(q_ref[...], k_ref[...], v_ref[...], dec_ref, xi_ref, zeta_ref, cdec_ref, r_ref)
    ng = ng_ref[...]
    gate = gate_ref[...].astype(F32)
    for h in range(C_HEADS):
        sl = slice(h * C_VAL_DIM, (h + 1) * C_VAL_DIM)
        oh = o[:, sl]
        mu = jnp.mean(oh, axis=1, keepdims=True)
        var = jnp.mean(jnp.square(oh - mu), axis=1, keepdims=True)
        y = (oh - mu) * lax.rsqrt(var + EPS) * ng[:, sl]
        gh = gate[:, sl]
        o_ref[:, sl] = (gh * jax.nn.sigmoid(gh) * y).astype(o_ref.dtype)


def _ret_tables(dec_param, chunk, strict):
    log_g = -jnp.exp(dec_param.astype(F32))
    j = jnp.arange(chunk, dtype=F32)
    lg = log_g[:, None, None]
    if strict:
        diff = j[None, :] - j[:, None]
        dec = jnp.where((diff > 0)[None], jnp.exp(jnp.maximum(diff, 0.0)[None] * lg), 0.0)
        xi = jnp.exp((chunk - j)[None, :, None] * lg)
        zeta = jnp.exp(j[None, :, None] * lg)
    else:
        diff = j[:, None] - j[None, :]
        dec = jnp.where((diff >= 0)[None], jnp.exp(jnp.maximum(diff, 0.0)[None] * lg), 0.0)
        xi = jnp.exp((j + 1.0)[None, :, None] * lg)
        zeta = jnp.exp((chunk - 1.0 - j)[None, :, None] * lg)
    xi = jnp.broadcast_to(xi, (C_HEADS, chunk, C_VAL_DIM))
    zeta = jnp.broadcast_to(zeta, (C_HEADS, chunk, C_KEY_DIM))
    cdec = jnp.broadcast_to(jnp.exp(chunk * log_g)[:, None, None], (C_HEADS, C_KEY_DIM, C_VAL_DIM))
    return dec, xi, zeta, cdec


C_CHUNK = 512


def _mixer_c(cq, ck, proj, tabs_f, tabs_b, norm_g, prev, row_off, b, s):
    n = cq.shape[0]
    prev_f, prev_o = (None, None) if prev is None else ([prev[0]], [prev[1]])
    nc = s // C_CHUNK
    r0 = row_off // C_CHUNK
    v_col = OFF["cv"] // C_V
    g_col = OFF["cg"] // C_V

    def fwd_rows(ib, c):
        return r0 + ib * nc + c

    def bwd_rows(ib, c):
        return r0 + ib * nc + (nc - 1 - c)

    def specs(rows):
        def full(shape):
            return pl.BlockSpec(shape, lambda ib, c: (0,) * len(shape))
        return [pl.BlockSpec((C_CHUNK, C_QK), lambda ib, c: (rows(ib, c), 0)),
                pl.BlockSpec((C_CHUNK, C_QK), lambda ib, c: (rows(ib, c), 0)),
                pl.BlockSpec((C_CHUNK, C_V), lambda ib, c: (rows(ib, c), v_col)),
                full((C_HEADS, C_CHUNK, C_CHUNK)), full((C_HEADS, C_CHUNK, C_VAL_DIM)),
                full((C_HEADS, C_CHUNK, C_KEY_DIM)), full((C_HEADS, C_KEY_DIM, C_VAL_DIM))]

    scratch = [pltpu.VMEM((C_HEADS, C_KEY_DIM, C_VAL_DIM), F32)]
    o_f, = _seq_call(
        _ret_fwd_kernel, prev_f,
        grid=(b, nc),
        in_specs=specs(fwd_rows),
        out_specs=[pl.BlockSpec((C_CHUNK, C_V), lambda ib, c: (fwd_rows(ib, c), 0))],
        out_shape=[jax.ShapeDtypeStruct((n, C_V), F32)],
        scratch_shapes=scratch,
        compiler_params=_cparams(("parallel", "arbitrary")),
        name="ret_fwd",
        args=(cq, ck, proj, *tabs_f))
    o_c, = _seq_call(
        _ret_bwd_kernel, prev_o,
        grid=(b, nc),
        in_specs=specs(bwd_rows) + [
            pl.BlockSpec((C_CHUNK, C_V), lambda ib, c: (bwd_rows(ib, c), 0)),
            pl.BlockSpec((C_CHUNK, C_V), lambda ib, c: (bwd_rows(ib, c), g_col)),
            pl.BlockSpec((1, C_V), lambda ib, c: (0, 0))],
        out_specs=[pl.BlockSpec((C_CHUNK, C_V), lambda ib, c: (bwd_rows(ib, c), 0))],
        out_shape=[jax.ShapeDtypeStruct((n, C_V), BF16)],
        scratch_shapes=scratch,
        compiler_params=_cparams(("parallel", "arbitrary")),
        name="ret_bwd",
        args=(cq, ck, proj, *tabs_b, o_f, proj, norm_g))
    return [o_f, o_c]


def _merge_kernel(oa_ref, ob0_ref, ob1_ref, ob2_ref, l0_ref, l1_ref, l2_ref, oc_ref,
                  ga_ref, gb_ref, gc_ref, x_ref, woa_ref, wob_ref, woc_ref, wout_ref, n2_ref,
                  xo_ref, hn_ref, scr_ref):
    tm = x_ref.shape[0]

    def natural(src_ref, dil, slot):
        if dil == 1:
            return src_ref[...].astype(F32)
        chunks = B_KV // LANES
        for r in range(dil):
            for c in range(chunks):
                col = r * B_KV + c * LANES
                scr_ref[slot, c, pl.ds(r, tm // dil, stride=dil), :] = src_ref[:, col:col + LANES].astype(F32)
        return jnp.concatenate([scr_ref[slot, c] for c in range(chunks)], axis=1)

    dils = [dil for _, dil in B_PATTERNS]
    l0, l1, l2 = [natural(ref, dil, 2 * g) for g, (ref, dil) in enumerate(zip((l0_ref, l1_ref, l2_ref), dils))]
    o0, o1, o2 = [natural(ref, dil, 2 * g + 1)
                  for g, (ref, dil) in enumerate(zip((ob0_ref, ob1_ref, ob2_ref), dils))]
    mx = jnp.maximum(jnp.maximum(l0, l1), l2)
    e0, e1, e2 = jnp.exp(l0 - mx), jnp.exp(l1 - mx), jnp.exp(l2 - mx)
    ob = (e0 * o0 + e1 * o1 + e2 * o2) / (e0 + e1 + e2)

    def branch(o, w_ref, gate_ref):
        y = jnp.dot(o, w_ref[...], preferred_element_type=F32)
        return jax.nn.sigmoid(gate_ref[...].astype(F32)) * y

    oa = jnp.transpose(oa_ref[...].astype(F32)).astype(BF16)
    merged = (branch(oa, woa_ref, ga_ref) + branch(ob.astype(BF16), wob_ref, gb_ref)
              + branch(oc_ref[...], woc_ref, gc_ref))
    x = x_ref[...] + jnp.dot(merged.astype(BF16), wout_ref[...], preferred_element_type=F32)
    xo_ref[...] = x
    ms = jnp.mean(x * x, axis=-1, keepdims=True)
    hn_ref[...] = (x * lax.rsqrt(ms + EPS) * n2_ref[...]).astype(hn_ref.dtype)


def _merge(o_a, o_b, lse_b, o_c, proj, x, w_oa, w_ob, w_oc, w_out, n2, hn_dtype):
    n = x.shape[0]
    tm = min(512, n)
    gl0 = OFF["gl"] // D_MODEL

    def rows(width):
        return pl.BlockSpec((tm, width), lambda i: (i, 0))

    def gate(k):
        return pl.BlockSpec((tm, D_MODEL), lambda i: (i, gl0 + k))

    def full(r, c):
        return pl.BlockSpec((r, c), lambda i: (0, 0))

    grouped = [pl.BlockSpec((tm // dil, dil * B_KV), lambda i: (i, 0)) for _, dil in B_PATTERNS]
    return pl.pallas_call(
        _merge_kernel,
        grid=(n // tm,),
        in_specs=[pl.BlockSpec((A_Q, tm), lambda i: (0, i))] + grouped + grouped
                 + [rows(C_V), gate(0), gate(1), gate(2), rows(D_MODEL),
                    full(A_Q, D_MODEL), full(B_KV, D_MODEL), full(C_V, D_MODEL), full(D_MODEL, D_MODEL),
                    full(1, D_MODEL)],
        out_specs=[rows(D_MODEL), rows(D_MODEL)],
        out_shape=[jax.ShapeDtypeStruct((n, D_MODEL), F32), jax.ShapeDtypeStruct((n, D_MODEL), hn_dtype)],
        scratch_shapes=[pltpu.VMEM((2 * B_GROUPS, B_KV // LANES, tm, LANES), F32)],
        compiler_params=_cparams(("parallel",)),
        name="merge_out",
    )(o_a, *o_b, *lse_b, o_c, proj, proj, proj, x, w_oa, w_ob, w_oc, w_out, n2)


def _ffn_kernel(hn_ref, x_ref, w1_ref, w3_ref, w2_ref, o_ref):
    j = pl.program_id(1)
    hn = hn_ref[...]
    a = jnp.dot(hn, w1_ref[...], preferred_element_type=F32)
    g = jnp.dot(hn, w3_ref[...], preferred_element_type=F32)
    h = (a * jax.nn.sigmoid(a) * g).astype(BF16)
    y = jnp.dot(h, w2_ref[...], preferred_element_type=F32)

    @pl.when(j == 0)
    def _():
        o_ref[...] = x_ref[...] + y

    @pl.when(j > 0)
    def _():
        o_ref[...] += y


def _ffn(hn, x, w1, w3, w2):
    n = x.shape[0]
    d_ff = w1.shape[1]
    tm = min(1024, n)
    tf = d_ff // 2
    nf = d_ff // tf
    return pl.pallas_call(
        _ffn_kernel,
        grid=(n // tm, nf),
        in_specs=[pl.BlockSpec((tm, D_MODEL), lambda i, j: (i, 0)),
                  pl.BlockSpec((tm, D_MODEL), lambda i, j: (i, 0)),
                  pl.BlockSpec((D_MODEL, tf), lambda i, j: (0, j)),
                  pl.BlockSpec((D_MODEL, tf), lambda i, j: (0, j)),
                  pl.BlockSpec((tf, D_MODEL), lambda i, j: (j, 0))],
        out_specs=pl.BlockSpec((tm, D_MODEL), lambda i, j: (i, 0)),
        out_shape=jax.ShapeDtypeStruct((n, D_MODEL), F32),
        compiler_params=_cparams(("parallel", "arbitrary")),
        name="ffn",
    )(hn, x, w1, w3, w2)


TOP_K = 2
MOE_TM = 512


def _router_kernel(hn_ref, wr_ref, idx_ref, wts_ref):
    logits = jnp.dot(hn_ref[...], wr_ref[...], preferred_element_type=F32,
                     precision=lax.Precision.HIGHEST)
    col = lax.broadcasted_iota(jnp.int32, logits.shape, 1)
    m1 = jnp.max(logits, axis=1, keepdims=True)
    i1 = jnp.min(jnp.where(logits == m1, col, N_EXPERTS), axis=1, keepdims=True)
    rest = jnp.where(col == i1, -jnp.inf, logits)
    m2 = jnp.max(rest, axis=1, keepdims=True)
    i2 = jnp.min(jnp.where(rest == m2, col, N_EXPERTS), axis=1, keepdims=True)
    e2 = jnp.exp(m2 - m1)
    w1 = 1.0 / (1.0 + e2)
    idx_ref[...] = jnp.concatenate([i1, i2], axis=1)
    wts_ref[...] = jnp.concatenate([w1, e2 * w1], axis=1)


def _router(hn, wr):
    n = hn.shape[0]
    tm = min(1024, n)
    return pl.pallas_call(
        _router_kernel,
        grid=(n // tm,),
        in_specs=[pl.BlockSpec((tm, D_MODEL), lambda i: (i, 0)),
                  pl.BlockSpec((D_MODEL, N_EXPERTS), lambda i: (0, 0))],
        out_specs=[pl.BlockSpec((tm, TOP_K), lambda i: (i, 0)), pl.BlockSpec((tm, TOP_K), lambda i: (i, 0))],
        out_shape=[jax.ShapeDtypeStruct((n, TOP_K), jnp.int32), jax.ShapeDtypeStruct((n, TOP_K), F32)],
        compiler_params=_cparams(("parallel",)),
        name="router",
    )(hn, wr)


def _route(idx, tm):
    n = idx.shape[0]
    e_flat = idx.reshape(-1)
    onehot = (e_flat[:, None] == jnp.arange(N_EXPERTS, dtype=jnp.int32)[None, :]).astype(jnp.int32)
    csum = jnp.cumsum(onehot, axis=0)
    rank = jnp.sum((csum - onehot) * onehot, axis=1)
    gsz = ((csum[-1] + tm - 1) // tm) * tm
    gend = jnp.cumsum(gsz)
    pos = (gend - gsz)[e_flat] + rank
    n_rows = n * TOP_K + N_EXPERTS * tm
    n_tiles = n_rows // tm
    row_token = jnp.zeros((n_rows,), jnp.int32).at[pos].set(jnp.arange(n * TOP_K, dtype=jnp.int32) // TOP_K)
    tile_start = jnp.arange(n_tiles, dtype=jnp.int32) * tm
    tile_expert = jnp.minimum(jnp.sum((tile_start[:, None] >= gend[None, :]).astype(jnp.int32), axis=1),
                              N_EXPERTS - 1)
    n_used = (gend[-1] // tm).astype(jnp.int32).reshape(1)
    return pos.reshape(n, TOP_K), row_token.reshape(n_tiles, 1, tm), tile_expert, n_used


def _row_copy(src_ref, src_row, dst_ref, dst_row, sem):
    return pltpu.make_async_copy(src_ref.at[pl.ds(src_row, 1)], dst_ref.at[pl.ds(dst_row, 1)], sem)


def _expert_kernel(te_ref, nu_ref, idx_ref, nxt_ref, src_ref, w1_ref, w3_ref, w2_ref, y_ref,
                   xs_ref, xb_ref, acc_ref, sem, *, nf, tm):
    i = pl.program_id(0)
    j = pl.program_id(1)
    n_used = nu_ref[0]
    active = i < n_used
    slot = i % 2
    share = tm // nf

    @pl.when((i == 0) & (j == 0))
    def _():
        def issue(r, carry):
            _row_copy(src_ref, idx_ref[0, 0, r], xs_ref.at[0], r, sem.at[0]).start()
            return carry
        lax.fori_loop(0, tm, issue, 0, unroll=8)

    @pl.when((j == 0) & (i <= n_used))
    def _():
        pltpu.make_async_copy(src_ref.at[pl.ds(0, tm)], xs_ref.at[slot], sem.at[slot]).wait()

    @pl.when(active)
    def _():
        @pl.when(j == 0)
        def _():
            xb_ref[...] = xs_ref[slot].astype(BF16)

        for r in range(share):
            row = j * share + r
            _row_copy(src_ref, nxt_ref[0, 0, row], xs_ref.at[1 - slot], row, sem.at[1 - slot]).start()

        xb = xb_ref[...]
        a = jnp.dot(xb, w1_ref[0], preferred_element_type=F32)
        g = jnp.dot(xb, w3_ref[0], preferred_element_type=F32)
        h = (a * jax.nn.sigmoid(a) * g).astype(BF16)
        y = jnp.dot(h, w2_ref[0], preferred_element_type=F32)

        @pl.when(j == 0)
        def _():
            acc_ref[...] = y

        @pl.when(j > 0)
        def _():
            acc_ref[...] += y

        @pl.when(j == nf - 1)
        def _():
            y_ref[...] = acc_ref[...]

    @pl.when(jnp.logical_not(active) & (j == nf - 1))
    def _():
        y_ref[...] = jnp.zeros(y_ref.shape, y_ref.dtype)


def _experts(src, row_token, tile_expert, n_used, w1, w3, w2):
    nt, _, tm = row_token.shape
    d_ff = w1.shape[2]
    nf = 2
    tf = d_ff // nf

    def jcol(i, j, nu):
        return jnp.where(i < nu[0], j, nf - 1)

    grid_spec = pltpu.PrefetchScalarGridSpec(
        num_scalar_prefetch=2,
        grid=(nt, nf),
        in_specs=[pl.BlockSpec((1, 1, tm), lambda i, j, te, nu: (i, 0, 0), memory_space=pltpu.SMEM),
                  pl.BlockSpec((1, 1, tm), lambda i, j, te, nu: (jnp.minimum(i + 1, nt - 1), 0, 0),
                               memory_space=pltpu.SMEM),
                  pl.BlockSpec(memory_space=pl.ANY),
                  pl.BlockSpec((1, D_MODEL, tf), lambda i, j, te, nu: (te[i], 0, jcol(i, j, nu))),
                  pl.BlockSpec((1, D_MODEL, tf), lambda i, j, te, nu: (te[i], 0, jcol(i, j, nu))),
                  pl.BlockSpec((1, tf, D_MODEL), lambda i, j, te, nu: (te[i], jcol(i, j, nu), 0))],
        out_specs=pl.BlockSpec((tm, D_MODEL), lambda i, j, te, nu: (i, 0)),
        scratch_shapes=[pltpu.VMEM((2, tm, D_MODEL), src.dtype), pltpu.VMEM((tm, D_MODEL), BF16),
                        pltpu.VMEM((tm, D_MODEL), F32), pltpu.SemaphoreType.DMA((2,))],
    )
    return pl.pallas_call(
        functools.partial(_expert_kernel, nf=nf, tm=tm),
        grid_spec=grid_spec,
        out_shape=jax.ShapeDtypeStruct((nt * tm, D_MODEL), F32),
        compiler_params=_cparams(("arbitrary", "arbitrary")),
        name="moe_experts",
    )(tile_expert, n_used, row_token, row_token, src, w1, w3, w2)


def _combine_kernel(pos_ref, nxt_ref, wts_ref, x_ref, y_ref, o_ref, g_ref, sem, *, tm, nt):
    i = pl.program_id(0)
    slot = i % 2

    def request(p_ref, dst_slot):
        def issue(r, carry):
            for k in range(TOP_K):
                _row_copy(y_ref, p_ref[0, 0, TOP_K * r + k], g_ref.at[dst_slot, k], r, sem.at[dst_slot]).start()
            return carry
        lax.fori_loop(0, tm, issue, 0, unroll=8)

    @pl.when(i == 0)
    def _():
        request(pos_ref, 0)

    @pl.when(i + 1 < nt)
    def _():
        request(nxt_ref, 1 - slot)

    for k in range(TOP_K):
        pltpu.make_async_copy(y_ref.at[pl.ds(0, tm)], g_ref.at[slot, k], sem.at[slot]).wait()
    w = wts_ref[...]
    o_ref[...] = x_ref[...] + w[:, 0:1] * g_ref[slot, 0] + w[:, 1:2] * g_ref[slot, 1]


def _combine(pos, wts, x, y):
    n = x.shape[0]
    tm = min(256, n)
    nt = n // tm
    pos3 = pos.reshape(nt, 1, TOP_K * tm)
    return pl.pallas_call(
        functools.partial(_combine_kernel, tm=tm, nt=nt),
        grid=(nt,),
        in_specs=[pl.BlockSpec((1, 1, TOP_K * tm), lambda i: (i, 0, 0), memory_space=pltpu.SMEM),
                  pl.BlockSpec((1, 1, TOP_K * tm), lambda i: (jnp.minimum(i + 1, nt - 1), 0, 0),
                               memory_space=pltpu.SMEM),
                  pl.BlockSpec((tm, TOP_K), lambda i: (i, 0)),
                  pl.BlockSpec((tm, D_MODEL), lambda i: (i, 0)),
                  pl.BlockSpec(memory_space=pl.ANY)],
        out_specs=pl.BlockSpec((tm, D_MODEL), lambda i: (i, 0)),
        out_shape=jax.ShapeDtypeStruct((n, D_MODEL), F32),
        scratch_shapes=[pltpu.VMEM((2, TOP_K, tm, D_MODEL), F32), pltpu.SemaphoreType.DMA((2,))],
        compiler_params=_cparams(("arbitrary",)),
        name="moe_combine",
    )(pos3, pos3, wts, x, y)


def _moe(hn, x, wr, w1, w3, w2):
    idx, wts = _router(hn, wr)
    pos, row_token, tile_expert, n_used = _route(idx, MOE_TM)
    y = _experts(hn, row_token, tile_expert, n_used, w1, w3, w2)
    return _combine(pos, wts, x, y)


def _permute_w_in(w):
    cols = [w[:, _ORIG[name][0]:_ORIG[name][0] + _ORIG[name][1]] for name in _NEW_ORDER]
    cols.append(jnp.zeros((w.shape[0], PROJ_W - _USED_W), w.dtype))
    return jnp.concatenate(cols, axis=1).astype(BF16)


def _angles(pos, dim):
    inv = ROPE_THETA ** (-jnp.arange(0, dim, 2, dtype=F32) / dim)
    return pos.astype(F32)[:, None] * inv[None, :]


def _rope_tables(seqs):
    pos = jnp.concatenate([jnp.tile(jnp.arange(s), b) for b, s in seqs])
    ang = _angles(pos, HEAD_DIM)
    cosb = jnp.tile(jnp.cos(ang), (1, 4))
    sinb = jnp.tile(jnp.concatenate([-jnp.sin(ang), jnp.sin(ang)], axis=1), (1, 2))
    ar = _angles(pos // GRID_W, HEAD_DIM // 2)
    ac = _angles(pos % GRID_W, HEAD_DIM // 2)
    cosa = jnp.tile(jnp.concatenate([jnp.cos(ar), jnp.cos(ar), jnp.cos(ac), jnp.cos(ac)], axis=1), (1, 2))
    sina = jnp.tile(jnp.concatenate([-jnp.sin(ar), jnp.sin(ar), -jnp.sin(ac), jnp.sin(ac)], axis=1), (1, 2))
    return cosb, sinb, cosa, sina


def _head_mean_matrix():
    blk = np.kron(np.eye(LANES // HEAD_DIM), np.ones((HEAD_DIM, HEAD_DIM))) / HEAD_DIM
    return jnp.asarray(blk, BF16)


def _gain_rows(a_qn, a_kn, b_qn, b_kn):
    scale = HEAD_DIM ** -0.5
    scale_a = scale * np.log2(np.e)
    rows = [jnp.tile(b_qn, 2) * scale, jnp.tile(b_kn, 2), jnp.tile(a_qn, 2) * scale_a, jnp.tile(a_kn, 2),
            jnp.ones((LANES,), F32), jnp.full((LANES,), C_KEY_DIM ** -0.5, F32),
            jnp.ones((LANES,), F32), jnp.ones((LANES,), F32)]
    return jnp.stack(rows).astype(F32)


def _trunk(x, seqs, norm1_g, w_in, a_qn, a_kn, b_qn, b_kn, ret_dec_f, ret_dec_b, ret_norm_g,
           w_oa, w_ob, w_oc, w_out, norm2_g, ffn_w1, ffn_w3, ffn_w2,
           moe_router, moe_w1, moe_w3, moe_w2):
    depth = w_in.shape[0]
    tables = _rope_tables(seqs)
    mmat = _head_mean_matrix()
    for l in range(depth):
        proj = _inproj(x, norm1_g[l][None, :], _permute_w_in(w_in[l]))
        prepped = _prep(proj, tables, _gain_rows(a_qn[l], a_kn[l], b_qn[l], b_kn[l]), mmat)
        bqs, bks, bvs = prepped[0:3], prepped[3:6], prepped[6:9]
        aq, ak, cq, ck = prepped[9:]
        aqt = aq.T
        avt = proj[:, OFF["av"]:OFF["av"] + A_KV].T
        tabs_f = _ret_tables(ret_dec_f[l], C_CHUNK, False)
        tabs_b = _ret_tables(ret_dec_b[l], C_CHUNK, True)
        ng = ret_norm_g[l][None, :].astype(F32)
        res_a, res_c = None, None
        res_b = [None] * B_GROUPS
        row_off = 0
        for b, s in seqs:
            res_a = _mixer_a(_a_scores_small(a_qn[l], a_kn[l]), aqt, ak, avt, res_a, row_off, b, s)
            for g, (_, dil) in enumerate(B_PATTERNS):
                res_b[g] = _mixer_b_group(bqs[g], bks[g], bvs[g], res_b[g], g, dil, row_off, b, s)
            res_c = _mixer_c(cq, ck, proj, tabs_f, tabs_b, ng, res_c, row_off, b, s)
            row_off += b * s
        x, hn = _merge(res_a[0], [r[0] for r in res_b], [r[1] for r in res_b], res_c[1], proj, x,
                       w_oa[l].astype(BF16), w_ob[l].astype(BF16), w_oc[l].astype(BF16),
                       w_out[l].astype(BF16), norm2_g[l][None, :], BF16 if l % 2 == 0 else F32)
        i = l // 2
        if l % 2 == 0:
            x = _ffn(hn, x, ffn_w1[i].astype(BF16), ffn_w3[i].astype(BF16), ffn_w2[i].astype(BF16))
        else:
            x = _moe(hn, x, moe_router[i], moe_w1[i].astype(BF16), moe_w3[i].astype(BF16),
                     moe_w2[i].astype(BF16))
    return x


def kernel(x_prompt, x_sample, norm1_g, w_in, a_qn, a_kn, b_qn, b_kn, ret_dec_f, ret_dec_b, ret_norm_g,
           w_oa, w_ob, w_oc, w_out, norm2_g, ffn_w1, ffn_w3, ffn_w2, moe_router, moe_w1, moe_w3, moe_w2):
    seqs = (x_prompt.shape[:2], x_sample.shape[:2])
    x = jnp.concatenate([x_prompt.reshape(-1, D_MODEL), x_sample.reshape(-1, D_MODEL)], axis=0)
    y = _trunk(x, seqs, norm1_g, w_in, a_qn, a_kn, b_qn, b_kn, ret_dec_f, ret_dec_b, ret_norm_g,
               w_oa, w_ob, w_oc, w_out, norm2_g, ffn_w1, ffn_w3, ffn_w2,
               moe_router, moe_w1, moe_w3, moe_w2)
    n_p = x_prompt.shape[0] * x_prompt.shape[1]
    return (y[:n_p].reshape(x_prompt.shape), y[n_p:].reshape(x_sample.shape))
```

```python
import functools

import numpy as np
import jax
import jax.numpy as jnp
from jax import lax
from jax.experimental import pallas as pl
from jax.experimental.pallas import tpu as pltpu

F32 = jnp.float32
BF16 = jnp.bfloat16

D_MODEL = 1024
GRID_W = 64
HEAD_DIM = 64
ROPE_THETA = 10000.0
EPS = 1e-6
A_HEADS = 8
A_KV_HEADS = 2
B_PATTERNS = ((128, 1), (512, 4), (2048, 16))
B_GROUPS = 3
B_HEADS = 8
C_HEADS = 4
C_KEY_DIM = 64
C_VAL_DIM = 128
N_EXPERTS = 8
N_BRANCHES = 3

A_Q = A_HEADS * HEAD_DIM
A_KV = A_KV_HEADS * HEAD_DIM
B_Q = B_GROUPS * B_HEADS * HEAD_DIM
B_KV = B_HEADS * HEAD_DIM
C_QK = C_HEADS * C_KEY_DIM
C_V = C_HEADS * C_VAL_DIM

LANES = 128
VMEM_LIMIT = 56 * 1024 * 1024

_ORIG_SPLITS = (("aq", A_Q), ("ak", A_KV), ("av", A_KV), ("bq", B_Q), ("bk", B_KV), ("bv", B_KV),
                ("cq", C_QK), ("ck", C_QK), ("cv", C_V), ("cg", C_V), ("gl", N_BRANCHES * D_MODEL))
_NEW_ORDER = ("bq", "bk", "aq", "bv", "cv", "cg", "gl", "cq", "ck", "ak", "av")
PROJ_W = 8192


def _layout():
    orig, o = {}, 0
    for name, w in _ORIG_SPLITS:
        orig[name] = (o, w)
        o += w
    new, o = {}, 0
    for name in _NEW_ORDER:
        new[name] = o
        o += orig[name][1]
    return orig, new, o


_ORIG, OFF, _USED_W = _layout()


def _cparams(sem):
    return pltpu.CompilerParams(dimension_semantics=sem, vmem_limit_bytes=VMEM_LIMIT)


def _seq_call(kern, prev, *, in_specs, out_specs, out_shape, args, **kwargs):
    n_in = len(in_specs)
    if prev is None:
        return pl.pallas_call(kern, in_specs=in_specs, out_specs=out_specs, out_shape=out_shape,
                              **kwargs)(*args)
    n_prev = len(prev)

    def chained(*refs):
        return kern(*refs[:n_in], *refs[n_in + n_prev:])

    return pl.pallas_call(
        chained,
        in_specs=list(in_specs) + [pl.BlockSpec(memory_space=pl.ANY)] * n_prev,
        out_specs=out_specs, out_shape=out_shape,
        input_output_aliases={n_in + k: k for k in range(n_prev)},
        **kwargs)(*args, *prev)


def _inproj_kernel(x_ref, g_ref, w_ref, o_ref, xn_ref):
    @pl.when(pl.program_id(1) == 0)
    def _():
        x = x_ref[...]
        ms = jnp.mean(x * x, axis=-1, keepdims=True)
        xn_ref[...] = (x * lax.rsqrt(ms + EPS) * g_ref[...]).astype(BF16)

    o_ref[...] = jnp.dot(xn_ref[...], w_ref[...], preferred_element_type=F32).astype(BF16)


def _inproj(x, g, w):
    n = x.shape[0]
    tm = min(2048, n)
    tn = 1024
    return pl.pallas_call(
        _inproj_kernel,
        grid=(n // tm, PROJ_W // tn),
        in_specs=[pl.BlockSpec((tm, D_MODEL), lambda i, j: (i, 0)),
                  pl.BlockSpec((1, D_MODEL), lambda i, j: (0, 0)),
                  pl.BlockSpec((D_MODEL, tn), lambda i, j: (0, j))],
        out_specs=pl.BlockSpec((tm, tn), lambda i, j: (i, j)),
        out_shape=jax.ShapeDtypeStruct((n, PROJ_W), BF16),
        scratch_shapes=[pltpu.VMEM((tm, D_MODEL), BF16)],
        compiler_params=_cparams(("parallel", "arbitrary")),
        name="inproj",
    )(x, g, w)


def _norm_rot(x, gain, cos, sin, mmat, half, do_norm):
    if do_norm:
        ms = jnp.dot((x * x).astype(BF16), mmat, preferred_element_type=F32)
        x = x * lax.rsqrt(ms + EPS)
    x = x * gain
    lane = lax.broadcasted_iota(jnp.int32, x.shape, 1)
    first = (lane % (2 * half)) < half
    swapped = jnp.where(first, pltpu.roll(x, LANES - half, 1), pltpu.roll(x, half, 1))
    return x * cos + swapped * sin


def _prep_kernel(bqk_ref, aq_ref, bv_ref, cq_ref, ck_ref, ak_ref,
                 cosb_ref, sinb_ref, cosa_ref, sina_ref, gain_ref, mmat_ref,
                 q0_ref, q1_ref, q2_ref, k0_ref, k1_ref, k2_ref, v0_ref, v1_ref, v2_ref,
                 oaq_ref, oak_ref, ocq_ref, ock_ref, scr_ref):
    cosb, sinb = cosb_ref[...], sinb_ref[...]
    cosa, sina = cosa_ref[...], sina_ref[...]
    mmat = mmat_ref[...]
    gains = gain_ref[...]
    tm = scr_ref.shape[0]

    def prepared(src_ref, src_off, c, gain_row, cos, sin, half, do_norm):
        x = src_ref[:, src_off + c * LANES:src_off + (c + 1) * LANES].astype(F32)
        return _norm_rot(x, gains[gain_row:gain_row + 1, :], cos, sin, mmat, half, do_norm)

    def run(src_ref, dst_ref, width, gain_row, cos, sin, half, do_norm):
        for c in range(width // LANES):
            y = prepared(src_ref, 0, c, gain_row, cos, sin, half, do_norm)
            dst_ref[:, c * LANES:(c + 1) * LANES] = y.astype(dst_ref.dtype)

    def store_dilated(y, dst_ref, dil, c):
        if dil == 1:
            dst_ref[:, c * LANES:(c + 1) * LANES] = y.astype(dst_ref.dtype)
            return
        scr_ref[...] = y
        for r in range(dil):
            rows = scr_ref[pl.ds(r, tm // dil, stride=dil), :]
            dst_ref[:, r * B_KV + c * LANES:r * B_KV + (c + 1) * LANES] = rows.astype(dst_ref.dtype)

    chunks = B_KV // LANES
    for g, (q_ref, (_, dil)) in enumerate(zip((q0_ref, q1_ref, q2_ref), B_PATTERNS)):
        for c in range(chunks):
            y = prepared(bqk_ref, g * B_KV, c, 0, cosb, sinb, HEAD_DIM // 2, True)
            store_dilated(y, q_ref, dil, c)
    for c in range(chunks):
        yk = prepared(bqk_ref, B_Q, c, 1, cosb, sinb, HEAD_DIM // 2, True)
        yv = bv_ref[:, c * LANES:(c + 1) * LANES].astype(F32)
        for (_, dil), k_ref, v_ref in zip(B_PATTERNS, (k0_ref, k1_ref, k2_ref), (v0_ref, v1_ref, v2_ref)):
            store_dilated(yk, k_ref, dil, c)
            store_dilated(yv, v_ref, dil, c)
    run(aq_ref, oaq_ref, A_Q, 2, cosa, sina, HEAD_DIM // 4, True)
    run(ak_ref, oak_ref, A_KV, 3, cosa, sina, HEAD_DIM // 4, True)
    run(cq_ref, ocq_ref, C_QK, 4, cosb, sinb, C_KEY_DIM // 2, False)
    run(ck_ref, ock_ref, C_QK, 5, cosb, sinb, C_KEY_DIM // 2, False)


def _prep(proj, tables, gains, mmat):
    n = proj.shape[0]
    tm = min(512, n)
    cosb, sinb, cosa, sina = tables

    def col(width, name):
        idx = OFF[name] // width
        return pl.BlockSpec((tm, width), lambda i: (i, idx))

    def tab():
        return pl.BlockSpec((tm, LANES), lambda i: (i, 0))

    dils = [dil for _, dil in B_PATTERNS]
    shapes = [(n // dil, dil * B_KV) for dil in dils] * 3 + [(n, A_Q), (n, A_KV), (n, C_QK), (n, C_QK)]
    blocks = [(tm // dil, dil * B_KV) for dil in dils] * 3 + [(tm, A_Q), (tm, A_KV), (tm, C_QK), (tm, C_QK)]
    return pl.pallas_call(
        _prep_kernel,
        grid=(n // tm,),
        in_specs=[col(B_Q + B_KV, "bq"), col(A_Q, "aq"), col(B_KV, "bv"), col(C_QK, "cq"),
                  col(C_QK, "ck"), col(A_KV, "ak"), tab(), tab(), tab(), tab(),
                  pl.BlockSpec((8, LANES), lambda i: (0, 0)),
                  pl.BlockSpec((LANES, LANES), lambda i: (0, 0))],
        out_specs=[pl.BlockSpec(blk, lambda i: (i, 0)) for blk in blocks],
        out_shape=[jax.ShapeDtypeStruct(shp, BF16) for shp in shapes],
        scratch_shapes=[pltpu.VMEM((tm, LANES), F32)],
        compiler_params=_cparams(("parallel",)),
        name="prep",
    )(proj, proj, proj, proj, proj, proj, cosb, sinb, cosa, sina, gains, mmat)


A_ONES_ROWS = 16


def _attn_a_kernel(small_ref, qt_ref, k0_ref, kn_ref, vt_ref, o_ref, qp_ref, m_ref, acc_ref, sa_ref, sb_ref,
                   *, bq, nk):
    ik = pl.program_id(2)
    rep = A_HEADS // A_KV_HEADS
    kv_heads = range(A_KV_HEADS)

    @pl.when(ik == 0)
    def _():
        qp_ref[...] = jnp.zeros(qp_ref.shape, BF16)
        for g in kv_heads:
            for r in range(rep):
                h = g * rep + r
                qp_ref[g, g * HEAD_DIM:(g + 1) * HEAD_DIM, r * bq:(r + 1) * bq] = (
                    qt_ref[h * HEAD_DIM:(h + 1) * HEAD_DIM, :])
        m_ref[...] = jnp.full(m_ref.shape, -jnp.inf, F32)
        acc_ref[...] = jnp.zeros(acc_ref.shape, F32)
        for g in kv_heads:
            sa_ref[g] = jnp.dot(k0_ref[...], qp_ref[g], preferred_element_type=F32)

    def step(cur_ref, nxt_ref, has_next, small):
        kn = kn_ref[...]
        ones = jnp.ones((A_ONES_ROWS, kn.shape[0]), BF16)
        for g in kv_heads:
            v_aug = jnp.concatenate([vt_ref[g * HEAD_DIM:(g + 1) * HEAD_DIM, :], ones], axis=0)
            for r in range(rep):
                cols = slice(r * bq, (r + 1) * bq)
                if has_next:
                    nxt_ref[g, :, cols] = jnp.dot(kn, qp_ref[g, :, cols], preferred_element_type=F32)
                s = cur_ref[g, :, cols]
                if small:
                    pv = jnp.dot(v_aug, jnp.exp2(s).astype(BF16), preferred_element_type=F32)
                    acc_ref[g, :, cols] += pv
                    continue
                m_prev = m_ref[g, :, cols]
                m_new = jnp.maximum(m_prev, jnp.max(s, axis=0, keepdims=True))
                p = jnp.exp2(s - m_new).astype(BF16)
                alpha = jnp.exp2(m_prev - m_new)
                pv = jnp.dot(v_aug, p, preferred_element_type=F32)
                acc_ref[g, :, cols] = alpha * acc_ref[g, :, cols] + pv
                m_ref[g, :, cols] = m_new

    last = ik == nk - 1
    is_small = small_ref[0] != 0
    for parity, cur_ref, nxt_ref in ((0, sa_ref, sb_ref), (1, sb_ref, sa_ref)):
        for has_next in (True, False):
            for small in (True, False):
                @pl.when((ik % 2 == parity) & (last != has_next) & (is_small == small))
                def _(cur_ref=cur_ref, nxt_ref=nxt_ref, has_next=has_next, small=small):
                    step(cur_ref, nxt_ref, has_next, small)

    @pl.when(last)
    def _():
        for g in range(A_KV_HEADS):
            acc = acc_ref[g]
            o = acc[:HEAD_DIM] / acc[HEAD_DIM:HEAD_DIM + 1]
            for r in range(rep):
                h = g * rep + r
                o_ref[h * HEAD_DIM:(h + 1) * HEAD_DIM, :] = o[:, r * bq:(r + 1) * bq].astype(o_ref.dtype)


A_SMALL_SCORE = 60.0


def _a_scores_small(a_qn, a_kn):
    bound = (HEAD_DIM ** 0.5) * np.log2(np.e) * 1.02 * jnp.max(jnp.abs(a_qn)) * jnp.max(jnp.abs(a_kn))
    return (bound < A_SMALL_SCORE).astype(jnp.int32).reshape(1)


def _mixer_a(small, aqt, ak, avt, prev, row_off, b, s):
    n = ak.shape[0]
    bq = min(256, s)
    bk = min(1024, s)
    nq, nk = s // bq, s // bk
    rep = A_HEADS // A_KV_HEADS
    q0, k0 = row_off // bq, row_off // bk
    kern = functools.partial(_attn_a_kernel, bq=bq, nk=nk)
    return _seq_call(
        kern, prev,
        grid=(b, nq, nk),
        in_specs=[pl.BlockSpec(memory_space=pltpu.SMEM),
                  pl.BlockSpec((A_Q, bq), lambda ib, iq, ik: (0, q0 + ib * nq + iq)),
                  pl.BlockSpec((bk, A_KV), lambda ib, iq, ik: (k0 + ib * nk, 0)),
                  pl.BlockSpec((bk, A_KV), lambda ib, iq, ik: (k0 + ib * nk + jnp.minimum(ik + 1, nk - 1), 0)),
                  pl.BlockSpec((A_KV, bk), lambda ib, iq, ik: (0, k0 + ib * nk + ik))],
        out_specs=[pl.BlockSpec((A_Q, bq), lambda ib, iq, ik: (0, q0 + ib * nq + iq))],
        out_shape=[jax.ShapeDtypeStruct((A_Q, n), BF16)],
        scratch_shapes=[pltpu.VMEM((A_KV_HEADS, A_KV, rep * bq), BF16),
                        pltpu.VMEM((A_KV_HEADS, 1, rep * bq), F32),
                        pltpu.VMEM((A_KV_HEADS, HEAD_DIM + A_ONES_ROWS, rep * bq), F32),
                        pltpu.VMEM((A_KV_HEADS, bk, rep * bq), F32),
                        pltpu.VMEM((A_KV_HEADS, bk, rep * bq), F32)],
        compiler_params=_cparams(("parallel", "parallel", "arbitrary")),
        name="mixer_a",
        args=(small, aqt, ak, ak, avt))


B_BLOCK_Q = 128
B_HALO = 64


def _attn_b_kernel(q_ref, k0_ref, k1_ref, k2_ref, k3_ref, v0_ref, v1_ref, v2_ref, v3_ref,
                   o_ref, lse_ref, *, u_len):
    i = pl.program_id(2)
    kcat = jnp.concatenate([k0_ref[...], k1_ref[...], k2_ref[...], k3_ref[...]], axis=0)
    vcat = jnp.concatenate([v0_ref[...], v1_ref[...], v2_ref[...], v3_ref[...]], axis=0)
    nkeys = B_BLOCK_Q + 2 * B_HALO
    a = lax.broadcasted_iota(jnp.int32, (B_BLOCK_Q, nkeys), 0)
    c = lax.broadcasted_iota(jnp.int32, (B_BLOCK_Q, nkeys), 1)
    key_pos = i * B_BLOCK_Q - B_HALO + c
    valid = (c >= a) & (c <= a + 2 * B_HALO) & (key_pos >= 0) & (key_pos < u_len)
    left = lax.broadcasted_iota(jnp.int32, (B_BLOCK_Q, LANES), 1) < HEAD_DIM
    zero = jnp.zeros((B_BLOCK_Q, LANES), BF16)
    heads = [(pair, own) for pair in range(B_HEADS // 2) for own in (left, jnp.logical_not(left))]
    lanes = [slice(pair * LANES, (pair + 1) * LANES) for pair in range(B_HEADS // 2)]
    s = [lax.dot_general(jnp.where(own, q_ref[:, lanes[pair]], zero), kcat[:, lanes[pair]],
                         (((1,), (1,)), ((), ())), preferred_element_type=F32) for pair, own in heads]
    s = [jnp.where(valid, sh, -1e30) for sh in s]
    m = [jnp.max(sh, axis=1, keepdims=True) for sh in s]
    e = [jnp.exp(sh - mh) for sh, mh in zip(s, m)]
    den = [jnp.sum(eh, axis=1, keepdims=True) for eh in e]
    pv = [jnp.dot(eh.astype(BF16), vcat[:, lanes[pair]], preferred_element_type=F32)
          for eh, (pair, _) in zip(e, heads)]
    outs = [pvh / dh for pvh, dh in zip(pv, den)]
    lses = [mh + jnp.log(dh) for mh, dh in zip(m, den)]
    for pair in range(B_HEADS // 2):
        o_ref[:, lanes[pair]] = jnp.where(left, outs[2 * pair], outs[2 * pair + 1]).astype(o_ref.dtype)
        lse_ref[:, lanes[pair]] = jnp.where(left, lses[2 * pair], lses[2 * pair + 1])


def _mixer_b_group(qd, kd, vd, prev, g, dil, row_off, b, s):
    u_len = s // dil
    nq = u_len // B_BLOCK_Q
    nkb = u_len // B_HALO
    q_rows0 = row_off // dil // B_BLOCK_Q
    k_rows0 = row_off // dil // B_HALO

    def kspec(t):
        def imap(ib, r, i):
            blk = jnp.clip(2 * i - 1 + t, 0, nkb - 1)
            return (k_rows0 + ib * nkb + blk, r)
        return pl.BlockSpec((B_HALO, B_KV), imap)

    def qspec():
        return pl.BlockSpec((B_BLOCK_Q, B_KV), lambda ib, r, i: (q_rows0 + ib * nq + i, r))

    kern = functools.partial(_attn_b_kernel, u_len=u_len)
    return _seq_call(
        kern, prev,
        grid=(b, dil, nq),
        in_specs=[qspec()] + [kspec(t) for t in range(4)] + [kspec(t) for t in range(4)],
        out_specs=[qspec(), qspec()],
        out_shape=[jax.ShapeDtypeStruct(qd.shape, BF16), jax.ShapeDtypeStruct(qd.shape, F32)],
        compiler_params=_cparams(("parallel", "parallel", "parallel")),
        name=f"mixer_b{g}",
        args=(qd, kd, kd, kd, kd, vd, vd, vd, vd))


def _ret_chunk(q, k, v, dec_ref, xi_ref, zeta_ref, cdec_ref, r_ref):
    outs = []
    for h in range(C_HEADS):
        qh = q[:, h * C_KEY_DIM:(h + 1) * C_KEY_DIM]
        kh = k[:, h * C_KEY_DIM:(h + 1) * C_KEY_DIM]
        vh = v[:, h * C_VAL_DIM:(h + 1) * C_VAL_DIM]
        att = lax.dot_general(qh, kh, (((1,), (1,)), ((), ())), preferred_element_type=F32) * dec_ref[h]
        inner = jnp.dot(att.astype(BF16), vh, preferred_element_type=F32)
        r = r_ref[h]
        cross = jnp.dot(qh, r.astype(BF16), preferred_element_type=F32) * xi_ref[h]
        kz = (kh.astype(F32) * zeta_ref[h]).astype(BF16)
        r_ref[h] = r * cdec_ref[h] + lax.dot_general(kz, vh, (((0,), (0,)), ((), ())),
                                                     preferred_element_type=F32)
        outs.append(inner + cross)
    return jnp.concatenate(outs, axis=1)


def _ret_fwd_kernel(q_ref, k_ref, v_ref, dec_ref, xi_ref, zeta_ref, cdec_ref, o_ref, r_ref):
    @pl.when(pl.program_id(1) == 0)
    def _():
        r_ref[...] = jnp.zeros(r_ref.shape, F32)

    o_ref[...] = _ret_chunk(q_ref[...], k_ref[...], v_ref[...], dec_ref, xi_ref, zeta_ref, cdec_ref, r_ref)


def _ret_bwd_kernel(q_ref, k_ref, v_ref, dec_ref, xi_ref, zeta_ref, cdec_ref, of_ref, gate_ref, ng_ref,
                    o_ref, r_ref):
    @pl.when(pl.program_id(1) == 0)
    def _():
        r_ref[...] = jnp.zeros(r_ref.shape, F32)

    o = of_ref[...] + _ret_chunk(q_ref[...], k_ref[...], v_ref[...], dec_ref, xi_ref, zeta_ref, cdec_ref, r_ref)
    ng = ng_ref[...]
    gate = gate_ref[...].astype(F32)
    for h in range(C_HEADS):
        sl = slice(h * C_VAL_DIM, (h + 1) * C_VAL_DIM)
        oh = o[:, sl]
        mu = jnp.mean(oh, axis=1, keepdims=True)
        var = jnp.mean(jnp.square(oh - mu), axis=1, keepdims=True)
        y = (oh - mu) * lax.rsqrt(var + EPS) * ng[:, sl]
        gh = gate[:, sl]
        o_ref[:, sl] = (gh * jax.nn.sigmoid(gh) * y).astype(o_ref.dtype)


def _ret_tables(dec_param, chunk, strict):
    log_g = -jnp.exp(dec_param.astype(F32))
    j = jnp.arange(chunk, dtype=F32)
    lg = log_g[:, None, None]
    if strict:
        diff = j[None, :] - j[:, None]
        dec = jnp.where((diff > 0)[None], jnp.exp(jnp.maximum(diff, 0.0)[None] * lg), 0.0)
        xi = jnp.exp((chunk - j)[None, :, None] * lg)
        zeta = jnp.exp(j[None, :, None] * lg)
    else:
        diff = j[:, None] - j[None, :]
        dec = jnp.where((diff >= 0)[None], jnp.exp(jnp.maximum(diff, 0.0)[None] * lg), 0.0)
        xi = jnp.exp((j + 1.0)[None, :, None] * lg)
        zeta = jnp.exp((chunk - 1.0 - j)[None, :, None] * lg)
    xi = jnp.broadcast_to(xi, (C_HEADS, chunk, C_VAL_DIM))
    zeta = jnp.broadcast_to(zeta, (C_HEADS, chunk, C_KEY_DIM))
    cdec = jnp.broadcast_to(jnp.exp(chunk * log_g)[:, None, None], (C_HEADS, C_KEY_DIM, C_VAL_DIM))
    return dec, xi, zeta, cdec


C_CHUNK = 512


def _mixer_c(cq, ck, proj, tabs_f, tabs_b, norm_g, prev, row_off, b, s):
    n = cq.shape[0]
    prev_f, prev_o = (None, None) if prev is None else ([prev[0]], [prev[1]])
    nc = s // C_CHUNK
    r0 = row_off // C_CHUNK
    v_col = OFF["cv"] // C_V
    g_col = OFF["cg"] // C_V

    def fwd_rows(ib, c):
        return r0 + ib * nc + c

    def bwd_rows(ib, c):
        return r0 + ib * nc + (nc - 1 - c)

    def specs(rows):
        def full(shape):
            return pl.BlockSpec(shape, lambda ib, c: (0,) * len(shape))
        return [pl.BlockSpec((C_CHUNK, C_QK), lambda ib, c: (rows(ib, c), 0)),
                pl.BlockSpec((C_CHUNK, C_QK), lambda ib, c: (rows(ib, c), 0)),
                pl.BlockSpec((C_CHUNK, C_V), lambda ib, c: (rows(ib, c), v_col)),
                full((C_HEADS, C_CHUNK, C_CHUNK)), full((C_HEADS, C_CHUNK, C_VAL_DIM)),
                full((C_HEADS, C_CHUNK, C_KEY_DIM)), full((C_HEADS, C_KEY_DIM, C_VAL_DIM))]

    scratch = [pltpu.VMEM((C_HEADS, C_KEY_DIM, C_VAL_DIM), F32)]
    o_f, = _seq_call(
        _ret_fwd_kernel, prev_f,
        grid=(b, nc),
        in_specs=specs(fwd_rows),
        out_specs=[pl.BlockSpec((C_CHUNK, C_V), lambda ib, c: (fwd_rows(ib, c), 0))],
        out_shape=[jax.ShapeDtypeStruct((n, C_V), F32)],
        scratch_shapes=scratch,
        compiler_params=_cparams(("parallel", "arbitrary")),
        name="ret_fwd",
        args=(cq, ck, proj, *tabs_f))
    o_c, = _seq_call(
        _ret_bwd_kernel, prev_o,
        grid=(b, nc),
        in_specs=specs(bwd_rows) + [
            pl.BlockSpec((C_CHUNK, C_V), lambda ib, c: (bwd_rows(ib, c), 0)),
            pl.BlockSpec((C_CHUNK, C_V), lambda ib, c: (bwd_rows(ib, c), g_col)),
            pl.BlockSpec((1, C_V), lambda ib, c: (0, 0))],
        out_specs=[pl.BlockSpec((C_CHUNK, C_V), lambda ib, c: (bwd_rows(ib, c), 0))],
        out_shape=[jax.ShapeDtypeStruct((n, C_V), BF16)],
        scratch_shapes=scratch,
        compiler_params=_cparams(("parallel", "arbitrary")),
        name="ret_bwd",
        args=(cq, ck, proj, *tabs_b, o_f, proj, norm_g))
    return [o_f, o_c]


def _merge_kernel(oa_ref, ob0_ref, ob1_ref, ob2_ref, l0_ref, l1_ref, l2_ref, oc_ref,
                  ga_ref, gb_ref, gc_ref, x_ref, woa_ref, wob_ref, woc_ref, wout_ref, n2_ref,
                  xo_ref, hn_ref, scr_ref):
    tm = x_ref.shape[0]

    def natural(src_ref, dil, slot):
        if dil == 1:
            return src_ref[...].astype(F32)
        chunks = B_KV // LANES
        for r in range(dil):
            for c in range(chunks):
                col = r * B_KV + c * LANES
                scr_ref[slot, c, pl.ds(r, tm // dil, stride=dil), :] = src_ref[:, col:col + LANES].astype(F32)
        return jnp.concatenate([scr_ref[slot, c] for c in range(chunks)], axis=1)

    dils = [dil for _, dil in B_PATTERNS]
    l0, l1, l2 = [natural(ref, dil, 2 * g) for g, (ref, dil) in enumerate(zip((l0_ref, l1_ref, l2_ref), dils))]
    o0, o1, o2 = [natural(ref, dil, 2 * g + 1)
                  for g, (ref, dil) in enumerate(zip((ob0_ref, ob1_ref, ob2_ref), dils))]
    mx = jnp.maximum(jnp.maximum(l0, l1), l2)
    e0, e1, e2 = jnp.exp(l0 - mx), jnp.exp(l1 - mx), jnp.exp(l2 - mx)
    ob = (e0 * o0 + e1 * o1 + e2 * o2) / (e0 + e1 + e2)

    def branch(o, w_ref, gate_ref):
        y = jnp.dot(o, w_ref[...], preferred_element_type=F32)
        return jax.nn.sigmoid(gate_ref[...].astype(F32)) * y

    oa = jnp.transpose(oa_ref[...].astype(F32)).astype(BF16)
    merged = (branch(oa, woa_ref, ga_ref) + branch(ob.astype(BF16), wob_ref, gb_ref)
              + branch(oc_ref[...], woc_ref, gc_ref))
    x = x_ref[...] + jnp.dot(merged.astype(BF16), wout_ref[...], preferred_element_type=F32)
    xo_ref[...] = x
    ms = jnp.mean(x * x, axis=-1, keepdims=True)
    hn_ref[...] = (x * lax.rsqrt(ms + EPS) * n2_ref[...]).astype(hn_ref.dtype)


def _merge(o_a, o_b, lse_b, o_c, proj, x, w_oa, w_ob, w_oc, w_out, n2, hn_dtype):
    n = x.shape[0]
    tm = min(512, n)
    gl0 = OFF["gl"] // D_MODEL

    def rows(width):
        return pl.BlockSpec((tm, width), lambda i: (i, 0))

    def gate(k):
        return pl.BlockSpec((tm, D_MODEL), lambda i: (i, gl0 + k))

    def full(r, c):
        return pl.BlockSpec((r, c), lambda i: (0, 0))

    grouped = [pl.BlockSpec((tm // dil, dil * B_KV), lambda i: (i, 0)) for _, dil in B_PATTERNS]
    return pl.pallas_call(
        _merge_kernel,
        grid=(n // tm,),
        in_specs=[pl.BlockSpec((A_Q, tm), lambda i: (0, i))] + grouped + grouped
                 + [rows(C_V), gate(0), gate(1), gate(2), rows(D_MODEL),
                    full(A_Q, D_MODEL), full(B_KV, D_MODEL), full(C_V, D_MODEL), full(D_MODEL, D_MODEL),
                    full(1, D_MODEL)],
        out_specs=[rows(D_MODEL), rows(D_MODEL)],
        out_shape=[jax.ShapeDtypeStruct((n, D_MODEL), F32), jax.ShapeDtypeStruct((n, D_MODEL), hn_dtype)],
        scratch_shapes=[pltpu.VMEM((2 * B_GROUPS, B_KV // LANES, tm, LANES), F32)],
        compiler_params=_cparams(("parallel",)),
        name="merge_out",
    )(o_a, *o_b, *lse_b, o_c, proj, proj, proj, x, w_oa, w_ob, w_oc, w_out, n2)


def _ffn_kernel(hn_ref, x_ref, w1_ref, w3_ref, w2_ref, o_ref):
    j = pl.program_id(1)
    hn = hn_ref[...]
    a = jnp.dot(hn, w1_ref[...], preferred_element_type=F32)
    g = jnp.dot(hn, w3_ref[...], preferred_element_type=F32)
    h = (a * jax.nn.sigmoid(a) * g).astype(BF16)
    y = jnp.dot(h, w2_ref[...], preferred_element_type=F32)

    @pl.when(j == 0)
    def _():
        o_ref[...] = x_ref[...] + y

    @pl.when(j > 0)
    def _():
        o_ref[...] += y


def _ffn(hn, x, w1, w3, w2):
    n = x.shape[0]
    d_ff = w1.shape[1]
    tm = min(1024, n)
    tf = d_ff // 2
    nf = d_ff // tf
    return pl.pallas_call(
        _ffn_kernel,
        grid=(n // tm, nf),
        in_specs=[pl.BlockSpec((tm, D_MODEL), lambda i, j: (i, 0)),
                  pl.BlockSpec((tm, D_MODEL), lambda i, j: (i, 0)),
                  pl.BlockSpec((D_MODEL, tf), lambda i, j: (0, j)),
                  pl.BlockSpec((D_MODEL, tf), lambda i, j: (0, j)),
                  pl.BlockSpec((tf, D_MODEL), lambda i, j: (j, 0))],
        out_specs=pl.BlockSpec((tm, D_MODEL), lambda i, j: (i, 0)),
        out_shape=jax.ShapeDtypeStruct((n, D_MODEL), F32),
        compiler_params=_cparams(("parallel", "arbitrary")),
        name="ffn",
    )(hn, x, w1, w3, w2)


TOP_K = 2
MOE_TM = 512


def _router_kernel(hn_ref, wr_ref, idx_ref, wts_ref):
    logits = jnp.dot(hn_ref[...], wr_ref[...], preferred_element_type=F32,
                     precision=lax.Precision.HIGHEST)
    col = lax.broadcasted_iota(jnp.int32, logits.shape, 1)
    m1 = jnp.max(logits, axis=1, keepdims=True)
    i1 = jnp.min(jnp.where(logits == m1, col, N_EXPERTS), axis=1, keepdims=True)
    rest = jnp.where(col == i1, -jnp.inf, logits)
    m2 = jnp.max(rest, axis=1, keepdims=True)
    i2 = jnp.min(jnp.where(rest == m2, col, N_EXPERTS), axis=1, keepdims=True)
    e2 = jnp.exp(m2 - m1)
    w1 = 1.0 / (1.0 + e2)
    idx_ref[...] = jnp.concatenate([i1, i2], axis=1)
    wts_ref[...] = jnp.concatenate([w1, e2 * w1], axis=1)


def _router(hn, wr):
    n = hn.shape[0]
    tm = min(1024, n)
    return pl.pallas_call(
        _router_kernel,
        grid=(n // tm,),
        in_specs=[pl.BlockSpec((tm, D_MODEL), lambda i: (i, 0)),
                  pl.BlockSpec((D_MODEL, N_EXPERTS), lambda i: (0, 0))],
        out_specs=[pl.BlockSpec((tm, TOP_K), lambda i: (i, 0)), pl.BlockSpec((tm, TOP_K), lambda i: (i, 0))],
        out_shape=[jax.ShapeDtypeStruct((n, TOP_K), jnp.int32), jax.ShapeDtypeStruct((n, TOP_K), F32)],
        compiler_params=_cparams(("parallel",)),
        name="router",
    )(hn, wr)


def _route(idx, tm):
    n = idx.shape[0]
    e_flat = idx.reshape(-1)
    onehot = (e_flat[:, None] == jnp.arange(N_EXPERTS, dtype=jnp.int32)[None, :]).astype(jnp.int32)
    csum = jnp.cumsum(onehot, axis=0)
    rank = jnp.sum((csum - onehot) * onehot, axis=1)
    gsz = ((csum[-1] + tm - 1) // tm) * tm
    gend = jnp.cumsum(gsz)
    pos = (gend - gsz)[e_flat] + rank
    n_rows = n * TOP_K + N_EXPERTS * tm
    n_tiles = n_rows // tm
    row_token = jnp.zeros((n_rows,), jnp.int32).at[pos].set(jnp.arange(n * TOP_K, dtype=jnp.int32) // TOP_K)
    tile_start = jnp.arange(n_tiles, dtype=jnp.int32) * tm
    tile_expert = jnp.minimum(jnp.sum((tile_start[:, None] >= gend[None, :]).astype(jnp.int32), axis=1),
                              N_EXPERTS - 1)
    n_used = (gend[-1] // tm).astype(jnp.int32).reshape(1)
    return pos.reshape(n, TOP_K), row_token.reshape(n_tiles, 1, tm), tile_expert, n_used


def _row_copy(src_ref, src_row, dst_ref, dst_row, sem):
    return pltpu.make_async_copy(src_ref.at[pl.ds(src_row, 1)], dst_ref.at[pl.ds(dst_row, 1)], sem)


def _expert_kernel(te_ref, nu_ref, idx_ref, nxt_ref, src_ref, w1_ref, w3_ref, w2_ref, y_ref,
                   xs_ref, xb_ref, acc_ref, sem, *, nf, tm):
    i = pl.program_id(0)
    j = pl.program_id(1)
    n_used = nu_ref[0]
    active = i < n_used
    slot = i % 2
    share = tm // nf

    @pl.when((i == 0) & (j == 0))
    def _():
        def issue(r, carry):
            _row_copy(src_ref, idx_ref[0, 0, r], xs_ref.at[0], r, sem.at[0]).start()
            return carry
        lax.fori_loop(0, tm, issue, 0, unroll=8)

    @pl.when((j == 0) & (i <= n_used))
    def _():
        pltpu.make_async_copy(src_ref.at[pl.ds(0, tm)], xs_ref.at[slot], sem.at[slot]).wait()

    @pl.when(active)
    def _():
        @pl.when(j == 0)
        def _():
            xb_ref[...] = xs_ref[slot].astype(BF16)

        for r in range(share):
            row = j * share + r
            _row_copy(src_ref, nxt_ref[0, 0, row], xs_ref.at[1 - slot], row, sem.at[1 - slot]).start()

        xb = xb_ref[...]
        a = jnp.dot(xb, w1_ref[0], preferred_element_type=F32)
        g = jnp.dot(xb, w3_ref[0], preferred_element_type=F32)
        h = (a * jax.nn.sigmoid(a) * g).astype(BF16)
        y = jnp.dot(h, w2_ref[0], preferred_element_type=F32)

        @pl.when(j == 0)
        def _():
            acc_ref[...] = y

        @pl.when(j > 0)
        def _():
            acc_ref[...] += y

        @pl.when(j == nf - 1)
        def _():
            y_ref[...] = acc_ref[...]

    @pl.when(jnp.logical_not(active) & (j == nf - 1))
    def _():
        y_ref[...] = jnp.zeros(y_ref.shape, y_ref.dtype)


def _experts(src, row_token, tile_expert, n_used, w1, w3, w2):
    nt, _, tm = row_token.shape
    d_ff = w1.shape[2]
    nf = 2
    tf = d_ff // nf

    def jcol(i, j, nu):
        return jnp.where(i < nu[0], j, nf - 1)

    grid_spec = pltpu.PrefetchScalarGridSpec(
        num_scalar_prefetch=2,
        grid=(nt, nf),
        in_specs=[pl.BlockSpec((1, 1, tm), lambda i, j, te, nu: (i, 0, 0), memory_space=pltpu.SMEM),
                  pl.BlockSpec((1, 1, tm), lambda i, j, te, nu: (jnp.minimum(i + 1, nt - 1), 0, 0),
                               memory_space=pltpu.SMEM),
                  pl.BlockSpec(memory_space=pl.ANY),
                  pl.BlockSpec((1, D_MODEL, tf), lambda i, j, te, nu: (te[i], 0, jcol(i, j, nu))),
                  pl.BlockSpec((1, D_MODEL, tf), lambda i, j, te, nu: (te[i], 0, jcol(i, j, nu))),
                  pl.BlockSpec((1, tf, D_MODEL), lambda i, j, te, nu: (te[i], jcol(i, j, nu), 0))],
        out_specs=pl.BlockSpec((tm, D_MODEL), lambda i, j, te, nu: (i, 0)),
        scratch_shapes=[pltpu.VMEM((2, tm, D_MODEL), src.dtype), pltpu.VMEM((tm, D_MODEL), BF16),
                        pltpu.VMEM((tm, D_MODEL), F32), pltpu.SemaphoreType.DMA((2,))],
    )
    return pl.pallas_call(
        functools.partial(_expert_kernel, nf=nf, tm=tm),
        grid_spec=grid_spec,
        out_shape=jax.ShapeDtypeStruct((nt * tm, D_MODEL), F32),
        compiler_params=_cparams(("arbitrary", "arbitrary")),
        name="moe_experts",
    )(tile_expert, n_used, row_token, row_token, src, w1, w3, w2)


def _combine_kernel(pos_ref, nxt_ref, wts_ref, x_ref, y_ref, o_ref, g_ref, sem, *, tm, nt):
    i = pl.program_id(0)
    slot = i % 2

    def request(p_ref, dst_slot):
        def issue(r, carry):
            for k in range(TOP_K):
                _row_copy(y_ref, p_ref[0, 0, TOP_K * r + k], g_ref.at[dst_slot, k], r, sem.at[dst_slot]).start()
            return carry
        lax.fori_loop(0, tm, issue, 0, unroll=8)

    @pl.when(i == 0)
    def _():
        request(pos_ref, 0)

    @pl.when(i + 1 < nt)
    def _():
        request(nxt_ref, 1 - slot)

    for k in range(TOP_K):
        pltpu.make_async_copy(y_ref.at[pl.ds(0, tm)], g_ref.at[slot, k], sem.at[slot]).wait()
    w = wts_ref[...]
    o_ref[...] = x_ref[...] + w[:, 0:1] * g_ref[slot, 0] + w[:, 1:2] * g_ref[slot, 1]


def _combine(pos, wts, x, y):
    n = x.shape[0]
    tm = min(256, n)
    nt = n // tm
    pos3 = pos.reshape(nt, 1, TOP_K * tm)
    return pl.pallas_call(
        functools.partial(_combine_kernel, tm=tm, nt=nt),
        grid=(nt,),
        in_specs=[pl.BlockSpec((1, 1, TOP_K * tm), lambda i: (i, 0, 0), memory_space=pltpu.SMEM),
                  pl.BlockSpec((1, 1, TOP_K * tm), lambda i: (jnp.minimum(i + 1, nt - 1), 0, 0),
                               memory_space=pltpu.SMEM),
                  pl.BlockSpec((tm, TOP_K), lambda i: (i, 0)),
                  pl.BlockSpec((tm, D_MODEL), lambda i: (i, 0)),
                  pl.BlockSpec(memory_space=pl.ANY)],
        out_specs=pl.BlockSpec((tm, D_MODEL), lambda i: (i, 0)),
        out_shape=jax.ShapeDtypeStruct((n, D_MODEL), F32),
        scratch_shapes=[pltpu.VMEM((2, TOP_K, tm, D_MODEL), F32), pltpu.SemaphoreType.DMA((2,))],
        compiler_params=_cparams(("arbitrary",)),
        name="moe_combine",
    )(pos3, pos3, wts, x, y)


def _moe(hn, x, wr, w1, w3, w2):
    idx, wts = _router(hn, wr)
    pos, row_token, tile_expert, n_used = _route(idx, MOE_TM)
    y = _experts(hn, row_token, tile_expert, n_used, w1, w3, w2)
    return _combine(pos, wts, x, y)


def _permute_w_in(w):
    cols = [w[:, _ORIG[name][0]:_ORIG[name][0] + _ORIG[name][1]] for name in _NEW_ORDER]
    cols.append(jnp.zeros((w.shape[0], PROJ_W - _USED_W), w.dtype))
    return jnp.concatenate(cols, axis=1).astype(BF16)


def _angles(pos, dim):
    inv = ROPE_THETA ** (-jnp.arange(0, dim, 2, dtype=F32) / dim)
    return pos.astype(F32)[:, None] * inv[None, :]


def _rope_tables(seqs):
    pos = jnp.concatenate([jnp.tile(jnp.arange(s), b) for b, s in seqs])
    ang = _angles(pos, HEAD_DIM)
    cosb = jnp.tile(jnp.cos(ang), (1, 4))
    sinb = jnp.tile(jnp.concatenate([-jnp.sin(ang), jnp.sin(ang)], axis=1), (1, 2))
    ar = _angles(pos // GRID_W, HEAD_DIM // 2)
    ac = _angles(pos % GRID_W, HEAD_DIM // 2)
    cosa = jnp.tile(jnp.concatenate([jnp.cos(ar), jnp.cos(ar), jnp.cos(ac), jnp.cos(ac)], axis=1), (1, 2))
    sina = jnp.tile(jnp.concatenate([-jnp.sin(ar), jnp.sin(ar), -jnp.sin(ac), jnp.sin(ac)], axis=1), (1, 2))
    return cosb, sinb, cosa, sina


def _head_mean_matrix():
    blk = np.kron(np.eye(LANES // HEAD_DIM), np.ones((HEAD_DIM, HEAD_DIM))) / HEAD_DIM
    return jnp.asarray(blk, BF16)


def _gain_rows(a_qn, a_kn, b_qn, b_kn):
    scale = HEAD_DIM ** -0.5
    scale_a = scale * np.log2(np.e)
    rows = [jnp.tile(b_qn, 2) * scale, jnp.tile(b_kn, 2), jnp.tile(a_qn, 2) * scale_a, jnp.tile(a_kn, 2),
            jnp.ones((LANES,), F32), jnp.full((LANES,), C_KEY_DIM ** -0.5, F32),
            jnp.ones((LANES,), F32), jnp.ones((LANES,), F32)]
    return jnp.stack(rows).astype(F32)


def _trunk(x, seqs, norm1_g, w_in, a_qn, a_kn, b_qn, b_kn, ret_dec_f, ret_dec_b, ret_norm_g,
           w_oa, w_ob, w_oc, w_out, norm2_g, ffn_w1, ffn_w3, ffn_w2,
           moe_router, moe_w1, moe_w3, moe_w2):
    depth = w_in.shape[0]
    tables = _rope_tables(seqs)
    mmat = _head_mean_matrix()
    for l in range(depth):
        proj = _inproj(x, norm1_g[l][None, :], _permute_w_in(w_in[l]))
        prepped = _prep(proj, tables, _gain_rows(a_qn[l], a_kn[l], b_qn[l], b_kn[l]), mmat)
        bqs, bks, bvs = prepped[0:3], prepped[3:6], prepped[6:9]
        aq, ak, cq, ck = prepped[9:]
        aqt = aq.T
        avt = proj[:, OFF["av"]:OFF["av"] + A_KV].T
        tabs_f = _ret_tables(ret_dec_f[l], C_CHUNK, False)
        tabs_b = _ret_tables(ret_dec_b[l], C_CHUNK, True)
        ng = ret_norm_g[l][None, :].astype(F32)
        res_a, res_c = None, None
        res_b = [None] * B_GROUPS
        row_off = 0
        for b, s in seqs:
            res_a = _mixer_a(_a_scores_small(a_qn[l], a_kn[l]), aqt, ak, avt, res_a, row_off, b, s)
            for g, (_, dil) in enumerate(B_PATTERNS):
                res_b[g] = _mixer_b_group(bqs[g], bks[g], bvs[g], res_b[g], g, dil, row_off, b, s)
            res_c = _mixer_c(cq, ck, proj, tabs_f, tabs_b, ng, res_c, row_off, b, s)
            row_off += b * s
        x, hn = _merge(res_a[0], [r[0] for r in res_b], [r[1] for r in res_b], res_c[1], proj, x,
                       w_oa[l].astype(BF16), w_ob[l].astype(BF16), w_oc[l].astype(BF16),
                       w_out[l].astype(BF16), norm2_g[l][None, :], BF16 if l % 2 == 0 else F32)
        i = l // 2
        if l % 2 == 0:
            x = _ffn(hn, x, ffn_w1[i].astype(BF16), ffn_w3[i].astype(BF16), ffn_w2[i].astype(BF16))
        else:
            x = _moe(hn, x, moe_router[i], moe_w1[i].astype(BF16), moe_w3[i].astype(BF16),
                     moe_w2[i].astype(BF16))
    return x


def kernel(x_prompt, x_sample, norm1_g, w_in, a_qn, a_kn, b_qn, b_kn, ret_dec_f, ret_dec_b, ret_norm_g,
           w_oa, w_ob, w_oc, w_out, norm2_g, ffn_w1, ffn_w3, ffn_w2, moe_router, moe_w1, moe_w3, moe_w2):
    seqs = (x_prompt.shape[:2], x_sample.shape[:2])
    x = jnp.concatenate([x_prompt.reshape(-1, D_MODEL), x_sample.reshape(-1, D_MODEL)], axis=0)
    y = _trunk(x, seqs, norm1_g, w_in, a_qn, a_kn, b_qn, b_kn, ret_dec_f, ret_dec_b, ret_norm_g,
               w_oa, w_ob, w_oc, w_out, norm2_g, ffn_w1, ffn_w3, ffn_w2,
               moe_router, moe_w1, moe_w3, moe_w2)
    n_p = x_prompt.shape[0] * x_prompt.shape[1]
    return (y[:n_p].reshape(x_prompt.shape), y[n_p:].reshape(x_sample.shape))
```

```python
import functools

import numpy as np
import jax
import jax.numpy as jnp
from jax import lax
from jax.experimental import pallas as pl
from jax.experimental.pallas import tpu as pltpu

F32 = jnp.float32
BF16 = jnp.bfloat16

D_MODEL = 1024
GRID_W = 64
HEAD_DIM = 64
ROPE_THETA = 10000.0
EPS = 1e-6
A_HEADS = 8
A_KV_HEADS = 2
B_PATTERNS = ((128, 1), (512, 4), (2048, 16))
B_GROUPS = 3
B_HEADS = 8
C_HEADS = 4
C_KEY_DIM = 64
C_VAL_DIM = 128
N_EXPERTS = 8
N_BRANCHES = 3

A_Q = A_HEADS * HEAD_DIM
A_KV = A_KV_HEADS * HEAD_DIM
B_Q = B_GROUPS * B_HEADS * HEAD_DIM
B_KV = B_HEADS * HEAD_DIM
C_QK = C_HEADS * C_KEY_DIM
C_V = C_HEADS * C_VAL_DIM

LANES = 128
VMEM_LIMIT = 56 * 1024 * 1024

_ORIG_SPLITS = (("aq", A_Q), ("ak", A_KV), ("av", A_KV), ("bq", B_Q), ("bk", B_KV), ("bv", B_KV),
                ("cq", C_QK), ("ck", C_QK), ("cv", C_V), ("cg", C_V), ("gl", N_BRANCHES * D_MODEL))
_NEW_ORDER = ("bq", "bk", "aq", "bv", "cv", "cg", "gl", "cq", "ck", "ak", "av")
PROJ_W = 8192


def _layout():
    orig, o = {}, 0
    for name, w in _ORIG_SPLITS:
        orig[name] = (o, w)
        o += w
    new, o = {}, 0
    for name in _NEW_ORDER:
        new[name] = o
        o += orig[name][1]
    return orig, new, o


_ORIG, OFF, _USED_W = _layout()


def _cparams(sem):
    return pltpu.CompilerParams(dimension_semantics=sem, vmem_limit_bytes=VMEM_LIMIT)


def _seq_call(kern, prev, *, in_specs, out_specs, out_shape, args, **kwargs):
    n_in = len(in_specs)
    if prev is None:
        return pl.pallas_call(kern, in_specs=in_specs, out_specs=out_specs, out_shape=out_shape,
                              **kwargs)(*args)
    n_prev = len(prev)

    def chained(*refs):
        return kern(*refs[:n_in], *refs[n_in + n_prev:])

    return pl.pallas_call(
        chained,
        in_specs=list(in_specs) + [pl.BlockSpec(memory_space=pl.ANY)] * n_prev,
        out_specs=out_specs, out_shape=out_shape,
        input_output_aliases={n_in + k: k for k in range(n_prev)},
        **kwargs)(*args, *prev)


def _inproj_kernel(x_ref, g_ref, w_ref, o_ref, xn_ref):
    @pl.when(pl.program_id(1) == 0)
    def _():
        x = x_ref[...]
        ms = jnp.mean(x * x, axis=-1, keepdims=True)
        xn_ref[...] = (x * lax.rsqrt(ms + EPS) * g_ref[...]).astype(BF16)

    o_ref[...] = jnp.dot(xn_ref[...], w_ref[...], preferred_element_type=F32).astype(BF16)


def _inproj(x, g, w):
    n = x.shape[0]
    tm = min(2048, n)
    tn = 1024
    return pl.pallas_call(
        _inproj_kernel,
        grid=(n // tm, PROJ_W // tn),
        in_specs=[pl.BlockSpec((tm, D_MODEL), lambda i, j: (i, 0)),
                  pl.BlockSpec((1, D_MODEL), lambda i, j: (0, 0)),
                  pl.BlockSpec((D_MODEL, tn), lambda i, j: (0, j))],
        out_specs=pl.BlockSpec((tm, tn), lambda i, j: (i, j)),
        out_shape=jax.ShapeDtypeStruct((n, PROJ_W), BF16),
        scratch_shapes=[pltpu.VMEM((tm, D_MODEL), BF16)],
        compiler_params=_cparams(("parallel", "arbitrary")),
        name="inproj",
    )(x, g, w)


def _norm_rot(x, gain, cos, sin, mmat, half, do_norm):
    if do_norm:
        ms = jnp.dot((x * x).astype(BF16), mmat, preferred_element_type=F32)
        x = x * lax.rsqrt(ms + EPS)
    x = x * gain
    lane = lax.broadcasted_iota(jnp.int32, x.shape, 1)
    first = (lane % (2 * half)) < half
    swapped = jnp.where(first, pltpu.roll(x, LANES - half, 1), pltpu.roll(x, half, 1))
    return x * cos + swapped * sin


def _prep_kernel(bqk_ref, aq_ref, bv_ref, cq_ref, ck_ref, ak_ref,
                 cosb_ref, sinb_ref, cosa_ref, sina_ref, gain_ref, mmat_ref,
                 q0_ref, q1_ref, q2_ref, k0_ref, k1_ref, k2_ref, v0_ref, v1_ref, v2_ref,
                 oaq_ref, oak_ref, ocq_ref, ock_ref, scr_ref):
    cosb, sinb = cosb_ref[...], sinb_ref[...]
    cosa, sina = cosa_ref[...], sina_ref[...]
    mmat = mmat_ref[...]
    gains = gain_ref[...]
    tm = scr_ref.shape[0]

    def prepared(src_ref, src_off, c, gain_row, cos, sin, half, do_norm):
        x = src_ref[:, src_off + c * LANES:src_off + (c + 1) * LANES].astype(F32)
        return _norm_rot(x, gains[gain_row:gain_row + 1, :], cos, sin, mmat, half, do_norm)

    def run(src_ref, dst_ref, width, gain_row, cos, sin, half, do_norm):
        for c in range(width // LANES):
            y = prepared(src_ref, 0, c, gain_row, cos, sin, half, do_norm)
            dst_ref[:, c * LANES:(c + 1) * LANES] = y.astype(dst_ref.dtype)

    def store_dilated(y, dst_ref, dil, c):
        if dil == 1:
            dst_ref[:, c * LANES:(c + 1) * LANES] = y.astype(dst_ref.dtype)
            return
        scr_ref[...] = y
        for r in range(dil):
            rows = scr_ref[pl.ds(r, tm // dil, stride=dil), :]
            dst_ref[:, r * B_KV + c * LANES:r * B_KV + (c + 1) * LANES] = rows.astype(dst_ref.dtype)

    chunks = B_KV // LANES
    for g, (q_ref, (_, dil)) in enumerate(zip((q0_ref, q1_ref, q2_ref), B_PATTERNS)):
        for c in range(chunks):
            y = prepared(bqk_ref, g * B_KV, c, 0, cosb, sinb, HEAD_DIM // 2, True)
            store_dilated(y, q_ref, dil, c)
    for c in range(chunks):
        yk = prepared(bqk_ref, B_Q, c, 1, cosb, sinb, HEAD_DIM // 2, True)
        yv = bv_ref[:, c * LANES:(c + 1) * LANES].astype(F32)
        for (_, dil), k_ref, v_ref in zip(B_PATTERNS, (k0_ref, k1_ref, k2_ref), (v0_ref, v1_ref, v2_ref)):
            store_dilated(yk, k_ref, dil, c)
            store_dilated(yv, v_ref, dil, c)
    run(aq_ref, oaq_ref, A_Q, 2, cosa, sina, HEAD_DIM // 4, True)
    run(ak_ref, oak_ref, A_KV, 3, cosa, sina, HEAD_DIM // 4, True)
    run(cq_ref, ocq_ref, C_QK, 4, cosb, sinb, C_KEY_DIM // 2, False)
    run(ck_ref, ock_ref, C_QK, 5, cosb, sinb, C_KEY_DIM // 2, False)


def _prep(proj, tables, gains, mmat):
    n = proj.shape[0]
    tm = min(512, n)
    cosb, sinb, cosa, sina = tables

    def col(width, name):
        idx = OFF[name] // width
        return pl.BlockSpec((tm, width), lambda i: (i, idx))

    def tab():
        return pl.BlockSpec((tm, LANES), lambda i: (i, 0))

    dils = [dil for _, dil in B_PATTERNS]
    shapes = [(n // dil, dil * B_KV) for dil in dils] * 3 + [(n, A_Q), (n, A_KV), (n, C_QK), (n, C_QK)]
    blocks = [(tm // dil, dil * B_KV) for dil in dils] * 3 + [(tm, A_Q), (tm, A_KV), (tm, C_QK), (tm, C_QK)]
    return pl.pallas_call(
        _prep_kernel,
        grid=(n // tm,),
        in_specs=[col(B_Q + B_KV, "bq"), col(A_Q, "aq"), col(B_KV, "bv"), col(C_QK, "cq"),
                  col(C_QK, "ck"), col(A_KV, "ak"), tab(), tab(), tab(), tab(),
                  pl.BlockSpec((8, LANES), lambda i: (0, 0)),
                  pl.BlockSpec((LANES, LANES), lambda i: (0, 0))],
        out_specs=[pl.BlockSpec(blk, lambda i: (i, 0)) for blk in blocks],
        out_shape=[jax.ShapeDtypeStruct(shp, BF16) for shp in shapes],
        scratch_shapes=[pltpu.VMEM((tm, LANES), F32)],
        compiler_params=_cparams(("parallel",)),
        name="prep",
    )(proj, proj, proj, proj, proj, proj, cosb, sinb, cosa, sina, gains, mmat)


A_ONES_ROWS = 16


def _attn_a_kernel(small_ref, qt_ref, k0_ref, kn_ref, vt_ref, o_ref, qp_ref, m_ref, acc_ref, sa_ref, sb_ref,
                   *, bq, nk):
    ik = pl.program_id(2)
    rep = A_HEADS // A_KV_HEADS
    kv_heads = range(A_KV_HEADS)

    @pl.when(ik == 0)
    def _():
        qp_ref[...] = jnp.zeros(qp_ref.shape, BF16)
        for g in kv_heads:
            for r in range(rep):
                h = g * rep + r
                qp_ref[g, g * HEAD_DIM:(g + 1) * HEAD_DIM, r * bq:(r + 1) * bq] = (
                    qt_ref[h * HEAD_DIM:(h + 1) * HEAD_DIM, :])
        m_ref[...] = jnp.full(m_ref.shape, -jnp.inf, F32)
        acc_ref[...] = jnp.zeros(acc_ref.shape, F32)
        for g in kv_heads:
            sa_ref[g] = jnp.dot(k0_ref[...], qp_ref[g], preferred_element_type=F32)

    def step(cur_ref, nxt_ref, has_next, small):
        kn = kn_ref[...]
        ones = jnp.ones((A_ONES_ROWS, kn.shape[0]), BF16)
        for g in kv_heads:
            v_aug = jnp.concatenate([vt_ref[g * HEAD_DIM:(g + 1) * HEAD_DIM, :], ones], axis=0)
            for r in range(rep):
                cols = slice(r * bq, (r + 1) * bq)
                if has_next:
                    nxt_ref[g, :, cols] = jnp.dot(kn, qp_ref[g, :, cols], preferred_element_type=F32)
                s = cur_ref[g, :, cols]
                if small:
                    pv = jnp.dot(v_aug, jnp.exp2(s).astype(BF16), preferred_element_type=F32)
                    acc_ref[g, :, cols] += pv
                    continue
                m_prev = m_ref[g, :, cols]
                m_new = jnp.maximum(m_prev, jnp.max(s, axis=0, keepdims=True))
                p = jnp.exp2(s - m_new).astype(BF16)
                alpha = jnp.exp2(m_prev - m_new)
                pv = jnp.dot(v_aug, p, preferred_element_type=F32)
                acc_ref[g, :, cols] = alpha * acc_ref[g, :, cols] + pv
                m_ref[g, :, cols] = m_new

    last = ik == nk - 1
    is_small = small_ref[0] != 0
    for parity, cur_ref, nxt_ref in ((0, sa_ref, sb_ref), (1, sb_ref, sa_ref)):
        for has_next in (True, False):
            for small in (True, False):
                @pl.when((ik % 2 == parity) & (last != has_next) & (is_small == small))
                def _(cur_ref=cur_ref, nxt_ref=nxt_ref, has_next=has_next, small=small):
                    step(cur_ref, nxt_ref, has_next, small)

    @pl.when(last)
    def _():
        for g in range(A_KV_HEADS):
            acc = acc_ref[g]
            o = acc[:HEAD_DIM] / acc[HEAD_DIM:HEAD_DIM + 1]
            for r in range(rep):
                h = g * rep + r
                o_ref[h * HEAD_DIM:(h + 1) * HEAD_DIM, :] = o[:, r * bq:(r + 1) * bq].astype(o_ref.dtype)


A_SMALL_SCORE = 60.0


def _a_scores_small(a_qn, a_kn):
    bound = (HEAD_DIM ** 0.5) * np.log2(np.e) * 1.02 * jnp.max(jnp.abs(a_qn)) * jnp.max(jnp.abs(a_kn))
    return (bound < A_SMALL_SCORE).astype(jnp.int32).reshape(1)


def _mixer_a(small, aqt, ak, avt, prev, row_off, b, s):
    n = ak.shape[0]
    bq = min(512, s)
    bk = min(1024, s)
    nq, nk = s // bq, s // bk
    rep = A_HEADS // A_KV_HEADS
    q0, k0 = row_off // bq, row_off // bk
    kern = functools.partial(_attn_a_kernel, bq=bq, nk=nk)
    return _seq_call(
        kern, prev,
        grid=(b, nq, nk),
        in_specs=[pl.BlockSpec(memory_space=pltpu.SMEM),
                  pl.BlockSpec((A_Q, bq), lambda ib, iq, ik: (0, q0 + ib * nq + iq)),
                  pl.BlockSpec((bk, A_KV), lambda ib, iq, ik: (k0 + ib * nk, 0)),
                  pl.BlockSpec((bk, A_KV), lambda ib, iq, ik: (k0 + ib * nk + jnp.minimum(ik + 1, nk - 1), 0)),
                  pl.BlockSpec((A_KV, bk), lambda ib, iq, ik: (0, k0 + ib * nk + ik))],
        out_specs=[pl.BlockSpec((A_Q, bq), lambda ib, iq, ik: (0, q0 + ib * nq + iq))],
        out_shape=[jax.ShapeDtypeStruct((A_Q, n), BF16)],
        scratch_shapes=[pltpu.VMEM((A_KV_HEADS, A_KV, rep * bq), BF16),
                        pltpu.VMEM((A_KV_HEADS, 1, rep * bq), F32),
                        pltpu.VMEM((A_KV_HEADS, HEAD_DIM + A_ONES_ROWS, rep * bq), F32),
                        pltpu.VMEM((A_KV_HEADS, bk, rep * bq), F32),
                        pltpu.VMEM((A_KV_HEADS, bk, rep * bq), F32)],
        compiler_params=_cparams(("parallel", "parallel", "arbitrary")),
        name="mixer_a",
        args=(small, aqt, ak, ak, avt))


B_BLOCK_Q = 128
B_HALO = 64


def _attn_b_kernel(q_ref, k0_ref, k1_ref, k2_ref, k3_ref, v0_ref, v1_ref, v2_ref, v3_ref,
                   o_ref, lse_ref, *, u_len):
    i = pl.program_id(2)
    kcat = jnp.concatenate([k0_ref[...], k1_ref[...], k2_ref[...], k3_ref[...]], axis=0)
    vcat = jnp.concatenate([v0_ref[...], v1_ref[...], v2_ref[...], v3_ref[...]], axis=0)
    nkeys = B_BLOCK_Q + 2 * B_HALO
    a = lax.broadcasted_iota(jnp.int32, (B_BLOCK_Q, nkeys), 0)
    c = lax.broadcasted_iota(jnp.int32, (B_BLOCK_Q, nkeys), 1)
    key_pos = i * B_BLOCK_Q - B_HALO + c
    valid = (c >= a) & (c <= a + 2 * B_HALO) & (key_pos >= 0) & (key_pos < u_len)
    left = lax.broadcasted_iota(jnp.int32, (B_BLOCK_Q, LANES), 1) < HEAD_DIM
    zero = jnp.zeros((B_BLOCK_Q, LANES), BF16)
    heads = [(pair, own) for pair in range(B_HEADS // 2) for own in (left, jnp.logical_not(left))]
    lanes = [slice(pair * LANES, (pair + 1) * LANES) for pair in range(B_HEADS // 2)]
    s = [lax.dot_general(jnp.where(own, q_ref[:, lanes[pair]], zero), kcat[:, lanes[pair]],
                         (((1,), (1,)), ((), ())), preferred_element_type=F32) for pair, own in heads]
    s = [jnp.where(valid, sh, -1e30) for sh in s]
    m = [jnp.max(sh, axis=1, keepdims=True) for sh in s]
    e = [jnp.exp(sh - mh) for sh, mh in zip(s, m)]
    den = [jnp.sum(eh, axis=1, keepdims=True) for eh in e]
    pv = [jnp.dot(eh.astype(BF16), vcat[:, lanes[pair]], preferred_element_type=F32)
          for eh, (pair, _) in zip(e, heads)]
    outs = [pvh / dh for pvh, dh in zip(pv, den)]
    lses = [mh + jnp.log(dh) for mh, dh in zip(m, den)]
    for pair in range(B_HEADS // 2):
        o_ref[:, lanes[pair]] = jnp.where(left, outs[2 * pair], outs[2 * pair + 1]).astype(o_ref.dtype)
        lse_ref[:, lanes[pair]] = jnp.where(left, lses[2 * pair], lses[2 * pair + 1])


def _mixer_b_group(qd, kd, vd, prev, g, dil, row_off, b, s):
    u_len = s // dil
    nq = u_len // B_BLOCK_Q
    nkb = u_len // B_HALO
    q_rows0 = row_off // dil // B_BLOCK_Q
    k_rows0 = row_off // dil // B_HALO

    def kspec(t):
        def imap(ib, r, i):
            blk = jnp.clip(2 * i - 1 + t, 0, nkb - 1)
            return (k_rows0 + ib * nkb + blk, r)
        return pl.BlockSpec((B_HALO, B_KV), imap)

    def qspec():
        return pl.BlockSpec((B_BLOCK_Q, B_KV), lambda ib, r, i: (q_rows0 + ib * nq + i, r))

    kern = functools.partial(_attn_b_kernel, u_len=u_len)
    return _seq_call(
        kern, prev,
        grid=(b, dil, nq),
        in_specs=[qspec()] + [kspec(t) for t in range(4)] + [kspec(t) for t in range(4)],
        out_specs=[qspec(), qspec()],
        out_shape=[jax.ShapeDtypeStruct(qd.shape, BF16), jax.ShapeDtypeStruct(qd.shape, F32)],
        compiler_params=_cparams(("parallel", "parallel", "parallel")),
        name=f"mixer_b{g}",
        args=(qd, kd, kd, kd, kd, vd, vd, vd, vd))


def _ret_chunk(q, k, v, dec_ref, xi_ref, zeta_ref, cdec_ref, r_ref):
    outs = []
    for h in range(C_HEADS):
        qh = q[:, h * C_KEY_DIM:(h + 1) * C_KEY_DIM]
        kh = k[:, h * C_KEY_DIM:(h + 1) * C_KEY_DIM]
        vh = v[:, h * C_VAL_DIM:(h + 1) * C_VAL_DIM]
        att = lax.dot_general(qh, kh, (((1,), (1,)), ((), ())), preferred_element_type=F32) * dec_ref[h]
        inner = jnp.dot(att.astype(BF16), vh, preferred_element_type=F32)
        r = r_ref[h]
        cross = jnp.dot(qh, r.astype(BF16), preferred_element_type=F32) * xi_ref[h]
        kz = (kh.astype(F32) * zeta_ref[h]).astype(BF16)
        r_ref[h] = r * cdec_ref[h] + lax.dot_general(kz, vh, (((0,), (0,)), ((), ())),
                                                     preferred_element_type=F32)
        outs.append(inner + cross)
    return jnp.concatenate(outs, axis=1)


def _ret_fwd_kernel(q_ref, k_ref, v_ref, dec_ref, xi_ref, zeta_ref, cdec_ref, o_ref, r_ref):
    @pl.when(pl.program_id(1) == 0)
    def _():
        r_ref[...] = jnp.zeros(r_ref.shape, F32)

    o_ref[...] = _ret_chunk(q_ref[...], k_ref[...], v_ref[...], dec_ref, xi_ref, zeta_ref, cdec_ref, r_ref)


def _ret_bwd_kernel(q_ref, k_ref, v_ref, dec_ref, xi_ref, zeta_ref, cdec_ref, of_ref, gate_ref, ng_ref,
                    o_ref, r_ref):
    @pl.when(pl.program_id(1) == 0)
    def _():
        r_ref[...] = jnp.zeros(r_ref.shape, F32)

    o = of_ref[...] + _ret_chunk(q_ref[...], k_ref[...], v_ref[...], dec_ref, xi_ref, zeta_ref, cdec_ref, r_ref)
    ng = ng_ref[...]
    gate = gate_ref[...].astype(F32)
    for h in range(C_HEADS):
        sl = slice(h * C_VAL_DIM, (h + 1) * C_VAL_DIM)
        oh = o[:, sl]
        mu = jnp.mean(oh, axis=1, keepdims=True)
        var = jnp.mean(jnp.square(oh - mu), axis=1, keepdims=True)
        y = (oh - mu) * lax.rsqrt(var + EPS) * ng[:, sl]
        gh = gate[:, sl]
        o_ref[:, sl] = (gh * jax.nn.sigmoid(gh) * y).astype(o_ref.dtype)


def _ret_tables(dec_param, chunk, strict):
    log_g = -jnp.exp(dec_param.astype(F32))
    j = jnp.arange(chunk, dtype=F32)
    lg = log_g[:, None, None]
    if strict:
        diff = j[None, :] - j[:, None]
        dec = jnp.where((diff > 0)[None], jnp.exp(jnp.maximum(diff, 0.0)[None] * lg), 0.0)
        xi = jnp.exp((chunk - j)[None, :, None] * lg)
        zeta = jnp.exp(j[None, :, None] * lg)
    else:
        diff = j[:, None] - j[None, :]
        dec = jnp.where((diff >= 0)[None], jnp.exp(jnp.maximum(diff, 0.0)[None] * lg), 0.0)
        xi = jnp.exp((j + 1.0)[None, :, None] * lg)
        zeta = jnp.exp((chunk - 1.0 - j)[None, :, None] * lg)
    xi = jnp.broadcast_to(xi, (C_HEADS, chunk, C_VAL_DIM))
    zeta = jnp.broadcast_to(zeta, (C_HEADS, chunk, C_KEY_DIM))
    cdec = jnp.broadcast_to(jnp.exp(chunk * log_g)[:, None, None], (C_HEADS, C_KEY_DIM, C_VAL_DIM))
    return dec, xi, zeta, cdec


C_CHUNK = 512


def _mixer_c(cq, ck, proj, tabs_f, tabs_b, norm_g, prev, row_off, b, s):
    n = cq.shape[0]
    prev_f, prev_o = (None, None) if prev is None else ([prev[0]], [prev[1]])
    nc = s // C_CHUNK
    r0 = row_off // C_CHUNK
    v_col = OFF["cv"] // C_V
    g_col = OFF["cg"] // C_V

    def fwd_rows(ib, c):
        return r0 + ib * nc + c

    def bwd_rows(ib, c):
        return r0 + ib * nc + (nc - 1 - c)

    def specs(rows):
        def full(shape):
            return pl.BlockSpec(shape, lambda ib, c: (0,) * len(shape))
        return [pl.BlockSpec((C_CHUNK, C_QK), lambda ib, c: (rows(ib, c), 0)),
                pl.BlockSpec((C_CHUNK, C_QK), lambda ib, c: (rows(ib, c), 0)),
                pl.BlockSpec((C_CHUNK, C_V), lambda ib, c: (rows(ib, c), v_col)),
                full((C_HEADS, C_CHUNK, C_CHUNK)), full((C_HEADS, C_CHUNK, C_VAL_DIM)),
                full((C_HEADS, C_CHUNK, C_KEY_DIM)), full((C_HEADS, C_KEY_DIM, C_VAL_DIM))]

    scratch = [pltpu.VMEM((C_HEADS, C_KEY_DIM, C_VAL_DIM), F32)]
    o_f, = _seq_call(
        _ret_fwd_kernel, prev_f,
        grid=(b, nc),
        in_specs=specs(fwd_rows),
        out_specs=[pl.BlockSpec((C_CHUNK, C_V), lambda ib, c: (fwd_rows(ib, c), 0))],
        out_shape=[jax.ShapeDtypeStruct((n, C_V), F32)],
        scratch_shapes=scratch,
        compiler_params=_cparams(("parallel", "arbitrary")),
        name="ret_fwd",
        args=(cq, ck, proj, *tabs_f))
    o_c, = _seq_call(
        _ret_bwd_kernel, prev_o,
        grid=(b, nc),
        in_specs=specs(bwd_rows) + [
            pl.BlockSpec((C_CHUNK, C_V), lambda ib, c: (bwd_rows(ib, c), 0)),
            pl.BlockSpec((C_CHUNK, C_V), lambda ib, c: (bwd_rows(ib, c), g_col)),
            pl.BlockSpec((1, C_V), lambda ib, c: (0, 0))],
        out_specs=[pl.BlockSpec((C_CHUNK, C_V), lambda ib, c: (bwd_rows(ib, c), 0))],
        out_shape=[jax.ShapeDtypeStruct((n, C_V), BF16)],
        scratch_shapes=scratch,
        compiler_params=_cparams(("parallel", "arbitrary")),
        name="ret_bwd",
        args=(cq, ck, proj, *tabs_b, o_f, proj, norm_g))
    return [o_f, o_c]


def _merge_kernel(oa_ref, ob0_ref, ob1_ref, ob2_ref, l0_ref, l1_ref, l2_ref, oc_ref,
                  ga_ref, gb_ref, gc_ref, x_ref, woa_ref, wob_ref, woc_ref, wout_ref, n2_ref,
                  xo_ref, hn_ref, scr_ref):
    tm = x_ref.shape[0]

    def natural(src_ref, dil, slot):
        if dil == 1:
            return src_ref[...].astype(F32)
        chunks = B_KV // LANES
        for r in range(dil):
            for c in range(chunks):
                col = r * B_KV + c * LANES
                scr_ref[slot, c, pl.ds(r, tm // dil, stride=dil), :] = src_ref[:, col:col + LANES].astype(F32)
        return jnp.concatenate([scr_ref[slot, c] for c in range(chunks)], axis=1)

    dils = [dil for _, dil in B_PATTERNS]
    l0, l1, l2 = [natural(ref, dil, 2 * g) for g, (ref, dil) in enumerate(zip((l0_ref, l1_ref, l2_ref), dils))]
    o0, o1, o2 = [natural(ref, dil, 2 * g + 1)
                  for g, (ref, dil) in enumerate(zip((ob0_ref, ob1_ref, ob2_ref), dils))]
    mx = jnp.maximum(jnp.maximum(l0, l1), l2)
    e0, e1, e2 = jnp.exp(l0 - mx), jnp.exp(l1 - mx), jnp.exp(l2 - mx)
    ob = (e0 * o0 + e1 * o1 + e2 * o2) / (e0 + e1 + e2)

    def branch(o, w_ref, gate_ref):
        y = jnp.dot(o, w_ref[...], preferred_element_type=F32)
        return jax.nn.sigmoid(gate_ref[...].astype(F32)) * y

    oa = jnp.transpose(oa_ref[...].astype(F32)).astype(BF16)
    merged = (branch(oa, woa_ref, ga_ref) + branch(ob.astype(BF16), wob_ref, gb_ref)
              + branch(oc_ref[...], woc_ref, gc_ref))
    x = x_ref[...] + jnp.dot(merged.astype(BF16), wout_ref[...], preferred_element_type=F32)
    xo_ref[...] = x
    ms = jnp.mean(x * x, axis=-1, keepdims=True)
    hn_ref[...] = (x * lax.rsqrt(ms + EPS) * n2_ref[...]).astype(hn_ref.dtype)


def _merge(o_a, o_b, lse_b, o_c, proj, x, w_oa, w_ob, w_oc, w_out, n2, hn_dtype):
    n = x.shape[0]
    tm = min(512, n)
    gl0 = OFF["gl"] // D_MODEL

    def rows(width):
        return pl.BlockSpec((tm, width), lambda i: (i, 0))

    def gate(k):
        return pl.BlockSpec((tm, D_MODEL), lambda i: (i, gl0 + k))

    def full(r, c):
        return pl.BlockSpec((r, c), lambda i: (0, 0))

    grouped = [pl.BlockSpec((tm // dil, dil * B_KV), lambda i: (i, 0)) for _, dil in B_PATTERNS]
    return pl.pallas_call(
        _merge_kernel,
        grid=(n // tm,),
        in_specs=[pl.BlockSpec((A_Q, tm), lambda i: (0, i))] + grouped + grouped
                 + [rows(C_V), gate(0), gate(1), gate(2), rows(D_MODEL),
                    full(A_Q, D_MODEL), full(B_KV, D_MODEL), full(C_V, D_MODEL), full(D_MODEL, D_MODEL),
                    full(1, D_MODEL)],
        out_specs=[rows(D_MODEL), rows(D_MODEL)],
        out_shape=[jax.ShapeDtypeStruct((n, D_MODEL), F32), jax.ShapeDtypeStruct((n, D_MODEL), hn_dtype)],
        scratch_shapes=[pltpu.VMEM((2 * B_GROUPS, B_KV // LANES, tm, LANES), F32)],
        compiler_params=_cparams(("parallel",)),
        name="merge_out",
    )(o_a, *o_b, *lse_b, o_c, proj, proj, proj, x, w_oa, w_ob, w_oc, w_out, n2)


def _ffn_kernel(hn_ref, x_ref, w1_ref, w3_ref, w2_ref, o_ref):
    j = pl.program_id(1)
    hn = hn_ref[...]
    a = jnp.dot(hn, w1_ref[...], preferred_element_type=F32)
    g = jnp.dot(hn, w3_ref[...], preferred_element_type=F32)
    h = (a * jax.nn.sigmoid(a) * g).astype(BF16)
    y = jnp.dot(h, w2_ref[...], preferred_element_type=F32)

    @pl.when(j == 0)
    def _():
        o_ref[...] = x_ref[...] + y

    @pl.when(j > 0)
    def _():
        o_ref[...] += y


def _ffn(hn, x, w1, w3, w2):
    n = x.shape[0]
    d_ff = w1.shape[1]
    tm = min(1024, n)
    tf = d_ff // 2
    nf = d_ff // tf
    return pl.pallas_call(
        _ffn_kernel,
        grid=(n // tm, nf),
        in_specs=[pl.BlockSpec((tm, D_MODEL), lambda i, j: (i, 0)),
                  pl.BlockSpec((tm, D_MODEL), lambda i, j: (i, 0)),
                  pl.BlockSpec((D_MODEL, tf), lambda i, j: (0, j)),
                  pl.BlockSpec((D_MODEL, tf), lambda i, j: (0, j)),
                  pl.BlockSpec((tf, D_MODEL), lambda i, j: (j, 0))],
        out_specs=pl.BlockSpec((tm, D_MODEL), lambda i, j: (i, 0)),
        out_shape=jax.ShapeDtypeStruct((n, D_MODEL), F32),
        compiler_params=_cparams(("parallel", "arbitrary")),
        name="ffn",
    )(hn, x, w1, w3, w2)


TOP_K = 2
MOE_TM = 512


def _router_kernel(hn_ref, wr_ref, idx_ref, wts_ref):
    logits = jnp.dot(hn_ref[...], wr_ref[...], preferred_element_type=F32,
                     precision=lax.Precision.HIGHEST)
    col = lax.broadcasted_iota(jnp.int32, logits.shape, 1)
    m1 = jnp.max(logits, axis=1, keepdims=True)
    i1 = jnp.min(jnp.where(logits == m1, col, N_EXPERTS), axis=1, keepdims=True)
    rest = jnp.where(col == i1, -jnp.inf, logits)
    m2 = jnp.max(rest, axis=1, keepdims=True)
    i2 = jnp.min(jnp.where(rest == m2, col, N_EXPERTS), axis=1, keepdims=True)
    e2 = jnp.exp(m2 - m1)
    w1 = 1.0 / (1.0 + e2)
    idx_ref[...] = jnp.concatenate([i1, i2], axis=1)
    wts_ref[...] = jnp.concatenate([w1, e2 * w1], axis=1)


def _router(hn, wr):
    n = hn.shape[0]
    tm = min(1024, n)
    return pl.pallas_call(
        _router_kernel,
        grid=(n // tm,),
        in_specs=[pl.BlockSpec((tm, D_MODEL), lambda i: (i, 0)),
                  pl.BlockSpec((D_MODEL, N_EXPERTS), lambda i: (0, 0))],
        out_specs=[pl.BlockSpec((tm, TOP_K), lambda i: (i, 0)), pl.BlockSpec((tm, TOP_K), lambda i: (i, 0))],
        out_shape=[jax.ShapeDtypeStruct((n, TOP_K), jnp.int32), jax.ShapeDtypeStruct((n, TOP_K), F32)],
        compiler_params=_cparams(("parallel",)),
        name="router",
    )(hn, wr)


def _route(idx, tm):
    n = idx.shape[0]
    e_flat = idx.reshape(-1)
    onehot = (e_flat[:, None] == jnp.arange(N_EXPERTS, dtype=jnp.int32)[None, :]).astype(jnp.int32)
    csum = jnp.cumsum(onehot, axis=0)
    rank = jnp.sum((csum - onehot) * onehot, axis=1)
    gsz = ((csum[-1] + tm - 1) // tm) * tm
    gend = jnp.cumsum(gsz)
    pos = (gend - gsz)[e_flat] + rank
    n_rows = n * TOP_K + N_EXPERTS * tm
    n_tiles = n_rows // tm
    row_token = jnp.zeros((n_rows,), jnp.int32).at[pos].set(jnp.arange(n * TOP_K, dtype=jnp.int32) // TOP_K)
    tile_start = jnp.arange(n_tiles, dtype=jnp.int32) * tm
    tile_expert = jnp.minimum(jnp.sum((tile_start[:, None] >= gend[None, :]).astype(jnp.int32), axis=1),
                              N_EXPERTS - 1)
    n_used = (gend[-1] // tm).astype(jnp.int32).reshape(1)
    return pos.reshape(n, TOP_K), row_token.reshape(n_tiles, 1, tm), tile_expert, n_used


def _row_copy(src_ref, src_row, dst_ref, dst_row, sem):
    return pltpu.make_async_copy(src_ref.at[pl.ds(src_row, 1)], dst_ref.at[pl.ds(dst_row, 1)], sem)


def _expert_kernel(te_ref, nu_ref, idx_ref, nxt_ref, src_ref, w1_ref, w3_ref, w2_ref, y_ref,
                   xs_ref, xb_ref, acc_ref, sem, *, nf, tm):
    i = pl.program_id(0)
    j = pl.program_id(1)
    n_used = nu_ref[0]
    active = i < n_used
    slot = i % 2
    share = tm // nf

    @pl.when((i == 0) & (j == 0))
    def _():
        def issue(r, carry):
            _row_copy(src_ref, idx_ref[0, 0, r], xs_ref.at[0], r, sem.at[0]).start()
            return carry
        lax.fori_loop(0, tm, issue, 0, unroll=8)

    @pl.when((j == 0) & (i <= n_used))
    def _():
        pltpu.make_async_copy(src_ref.at[pl.ds(0, tm)], xs_ref.at[slot], sem.at[slot]).wait()

    @pl.when(active)
    def _():
        @pl.when(j == 0)
        def _():
            xb_ref[...] = xs_ref[slot].astype(BF16)

        for r in range(share):
            row = j * share + r
            _row_copy(src_ref, nxt_ref[0, 0, row], xs_ref.at[1 - slot], row, sem.at[1 - slot]).start()

        xb = xb_ref[...]
        a = jnp.dot(xb, w1_ref[0], preferred_element_type=F32)
        g = jnp.dot(xb, w3_ref[0], preferred_element_type=F32)
        h = (a * jax.nn.sigmoid(a) * g).astype(BF16)
        y = jnp.dot(h, w2_ref[0], preferred_element_type=F32)

        @pl.when(j == 0)
        def _():
            acc_ref[...] = y

        @pl.when(j > 0)
        def _():
            acc_ref[...] += y

        @pl.when(j == nf - 1)
        def _():
            y_ref[...] = acc_ref[...]

    @pl.when(jnp.logical_not(active) & (j == nf - 1))
    def _():
        y_ref[...] = jnp.zeros(y_ref.shape, y_ref.dtype)


def _experts(src, row_token, tile_expert, n_used, w1, w3, w2):
    nt, _, tm = row_token.shape
    d_ff = w1.shape[2]
    nf = 2
    tf = d_ff // nf

    def jcol(i, j, nu):
        return jnp.where(i < nu[0], j, nf - 1)

    grid_spec = pltpu.PrefetchScalarGridSpec(
        num_scalar_prefetch=2,
        grid=(nt, nf),
        in_specs=[pl.BlockSpec((1, 1, tm), lambda i, j, te, nu: (i, 0, 0), memory_space=pltpu.SMEM),
                  pl.BlockSpec((1, 1, tm), lambda i, j, te, nu: (jnp.minimum(i + 1, nt - 1), 0, 0),
                               memory_space=pltpu.SMEM),
                  pl.BlockSpec(memory_space=pl.ANY),
                  pl.BlockSpec((1, D_MODEL, tf), lambda i, j, te, nu: (te[i], 0, jcol(i, j, nu))),
                  pl.BlockSpec((1, D_MODEL, tf), lambda i, j, te, nu: (te[i], 0, jcol(i, j, nu))),
                  pl.BlockSpec((1, tf, D_MODEL), lambda i, j, te, nu: (te[i], jcol(i, j, nu), 0))],
        out_specs=pl.BlockSpec((tm, D_MODEL), lambda i, j, te, nu: (i, 0)),
        scratch_shapes=[pltpu.VMEM((2, tm, D_MODEL), src.dtype), pltpu.VMEM((tm, D_MODEL), BF16),
                        pltpu.VMEM((tm, D_MODEL), F32), pltpu.SemaphoreType.DMA((2,))],
    )
    return pl.pallas_call(
        functools.partial(_expert_kernel, nf=nf, tm=tm),
        grid_spec=grid_spec,
        out_shape=jax.ShapeDtypeStruct((nt * tm, D_MODEL), F32),
        compiler_params=_cparams(("arbitrary", "arbitrary")),
        name="moe_experts",
    )(tile_expert, n_used, row_token, row_token, src, w1, w3, w2)


def _combine_kernel(pos_ref, nxt_ref, wts_ref, x_ref, y_ref, o_ref, g_ref, sem, *, tm, nt):
    i = pl.program_id(0)
    slot = i % 2

    def request(p_ref, dst_slot):
        def issue(r, carry):
            for k in range(TOP_K):
                _row_copy(y_ref, p_ref[0, 0, TOP_K * r + k], g_ref.at[dst_slot, k], r, sem.at[dst_slot]).start()
            return carry
        lax.fori_loop(0, tm, issue, 0, unroll=8)

    @pl.when(i == 0)
    def _():
        request(pos_ref, 0)

    @pl.when(i + 1 < nt)
    def _():
        request(nxt_ref, 1 - slot)

    for k in range(TOP_K):
        pltpu.make_async_copy(y_ref.at[pl.ds(0, tm)], g_ref.at[slot, k], sem.at[slot]).wait()
    w = wts_ref[...]
    o_ref[...] = x_ref[...] + w[:, 0:1] * g_ref[slot, 0] + w[:, 1:2] * g_ref[slot, 1]


def _combine(pos, wts, x, y):
    n = x.shape[0]
    tm = min(256, n)
    nt = n // tm
    pos3 = pos.reshape(nt, 1, TOP_K * tm)
    return pl.pallas_call(
        functools.partial(_combine_kernel, tm=tm, nt=nt),
        grid=(nt,),
        in_specs=[pl.BlockSpec((1, 1, TOP_K * tm), lambda i: (i, 0, 0), memory_space=pltpu.SMEM),
                  pl.BlockSpec((1, 1, TOP_K * tm), lambda i: (jnp.minimum(i + 1, nt - 1), 0, 0),
                               memory_space=pltpu.SMEM),
                  pl.BlockSpec((tm, TOP_K), lambda i: (i, 0)),
                  pl.BlockSpec((tm, D_MODEL), lambda i: (i, 0)),
                  pl.BlockSpec(memory_space=pl.ANY)],
        out_specs=pl.BlockSpec((tm, D_MODEL), lambda i: (i, 0)),
        out_shape=jax.ShapeDtypeStruct((n, D_MODEL), F32),
        scratch_shapes=[pltpu.VMEM((2, TOP_K, tm, D_MODEL), F32), pltpu.SemaphoreType.DMA((2,))],
        compiler_params=_cparams(("arbitrary",)),
        name="moe_combine",
    )(pos3, pos3, wts, x, y)


def _moe(hn, x, wr, w1, w3, w2):
    idx, wts = _router(hn, wr)
    pos, row_token, tile_expert, n_used = _route(idx, MOE_TM)
    y = _experts(hn, row_token, tile_expert, n_used, w1, w3, w2)
    return _combine(pos, wts, x, y)


def _permute_w_in(w):
    cols = [w[:, _ORIG[name][0]:_ORIG[name][0] + _ORIG[name][1]] for name in _NEW_ORDER]
    cols.append(jnp.zeros((w.shape[0], PROJ_W - _USED_W), w.dtype))
    return jnp.concatenate(cols, axis=1).astype(BF16)


def _angles(pos, dim):
    inv = ROPE_THETA ** (-jnp.arange(0, dim, 2, dtype=F32) / dim)
    return pos.astype(F32)[:, None] * inv[None, :]


def _rope_tables(seqs):
    pos = jnp.concatenate([jnp.tile(jnp.arange(s), b) for b, s in seqs])
    ang = _angles(pos, HEAD_DIM)
    cosb = jnp.tile(jnp.cos(ang), (1, 4))
    sinb = jnp.tile(jnp.concatenate([-jnp.sin(ang), jnp.sin(ang)], axis=1), (1, 2))
    ar = _angles(pos // GRID_W, HEAD_DIM // 2)
    ac = _angles(pos % GRID_W, HEAD_DIM // 2)
    cosa = jnp.tile(jnp.concatenate([jnp.cos(ar), jnp.cos(ar), jnp.cos(ac), jnp.cos(ac)], axis=1), (1, 2))
    sina = jnp.tile(jnp.concatenate([-jnp.sin(ar), jnp.sin(ar), -jnp.sin(ac), jnp.sin(ac)], axis=1), (1, 2))
    return cosb, sinb, cosa, sina


def _head_mean_matrix():
    blk = np.kron(np.eye(LANES // HEAD_DIM), np.ones((HEAD_DIM, HEAD_DIM))) / HEAD_DIM
    return jnp.asarray(blk, BF16)


def _gain_rows(a_qn, a_kn, b_qn, b_kn):
    scale = HEAD_DIM ** -0.5
    scale_a = scale * np.log2(np.e)
    rows = [jnp.tile(b_qn, 2) * scale, jnp.tile(b_kn, 2), jnp.tile(a_qn, 2) * scale_a, jnp.tile(a_kn, 2),
            jnp.ones((LANES,), F32), jnp.full((LANES,), C_KEY_DIM ** -0.5, F32),
            jnp.ones((LANES,), F32), jnp.ones((LANES,), F32)]
    return jnp.stack(rows).astype(F32)


def _trunk(x, seqs, norm1_g, w_in, a_qn, a_kn, b_qn, b_kn, ret_dec_f, ret_dec_b, ret_norm_g,
           w_oa, w_ob, w_oc, w_out, norm2_g, ffn_w1, ffn_w3, ffn_w2,
           moe_router, moe_w1, moe_w3, moe_w2):
    depth = w_in.shape[0]
    tables = _rope_tables(seqs)
    mmat = _head_mean_matrix()
    for l in range(depth):
        proj = _inproj(x, norm1_g[l][None, :], _permute_w_in(w_in[l]))
        prepped = _prep(proj, tables, _gain_rows(a_qn[l], a_kn[l], b_qn[l], b_kn[l]), mmat)
        bqs, bks, bvs = prepped[0:3], prepped[3:6], prepped[6:9]
        aq, ak, cq, ck = prepped[9:]
        aqt = aq.T
        avt = proj[:, OFF["av"]:OFF["av"] + A_KV].T
        tabs_f = _ret_tables(ret_dec_f[l], C_CHUNK, False)
        tabs_b = _ret_tables(ret_dec_b[l], C_CHUNK, True)
        ng = ret_norm_g[l][None, :].astype(F32)
        res_a, res_c = None, None
        res_b = [None] * B_GROUPS
        row_off = 0
        for b, s in seqs:
            res_a = _mixer_a(_a_scores_small(a_qn[l], a_kn[l]), aqt, ak, avt, res_a, row_off, b, s)
            for g, (_, dil) in enumerate(B_PATTERNS):
                res_b[g] = _mixer_b_group(bqs[g], bks[g], bvs[g], res_b[g], g, dil, row_off, b, s)
            res_c = _mixer_c(cq, ck, proj, tabs_f, tabs_b, ng, res_c, row_off, b, s)
            row_off += b * s
        x, hn = _merge(res_a[0], [r[0] for r in res_b], [r[1] for r in res_b], res_c[1], proj, x,
                       w_oa[l].astype(BF16), w_ob[l].astype(BF16), w_oc[l].astype(BF16),
                       w_out[l].astype(BF16), norm2_g[l][None, :], BF16 if l % 2 == 0 else F32)
        i = l // 2
        if l % 2 == 0:
            x = _ffn(hn, x, ffn_w1[i].astype(BF16), ffn_w3[i].astype(BF16), ffn_w2[i].astype(BF16))
        else:
            x = _moe(hn, x, moe_router[i], moe_w1[i].astype(BF16), moe_w3[i].astype(BF16),
                     moe_w2[i].astype(BF16))
    return x


def kernel(x_prompt, x_sample, norm1_g, w_in, a_qn, a_kn, b_qn, b_kn, ret_dec_f, ret_dec_b, ret_norm_g,
           w_oa, w_ob, w_oc, w_out, norm2_g, ffn_w1, ffn_w3, ffn_w2, moe_router, moe_w1, moe_w3, moe_w2):
    seqs = (x_prompt.shape[:2], x_sample.shape[:2])
    x = jnp.concatenate([x_prompt.reshape(-1, D_MODEL), x_sample.reshape(-1, D_MODEL)], axis=0)
    y = _trunk(x, seqs, norm1_g, w_in, a_qn, a_kn, b_qn, b_kn, ret_dec_f, ret_dec_b, ret_norm_g,
               w_oa, w_ob, w_oc, w_out, norm2_g, ffn_w1, ffn_w3, ffn_w2,
               moe_router, moe_w1, moe_w3, moe_w2)
    n_p = x_prompt.shape[0] * x_prompt.shape[1]
    return (y[:n_p].reshape(x_prompt.shape), y[n_p:].reshape(x_sample.shape))
```

```python
import functools

import numpy as np
import jax
import jax.numpy as jnp
from jax import lax
from jax.experimental import pallas as pl
from jax.experimental.pallas import tpu as pltpu

F32 = jnp.float32
BF16 = jnp.bfloat16

D_MODEL = 1024
GRID_W = 64
HEAD_DIM = 64
ROPE_THETA = 10000.0
EPS = 1e-6
A_HEADS = 8
A_KV_HEADS = 2
B_PATTERNS = ((128, 1), (512, 4), (2048, 16))
B_GROUPS = 3
B_HEADS = 8
C_HEADS = 4
C_KEY_DIM = 64
C_VAL_DIM = 128
N_EXPERTS = 8
N_BRANCHES = 3

A_Q = A_HEADS * HEAD_DIM
A_KV = A_KV_HEADS * HEAD_DIM
B_Q = B_GROUPS * B_HEADS * HEAD_DIM
B_KV = B_HEADS * HEAD_DIM
C_QK = C_HEADS * C_KEY_DIM
C_V = C_HEADS * C_VAL_DIM

LANES = 128
VMEM_LIMIT = 56 * 1024 * 1024

_ORIG_SPLITS = (("aq", A_Q), ("ak", A_KV), ("av", A_KV), ("bq", B_Q), ("bk", B_KV), ("bv", B_KV),
                ("cq", C_QK), ("ck", C_QK), ("cv", C_V), ("cg", C_V), ("gl", N_BRANCHES * D_MODEL))
_NEW_ORDER = ("bq", "bk", "aq", "bv", "cv", "cg", "gl", "cq", "ck", "ak", "av")
PROJ_W = 8192


def _layout():
    orig, o = {}, 0
    for name, w in _ORIG_SPLITS:
        orig[name] = (o, w)
        o += w
    new, o = {}, 0
    for name in _NEW_ORDER:
        new[name] = o
        o += orig[name][1]
    return orig, new, o


_ORIG, OFF, _USED_W = _layout()


def _cparams(sem):
    return pltpu.CompilerParams(dimension_semantics=sem, vmem_limit_bytes=VMEM_LIMIT)


def _seq_call(kern, prev, *, in_specs, out_specs, out_shape, args, **kwargs):
    n_in = len(in_specs)
    if prev is None:
        return pl.pallas_call(kern, in_specs=in_specs, out_specs=out_specs, out_shape=out_shape,
                              **kwargs)(*args)
    n_prev = len(prev)

    def chained(*refs):
        return kern(*refs[:n_in], *refs[n_in + n_prev:])

    return pl.pallas_call(
        chained,
        in_specs=list(in_specs) + [pl.BlockSpec(memory_space=pl.ANY)] * n_prev,
        out_specs=out_specs, out_shape=out_shape,
        input_output_aliases={n_in + k: k for k in range(n_prev)},
        **kwargs)(*args, *prev)


def _inproj_kernel(x_ref, g_ref, w_ref, o_ref, xn_ref):
    @pl.when(pl.program_id(1) == 0)
    def _():
        x = x_ref[...]
        ms = jnp.mean(x * x, axis=-1, keepdims=True)
        xn_ref[...] = (x * lax.rsqrt(ms + EPS) * g_ref[...]).astype(BF16)

    o_ref[...] = jnp.dot(xn_ref[...], w_ref[...], preferred_element_type=F32).astype(BF16)


def _inproj(x, g, w):
    n = x.shape[0]
    tm = min(2048, n)
    tn = 1024
    return pl.pallas_call(
        _inproj_kernel,
        grid=(n // tm, PROJ_W // tn),
        in_specs=[pl.BlockSpec((tm, D_MODEL), lambda i, j: (i, 0)),
                  pl.BlockSpec((1, D_MODEL), lambda i, j: (0, 0)),
                  pl.BlockSpec((D_MODEL, tn), lambda i, j: (0, j))],
        out_specs=pl.BlockSpec((tm, tn), lambda i, j: (i, j)),
        out_shape=jax.ShapeDtypeStruct((n, PROJ_W), BF16),
        scratch_shapes=[pltpu.VMEM((tm, D_MODEL), BF16)],
        compiler_params=_cparams(("parallel", "arbitrary")),
        name="inproj",
    )(x, g, w)


def _norm_rot(x, gain, cos, sin, mmat, half, do_norm):
    if do_norm:
        ms = jnp.dot((x * x).astype(BF16), mmat, preferred_element_type=F32)
        x = x * lax.rsqrt(ms + EPS)
    x = x * gain
    lane = lax.broadcasted_iota(jnp.int32, x.shape, 1)
    first = (lane % (2 * half)) < half
    swapped = jnp.where(first, pltpu.roll(x, LANES - half, 1), pltpu.roll(x, half, 1))
    return x * cos + swapped * sin


def _prep_kernel(bqk_ref, aq_ref, bv_ref, cq_ref, ck_ref, ak_ref,
                 cosb_ref, sinb_ref, cosa_ref, sina_ref, gain_ref, mmat_ref,
                 q0_ref, q1_ref, q2_ref, k0_ref, k1_ref, k2_ref, v0_ref, v1_ref, v2_ref,
                 oaq_ref, oak_ref, ocq_ref, ock_ref, scr_ref):
    cosb, sinb = cosb_ref[...], sinb_ref[...]
    cosa, sina = cosa_ref[...], sina_ref[...]
    mmat = mmat_ref[...]
    gains = gain_ref[...]
    tm = scr_ref.shape[0]

    def prepared(src_ref, src_off, c, gain_row, cos, sin, half, do_norm):
        x = src_ref[:, src_off + c * LANES:src_off + (c + 1) * LANES].astype(F32)
        return _norm_rot(x, gains[gain_row:gain_row + 1, :], cos, sin, mmat, half, do_norm)

    def run(src_ref, dst_ref, width, gain_row, cos, sin, half, do_norm):
        for c in range(width // LANES):
            y = prepared(src_ref, 0, c, gain_row, cos, sin, half, do_norm)
            dst_ref[:, c * LANES:(c + 1) * LANES] = y.astype(dst_ref.dtype)

    def store_dilated(y, dst_ref, dil, c):
        if dil == 1:
            dst_ref[:, c * LANES:(c + 1) * LANES] = y.astype(dst_ref.dtype)
            return
        scr_ref[...] = y
        for r in range(dil):
            rows = scr_ref[pl.ds(r, tm // dil, stride=dil), :]
            dst_ref[:, r * B_KV + c * LANES:r * B_KV + (c + 1) * LANES] = rows.astype(dst_ref.dtype)

    chunks = B_KV // LANES
    for g, (q_ref, (_, dil)) in enumerate(zip((q0_ref, q1_ref, q2_ref), B_PATTERNS)):
        for c in range(chunks):
            y = prepared(bqk_ref, g * B_KV, c, 0, cosb, sinb, HEAD_DIM // 2, True)
            store_dilated(y, q_ref, dil, c)
    for c in range(chunks):
        yk = prepared(bqk_ref, B_Q, c, 1, cosb, sinb, HEAD_DIM // 2, True)
        yv = bv_ref[:, c * LANES:(c + 1) * LANES].astype(F32)
        for (_, dil), k_ref, v_ref in zip(B_PATTERNS, (k0_ref, k1_ref, k2_ref), (v0_ref, v1_ref, v2_ref)):
            store_dilated(yk, k_ref, dil, c)
            store_dilated(yv, v_ref, dil, c)
    run(aq_ref, oaq_ref, A_Q, 2, cosa, sina, HEAD_DIM // 4, True)
    run(ak_ref, oak_ref, A_KV, 3, cosa, sina, HEAD_DIM // 4, True)
    run(cq_ref, ocq_ref, C_QK, 4, cosb, sinb, C_KEY_DIM // 2, False)
    run(ck_ref, ock_ref, C_QK, 5, cosb, sinb, C_KEY_DIM // 2, False)


def _prep(proj, tables, gains, mmat):
    n = proj.shape[0]
    tm = min(512, n)
    cosb, sinb, cosa, sina = tables

    def col(width, name):
        idx = OFF[name] // width
        return pl.BlockSpec((tm, width), lambda i: (i, idx))

    def tab():
        return pl.BlockSpec((tm, LANES), lambda i: (i, 0))

    dils = [dil for _, dil in B_PATTERNS]
    shapes = [(n // dil, dil * B_KV) for dil in dils] * 3 + [(n, A_Q), (n, A_KV), (n, C_QK), (n, C_QK)]
    blocks = [(tm // dil, dil * B_KV) for dil in dils] * 3 + [(tm, A_Q), (tm, A_KV), (tm, C_QK), (tm, C_QK)]
    return pl.pallas_call(
        _prep_kernel,
        grid=(n // tm,),
        in_specs=[col(B_Q + B_KV, "bq"), col(A_Q, "aq"), col(B_KV, "bv"), col(C_QK, "cq"),
                  col(C_QK, "ck"), col(A_KV, "ak"), tab(), tab(), tab(), tab(),
                  pl.BlockSpec((8, LANES), lambda i: (0, 0)),
                  pl.BlockSpec((LANES, LANES), lambda i: (0, 0))],
        out_specs=[pl.BlockSpec(blk, lambda i: (i, 0)) for blk in blocks],
        out_shape=[jax.ShapeDtypeStruct(shp, BF16) for shp in shapes],
        scratch_shapes=[pltpu.VMEM((tm, LANES), F32)],
        compiler_params=_cparams(("parallel",)),
        name="prep",
    )(proj, proj, proj, proj, proj, proj, cosb, sinb, cosa, sina, gains, mmat)


A_ONES_ROWS = 16


def _attn_a_kernel(small_ref, qt_ref, k0_ref, kn_ref, vt_ref, o_ref, qp_ref, m_ref, acc_ref, sa_ref, sb_ref,
                   *, bq, nk):
    ik = pl.program_id(2)
    rep = A_HEADS // A_KV_HEADS
    kv_heads = range(A_KV_HEADS)

    @pl.when(ik == 0)
    def _():
        qp_ref[...] = jnp.zeros(qp_ref.shape, BF16)
        for g in kv_heads:
            for r in range(rep):
                h = g * rep + r
                qp_ref[g, g * HEAD_DIM:(g + 1) * HEAD_DIM, r * bq:(r + 1) * bq] = (
                    qt_ref[h * HEAD_DIM:(h + 1) * HEAD_DIM, :])
        m_ref[...] = jnp.full(m_ref.shape, -jnp.inf, F32)
        acc_ref[...] = jnp.zeros(acc_ref.shape, F32)
        for g in kv_heads:
            sa_ref[g] = jnp.dot(k0_ref[...], qp_ref[g], preferred_element_type=F32)

    def step(cur_ref, nxt_ref, has_next, small):
        kn = kn_ref[...]
        ones = jnp.ones((A_ONES_ROWS, kn.shape[0]), BF16)
        for g in kv_heads:
            v_aug = jnp.concatenate([vt_ref[g * HEAD_DIM:(g + 1) * HEAD_DIM, :], ones], axis=0)
            for r in range(rep):
                cols = slice(r * bq, (r + 1) * bq)
                if has_next:
                    nxt_ref[g, :, cols] = jnp.dot(kn, qp_ref[g, :, cols], preferred_element_type=F32)
                s = cur_ref[g, :, cols]
                if small:
                    pv = jnp.dot(v_aug, jnp.exp2(s).astype(BF16), preferred_element_type=F32)
                    acc_ref[g, :, cols] += pv
                    continue
                m_prev = m_ref[g, :, cols]
                m_new = jnp.maximum(m_prev, jnp.max(s, axis=0, keepdims=True))
                p = jnp.exp2(s - m_new).astype(BF16)
                alpha = jnp.exp2(m_prev - m_new)
                pv = jnp.dot(v_aug, p, preferred_element_type=F32)
                acc_ref[g, :, cols] = alpha * acc_ref[g, :, cols] + pv
                m_ref[g, :, cols] = m_new

    last = ik == nk - 1
    is_small = small_ref[0] != 0
    for parity, cur_ref, nxt_ref in ((0, sa_ref, sb_ref), (1, sb_ref, sa_ref)):
        for has_next in (True, False):
            for small in (True, False):
                @pl.when((ik % 2 == parity) & (last != has_next) & (is_small == small))
                def _(cur_ref=cur_ref, nxt_ref=nxt_ref, has_next=has_next, small=small):
                    step(cur_ref, nxt_ref, has_next, small)

    @pl.when(last)
    def _():
        for g in range(A_KV_HEADS):
            acc = acc_ref[g]
            o = acc[:HEAD_DIM] / acc[HEAD_DIM:HEAD_DIM + 1]
            for r in range(rep):
                h = g * rep + r
                o_ref[h * HEAD_DIM:(h + 1) * HEAD_DIM, :] = o[:, r * bq:(r + 1) * bq].astype(o_ref.dtype)


A_SMALL_SCORE = 60.0


def _a_scores_small(a_qn, a_kn):
    bound = (HEAD_DIM ** 0.5) * np.log2(np.e) * 1.02 * jnp.max(jnp.abs(a_qn)) * jnp.max(jnp.abs(a_kn))
    return (bound < A_SMALL_SCORE).astype(jnp.int32).reshape(1)


def _mixer_a(small, aqt, ak, avt, prev, row_off, b, s):
    n = ak.shape[0]
    bq = min(256, s)
    bk = min(1024, s)
    nq, nk = s // bq, s // bk
    rep = A_HEADS // A_KV_HEADS
    q0, k0 = row_off // bq, row_off // bk
    kern = functools.partial(_attn_a_kernel, bq=bq, nk=nk)
    return _seq_call(
        kern, prev,
        grid=(b, nq, nk),
        in_specs=[pl.BlockSpec(memory_space=pltpu.SMEM),
                  pl.BlockSpec((A_Q, bq), lambda ib, iq, ik: (0, q0 + ib * nq + iq)),
                  pl.BlockSpec((bk, A_KV), lambda ib, iq, ik: (k0 + ib * nk, 0)),
                  pl.BlockSpec((bk, A_KV), lambda ib, iq, ik: (k0 + ib * nk + jnp.minimum(ik + 1, nk - 1), 0)),
                  pl.BlockSpec((A_KV, bk), lambda ib, iq, ik: (0, k0 + ib * nk + ik))],
        out_specs=[pl.BlockSpec((A_Q, bq), lambda ib, iq, ik: (0, q0 + ib * nq + iq))],
        out_shape=[jax.ShapeDtypeStruct((A_Q, n), BF16)],
        scratch_shapes=[pltpu.VMEM((A_KV_HEADS, A_KV, rep * bq), BF16),
                        pltpu.VMEM((A_KV_HEADS, 1, rep * bq), F32),
                        pltpu.VMEM((A_KV_HEADS, HEAD_DIM + A_ONES_ROWS, rep * bq), F32),
                        pltpu.VMEM((A_KV_HEADS, bk, rep * bq), F32),
                        pltpu.VMEM((A_KV_HEADS, bk, rep * bq), F32)],
        compiler_params=_cparams(("parallel", "parallel", "arbitrary")),
        name="mixer_a",
        args=(small, aqt, ak, ak, avt))


B_BLOCK_Q = 128
B_HALO = 64


def _attn_b_kernel(q_ref, k0_ref, k1_ref, k2_ref, k3_ref, v0_ref, v1_ref, v2_ref, v3_ref,
                   o_ref, lse_ref, *, u_len):
    i = pl.program_id(2)
    kcat = jnp.concatenate([k0_ref[...], k1_ref[...], k2_ref[...], k3_ref[...]], axis=0)
    vcat = jnp.concatenate([v0_ref[...], v1_ref[...], v2_ref[...], v3_ref[...]], axis=0)
    nkeys = B_BLOCK_Q + 2 * B_HALO
    a = lax.broadcasted_iota(jnp.int32, (B_BLOCK_Q, nkeys), 0)
    c = lax.broadcasted_iota(jnp.int32, (B_BLOCK_Q, nkeys), 1)
    key_pos = i * B_BLOCK_Q - B_HALO + c
    valid = (c >= a) & (c <= a + 2 * B_HALO) & (key_pos >= 0) & (key_pos < u_len)
    left = lax.broadcasted_iota(jnp.int32, (B_BLOCK_Q, LANES), 1) < HEAD_DIM
    zero = jnp.zeros((B_BLOCK_Q, LANES), BF16)
    heads = [(pair, own) for pair in range(B_HEADS // 2) for own in (left, jnp.logical_not(left))]
    lanes = [slice(pair * LANES, (pair + 1) * LANES) for pair in range(B_HEADS // 2)]
    s = [lax.dot_general(jnp.where(own, q_ref[:, lanes[pair]], zero), kcat[:, lanes[pair]],
                         (((1,), (1,)), ((), ())), preferred_element_type=F32) for pair, own in heads]
    s = [jnp.where(valid, sh, -1e30) for sh in s]
    m = [jnp.max(sh, axis=1, keepdims=True) for sh in s]
    e = [jnp.exp(sh - mh) for sh, mh in zip(s, m)]
    den = [jnp.sum(eh, axis=1, keepdims=True) for eh in e]
    pv = [jnp.dot(eh.astype(BF16), vcat[:, lanes[pair]], preferred_element_type=F32)
          for eh, (pair, _) in zip(e, heads)]
    outs = [pvh / dh for pvh, dh in zip(pv, den)]
    lses = [mh + jnp.log(dh) for mh, dh in zip(m, den)]
    for pair in range(B_HEADS // 2):
        o_ref[:, lanes[pair]] = jnp.where(left, outs[2 * pair], outs[2 * pair + 1]).astype(o_ref.dtype)
        lse_ref[:, lanes[pair]] = jnp.where(left, lses[2 * pair], lses[2 * pair + 1])


def _mixer_b_group(qd, kd, vd, prev, g, dil, row_off, b, s):
    u_len = s // dil
    nq = u_len // B_BLOCK_Q
    nkb = u_len // B_HALO
    q_rows0 = row_off // dil // B_BLOCK_Q
    k_rows0 = row_off // dil // B_HALO

    def kspec(t):
        def imap(ib, r, i):
            blk = jnp.clip(2 * i - 1 + t, 0, nkb - 1)
            return (k_rows0 + ib * nkb + blk, r)
        return pl.BlockSpec((B_HALO, B_KV), imap)

    def qspec():
        return pl.BlockSpec((B_BLOCK_Q, B_KV), lambda ib, r, i: (q_rows0 + ib * nq + i, r))

    kern = functools.partial(_attn_b_kernel, u_len=u_len)
    return _seq_call(
        kern, prev,
        grid=(b, dil, nq),
        in_specs=[qspec()] + [kspec(t) for t in range(4)] + [kspec(t) for t in range(4)],
        out_specs=[qspec(), qspec()],
        out_shape=[jax.ShapeDtypeStruct(qd.shape, BF16), jax.ShapeDtypeStruct(qd.shape, F32)],
        compiler_params=_cparams(("parallel", "parallel", "parallel")),
        name=f"mixer_b{g}",
        args=(qd, kd, kd, kd, kd, vd, vd, vd, vd))


def _ret_chunk(q, k, v, dec_ref, xi_ref, zeta_ref, cdec_ref, r_ref):
    outs = []
    for h in range(C_HEADS):
        qh = q[:, h * C_KEY_DIM:(h + 1) * C_KEY_DIM]
        kh = k[:, h * C_KEY_DIM:(h + 1) * C_KEY_DIM]
        vh = v[:, h * C_VAL_DIM:(h + 1) * C_VAL_DIM]
        att = lax.dot_general(qh, kh, (((1,), (1,)), ((), ())), preferred_element_type=F32) * dec_ref[h]
        inner = jnp.dot(att.astype(BF16), vh, preferred_element_type=F32)
        r = r_ref[h]
        cross = jnp.dot(qh, r.astype(BF16), preferred_element_type=F32) * xi_ref[h]
        kz = (kh.astype(F32) * zeta_ref[h]).astype(BF16)
        r_ref[h] = r * cdec_ref[h] + lax.dot_general(kz, vh, (((0,), (0,)), ((), ())),
                                                     preferred_element_type=F32)
        outs.append(inner + cross)
    return jnp.concatenate(outs, axis=1)


def _ret_fwd_kernel(q_ref, k_ref, v_ref, dec_ref, xi_ref, zeta_ref, cdec_ref, o_ref, r_ref):
    @pl.when(pl.program_id(1) == 0)
    def _():
        r_ref[...] = jnp.zeros(r_ref.shape, F32)

    o_ref[...] = _ret_chunk(q_ref[...], k_ref[...], v_ref[...], dec_ref, xi_ref, zeta_ref, cdec_ref, r_ref)


def _ret_bwd_kernel(q_ref, k_ref, v_ref, dec_ref, xi_ref, zeta_ref, cdec_ref, of_ref, gate_ref, ng_ref,
                    o_ref, r_ref):
    @pl.when(pl.program_id(1) == 0)
    def _():
        r_ref[...] = jnp.zeros(r_ref.shape, F32)

    o = of_ref[...] + _ret_chunk(q_ref[...], k_ref[...], v_ref[...], dec_ref, xi_ref, zeta_ref, cdec_ref, r_ref)
    ng = ng_ref[...]
    gate = gate_ref[...].astype(F32)
    for h in range(C_HEADS):
        sl = slice(h * C_VAL_DIM, (h + 1) * C_VAL_DIM)
        oh = o[:, sl]
        mu = jnp.mean(oh, axis=1, keepdims=True)
        var = jnp.mean(jnp.square(oh - mu), axis=1, keepdims=True)
        y = (oh - mu) * lax.rsqrt(var + EPS) * ng[:, sl]
        gh = gate[:, sl]
        o_ref[:, sl] = (gh * jax.nn.sigmoid(gh) * y).astype(o_ref.dtype)


def _ret_tables(dec_param, chunk, strict):
    log_g = -jnp.exp(dec_param.astype(F32))
    j = jnp.arange(chunk, dtype=F32)
    lg = log_g[:, None, None]
    if strict:
        diff = j[None, :] - j[:, None]
        dec = jnp.where((diff > 0)[None], jnp.exp(jnp.maximum(diff, 0.0)[None] * lg), 0.0)
        xi = jnp.exp((chunk - j)[None, :, None] * lg)
        zeta = jnp.exp(j[None, :, None] * lg)
    else:
        diff = j[:, None] - j[None, :]
        dec = jnp.where((diff >= 0)[None], jnp.exp(jnp.maximum(diff, 0.0)[None] * lg), 0.0)
        xi = jnp.exp((j + 1.0)[None, :, None] * lg)
        zeta = jnp.exp((chunk - 1.0 - j)[None, :, None] * lg)
    xi = jnp.broadcast_to(xi, (C_HEADS, chunk, C_VAL_DIM))
    zeta = jnp.broadcast_to(zeta, (C_HEADS, chunk, C_KEY_DIM))
    cdec = jnp.broadcast_to(jnp.exp(chunk * log_g)[:, None, None], (C_HEADS, C_KEY_DIM, C_VAL_DIM))
    return dec, xi, zeta, cdec


C_CHUNK = 512


def _mixer_c(cq, ck, proj, tabs_f, tabs_b, norm_g, prev, row_off, b, s):
    n = cq.shape[0]
    prev_f, prev_o = (None, None) if prev is None else ([prev[0]], [prev[1]])
    nc = s // C_CHUNK
    r0 = row_off // C_CHUNK
    v_col = OFF["cv"] // C_V
    g_col = OFF["cg"] // C_V

    def fwd_rows(ib, c):
        return r0 + ib * nc + c

    def bwd_rows(ib, c):
        return r0 + ib * nc + (nc - 1 - c)

    def specs(rows):
        def full(shape):
            return pl.BlockSpec(shape, lambda ib, c: (0,) * len(shape))
        return [pl.BlockSpec((C_CHUNK, C_QK), lambda ib, c: (rows(ib, c), 0)),
                pl.BlockSpec((C_CHUNK, C_QK), lambda ib, c: (rows(ib, c), 0)),
                pl.BlockSpec((C_CHUNK, C_V), lambda ib, c: (rows(ib, c), v_col)),
                full((C_HEADS, C_CHUNK, C_CHUNK)), full((C_HEADS, C_CHUNK, C_VAL_DIM)),
                full((C_HEADS, C_CHUNK, C_KEY_DIM)), full((C_HEADS, C_KEY_DIM, C_VAL_DIM))]

    scratch = [pltpu.VMEM((C_HEADS, C_KEY_DIM, C_VAL_DIM), F32)]
    o_f, = _seq_call(
        _ret_fwd_kernel, prev_f,
        grid=(b, nc),
        in_specs=specs(fwd_rows),
        out_specs=[pl.BlockSpec((C_CHUNK, C_V), lambda ib, c: (fwd_rows(ib, c), 0))],
        out_shape=[jax.ShapeDtypeStruct((n, C_V), F32)],
        scratch_shapes=scratch,
        compiler_params=_cparams(("parallel", "arbitrary")),
        name="ret_fwd",
        args=(cq, ck, proj, *tabs_f))
    o_c, = _seq_call(
        _ret_bwd_kernel, prev_o,
        grid=(b, nc),
        in_specs=specs(bwd_rows) + [
            pl.BlockSpec((C_CHUNK, C_V), lambda ib, c: (bwd_rows(ib, c), 0)),
            pl.BlockSpec((C_CHUNK, C_V), lambda ib, c: (bwd_rows(ib, c), g_col)),
            pl.BlockSpec((1, C_V), lambda ib, c: (0, 0))],
        out_specs=[pl.BlockSpec((C_CHUNK, C_V), lambda ib, c: (bwd_rows(ib, c), 0))],
        out_shape=[jax.ShapeDtypeStruct((n, C_V), BF16)],
        scratch_shapes=scratch,
        compiler_params=_cparams(("parallel", "arbitrary")),
        name="ret_bwd",
        args=(cq, ck, proj, *tabs_b, o_f, proj, norm_g))
    return [o_f, o_c]


def _merge_kernel(oa_ref, ob0_ref, ob1_ref, ob2_ref, l0_ref, l1_ref, l2_ref, oc_ref,
                  ga_ref, gb_ref, gc_ref, x_ref, woa_ref, wob_ref, woc_ref, wout_ref, n2_ref,
                  xo_ref, hn_ref, scr_ref):
    tm = x_ref.shape[0]

    def natural(src_ref, dil, slot):
        if dil == 1:
            return src_ref[...].astype(F32)
        chunks = B_KV // LANES
        for r in range(dil):
            for c in range(chunks):
                col = r * B_KV + c * LANES
                scr_ref[slot, c, pl.ds(r, tm // dil, stride=dil), :] = src_ref[:, col:col + LANES].astype(F32)
        return jnp.concatenate([scr_ref[slot, c] for c in range(chunks)], axis=1)

    dils = [dil for _, dil in B_PATTERNS]
    l0, l1, l2 = [natural(ref, dil, 2 * g) for g, (ref, dil) in enumerate(zip((l0_ref, l1_ref, l2_ref), dils))]
    o0, o1, o2 = [natural(ref, dil, 2 * g + 1)
                  for g, (ref, dil) in enumerate(zip((ob0_ref, ob1_ref, ob2_ref), dils))]
    mx = jnp.maximum(jnp.maximum(l0, l1), l2)
    e0, e1, e2 = jnp.exp(l0 - mx), jnp.exp(l1 - mx), jnp.exp(l2 - mx)
    ob = (e0 * o0 + e1 * o1 + e2 * o2) / (e0 + e1 + e2)

    def branch(o, w_ref, gate_ref):
        y = jnp.dot(o, w_ref[...], preferred_element_type=F32)
        return jax.nn.sigmoid(gate_ref[...].astype(F32)) * y

    oa = jnp.transpose(oa_ref[...].astype(F32)).astype(BF16)
    merged = (branch(oa, woa_ref, ga_ref) + branch(ob.astype(BF16), wob_ref, gb_ref)
              + branch(oc_ref[...], woc_ref, gc_ref))
    x = x_ref[...] + jnp.dot(merged.astype(BF16), wout_ref[...], preferred_element_type=F32)
    xo_ref[...] = x
    ms = jnp.mean(x * x, axis=-1, keepdims=True)
    hn_ref[...] = (x * lax.rsqrt(ms + EPS) * n2_ref[...]).astype(hn_ref.dtype)


def _merge(o_a, o_b, lse_b, o_c, proj, x, w_oa, w_ob, w_oc, w_out, n2, hn_dtype):
    n = x.shape[0]
    tm = min(512, n)
    gl0 = OFF["gl"] // D_MODEL

    def rows(width):
        return pl.BlockSpec((tm, width), lambda i: (i, 0))

    def gate(k):
        return pl.BlockSpec((tm, D_MODEL), lambda i: (i, gl0 + k))

    def full(r, c):
        return pl.BlockSpec((r, c), lambda i: (0, 0))

    grouped = [pl.BlockSpec((tm // dil, dil * B_KV), lambda i: (i, 0)) for _, dil in B_PATTERNS]
    return pl.pallas_call(
        _merge_kernel,
        grid=(n // tm,),
        in_specs=[pl.BlockSpec((A_Q, tm), lambda i: (0, i))] + grouped + grouped
                 + [rows(C_V), gate(0), gate(1), gate(2), rows(D_MODEL),
                    full(A_Q, D_MODEL), full(B_KV, D_MODEL), full(C_V, D_MODEL), full(D_MODEL, D_MODEL),
                    full(1, D_MODEL)],
        out_specs=[rows(D_MODEL), rows(D_MODEL)],
        out_shape=[jax.ShapeDtypeStruct((n, D_MODEL), F32), jax.ShapeDtypeStruct((n, D_MODEL), hn_dtype)],
        scratch_shapes=[pltpu.VMEM((2 * B_GROUPS, B_KV // LANES, tm, LANES), F32)],
        compiler_params=_cparams(("parallel",)),
        name="merge_out",
    )(o_a, *o_b, *lse_b, o_c, proj, proj, proj, x, w_oa, w_ob, w_oc, w_out, n2)


def _ffn_kernel(hn_ref, x_ref, w1_ref, w3_ref, w2_ref, o_ref):
    j = pl.program_id(1)
    hn = hn_ref[...]
    a = jnp.dot(hn, w1_ref[...], preferred_element_type=F32)
    g = jnp.dot(hn, w3_ref[...], preferred_element_type=F32)
    h = (a * jax.nn.sigmoid(a) * g).astype(BF16)
    y = jnp.dot(h, w2_ref[...], preferred_element_type=F32)

    @pl.when(j == 0)
    def _():
        o_ref[...] = x_ref[...] + y

    @pl.when(j > 0)
    def _():
        o_ref[...] += y


def _ffn(hn, x, w1, w3, w2):
    n = x.shape[0]
    d_ff = w1.shape[1]
    tm = min(1024, n)
    tf = d_ff // 2
    nf = d_ff // tf
    return pl.pallas_call(
        _ffn_kernel,
        grid=(n // tm, nf),
        in_specs=[pl.BlockSpec((tm, D_MODEL), lambda i, j: (i, 0)),
                  pl.BlockSpec((tm, D_MODEL), lambda i, j: (i, 0)),
                  pl.BlockSpec((D_MODEL, tf), lambda i, j: (0, j)),
                  pl.BlockSpec((D_MODEL, tf), lambda i, j: (0, j)),
                  pl.BlockSpec((tf, D_MODEL), lambda i, j: (j, 0))],
        out_specs=pl.BlockSpec((tm, D_MODEL), lambda i, j: (i, 0)),
        out_shape=jax.ShapeDtypeStruct((n, D_MODEL), F32),
        compiler_params=_cparams(("parallel", "arbitrary")),
        name="ffn",
    )(hn, x, w1, w3, w2)


TOP_K = 2
MOE_TM = 512


def _router_kernel(hn_ref, wr_ref, idx_ref, wts_ref):
    logits = jnp.dot(hn_ref[...], wr_ref[...], preferred_element_type=F32,
                     precision=lax.Precision.HIGHEST)
    col = lax.broadcasted_iota(jnp.int32, logits.shape, 1)
    m1 = jnp.max(logits, axis=1, keepdims=True)
    i1 = jnp.min(jnp.where(logits == m1, col, N_EXPERTS), axis=1, keepdims=True)
    rest = jnp.where(col == i1, -jnp.inf, logits)
    m2 = jnp.max(rest, axis=1, keepdims=True)
    i2 = jnp.min(jnp.where(rest == m2, col, N_EXPERTS), axis=1, keepdims=True)
    e2 = jnp.exp(m2 - m1)
    w1 = 1.0 / (1.0 + e2)
    idx_ref[...] = jnp.concatenate([i1, i2], axis=1)
    wts_ref[...] = jnp.concatenate([w1, e2 * w1], axis=1)


def _router(hn, wr):
    n = hn.shape[0]
    tm = min(1024, n)
    return pl.pallas_call(
        _router_kernel,
        grid=(n // tm,),
        in_specs=[pl.BlockSpec((tm, D_MODEL), lambda i: (i, 0)),
                  pl.BlockSpec((D_MODEL, N_EXPERTS), lambda i: (0, 0))],
        out_specs=[pl.BlockSpec((tm, TOP_K), lambda i: (i, 0)), pl.BlockSpec((tm, TOP_K), lambda i: (i, 0))],
        out_shape=[jax.ShapeDtypeStruct((n, TOP_K), jnp.int32), jax.ShapeDtypeStruct((n, TOP_K), F32)],
        compiler_params=_cparams(("parallel",)),
        name="router",
    )(hn, wr)


def _route(idx, tm):
    n = idx.shape[0]
    e_flat = idx.reshape(-1)
    onehot = (e_flat[:, None] == jnp.arange(N_EXPERTS, dtype=jnp.int32)[None, :]).astype(jnp.int32)
    csum = jnp.cumsum(onehot, axis=0)
    rank = jnp.sum((csum - onehot) * onehot, axis=1)
    gsz = ((csum[-1] + tm - 1) // tm) * tm
    gend = jnp.cumsum(gsz)
    pos = (gend - gsz)[e_flat] + rank
    n_rows = n * TOP_K + N_EXPERTS * tm
    n_tiles = n_rows // tm
    row_token = jnp.zeros((n_rows,), jnp.int32).at[pos].set(jnp.arange(n * TOP_K, dtype=jnp.int32) // TOP_K)
    tile_start = jnp.arange(n_tiles, dtype=jnp.int32) * tm
    tile_expert = jnp.minimum(jnp.sum((tile_start[:, None] >= gend[None, :]).astype(jnp.int32), axis=1),
                              N_EXPERTS - 1)
    n_used = (gend[-1] // tm).astype(jnp.int32).reshape(1)
    return pos.reshape(n, TOP_K), row_token.reshape(n_tiles, 1, tm), tile_expert, n_used


def _row_copy(src_ref, src_row, dst_ref, dst_row, sem):
    return pltpu.make_async_copy(src_ref.at[pl.ds(src_row, 1)], dst_ref.at[pl.ds(dst_row, 1)], sem)


def _expert_kernel(te_ref, nu_ref, idx_ref, nxt_ref, src_ref, w1_ref, w3_ref, w2_ref, y_ref,
                   xs_ref, xb_ref, acc_ref, sem, *, nf, tm):
    i = pl.program_id(0)
    j = pl.program_id(1)
    n_used = nu_ref[0]
    active = i < n_used
    slot = i % 2
    share = tm // nf

    @pl.when((i == 0) & (j == 0))
    def _():
        def issue(r, carry):
            _row_copy(src_ref, idx_ref[0, 0, r], xs_ref.at[0], r, sem.at[0]).start()
            return carry
        lax.fori_loop(0, tm, issue, 0, unroll=8)

    @pl.when((j == 0) & (i <= n_used))
    def _():
        pltpu.make_async_copy(src_ref.at[pl.ds(0, tm)], xs_ref.at[slot], sem.at[slot]).wait()

    @pl.when(active)
    def _():
        @pl.when(j == 0)
        def _():
            xb_ref[...] = xs_ref[slot].astype(BF16)

        for r in range(share):
            row = j * share + r
            _row_copy(src_ref, nxt_ref[0, 0, row], xs_ref.at[1 - slot], row, sem.at[1 - slot]).start()

        xb = xb_ref[...]
        a = jnp.dot(xb, w1_ref[0], preferred_element_type=F32)
        g = jnp.dot(xb, w3_ref[0], preferred_element_type=F32)
        h = (a * jax.nn.sigmoid(a) * g).astype(BF16)
        y = jnp.dot(h, w2_ref[0], preferred_element_type=F32)

        @pl.when(j == 0)
        def _():
            acc_ref[...] = y

        @pl.when(j > 0)
        def _():
            acc_ref[...] += y

        @pl.when(j == nf - 1)
        def _():
            y_ref[...] = acc_ref[...]

    @pl.when(jnp.logical_not(active) & (j == nf - 1))
    def _():
        y_ref[...] = jnp.zeros(y_ref.shape, y_ref.dtype)


def _experts(src, row_token, tile_expert, n_used, w1, w3, w2):
    nt, _, tm = row_token.shape
    d_ff = w1.shape[2]
    nf = 2
    tf = d_ff // nf

    def jcol(i, j, nu):
        return jnp.where(i < nu[0], j, nf - 1)

    grid_spec = pltpu.PrefetchScalarGridSpec(
        num_scalar_prefetch=2,
        grid=(nt, nf),
        in_specs=[pl.BlockSpec((1, 1, tm), lambda i, j, te, nu: (i, 0, 0), memory_space=pltpu.SMEM),
                  pl.BlockSpec((1, 1, tm), lambda i, j, te, nu: (jnp.minimum(i + 1, nt - 1), 0, 0),
                               memory_space=pltpu.SMEM),
                  pl.BlockSpec(memory_space=pl.ANY),
                  pl.BlockSpec((1, D_MODEL, tf), lambda i, j, te, nu: (te[i], 0, jcol(i, j, nu))),
                  pl.BlockSpec((1, D_MODEL, tf), lambda i, j, te, nu: (te[i], 0, jcol(i, j, nu))),
                  pl.BlockSpec((1, tf, D_MODEL), lambda i, j, te, nu: (te[i], jcol(i, j, nu), 0))],
        out_specs=pl.BlockSpec((tm, D_MODEL), lambda i, j, te, nu: (i, 0)),
        scratch_shapes=[pltpu.VMEM((2, tm, D_MODEL), src.dtype), pltpu.VMEM((tm, D_MODEL), BF16),
                        pltpu.VMEM((tm, D_MODEL), F32), pltpu.SemaphoreType.DMA((2,))],
    )
    return pl.pallas_call(
        functools.partial(_expert_kernel, nf=nf, tm=tm),
        grid_spec=grid_spec,
        out_shape=jax.ShapeDtypeStruct((nt * tm, D_MODEL), F32),
        compiler_params=_cparams(("arbitrary", "arbitrary")),
        name="moe_experts",
    )(tile_expert, n_used, row_token, row_token, src, w1, w3, w2)


def _combine_kernel(pos_ref, nxt_ref, wts_ref, x_ref, y_ref, o_ref, g_ref, sem, *, tm, nt):
    i = pl.program_id(0)
    slot = i % 2

    def request(p_ref, dst_slot):
        def issue(r, carry):
            for k in range(TOP_K):
                _row_copy(y_ref, p_ref[0, 0, TOP_K * r + k], g_ref.at[dst_slot, k], r,
                          sem.at[dst_slot]).start(priority=k % 2)
            return carry
        lax.fori_loop(0, tm, issue, 0, unroll=8)

    @pl.when(i == 0)
    def _():
        request(pos_ref, 0)

    @pl.when(i + 1 < nt)
    def _():
        request(nxt_ref, 1 - slot)

    for k in range(TOP_K):
        pltpu.make_async_copy(y_ref.at[pl.ds(0, tm)], g_ref.at[slot, k], sem.at[slot]).wait()
    w = wts_ref[...]
    o_ref[...] = x_ref[...] + w[:, 0:1] * g_ref[slot, 0] + w[:, 1:2] * g_ref[slot, 1]


def _combine(pos, wts, x, y):
    n = x.shape[0]
    tm = min(256, n)
    nt = n // tm
    pos3 = pos.reshape(nt, 1, TOP_K * tm)
    return pl.pallas_call(
        functools.partial(_combine_kernel, tm=tm, nt=nt),
        grid=(nt,),
        in_specs=[pl.BlockSpec((1, 1, TOP_K * tm), lambda i: (i, 0, 0), memory_space=pltpu.SMEM),
                  pl.BlockSpec((1, 1, TOP_K * tm), lambda i: (jnp.minimum(i + 1, nt - 1), 0, 0),
                               memory_space=pltpu.SMEM),
                  pl.BlockSpec((tm, TOP_K), lambda i: (i, 0)),
                  pl.BlockSpec((tm, D_MODEL), lambda i: (i, 0)),
                  pl.BlockSpec(memory_space=pl.ANY)],
        out_specs=pl.BlockSpec((tm, D_MODEL), lambda i: (i, 0)),
        out_shape=jax.ShapeDtypeStruct((n, D_MODEL), F32),
        scratch_shapes=[pltpu.VMEM((2, TOP_K, tm, D_MODEL), F32), pltpu.SemaphoreType.DMA((2,))],
        compiler_params=_cparams(("arbitrary",)),
        name="moe_combine",
    )(pos3, pos3, wts, x, y)


def _moe(hn, x, wr, w1, w3, w2):
    idx, wts = _router(hn, wr)
    pos, row_token, tile_expert, n_used = _route(idx, MOE_TM)
    y = _experts(hn, row_token, tile_expert, n_used, w1, w3, w2)
    return _combine(pos, wts, x, y)


def _permute_w_in(w):
    cols = [w[:, _ORIG[name][0]:_ORIG[name][0] + _ORIG[name][1]] for name in _NEW_ORDER]
    cols.append(jnp.zeros((w.shape[0], PROJ_W - _USED_W), w.dtype))
    return jnp.concatenate(cols, axis=1).astype(BF16)


def _angles(pos, dim):
    inv = ROPE_THETA ** (-jnp.arange(0, dim, 2, dtype=F32) / dim)
    return pos.astype(F32)[:, None] * inv[None, :]


def _rope_tables(seqs):
    pos = jnp.concatenate([jnp.tile(jnp.arange(s), b) for b, s in seqs])
    ang = _angles(pos, HEAD_DIM)
    cosb = jnp.tile(jnp.cos(ang), (1, 4))
    sinb = jnp.tile(jnp.concatenate([-jnp.sin(ang), jnp.sin(ang)], axis=1), (1, 2))
    ar = _angles(pos // GRID_W, HEAD_DIM // 2)
    ac = _angles(pos % GRID_W, HEAD_DIM // 2)
    cosa = jnp.tile(jnp.concatenate([jnp.cos(ar), jnp.cos(ar), jnp.cos(ac), jnp.cos(ac)], axis=1), (1, 2))
    sina = jnp.tile(jnp.concatenate([-jnp.sin(ar), jnp.sin(ar), -jnp.sin(ac), jnp.sin(ac)], axis=1), (1, 2))
    return cosb, sinb, cosa, sina


def _head_mean_matrix():
    blk = np.kron(np.eye(LANES // HEAD_DIM), np.ones((HEAD_DIM, HEAD_DIM))) / HEAD_DIM
    return jnp.asarray(blk, BF16)


def _gain_rows(a_qn, a_kn, b_qn, b_kn):
    scale = HEAD_DIM ** -0.5
    scale_a = scale * np.log2(np.e)
    rows = [jnp.tile(b_qn, 2) * scale, jnp.tile(b_kn, 2), jnp.tile(a_qn, 2) * scale_a, jnp.tile(a_kn, 2),
            jnp.ones((LANES,), F32), jnp.full((LANES,), C_KEY_DIM ** -0.5, F32),
            jnp.ones((LANES,), F32), jnp.ones((LANES,), F32)]
    return jnp.stack(rows).astype(F32)


def _trunk(x, seqs, norm1_g, w_in, a_qn, a_kn, b_qn, b_kn, ret_dec_f, ret_dec_b, ret_norm_g,
           w_oa, w_ob, w_oc, w_out, norm2_g, ffn_w1, ffn_w3, ffn_w2,
           moe_router, moe_w1, moe_w3, moe_w2):
    depth = w_in.shape[0]
    tables = _rope_tables(seqs)
    mmat = _head_mean_matrix()
    for l in range(depth):
        proj = _inproj(x, norm1_g[l][None, :], _permute_w_in(w_in[l]))
        prepped = _prep(proj, tables, _gain_rows(a_qn[l], a_kn[l], b_qn[l], b_kn[l]), mmat)
        bqs, bks, bvs = prepped[0:3], prepped[3:6], prepped[6:9]
        aq, ak, cq, ck = prepped[9:]
        aqt = aq.T
        avt = proj[:, OFF["av"]:OFF["av"] + A_KV].T
        tabs_f = _ret_tables(ret_dec_f[l], C_CHUNK, False)
        tabs_b = _ret_tables(ret_dec_b[l], C_CHUNK, True)
        ng = ret_norm_g[l][None, :].astype(F32)
        res_a, res_c = None, None
        res_b = [None] * B_GROUPS
        row_off = 0
        for b, s in seqs:
            res_a = _mixer_a(_a_scores_small(a_qn[l], a_kn[l]), aqt, ak, avt, res_a, row_off, b, s)
            for g, (_, dil) in enumerate(B_PATTERNS):
                res_b[g] = _mixer_b_group(bqs[g], bks[g], bvs[g], res_b[g], g, dil, row_off, b, s)
            res_c = _mixer_c(cq, ck, proj, tabs_f, tabs_b, ng, res_c, row_off, b, s)
            row_off += b * s
        x, hn = _merge(res_a[0], [r[0] for r in res_b], [r[1] for r in res_b], res_c[1], proj, x,
                       w_oa[l].astype(BF16), w_ob[l].astype(BF16), w_oc[l].astype(BF16),
                       w_out[l].astype(BF16), norm2_g[l][None, :], BF16 if l % 2 == 0 else F32)
        i = l // 2
        if l % 2 == 0:
            x = _ffn(hn, x, ffn_w1[i].astype(BF16), ffn_w3[i].astype(BF16), ffn_w2[i].astype(BF16))
        else:
            x = _moe(hn, x, moe_router[i], moe_w1[i].astype(BF16), moe_w3[i].astype(BF16),
                     moe_w2[i].astype(BF16))
    return x


def kernel(x_prompt, x_sample, norm1_g, w_in, a_qn, a_kn, b_qn, b_kn, ret_dec_f, ret_dec_b, ret_norm_g,
           w_oa, w_ob, w_oc, w_out, norm2_g, ffn_w1, ffn_w3, ffn_w2, moe_router, moe_w1, moe_w3, moe_w2):
    seqs = (x_prompt.shape[:2], x_sample.shape[:2])
    x = jnp.concatenate([x_prompt.reshape(-1, D_MODEL), x_sample.reshape(-1, D_MODEL)], axis=0)
    y = _trunk(x, seqs, norm1_g, w_in, a_qn, a_kn, b_qn, b_kn, ret_dec_f, ret_dec_b, ret_norm_g,
               w_oa, w_ob, w_oc, w_out, norm2_g, ffn_w1, ffn_w3, ffn_w2,
               moe_router, moe_w1, moe_w3, moe_w2)
    n_p = x_prompt.shape[0] * x_prompt.shape[1]
    return (y[:n_p].reshape(x_prompt.shape), y[n_p:].reshape(x_sample.shape))
```
